```python
import math
import jax
import jax.numpy as jnp
from jax import lax
import numpy as np

D_MODEL = 1024
BATCH = 16
SEQ = 4096
DEPTH = 4

GRID_W = 64
CTX_LEN = 256
HEAD_DIM = 64
MIX_WIDTH = D_MODEL
RET_WIDTH = MIX_WIDTH // 4
SSD_WIDTH = MIX_WIDTH // 2
NA_WIDTH = MIX_WIDTH - RET_WIDTH - SSD_WIDTH
RET_HEADS = RET_WIDTH // HEAD_DIM
RET_DK = HEAD_DIM
SSD_HEADS = SSD_WIDTH // HEAD_DIM
SSD_HEADDIM = HEAD_DIM
SSD_GROUPS = 2
SSD_STATE = 64
SSD_CONV = 3
NA_HEADS = NA_WIDTH // HEAD_DIM
NA_WIN_ROWS = 8
NA_WIN_COLS = 16
NA_QBLOCK_COLS = 16
NA_KEY_COLS = NA_QBLOCK_COLS + NA_WIN_COLS
SCAN_CHUNK = 128
N_EXPERTS = 16
N_GROUPS = 4
EXPERTS_PER_GROUP = N_EXPERTS // N_GROUPS
TOP_K = 2
D_EXPERT = D_MODEL // 2
ROPE_BASE = 10000.0
EPS = 1e-6

SEGMENTS = (
    ("ret_q", RET_WIDTH), ("ret_k", RET_WIDTH), ("ret_v", RET_WIDTH), ("ret_g", RET_WIDTH),
    ("ssd_z", SSD_WIDTH), ("ssd_x", SSD_WIDTH), ("ssd_b", SSD_GROUPS * SSD_STATE),
    ("ssd_c", SSD_GROUPS * SSD_STATE), ("ssd_dt", 2 * SSD_HEADS),
    ("na_q", NA_WIDTH), ("na_k", NA_WIDTH), ("na_v", NA_WIDTH),
)
IN_COLS = 4 * RET_WIDTH + 2 * SSD_WIDTH + 2 * SSD_GROUPS * SSD_STATE + 2 * SSD_HEADS + 3 * NA_WIDTH
SSD_XBC = SSD_WIDTH + 2 * SSD_GROUPS * SSD_STATE
SSD_XB = SSD_WIDTH + SSD_GROUPS * SSD_STATE

kernel_name = "hybrid_prefix_retention_ssd_natten_moe"


def _offsets():
    off, start = {}, 0
    for name, width in SEGMENTS:
        off[name] = (start, start + width)
        start += width
    return off


def _segment_getter(h, w_in, full):
    off = _offsets()
    if full:
        u = h @ w_in
        return lambda a, b=None: u[..., off[a][0]:off[b or a][1]]
    return lambda a, b=None: h @ w_in[:, off[a][0]:off[b or a][1]]


def rmsnorm(t, w):
    t32 = t.astype(jnp.float32)
    y = t32 * lax.rsqrt(jnp.mean(t32 * t32, axis=-1, keepdims=True) + EPS) * w.astype(jnp.float32)
    return y.astype(t.dtype)


def modulate(h, shift, scale):
    return h * (1 + scale) + shift


def head_groupnorm(y):
    y = y.astype(jnp.float32)
    mu = jnp.mean(y, axis=-1, keepdims=True)
    var = jnp.mean(jnp.square(y - mu), axis=-1, keepdims=True)
    return (y - mu) * lax.rsqrt(var + EPS)


def axial_rope_tables(n_tokens):
    t = jnp.arange(n_tokens)
    pos = jnp.stack([t // GRID_W, t % GRID_W], axis=-1).astype(jnp.float32)
    n_freq = RET_DK // 4
    inv = 1.0 / (ROPE_BASE ** (jnp.arange(n_freq, dtype=jnp.float32) / n_freq))
    ang = pos[:, :, None] * inv
    return jnp.cos(ang), jnp.sin(ang)


def apply_axial_rope(t, cos, sin):
    b_, l_, h_, dk = t.shape
    tr = t.reshape(b_, l_, h_, 2, 2, dk // 4).astype(jnp.float32)
    x1, x2 = tr[..., 0, :], tr[..., 1, :]
    cs, sn = cos[None, :, None], sin[None, :, None]
    out = jnp.stack([x1 * cs - x2 * sn, x2 * cs + x1 * sn], axis=-2)
    return out.reshape(t.shape).astype(t.dtype)


def depthwise_conv(t, w, b):
    k = w.shape[0]
    y = lax.conv_general_dilated(
        t, w.astype(t.dtype)[:, None, :], window_strides=(1,), padding=[(k // 2, k // 2)],
        dimension_numbers=("NWC", "WIO", "NWC"), feature_group_count=t.shape[-1])
    return y + b.astype(t.dtype)


def chunk_scan(q, k, v, a, s0):
    bsz, seq, g, n = q.shape
    r, p = v.shape[3], v.shape[4]
    nc = seq // SCAN_CHUNK
    qc = q.reshape(bsz, nc, SCAN_CHUNK, g, n)
    kc = k.reshape(bsz, nc, SCAN_CHUNK, g, n)
    vc = v.reshape(bsz, nc, SCAN_CHUNK, g, r, p)
    acum = jnp.cumsum(a.astype(jnp.float32).reshape(bsz, nc, SCAN_CHUNK, g, r), axis=2)
    tril = jnp.tril(jnp.ones((SCAN_CHUNK, SCAN_CHUNK), dtype=bool))[None, None, :, :, None, None]
    seg = acum[:, :, :, None] - acum[:, :, None, :]
    decay = jnp.exp(jnp.where(tril, seg, -jnp.inf))
    scores = jnp.einsum("bclgn,bcsgn->bclsg", qc, kc)
    y_diag = jnp.einsum("bclsgr,bcsgrp->bclgrp", scores[..., None] * decay, vc)
    w_end = jnp.exp(acum[:, :, -1:] - acum)
    states = jnp.einsum("bcsgn,bcsgrp->bcgrpn", kc, vc * w_end[..., None])
    chunk_decay = jnp.exp(acum[:, :, -1])
    if s0 is None:
        s0 = jnp.zeros((bsz, g, r, p, n), jnp.float32)

    def step(h, inp):
        st, dc = inp
        return h * dc[..., None, None] + st, h

    s_final, h_prev = lax.scan(step, s0.astype(jnp.float32),
                               (jnp.moveaxis(states, 1, 0), jnp.moveaxis(chunk_decay, 1, 0)))
    h_prev = jnp.moveaxis(h_prev, 0, 1)
    y_off = jnp.einsum("bclgn,bcgrpn->bclgrp", qc, h_prev) * jnp.exp(acum)[..., None]
    return (y_diag + y_off).reshape(bsz, seq, g, r, p), s_final


def final_state(k, v, a):
    acum = jnp.cumsum(a.astype(jnp.float32), axis=1)
    w = jnp.exp(acum[:, -1:] - acum)
    return jnp.einsum("blgn,blgrp->bgrpn", k, v * w[..., None])


def _flip(t):
    return None if t is None else jnp.flip(t, axis=1)


def prefix_scan(q_c, k_c, v_c, a_c, q_l, k_l, v_l, a_l, need_ctx, reverse):
    if reverse:
        q_c, k_c, v_c, a_c = _flip(q_c), _flip(k_c), _flip(v_c), _flip(a_c)
        q_l, k_l, v_l, a_l = _flip(q_l), _flip(k_l), _flip(v_l), _flip(a_l)
    if need_ctx:
        y_c, s_c = chunk_scan(q_c, k_c, v_c, a_c, None)
    else:
        y_c, s_c = None, final_state(k_c, v_c, a_c)
    y_l, _ = chunk_scan(q_l, k_l, v_l, a_l, s_c)
    if reverse:
        y_c, y_l = _flip(y_c), _flip(y_l)
    return y_c, y_l


def retention_group(get_l, get_c, decay_f, decay_b, cos, sin, need_ctx):
    heads = lambda t: t.reshape(t.shape[0], t.shape[1], RET_HEADS, RET_DK)
    k_scale = RET_DK ** -0.5
    q_l = apply_axial_rope(heads(get_l("ret_q")), cos, sin)
    k_l = apply_axial_rope(heads(get_l("ret_k")), cos, sin) * k_scale
    v_l = heads(get_l("ret_v"))[:, :, :, None, :]
    k_c = heads(get_c("ret_k")) * k_scale
    v_c = heads(get_c("ret_v"))[:, :, :, None, :]
    q_c = heads(get_c("ret_q")) if need_ctx else None
    y_l, y_c = 0.0, 0.0
    for param, reverse in ((decay_f, False), (decay_b, True)):
        log_gamma = -jnp.exp(param.astype(jnp.float32))[:, None]
        a_l = jnp.broadcast_to(log_gamma, q_l.shape[:2] + log_gamma.shape)
        a_c = jnp.broadcast_to(log_gamma, k_c.shape[:2] + log_gamma.shape)
        yc, yl = prefix_scan(q_c, k_c, v_c, a_c, q_l, k_l, v_l, a_l, need_ctx, reverse)
        y_l = y_l + yl
        if need_ctx:
            y_c = y_c + yc

    def finish(y, g):
        y = head_groupnorm(y[:, :, :, 0, :]).reshape(g.shape)
        return (y * jax.nn.silu(g.astype(jnp.float32))).astype(g.dtype)

    out_l = finish(y_l, get_l("ret_g"))
    out_c = finish(y_c, get_c("ret_g")) if need_ctx else None
    return out_l, out_c


def ssd_group(get_l, get_c, conv_w, conv_b, dtb_f, dtb_b, alog_f, alog_b, d_skip, norm_w, need_ctx):
    def split_xbc(t):
        x = t[..., :SSD_WIDTH]
        b = t[..., SSD_WIDTH:SSD_XB]
        c = t[..., SSD_XB:] if t.shape[-1] == SSD_XBC else None
        return x, b, c

    x_l, b_l, c_l = split_xbc(jax.nn.silu(depthwise_conv(get_l("ssd_x", "ssd_c"), conv_w, conv_b)))
    if need_ctx:
        xbc_c = jax.nn.silu(depthwise_conv(get_c("ssd_x", "ssd_c"), conv_w, conv_b))
    else:
        xbc_c = jax.nn.silu(depthwise_conv(get_c("ssd_x", "ssd_b"), conv_w[:, :SSD_XB], conv_b[:SSD_XB]))
    x_c, b_c, c_c = split_xbc(xbc_c)
    dt_l, dt_c = get_l("ssd_dt"), get_c("ssd_dt")
    rep = SSD_HEADS // SSD_GROUPS
    grp = lambda t: None if t is None else t.reshape(t.shape[0], t.shape[1], SSD_GROUPS, SSD_STATE)
    y_l, y_c = 0.0, 0.0
    for d, (dtb, alog, reverse) in enumerate(((dtb_f, alog_f, False), (dtb_b, alog_b, True))):
        a_coef = -jnp.exp(alog.astype(jnp.float32))

        def prep(x, dt_raw):
            bsz, seq = x.shape[:2]
            dt = jax.nn.softplus(dt_raw[..., d * SSD_HEADS:(d + 1) * SSD_HEADS].astype(jnp.float32)
                                 + dtb.astype(jnp.float32))
            a = (dt * a_coef).reshape(bsz, seq, SSD_GROUPS, rep)
            v = x.reshape(bsz, seq, SSD_GROUPS, rep, SSD_HEADDIM) * dt.reshape(bsz, seq, SSD_GROUPS, rep)[..., None]
            return v, a

        v_l, a_l = prep(x_l, dt_l)
        v_c, a_c = prep(x_c, dt_c)
        yc, yl = prefix_scan(grp(c_c), grp(b_c), v_c, a_c, grp(c_l), grp(b_l), v_l, a_l, need_ctx, reverse)
        y_l = y_l + yl
        if need_ctx:
            y_c = y_c + yc

    def finish(y, x, z):
        bsz, seq = x.shape[:2]
        xh = x.reshape(bsz, seq, SSD_HEADS, SSD_HEADDIM).astype(jnp.float32)
        y = y.reshape(bsz, seq, SSD_HEADS, SSD_HEADDIM) + d_skip.astype(jnp.float32)[:, None] * xh
        y = y.reshape(bsz, seq, SSD_WIDTH) * jax.nn.silu(z.astype(jnp.float32))
        return rmsnorm(y, norm_w).astype(z.dtype)

    out_l = finish(y_l, x_l, get_l("ssd_z"))
    out_c = finish(y_c, x_c, get_c("ssd_z")) if need_ctx else None
    return out_l, out_c


def _na_column_tables():
    ncb = GRID_W // NA_QBLOCK_COLS
    c0 = np.arange(ncb) * NA_QBLOCK_COLS
    kc0 = np.clip(c0 - NA_WIN_COLS // 2, 0, GRID_W - NA_KEY_COLS)
    col_idx = kc0[:, None] + np.arange(NA_KEY_COLS)[None]
    qcol = c0[:, None] + np.arange(NA_QBLOCK_COLS)[None]
    cs = np.clip(qcol - NA_WIN_COLS // 2, 0, GRID_W - NA_WIN_COLS)[..., None]
    kcol = col_idx[:, None, :]
    mask = (kcol >= cs) & (kcol < cs + NA_WIN_COLS)
    dc = np.clip(kcol - qcol[..., None] + NA_WIN_COLS - 1, 0, 2 * NA_WIN_COLS - 2)
    return col_idx, dc, mask


def neighbourhood_attention(q, k, v, k_ctx, v_ctx, rpb, rows):
    bsz, seq, nh, d = q.shape
    wr = min(NA_WIN_ROWS, rows)
    col_idx, dc_idx, col_mask = _na_column_tables()
    ncb = col_idx.shape[0]
    scale = d ** -0.5
    qg = jnp.moveaxis(q.reshape(bsz, rows, ncb, NA_QBLOCK_COLS, nh, d), 1, 0)
    kg = k.reshape(bsz, rows, GRID_W, nh, d)[:, :, col_idx]
    vg = v.reshape(bsz, rows, GRID_W, nh, d)[:, :, col_idx]
    mask = jnp.asarray(col_mask)[:, :, None, :]
    n_lat_keys = wr * NA_KEY_COLS

    def one_row(args):
        q_r, r = args
        r0 = jnp.clip(r - wr // 2, 0, rows - wr)
        k_win = lax.dynamic_slice_in_dim(kg, r0, wr, axis=1)
        v_win = lax.dynamic_slice_in_dim(vg, r0, wr, axis=1)
        dr = r0 + jnp.arange(wr) - r + (NA_WIN_ROWS - 1)
        bias = rpb[:, dr[:, None, None, None], dc_idx[None]]
        bias = jnp.transpose(bias, (0, 2, 3, 1, 4)).astype(jnp.float32)
        s_lat = jnp.einsum("bjqhd,bijkhd->bhjqik", q_r, k_win).astype(jnp.float32) * scale + bias
        s_lat = jnp.where(mask, s_lat, -jnp.inf)
        s_ctx = jnp.einsum("bjqhd,bmhd->bhjqm", q_r, k_ctx).astype(jnp.float32) * scale
        s = jnp.concatenate([s_lat.reshape(bsz, nh, ncb, NA_QBLOCK_COLS, n_lat_keys), s_ctx], axis=-1)
        prob = jax.nn.softmax(s, axis=-1)
        p_lat = prob[..., :n_lat_keys].reshape(s_lat.shape).astype(v.dtype)
        p_ctx = prob[..., n_lat_keys:].astype(v.dtype)
        return (jnp.einsum("bhjqik,bijkhd->bjqhd", p_lat, v_win)
                + jnp.einsum("bhjqm,bmhd->bjqhd", p_ctx, v_ctx))

    o = lax.map(one_row, (qg, jnp.arange(rows)))
    return jnp.moveaxis(o, 0, 1).reshape(bsz, seq, nh * d)


def context_attention(q, k, v):
    s = jnp.einsum("bqhd,bkhd->bhqk", q, k).astype(jnp.float32) * q.shape[-1] ** -0.5
    prob = jax.nn.softmax(s, axis=-1).astype(v.dtype)
    o = jnp.einsum("bhqk,bkhd->bqhd", prob, v)
    return o.reshape(o.shape[0], o.shape[1], -1)


def na_group(get_l, get_c, q_norm, k_norm, rpb, rows, need_ctx):
    heads = lambda t: t.reshape(t.shape[0], t.shape[1], NA_HEADS, HEAD_DIM)
    q_l = rmsnorm(heads(get_l("na_q")), q_norm)
    k_l = rmsnorm(heads(get_l("na_k")), k_norm)
    v_l = heads(get_l("na_v"))
    k_c = rmsnorm(heads(get_c("na_k")), k_norm)
    v_c = heads(get_c("na_v"))
    out_l = neighbourhood_attention(q_l, k_l, v_l, k_c, v_c, rpb, rows)
    out_c = context_attention(rmsnorm(heads(get_c("na_q")), q_norm), k_c, v_c) if need_ctx else None
    return out_l, out_c


def moe_ffn(h, router_w, router_b, w_gate, w_up, w_down):
    shp = h.shape
    t = h.reshape(-1, shp[-1])
    aff = jax.nn.sigmoid((t @ router_w).astype(jnp.float32))
    sel = (aff + router_b.astype(jnp.float32)).reshape(-1, N_GROUPS, EXPERTS_PER_GROUP)
    group_score = jnp.sum(lax.top_k(sel, TOP_K)[0], axis=-1)
    best = jnp.argmax(group_score, axis=-1)
    in_group = (jnp.arange(N_GROUPS)[None, :] == best[:, None])[..., None]
    sel = jnp.where(in_group, sel, -jnp.inf).reshape(-1, N_EXPERTS)
    _, idx = lax.top_k(sel, TOP_K)
    w = jnp.take_along_axis(aff, idx, axis=-1)
    w = w / jnp.sum(w, axis=-1, keepdims=True)
    gates = jnp.sum(jax.nn.one_hot(idx, N_EXPERTS, dtype=jnp.float32) * w[..., None], axis=1).astype(h.dtype)
    out = jnp.zeros_like(t)
    for e in range(N_EXPERTS):
        he = jax.nn.silu(t @ w_gate[e]) * (t @ w_up[e])
        out = out + gates[:, e:e + 1] * (he @ w_down[e])
    return out.reshape(shp)


def setup_inputs(seed: int = 0) -> dict:
    key = jax.random.key(seed)
    ks = jax.random.split(key, 32)
    f32 = jnp.float32
    nrm = lambda k, shape, s: jax.random.normal(k, shape, f32) * s
    ret_base = jnp.log(-jnp.log(1.0 - 2.0 ** (-5.0 - jnp.arange(RET_HEADS, dtype=f32))))
    dt_f = jnp.exp(jax.random.uniform(ks[14], (DEPTH, SSD_HEADS), f32, math.log(1e-3), math.log(1e-1)))
    dt_b = jnp.exp(jax.random.uniform(ks[15], (DEPTH, SSD_HEADS), f32, math.log(1e-3), math.log(1e-1)))
    return {
        "x": nrm(ks[0], (BATCH, SEQ, D_MODEL), 1.0),
        "c": nrm(ks[1], (BATCH, D_MODEL), 1.0),
        "ctx": nrm(ks[2], (BATCH, CTX_LEN, D_MODEL), 1.0),
        "c_ctx": nrm(ks[3], (D_MODEL,), 1.0),
        "w_mod": nrm(ks[4], (DEPTH, D_MODEL, 6 * D_MODEL), 0.5 * D_MODEL ** -0.5),
        "b_mod": nrm(ks[5], (DEPTH, 6 * D_MODEL), 0.02),
        "norm_mix": 1.0 + nrm(ks[6], (DEPTH, D_MODEL), 0.02),
        "norm_ffn": 1.0 + nrm(ks[7], (DEPTH, D_MODEL), 0.02),
        "w_in": nrm(ks[8], (DEPTH, D_MODEL, IN_COLS), D_MODEL ** -0.5),
        "w_out": nrm(ks[9], (DEPTH, MIX_WIDTH, D_MODEL), MIX_WIDTH ** -0.5),
        "ret_decay_f": ret_base + nrm(ks[10], (DEPTH, RET_HEADS), 0.1),
        "ret_decay_b": ret_base + nrm(ks[11], (DEPTH, RET_HEADS), 0.1),
        "ssd_conv_w": nrm(ks[12], (DEPTH, SSD_CONV, SSD_XBC), SSD_CONV ** -0.5),
        "ssd_conv_b": nrm(ks[13], (DEPTH, SSD_XBC), 0.02),
        "ssd_dt_bias_f": dt_f + jnp.log(-jnp.expm1(-dt_f)),
        "ssd_dt_bias_b": dt_b + jnp.log(-jnp.expm1(-dt_b)),
        "ssd_a_log_f": jnp.log(jax.random.uniform(ks[16], (DEPTH, SSD_HEADS), f32, 1.0, 16.0)),
        "ssd_a_log_b": jnp.log(jax.random.uniform(ks[17], (DEPTH, SSD_HEADS), f32, 1.0, 16.0)),
        "ssd_d": 1.0 + nrm(ks[18], (DEPTH, SSD_HEADS), 0.1),
        "ssd_norm": 1.0 + nrm(ks[19], (DEPTH, SSD_WIDTH), 0.02),
        "na_q_norm": 1.0 + nrm(ks[20], (DEPTH, HEAD_DIM), 0.02),
        "na_k_norm": 1.0 + nrm(ks[21], (DEPTH, HEAD_DIM), 0.02),
        "na_rpb": nrm(ks[22], (DEPTH, NA_HEADS, 2 * NA_WIN_ROWS - 1, 2 * NA_WIN_COLS - 1), 0.1),
        "router_w": nrm(ks[23], (D_MODEL, N_EXPERTS), D_MODEL ** -0.5),
        "router_b": nrm(ks[24], (N_EXPERTS,), 0.01),
        "w_gate": nrm(ks[25], (DEPTH, N_EXPERTS, D_MODEL, D_EXPERT), D_MODEL ** -0.5),
        "w_up": nrm(ks[26], (DEPTH, N_EXPERTS, D_MODEL, D_EXPERT), D_MODEL ** -0.5),
        "w_down": nrm(ks[27], (DEPTH, N_EXPERTS, D_EXPERT, D_MODEL), D_EXPERT ** -0.5),
    }


def reference(x, c, ctx, c_ctx, w_mod, b_mod, norm_mix, norm_ffn, w_in, w_out, ret_decay_f, ret_decay_b,
              ssd_conv_w, ssd_conv_b, ssd_dt_bias_f, ssd_dt_bias_b, ssd_a_log_f, ssd_a_log_b, ssd_d, ssd_norm,
              na_q_norm, na_k_norm, na_rpb, router_w, router_b, w_gate, w_up, w_down):
    n_lat = x.shape[1]
    rows = n_lat // GRID_W
    cos, sin = axial_rope_tables(n_lat)
    s_lat = jax.nn.silu(c)
    s_ctx = jax.nn.silu(c_ctx)
    h_lat, h_ctx = x, ctx
    for layer in range(DEPTH):
        need_ctx = layer < DEPTH - 1
        mod_l = s_lat @ w_mod[layer] + b_mod[layer]
        sh_a, sc_a, g_a, sh_f, sc_f, g_f = [m[:, None, :] for m in jnp.split(mod_l, 6, axis=-1)]
        n_mod_c = 6 * D_MODEL if need_ctx else 2 * D_MODEL
        mod_c = s_ctx @ w_mod[layer][:, :n_mod_c] + b_mod[layer][:n_mod_c]
        mc = jnp.split(mod_c, n_mod_c // D_MODEL)

        a_l = modulate(rmsnorm(h_lat, norm_mix[layer]), sh_a, sc_a)
        a_c = modulate(rmsnorm(h_ctx, norm_mix[layer]), mc[0], mc[1])
        get_l = _segment_getter(a_l, w_in[layer], True)
        get_c = _segment_getter(a_c, w_in[layer], need_ctx)
        ret_l, ret_c = retention_group(get_l, get_c, ret_decay_f[layer], ret_decay_b[layer], cos, sin, need_ctx)
        ssd_l, ssd_c = ssd_group(get_l, get_c, ssd_conv_w[layer], ssd_conv_b[layer], ssd_dt_bias_f[layer],
                                 ssd_dt_bias_b[layer], ssd_a_log_f[layer], ssd_a_log_b[layer], ssd_d[layer],
                                 ssd_norm[layer], need_ctx)
        na_l, na_c = na_group(get_l, get_c, na_q_norm[layer], na_k_norm[layer], na_rpb[layer], rows, need_ctx)

        mix_l = jnp.concatenate([ret_l, ssd_l, na_l.astype(ret_l.dtype)], axis=-1)
        h_lat = h_lat + g_a * (mix_l @ w_out[layer])
        f_l = modulate(rmsnorm(h_lat, norm_ffn[layer]), sh_f, sc_f)
        h_lat = h_lat + g_f * moe_ffn(f_l, router_w, router_b, w_gate[layer], w_up[layer], w_down[layer])
        if need_ctx:
            mix_c = jnp.concatenate([ret_c, ssd_c, na_c.astype(ret_c.dtype)], axis=-1)
            h_ctx = h_ctx + mc[2] * (mix_c @ w_out[layer])
            f_c = modulate(rmsnorm(h_ctx, norm_ffn[layer]), mc[3], mc[4])
            h_ctx = h_ctx + mc[5] * moe_ffn(f_c, router_w, router_b, w_gate[layer], w_up[layer], w_down[layer])
    return h_lat
```

```python
import functools
import math

import jax
import jax.numpy as jnp
import numpy as np
from jax import lax
from jax.experimental import pallas as pl
from jax.experimental.pallas import tpu as pltpu

F32 = jnp.float32
BF16 = jnp.bfloat16
HIGHEST = lax.Precision.HIGHEST

GRID_W = 64
HEAD_DIM = 64
SSD_GROUPS = 2
SSD_STATE = 64
SSD_CONV = 3
NA_WIN_ROWS = 8
NA_WIN_COLS = 16
NA_QBLOCK_COLS = 16
NA_KEY_COLS = NA_QBLOCK_COLS + NA_WIN_COLS
SCAN_CHUNK = 128
N_GROUPS = 4
EXPERTS_PER_GROUP = 4
N_EXPERTS = N_GROUPS * EXPERTS_PER_GROUP
TOP_K = 2
ROPE_BASE = 10000.0
EPS = 1e-6

VMEM_LIMIT_BYTES = 56 * 1024 * 1024
MOE_TILE = 512


def _cparams(*sem):
    return pltpu.CompilerParams(dimension_semantics=sem, vmem_limit_bytes=VMEM_LIMIT_BYTES)


def _silu(t):
    return t * (1.0 / (1.0 + jnp.exp(-t)))


def _mod_kernel(s_ref, w_ref, b_ref, o_ref):
    s = _silu(s_ref[...])
    o_ref[0] = jnp.dot(s, w_ref[0], precision=HIGHEST, preferred_element_type=F32) + b_ref[0]


def _modulation(cond, w_mod, b_mod):
    depth, d, n = w_mod.shape
    rows = cond.shape[0]
    tn = 1024
    return pl.pallas_call(
        _mod_kernel,
        grid=(depth, n // tn),
        in_specs=[
            pl.BlockSpec((rows, d), lambda l, j: (0, 0)),
            pl.BlockSpec((1, d, tn), lambda l, j: (l, 0, j)),
            pl.BlockSpec((1, 1, tn), lambda l, j: (l, 0, j)),
        ],
        out_specs=pl.BlockSpec((1, rows, tn), lambda l, j: (l, 0, j)),
        out_shape=jax.ShapeDtypeStruct((depth, rows, n), F32),
        compiler_params=_cparams("parallel", "parallel"),
    )(cond, w_mod, b_mod.reshape(depth, 1, n))


def _inproj_kernel(h_ref, mod_ref, nw_ref, w_ref, cos_ref, sin_ref, qkn_ref, hm_ref,
                   ret_ref, z_ref, xbc_ref, dt_ref, na_ref, *, dims, rope):
    rw, sw, nw_ = dims
    h = h_ref[...]
    a = h * lax.rsqrt(jnp.mean(h * h, axis=-1, keepdims=True) + EPS) * nw_ref[...]
    a = a * (1.0 + mod_ref[0, 1:2, :]) + mod_ref[0, 0:1, :]
    ab = a.astype(BF16)

    def proj(lo, hi):
        return jnp.dot(ab, w_ref[:, lo:hi], preferred_element_type=F32)

    k_scale = HEAD_DIM ** -0.5
    o = 0
    q = proj(o, o + rw)
    k = proj(o + rw, o + 2 * rw)
    if rope:
        cos, sin = cos_ref[...], sin_ref[...]
        q = q * cos + proj(o + 2 * rw, o + 3 * rw) * sin
        k = k * cos + proj(o + 3 * rw, o + 4 * rw) * sin
    ret_ref[:, 0:rw] = q
    ret_ref[:, rw:2 * rw] = k * k_scale
    ret_ref[:, 2 * rw:4 * rw] = proj(o + 4 * rw, o + 6 * rw)
    o += 6 * rw
    z_ref[...] = proj(o, o + sw)
    o += sw
    xbc_w = sw + 2 * SSD_GROUPS * SSD_STATE
    xbc_ref[...] = proj(o, o + xbc_w)
    o += xbc_w
    hm = hm_ref[...]
    for i in range(2):
        t = proj(o + i * nw_, o + (i + 1) * nw_)
        ms = jnp.dot(t * t, hm, precision=HIGHEST, preferred_element_type=F32)
        na_ref[:, i * nw_:(i + 1) * nw_] = t * lax.rsqrt(ms + EPS) * qkn_ref[i:i + 1, :]
    na_ref[:, 2 * nw_:3 * nw_] = proj(o + 2 * nw_, o + 3 * nw_)
    o += 3 * nw_
    dt_ref[...] = proj(o, o + 128)[:, 0:dt_ref.shape[1]]


def _inproj(h, mod, mod_row_of_tile, norm_w, w_all, cos, sin, qkn, hm, dims, tm, rope):
    t, d = h.shape
    rw, sw, nw_ = dims
    n_all = w_all.shape[1]
    n_dt = 2 * (sw // HEAD_DIM)
    xbc_w = sw + 2 * SSD_GROUPS * SSD_STATE
    cos_tiles = cos.shape[0] // tm
    row = lambda n: pl.BlockSpec((tm, n), lambda i: (i, 0))
    const = lambda shape: pl.BlockSpec(shape, lambda i: (0,) * len(shape))
    return pl.pallas_call(
        functools.partial(_inproj_kernel, dims=dims, rope=rope),
        grid=(t // tm,),
        in_specs=[
            row(d),
            pl.BlockSpec((1, 6, d), lambda i: (mod_row_of_tile(i), 0, 0)),
            const((1, d)),
            const((d, n_all)),
            pl.BlockSpec((tm, rw), lambda i: (i % cos_tiles, 0)),
            pl.BlockSpec((tm, rw), lambda i: (i % cos_tiles, 0)),
            const((2, nw_)),
            const((nw_, nw_)),
        ],
        out_specs=[row(4 * rw), row(sw), row(xbc_w), row(n_dt), row(3 * nw_)],
        out_shape=[
            jax.ShapeDtypeStruct((t, 4 * rw), F32),
            jax.ShapeDtypeStruct((t, sw), F32),
            jax.ShapeDtypeStruct((t, xbc_w), F32),
            jax.ShapeDtypeStruct((t, n_dt), F32),
            jax.ShapeDtypeStruct((t, 3 * nw_), F32),
        ],
        compiler_params=_cparams("parallel"),
    )(h, mod, norm_w, w_all, cos, sin, qkn, hm)


def _route(aff, sel):
    rows = [sel[e:e + 1, :] for e in range(N_EXPERTS)]
    arow = [aff[e:e + 1, :] for e in range(N_EXPERTS)]
    gscore = []
    for g in range(N_GROUPS):
        a, b, c, d = rows[4 * g:4 * g + 4]
        hi1, lo1 = jnp.maximum(a, b), jnp.minimum(a, b)
        hi2, lo2 = jnp.maximum(c, d), jnp.minimum(c, d)
        gscore.append(jnp.maximum(hi1, hi2) + jnp.maximum(jnp.minimum(hi1, hi2), jnp.maximum(lo1, lo2)))
    best = jnp.zeros_like(gscore[0], dtype=jnp.int32)
    top = gscore[0]
    for g in range(1, N_GROUPS):
        upd = gscore[g] > top
        best = jnp.where(upd, g, best)
        top = jnp.where(upd, gscore[g], top)
    sv, av = [], []
    for j in range(EXPERTS_PER_GROUP):
        s_j, a_j = rows[j], arow[j]
        for g in range(1, N_GROUPS):
            s_j = jnp.where(best == g, rows[4 * g + j], s_j)
            a_j = jnp.where(best == g, arow[4 * g + j], a_j)
        sv.append(s_j)
        av.append(a_j)
    picked = []
    for j in range(EXPERTS_PER_GROUP):
        rank = jnp.zeros_like(best)
        for i in range(EXPERTS_PER_GROUP):
            if i == j:
                continue
            ahead = (sv[i] >= sv[j]) if i < j else (sv[i] > sv[j])
            rank = rank + ahead.astype(jnp.int32)
        picked.append(rank < TOP_K)
    first = jnp.full_like(best, EXPERTS_PER_GROUP - 1)
    last = jnp.zeros_like(best)
    for j in range(EXPERTS_PER_GROUP - 1, -1, -1):
        first = jnp.where(picked[j], j, first)
    for j in range(EXPERTS_PER_GROUP):
        last = jnp.where(picked[j], j, last)
    a_first, a_last = av[0], av[0]
    for j in range(1, EXPERTS_PER_GROUP):
        a_first = jnp.where(first == j, av[j], a_first)
        a_last = jnp.where(last == j, av[j], a_last)
    tot = a_first + a_last
    idx = jnp.concatenate([best * EXPERTS_PER_GROUP + first, best * EXPERTS_PER_GROUP + last], axis=0)
    wts = jnp.concatenate([a_first / tot, a_last / tot], axis=0)
    return idx, wts


def _outproj_kernel(h_ref, ret_ref, ssd_ref, na_ref, wo_ref, mod_ref, nw_ref, rwt_ref, rb_ref,
                    hn_ref, f_ref, idx_ref, wts_ref):
    rw, sw = ret_ref.shape[1], ssd_ref.shape[1]
    acc = jnp.dot(ret_ref[...].astype(BF16), wo_ref[0:rw, :], preferred_element_type=F32)
    acc += jnp.dot(ssd_ref[...].astype(BF16), wo_ref[rw:rw + sw, :], preferred_element_type=F32)
    acc += jnp.dot(na_ref[...].astype(BF16), wo_ref[rw + sw:, :], preferred_element_type=F32)
    hn = h_ref[...] + mod_ref[0, 2:3, :] * acc
    hn_ref[...] = hn
    f = hn * lax.rsqrt(jnp.mean(hn * hn, axis=-1, keepdims=True) + EPS) * nw_ref[...]
    f = f * (1.0 + mod_ref[0, 4:5, :]) + mod_ref[0, 3:4, :]
    f_ref[...] = f.astype(BF16)
    logits = lax.dot_general(rwt_ref[...], f, (((1,), (1,)), ((), ())),
                             precision=HIGHEST, preferred_element_type=F32)
    aff = 1.0 / (1.0 + jnp.exp(-logits))
    idx, wts = _route(aff, aff + rb_ref[...])
    idx_ref[...] = idx
    wts_ref[...] = wts


def _outproj(h, ret_o, ssd_o, na_o, w_out, mod, mod_row_of_tile, norm_w, rw_t, rb, tm):
    t, d = h.shape
    row = lambda n: pl.BlockSpec((tm, n), lambda i: (i, 0))
    const = lambda shape: pl.BlockSpec(shape, lambda i: (0,) * len(shape))
    col = lambda: pl.BlockSpec((TOP_K, tm), lambda i: (0, i))
    return pl.pallas_call(
        _outproj_kernel,
        grid=(t // tm,),
        in_specs=[
            row(d), row(ret_o.shape[1]), row(ssd_o.shape[1]), row(na_o.shape[1]),
            const(w_out.shape),
            pl.BlockSpec((1, 6, d), lambda i: (mod_row_of_tile(i), 0, 0)),
            const((1, d)), const(rw_t.shape), const(rb.shape),
        ],
        out_specs=[row(d), row(d), col(), col()],
        out_shape=[
            jax.ShapeDtypeStruct((t, d), F32),
            jax.ShapeDtypeStruct((t, d), BF16),
            jax.ShapeDtypeStruct((TOP_K, t), jnp.int32),
            jax.ShapeDtypeStruct((TOP_K, t), F32),
        ],
        compiler_params=_cparams("parallel"),
    )(h, ret_o, ssd_o, na_o, w_out, mod, norm_w, rw_t, rb)


def _moe_kernel(te_ref, nt_ref, x_ref, wg_ref, wu_ref, wd_ref, y_ref, wg_s, wu_s, wd_s):
    i = pl.program_id(0)
    fresh = jnp.logical_or(i == 0, te_ref[i] != te_ref[jnp.maximum(i - 1, 0)])

    @pl.when(fresh)
    def _():
        wg_s[...] = wg_ref[0].astype(BF16)
        wu_s[...] = wu_ref[0].astype(BF16)
        wd_s[...] = wd_ref[0].astype(BF16)

    @pl.when(i < nt_ref[0])
    def _():
        x = x_ref[...]
        g = jnp.dot(x, wg_s[...], preferred_element_type=F32)
        u = jnp.dot(x, wu_s[...], preferred_element_type=F32)
        he = (_silu(g) * u).astype(BF16)
        y_ref[...] = jnp.dot(he, wd_s[...], preferred_element_type=F32)

    @pl.when(i >= nt_ref[0])
    def _():
        y_ref[...] = jnp.zeros_like(y_ref)


def _moe_grouped(x_sorted, tile_expert, n_tiles_used, w_gate, w_up, w_down):
    p, d = x_sorted.shape
    de = w_gate.shape[2]
    tm = MOE_TILE
    grid_spec = pltpu.PrefetchScalarGridSpec(
        num_scalar_prefetch=2,
        grid=(p // tm,),
        in_specs=[
            pl.BlockSpec((tm, d), lambda i, te, nt: (i, 0)),
            pl.BlockSpec((1, d, de), lambda i, te, nt: (te[i], 0, 0)),
            pl.BlockSpec((1, d, de), lambda i, te, nt: (te[i], 0, 0)),
            pl.BlockSpec((1, de, d), lambda i, te, nt: (te[i], 0, 0)),
        ],
        out_specs=pl.BlockSpec((tm, d), lambda i, te, nt: (i, 0)),
        scratch_shapes=[pltpu.VMEM((d, de), BF16), pltpu.VMEM((d, de), BF16), pltpu.VMEM((de, d), BF16)],
    )
    return pl.pallas_call(
        _moe_kernel,
        grid_spec=grid_spec,
        out_shape=jax.ShapeDtypeStruct((p, d), F32),
        compiler_params=_cparams("arbitrary"),
    )(tile_expert, n_tiles_used, x_sorted, w_gate, w_up, w_down)


def _combine_kernel(h_ref, y0_ref, y1_ref, w_ref, mod_ref, o_ref):
    w = w_ref[...]
    y = w[:, 0:1] * y0_ref[...] + w[:, 1:2] * y1_ref[...]
    o_ref[...] = h_ref[...] + mod_ref[0, 5:6, :] * y


def _combine(h, y0, y1, w_tok, mod, mod_row_of_tile, tm):
    t, d = h.shape
    row = lambda n: pl.BlockSpec((tm, n), lambda i: (i, 0))
    return pl.pallas_call(
        _combine_kernel,
        grid=(t // tm,),
        in_specs=[row(d), row(d), row(d), row(TOP_K),
                  pl.BlockSpec((1, 6, d), lambda i: (mod_row_of_tile(i), 0, 0))],
        out_specs=row(d),
        out_shape=jax.ShapeDtypeStruct((t, d), F32),
        compiler_params=_cparams("parallel"),
    )(h, y0, y1, w_tok, mod)


def _moe_block(h, f, idx, wts, mod, mod_row_of_tile, w_gate, w_up, w_down, tm):
    t, d = h.shape
    n_e = w_gate.shape[0]
    mt = MOE_TILE
    e_flat = idx.reshape(-1)
    onehot = (e_flat[:, None] == jnp.arange(n_e, dtype=jnp.int32)[None, :]).astype(jnp.int32)
    csum = jnp.cumsum(onehot, axis=0)
    counts = csum[-1]
    rank = jnp.sum(csum * onehot, axis=1) - 1
    padded = ((counts + mt - 1) // mt) * mt
    ends = jnp.cumsum(padded)
    starts = ends - padded
    pos = starts[e_flat] + rank
    p_total = ((TOP_K * t + mt - 1) // mt + n_e) * mt
    tok = jnp.tile(jnp.arange(t, dtype=jnp.int32), TOP_K)
    row_src = jnp.zeros((p_total,), jnp.int32).at[pos].set(tok)
    n_tiles = p_total // mt
    tile_expert = jnp.minimum(
        jnp.searchsorted(ends, jnp.arange(n_tiles, dtype=jnp.int32) * mt, side="right"), n_e - 1
    ).astype(jnp.int32)
    n_used = (ends[-1] // mt).astype(jnp.int32).reshape(1)
    x_sorted = jnp.take(f, row_src, axis=0)
    y_sorted = _moe_grouped(x_sorted, tile_expert, n_used, w_gate, w_up, w_down)
    y0 = jnp.take(y_sorted, pos[:t], axis=0)
    y1 = jnp.take(y_sorted, pos[t:], axis=0)
    return _combine(h, y0, y1, wts.T, mod, mod_row_of_tile, tm)


def _chunk_scan(q, k, v, a, s0):
    bsz, seq, g, n = q.shape
    r, p = v.shape[3], v.shape[4]
    nc = seq // SCAN_CHUNK
    qc = q.reshape(bsz, nc, SCAN_CHUNK, g, n)
    kc = k.reshape(bsz, nc, SCAN_CHUNK, g, n)
    vc = v.reshape(bsz, nc, SCAN_CHUNK, g, r, p)
    acum = jnp.cumsum(a.astype(F32).reshape(bsz, nc, SCAN_CHUNK, g, r), axis=2)
    tril = jnp.tril(jnp.ones((SCAN_CHUNK, SCAN_CHUNK), dtype=bool))[None, None, :, :, None, None]
    seg = acum[:, :, :, None] - acum[:, :, None, :]
    decay = jnp.exp(jnp.where(tril, seg, -jnp.inf))
    scores = jnp.einsum("bclgn,bcsgn->bclsg", qc, kc)
    y_diag = jnp.einsum("bclsgr,bcsgrp->bclgrp", scores[..., None] * decay, vc)
    w_end = jnp.exp(acum[:, :, -1:] - acum)
    states = jnp.einsum("bcsgn,bcsgrp->bcgrpn", kc, vc * w_end[..., None])
    chunk_decay = jnp.exp(acum[:, :, -1])
    if s0 is None:
        s0 = jnp.zeros((bsz, g, r, p, n), F32)

    def step(hs, inp):
        st, dc = inp
        return hs * dc[..., None, None] + st, hs

    s_final, h_prev = lax.scan(step, s0.astype(F32),
                               (jnp.moveaxis(states, 1, 0), jnp.moveaxis(chunk_decay, 1, 0)))
    h_prev = jnp.moveaxis(h_prev, 0, 1)
    y_off = jnp.einsum("bclgn,bcgrpn->bclgrp", qc, h_prev) * jnp.exp(acum)[..., None]
    return (y_diag + y_off).reshape(bsz, seq, g, r, p), s_final


def _flip(t):
    return None if t is None else jnp.flip(t, axis=1)


def _prefix_scan(q_c, k_c, v_c, a_c, q_l, k_l, v_l, a_l, reverse):
    if reverse:
        q_c, k_c, v_c, a_c = _flip(q_c), _flip(k_c), _flip(v_c), _flip(a_c)
        q_l, k_l, v_l, a_l = _flip(q_l), _flip(k_l), _flip(v_l), _flip(a_l)
    y_c, s_c = _chunk_scan(q_c, k_c, v_c, a_c, None)
    y_l, _ = _chunk_scan(q_l, k_l, v_l, a_l, s_c)
    if reverse:
        y_c, y_l = _flip(y_c), _flip(y_l)
    return y_c, y_l


def _head_groupnorm(y):
    mu = jnp.mean(y, axis=-1, keepdims=True)
    var = jnp.mean(jnp.square(y - mu), axis=-1, keepdims=True)
    return (y - mu) * lax.rsqrt(var + EPS)


def _retention_jnp(ret_l, ret_c, decay_f, decay_b):
    rw = ret_l.shape[-1] // 4
    nh = rw // HEAD_DIM
    heads = lambda t: t.reshape(t.shape[0], t.shape[1], nh, HEAD_DIM)
    part = lambda t, i: t[..., i * rw:(i + 1) * rw]
    q_l, k_l, v_l = heads(part(ret_l, 0)), heads(part(ret_l, 1)), heads(part(ret_l, 2))[:, :, :, None, :]
    q_c, k_c, v_c = heads(part(ret_c, 0)), heads(part(ret_c, 1)), heads(part(ret_c, 2))[:, :, :, None, :]
    y_l, y_c = 0.0, 0.0
    for param, reverse in ((decay_f, False), (decay_b, True)):
        log_gamma = -jnp.exp(param.astype(F32))[:, None]
        a_l = jnp.broadcast_to(log_gamma, q_l.shape[:2] + log_gamma.shape)
        a_c = jnp.broadcast_to(log_gamma, k_c.shape[:2] + log_gamma.shape)
        yc, yl = _prefix_scan(q_c, k_c, v_c, a_c, q_l, k_l, v_l, a_l, reverse)
        y_l, y_c = y_l + yl, y_c + yc

    def finish(y, g):
        y = _head_groupnorm(y[:, :, :, 0, :]).reshape(g.shape)
        return y * jax.nn.silu(g)

    return finish(y_l, part(ret_l, 3)), finish(y_c, part(ret_c, 3))


def _depthwise_conv(t, w, b):
    k = w.shape[0]
    y = lax.conv_general_dilated(
        t, w[:, None, :], window_strides=(1,), padding=[(k // 2, k // 2)],
        dimension_numbers=("NWC", "WIO", "NWC"), feature_group_count=t.shape[-1])
    return y + b


def _ssd_jnp(z_l, xbc_l, dt_l, z_c, xbc_c, dt_c, conv_w, conv_b, dtb_f, dtb_b, alog_f, alog_b, d_skip, norm_w):
    sw = z_l.shape[-1]
    nh = sw // HEAD_DIM
    xb = sw + SSD_GROUPS * SSD_STATE
    rep = nh // SSD_GROUPS

    def split(t):
        return t[..., :sw], t[..., sw:xb], t[..., xb:]

    x_l, b_l, c_l = split(jax.nn.silu(_depthwise_conv(xbc_l, conv_w, conv_b)))
    x_c, b_c, c_c = split(jax.nn.silu(_depthwise_conv(xbc_c, conv_w, conv_b)))
    grp = lambda t: t.reshape(t.shape[0], t.shape[1], SSD_GROUPS, SSD_STATE)
    y_l, y_c = 0.0, 0.0
    for d, (dtb, alog, reverse) in enumerate(((dtb_f, alog_f, False), (dtb_b, alog_b, True))):
        a_coef = -jnp.exp(alog.astype(F32))

        def prep(x, dt_raw):
            bsz, seq = x.shape[:2]
            dt = jax.nn.softplus(dt_raw[..., d * nh:(d + 1) * nh] + dtb)
            a = (dt * a_coef).reshape(bsz, seq, SSD_GROUPS, rep)
            v = x.reshape(bsz, seq, SSD_GROUPS, rep, HEAD_DIM) * dt.reshape(bsz, seq, SSD_GROUPS, rep)[..., None]
            return v, a

        v_l, a_l = prep(x_l, dt_l)
        v_c, a_c = prep(x_c, dt_c)
        yc, yl = _prefix_scan(grp(c_c), grp(b_c), v_c, a_c, grp(c_l), grp(b_l), v_l, a_l, reverse)
        y_l, y_c = y_l + yl, y_c + yc

    def finish(y, x, z):
        bsz, seq = x.shape[:2]
        xh = x.reshape(bsz, seq, nh, HEAD_DIM)
        y = y.reshape(bsz, seq, nh, HEAD_DIM) + d_skip[:, None] * xh
        y = y.reshape(bsz, seq, sw) * jax.nn.silu(z)
        return y * lax.rsqrt(jnp.mean(y * y, axis=-1, keepdims=True) + EPS) * norm_w

    return finish(y_l, x_l, z_l), finish(y_c, x_c, z_c)


def _na_column_tables():
    ncb = GRID_W // NA_QBLOCK_COLS
    c0 = np.arange(ncb) * NA_QBLOCK_COLS
    kc0 = np.clip(c0 - NA_WIN_COLS // 2, 0, GRID_W - NA_KEY_COLS)
    col_idx = kc0[:, None] + np.arange(NA_KEY_COLS)[None]
    qcol = c0[:, None] + np.arange(NA_QBLOCK_COLS)[None]
    cs = np.clip(qcol - NA_WIN_COLS // 2, 0, GRID_W - NA_WIN_COLS)[..., None]
    kcol = col_idx[:, None, :]
    mask = (kcol >= cs) & (kcol < cs + NA_WIN_COLS)
    dc = np.clip(kcol - qcol[..., None] + NA_WIN_COLS - 1, 0, 2 * NA_WIN_COLS - 2)
    return col_idx, dc, mask


def _na_jnp(na_l, na_c, rpb, rows):
    nw_ = na_l.shape[-1] // 3
    nh = nw_ // HEAD_DIM
    heads = lambda t: t.reshape(t.shape[0], t.shape[1], nh, HEAD_DIM)
    part = lambda t, i: t[..., i * nw_:(i + 1) * nw_]
    q, k, v = heads(part(na_l, 0)), heads(part(na_l, 1)), heads(part(na_l, 2))
    q_c, k_ctx, v_ctx = heads(part(na_c, 0)), heads(part(na_c, 1)), heads(part(na_c, 2))
    bsz, seq, _, d = q.shape
    wr = min(NA_WIN_ROWS, rows)
    col_idx, dc_idx, col_mask = _na_column_tables()
    ncb = col_idx.shape[0]
    scale = d ** -0.5
    qg = jnp.moveaxis(q.reshape(bsz, rows, ncb, NA_QBLOCK_COLS, nh, d), 1, 0)
    kg = k.reshape(bsz, rows, GRID_W, nh, d)[:, :, col_idx]
    vg = v.reshape(bsz, rows, GRID_W, nh, d)[:, :, col_idx]
    mask = jnp.asarray(col_mask)[:, :, None, :]
    n_lat_keys = wr * NA_KEY_COLS

    def one_row(args):
        q_r, r = args
        r0 = jnp.clip(r - wr // 2, 0, rows - wr)
        k_win = lax.dynamic_slice_in_dim(kg, r0, wr, axis=1)
        v_win = lax.dynamic_slice_in_dim(vg, r0, wr, axis=1)
        dr = r0 + jnp.arange(wr) - r + (NA_WIN_ROWS - 1)
        bias = rpb[:, dr[:, None, None, None], dc_idx[None]]
        bias = jnp.transpose(bias, (0, 2, 3, 1, 4)).astype(F32)
        s_lat = jnp.einsum("bjqhd,bijkhd->bhjqik", q_r, k_win).astype(F32) * scale + bias
        s_lat = jnp.where(mask, s_lat, -jnp.inf)
        s_ctx = jnp.einsum("bjqhd,bmhd->bhjqm", q_r, k_ctx).astype(F32) * scale
        s = jnp.concatenate([s_lat.reshape(bsz, nh, ncb, NA_QBLOCK_COLS, n_lat_keys), s_ctx], axis=-1)
        prob = jax.nn.softmax(s, axis=-1)
        p_lat = prob[..., :n_lat_keys].reshape(s_lat.shape)
        p_ctx = prob[..., n_lat_keys:]
        return (jnp.einsum("bhjqik,bijkhd->bjqhd", p_lat, v_win)
                + jnp.einsum("bhjqm,bmhd->bjqhd", p_ctx, v_ctx))

    o = lax.map(one_row, (qg, jnp.arange(rows)))
    out_l = jnp.moveaxis(o, 0, 1).reshape(bsz, seq, nh * d)
    s = jnp.einsum("bqhd,bkhd->bhqk", q_c, k_ctx) * scale
    prob = jax.nn.softmax(s, axis=-1)
    out_c = jnp.einsum("bhqk,bkhd->bqhd", prob, v_ctx).reshape(bsz, q_c.shape[1], nh * d)
    return out_l, out_c


def _rope_tables(n_tokens, n_heads):
    t = jnp.arange(n_tokens)
    pos = jnp.stack([t // GRID_W, t % GRID_W], axis=-1).astype(F32)
    n_freq = HEAD_DIM // 4
    inv = 1.0 / (ROPE_BASE ** (jnp.arange(n_freq, dtype=F32) / n_freq))
    ang = pos[:, :, None] * inv
    cos, sin = jnp.cos(ang), jnp.sin(ang)
    cos_h = jnp.concatenate([cos[:, 0], cos[:, 0], cos[:, 1], cos[:, 1]], axis=-1)
    sin_h = jnp.concatenate([-sin[:, 0], sin[:, 0], -sin[:, 1], sin[:, 1]], axis=-1)
    return jnp.tile(cos_h, (1, n_heads)), jnp.tile(sin_h, (1, n_heads))


def _swap_halves_perm(width):
    j = np.arange(width)
    nf = HEAD_DIM // 4
    return np.where((j % (2 * nf)) < nf, j + nf, j - nf)


def _pack_w_in(w_in, rw, sw, nw_):
    o = 0
    seg = {}
    for name, width in (("rq", rw), ("rk", rw), ("rv", rw), ("rg", rw), ("z", sw), ("x", sw),
                        ("b", SSD_GROUPS * SSD_STATE), ("c", SSD_GROUPS * SSD_STATE),
                        ("dt", 2 * (sw // HEAD_DIM)), ("nq", nw_), ("nk", nw_), ("nv", nw_)):
        seg[name] = w_in[:, o:o + width]
        o += width
    perm = _swap_halves_perm(rw)
    dt_pad = jnp.zeros((w_in.shape[0], 128 - seg["dt"].shape[1]), w_in.dtype)
    cols = [seg["rq"], seg["rk"], seg["rq"][:, perm], seg["rk"][:, perm], seg["rv"], seg["rg"],
            seg["z"], seg["x"], seg["b"], seg["c"], seg["nq"], seg["nk"], seg["nv"], seg["dt"], dt_pad]
    return jnp.concatenate(cols, axis=1).astype(BF16)


def kernel(x, c, ctx, c_ctx, w_mod, b_mod, norm_mix, norm_ffn, w_in, w_out, ret_decay_f, ret_decay_b,
           ssd_conv_w, ssd_conv_b, ssd_dt_bias_f, ssd_dt_bias_b, ssd_a_log_f, ssd_a_log_b, ssd_d, ssd_norm,
           na_q_norm, na_k_norm, na_rpb, router_w, router_b, w_gate, w_up, w_down):
    bsz, n_lat, d = x.shape
    n_ctx = ctx.shape[1]
    depth = w_mod.shape[0]
    rows = n_lat // GRID_W
    rw = ret_decay_f.shape[1] * HEAD_DIM
    sw = ssd_d.shape[1] * HEAD_DIM
    nw_ = na_rpb.shape[1] * HEAD_DIM
    dims = (rw, sw, nw_)

    n_cond = ((bsz + 1 + 7) // 8) * 8
    cond = jnp.zeros((n_cond, d), F32).at[:bsz].set(c).at[bsz].set(c_ctx)
    mod = _modulation(cond, w_mod, b_mod).reshape(depth, n_cond, 6, d)

    tm_l = 512 if n_lat % 512 == 0 else 256
    tm_c = n_ctx if n_ctx <= 512 else 256
    lat_row = lambda i: i // (n_lat // tm_l)
    ctx_row = lambda i: bsz

    cos, sin = _rope_tables(n_lat, rw // HEAD_DIM)
    ones_c = jnp.ones((tm_c, rw), F32)
    hm = jnp.asarray(np.kron(np.eye(nw_ // HEAD_DIM), np.full((HEAD_DIM, HEAD_DIM), 1.0 / HEAD_DIM)), F32)
    rw_t = router_w.T
    rb = router_b.reshape(-1, 1)

    h_lat = x.reshape(bsz * n_lat, d)
    h_ctx = ctx.reshape(bsz * n_ctx, d)
    for l in range(depth):
        need_ctx = l < depth - 1
        w_all = _pack_w_in(w_in[l], rw, sw, nw_)
        qkn = jnp.stack([jnp.tile(na_q_norm[l], nw_ // HEAD_DIM), jnp.tile(na_k_norm[l], nw_ // HEAD_DIM)])
        nmix = norm_mix[l].reshape(1, d)
        nffn = norm_ffn[l].reshape(1, d)
        wo = w_out[l].astype(BF16)

        ret_l, z_l, xbc_l, dt_l, na_l = _inproj(h_lat, mod[l], lat_row, nmix, w_all, cos, sin, qkn, hm,
                                                dims, tm_l, True)
        ret_c, z_c, xbc_c, dt_c, na_c = _inproj(h_ctx, mod[l], ctx_row, nmix, w_all, ones_c, ones_c, qkn, hm,
                                                dims, tm_c, False)
        b3 = lambda t, n: t.reshape(bsz, n, t.shape[-1])
        ro_l, ro_c = _retention_jnp(b3(ret_l, n_lat), b3(ret_c, n_ctx), ret_decay_f[l], ret_decay_b[l])
        so_l, so_c = _ssd_jnp(b3(z_l, n_lat), b3(xbc_l, n_lat), b3(dt_l, n_lat),
                              b3(z_c, n_ctx), b3(xbc_c, n_ctx), b3(dt_c, n_ctx),
                              ssd_conv_w[l], ssd_conv_b[l], ssd_dt_bias_f[l], ssd_dt_bias_b[l],
                              ssd_a_log_f[l], ssd_a_log_b[l], ssd_d[l], ssd_norm[l])
        no_l, no_c = _na_jnp(b3(na_l, n_lat), b3(na_c, n_ctx), na_rpb[l], rows)
        flat = lambda t: t.reshape(-1, t.shape[-1])

        h_lat, f_l, idx_l, wts_l = _outproj(h_lat, flat(ro_l), flat(so_l), flat(no_l), wo, mod[l], lat_row,
                                            nffn, rw_t, rb, tm_l)
        h_lat = _moe_block(h_lat, f_l, idx_l, wts_l, mod[l], lat_row, w_gate[l], w_up[l], w_down[l], tm_l)
        if need_ctx:
            h_ctx, f_c, idx_c, wts_c = _outproj(h_ctx, flat(ro_c), flat(so_c), flat(no_c), wo, mod[l], ctx_row,
                                                nffn, rw_t, rb, tm_c)
            h_ctx = _moe_block(h_ctx, f_c, idx_c, wts_c, mod[l], ctx_row, w_gate[l], w_up[l], w_down[l], tm_c)
    return h_lat.reshape(bsz, n_lat, d)
```

```python
import functools
import math

import jax
import jax.numpy as jnp
import numpy as np
from jax import lax
from jax.experimental import pallas as pl
from jax.experimental.pallas import tpu as pltpu

F32 = jnp.float32
BF16 = jnp.bfloat16
HIGHEST = lax.Precision.HIGHEST

GRID_W = 64
HEAD_DIM = 64
SSD_GROUPS = 2
SSD_STATE = 64
SSD_CONV = 3
NA_WIN_ROWS = 8
NA_WIN_COLS = 16
NA_QBLOCK_COLS = 16
NA_KEY_COLS = NA_QBLOCK_COLS + NA_WIN_COLS
SCAN_CHUNK = 128
N_GROUPS = 4
EXPERTS_PER_GROUP = 4
N_EXPERTS = N_GROUPS * EXPERTS_PER_GROUP
TOP_K = 2
ROPE_BASE = 10000.0
EPS = 1e-6

VMEM_LIMIT_BYTES = 56 * 1024 * 1024
MOE_TILE = 512


def _cparams(*sem):
    return pltpu.CompilerParams(dimension_semantics=sem, vmem_limit_bytes=VMEM_LIMIT_BYTES)


def _silu(t):
    return t * (1.0 / (1.0 + jnp.exp(-t)))


def _mod_kernel(s_ref, w_ref, b_ref, o_ref):
    s = _silu(s_ref[...])
    o_ref[0] = jnp.dot(s, w_ref[0], precision=HIGHEST, preferred_element_type=F32) + b_ref[0]


def _modulation(cond, w_mod, b_mod):
    depth, d, n = w_mod.shape
    rows = cond.shape[0]
    tn = 1024
    return pl.pallas_call(
        _mod_kernel,
        grid=(depth, n // tn),
        in_specs=[
            pl.BlockSpec((rows, d), lambda l, j: (0, 0)),
            pl.BlockSpec((1, d, tn), lambda l, j: (l, 0, j)),
            pl.BlockSpec((1, 1, tn), lambda l, j: (l, 0, j)),
        ],
        out_specs=pl.BlockSpec((1, rows, tn), lambda l, j: (l, 0, j)),
        out_shape=jax.ShapeDtypeStruct((depth, rows, n), F32),
        compiler_params=_cparams("parallel", "parallel"),
    )(cond, w_mod, b_mod.reshape(depth, 1, n))


def _inproj_kernel(h_ref, mod_ref, nw_ref, w_ref, cos_ref, sin_ref, qkn_ref, hm_ref,
                   ret_ref, z_ref, xbc_ref, dt_ref, na_ref, *, dims, rope):
    rw, sw, nw_ = dims
    h = h_ref[...]
    a = h * lax.rsqrt(jnp.mean(h * h, axis=-1, keepdims=True) + EPS) * nw_ref[...]
    a = a * (1.0 + mod_ref[0, 1:2, :]) + mod_ref[0, 0:1, :]
    ab = a.astype(BF16)

    def proj(lo, hi):
        return jnp.dot(ab, w_ref[:, lo:hi], preferred_element_type=F32)

    k_scale = HEAD_DIM ** -0.5
    o = 0
    q = proj(o, o + rw)
    k = proj(o + rw, o + 2 * rw)
    if rope:
        cos, sin = cos_ref[...], sin_ref[...]
        q = q * cos + proj(o + 2 * rw, o + 3 * rw) * sin
        k = k * cos + proj(o + 3 * rw, o + 4 * rw) * sin
    ret_ref[:, 0:rw] = q
    ret_ref[:, rw:2 * rw] = k * k_scale
    ret_ref[:, 2 * rw:4 * rw] = proj(o + 4 * rw, o + 6 * rw)
    o += 6 * rw
    z_ref[...] = proj(o, o + sw)
    o += sw
    xbc_w = sw + 2 * SSD_GROUPS * SSD_STATE
    xbc_ref[...] = proj(o, o + xbc_w)
    o += xbc_w
    hm = hm_ref[...]
    for i in range(2):
        t = proj(o + i * nw_, o + (i + 1) * nw_)
        ms = jnp.dot(t * t, hm, precision=HIGHEST, preferred_element_type=F32)
        na_ref[:, i * nw_:(i + 1) * nw_] = (t * lax.rsqrt(ms + EPS) * qkn_ref[i:i + 1, :]).astype(BF16)
    na_ref[:, 2 * nw_:3 * nw_] = proj(o + 2 * nw_, o + 3 * nw_).astype(BF16)
    o += 3 * nw_
    dt_ref[...] = proj(o, o + 128)[:, 0:dt_ref.shape[1]]


def _inproj(h, mod, mod_row_of_tile, norm_w, w_all, cos, sin, qkn, hm, dims, tm, rope):
    t, d = h.shape
    rw, sw, nw_ = dims
    n_all = w_all.shape[1]
    n_dt = 2 * (sw // HEAD_DIM)
    xbc_w = sw + 2 * SSD_GROUPS * SSD_STATE
    cos_tiles = cos.shape[0] // tm
    row = lambda n: pl.BlockSpec((tm, n), lambda i: (i, 0))
    const = lambda shape: pl.BlockSpec(shape, lambda i: (0,) * len(shape))
    return pl.pallas_call(
        functools.partial(_inproj_kernel, dims=dims, rope=rope),
        grid=(t // tm,),
        in_specs=[
            row(d),
            pl.BlockSpec((1, 6, d), lambda i: (mod_row_of_tile(i), 0, 0)),
            const((1, d)),
            const((d, n_all)),
            pl.BlockSpec((tm, rw), lambda i: (i % cos_tiles, 0)),
            pl.BlockSpec((tm, rw), lambda i: (i % cos_tiles, 0)),
            const((2, nw_)),
            const((nw_, nw_)),
        ],
        out_specs=[row(4 * rw), row(sw), row(xbc_w), row(n_dt), row(3 * nw_)],
        out_shape=[
            jax.ShapeDtypeStruct((t, 4 * rw), F32),
            jax.ShapeDtypeStruct((t, sw), F32),
            jax.ShapeDtypeStruct((t, xbc_w), F32),
            jax.ShapeDtypeStruct((t, n_dt), F32),
            jax.ShapeDtypeStruct((t, 3 * nw_), BF16),
        ],
        compiler_params=_cparams("parallel"),
    )(h, mod, norm_w, w_all, cos, sin, qkn, hm)


def _route(aff, sel):
    rows = [sel[e:e + 1, :] for e in range(N_EXPERTS)]
    arow = [aff[e:e + 1, :] for e in range(N_EXPERTS)]
    gscore = []
    for g in range(N_GROUPS):
        a, b, c, d = rows[4 * g:4 * g + 4]
        hi1, lo1 = jnp.maximum(a, b), jnp.minimum(a, b)
        hi2, lo2 = jnp.maximum(c, d), jnp.minimum(c, d)
        gscore.append(jnp.maximum(hi1, hi2) + jnp.maximum(jnp.minimum(hi1, hi2), jnp.maximum(lo1, lo2)))
    best = jnp.zeros_like(gscore[0], dtype=jnp.int32)
    top = gscore[0]
    for g in range(1, N_GROUPS):
        upd = gscore[g] > top
        best = jnp.where(upd, g, best)
        top = jnp.where(upd, gscore[g], top)
    sv, av = [], []
    for j in range(EXPERTS_PER_GROUP):
        s_j, a_j = rows[j], arow[j]
        for g in range(1, N_GROUPS):
            s_j = jnp.where(best == g, rows[4 * g + j], s_j)
            a_j = jnp.where(best == g, arow[4 * g + j], a_j)
        sv.append(s_j)
        av.append(a_j)
    picked = []
    for j in range(EXPERTS_PER_GROUP):
        rank = jnp.zeros_like(best)
        for i in range(EXPERTS_PER_GROUP):
            if i == j:
                continue
            ahead = (sv[i] >= sv[j]) if i < j else (sv[i] > sv[j])
            rank = rank + ahead.astype(jnp.int32)
        picked.append(rank < TOP_K)
    first = jnp.full_like(best, EXPERTS_PER_GROUP - 1)
    last = jnp.zeros_like(best)
    for j in range(EXPERTS_PER_GROUP - 1, -1, -1):
        first = jnp.where(picked[j], j, first)
    for j in range(EXPERTS_PER_GROUP):
        last = jnp.where(picked[j], j, last)
    a_first, a_last = av[0], av[0]
    for j in range(1, EXPERTS_PER_GROUP):
        a_first = jnp.where(first == j, av[j], a_first)
        a_last = jnp.where(last == j, av[j], a_last)
    tot = a_first + a_last
    idx = jnp.concatenate([best * EXPERTS_PER_GROUP + first, best * EXPERTS_PER_GROUP + last], axis=0)
    wts = jnp.concatenate([a_first / tot, a_last / tot], axis=0)
    return idx, wts


def _outproj_kernel(h_ref, ret_ref, ssd_ref, na_ref, wo_ref, mod_ref, nw_ref, rwt_ref, rb_ref,
                    hn_ref, f_ref, idx_ref, wts_ref):
    rw, sw = ret_ref.shape[1], ssd_ref.shape[1]
    acc = jnp.dot(ret_ref[...].astype(BF16), wo_ref[0:rw, :], preferred_element_type=F32)
    acc += jnp.dot(ssd_ref[...].astype(BF16), wo_ref[rw:rw + sw, :], preferred_element_type=F32)
    acc += jnp.dot(na_ref[...].astype(BF16), wo_ref[rw + sw:, :], preferred_element_type=F32)
    hn = h_ref[...] + mod_ref[0, 2:3, :] * acc
    hn_ref[...] = hn
    f = hn * lax.rsqrt(jnp.mean(hn * hn, axis=-1, keepdims=True) + EPS) * nw_ref[...]
    f = f * (1.0 + mod_ref[0, 4:5, :]) + mod_ref[0, 3:4, :]
    f_ref[...] = f.astype(BF16)
    logits = lax.dot_general(rwt_ref[...], f, (((1,), (1,)), ((), ())),
                             precision=HIGHEST, preferred_element_type=F32)
    aff = 1.0 / (1.0 + jnp.exp(-logits))
    idx, wts = _route(aff, aff + rb_ref[...])
    idx_ref[...] = idx
    wts_ref[...] = wts


def _outproj(h, ret_o, ssd_o, na_o, w_out, mod, mod_row_of_tile, norm_w, rw_t, rb, tm):
    t, d = h.shape
    row = lambda n: pl.BlockSpec((tm, n), lambda i: (i, 0))
    const = lambda shape: pl.BlockSpec(shape, lambda i: (0,) * len(shape))
    col = lambda: pl.BlockSpec((TOP_K, tm), lambda i: (0, i))
    return pl.pallas_call(
        _outproj_kernel,
        grid=(t // tm,),
        in_specs=[
            row(d), row(ret_o.shape[1]), row(ssd_o.shape[1]), row(na_o.shape[1]),
            const(w_out.shape),
            pl.BlockSpec((1, 6, d), lambda i: (mod_row_of_tile(i), 0, 0)),
            const((1, d)), const(rw_t.shape), const(rb.shape),
        ],
        out_specs=[row(d), row(d), col(), col()],
        out_shape=[
            jax.ShapeDtypeStruct((t, d), F32),
            jax.ShapeDtypeStruct((t, d), BF16),
            jax.ShapeDtypeStruct((TOP_K, t), jnp.int32),
            jax.ShapeDtypeStruct((TOP_K, t), F32),
        ],
        compiler_params=_cparams("parallel"),
    )(h, ret_o, ssd_o, na_o, w_out, mod, norm_w, rw_t, rb)


def _moe_kernel(te_ref, nt_ref, x_ref, wg_ref, wu_ref, wd_ref, y_ref, wg_s, wu_s, wd_s):
    i = pl.program_id(0)
    fresh = jnp.logical_or(i == 0, te_ref[i] != te_ref[jnp.maximum(i - 1, 0)])

    @pl.when(fresh)
    def _():
        wg_s[...] = wg_ref[0].astype(BF16)
        wu_s[...] = wu_ref[0].astype(BF16)
        wd_s[...] = wd_ref[0].astype(BF16)

    @pl.when(i < nt_ref[0])
    def _():
        x = x_ref[...]
        g = jnp.dot(x, wg_s[...], preferred_element_type=F32)
        u = jnp.dot(x, wu_s[...], preferred_element_type=F32)
        he = (_silu(g) * u).astype(BF16)
        y_ref[...] = jnp.dot(he, wd_s[...], preferred_element_type=F32)

    @pl.when(i >= nt_ref[0])
    def _():
        y_ref[...] = jnp.zeros_like(y_ref)


def _moe_grouped(x_sorted, tile_expert, n_tiles_used, w_gate, w_up, w_down):
    p, d = x_sorted.shape
    de = w_gate.shape[2]
    tm = MOE_TILE
    grid_spec = pltpu.PrefetchScalarGridSpec(
        num_scalar_prefetch=2,
        grid=(p // tm,),
        in_specs=[
            pl.BlockSpec((tm, d), lambda i, te, nt: (i, 0)),
            pl.BlockSpec((1, d, de), lambda i, te, nt: (te[i], 0, 0)),
            pl.BlockSpec((1, d, de), lambda i, te, nt: (te[i], 0, 0)),
            pl.BlockSpec((1, de, d), lambda i, te, nt: (te[i], 0, 0)),
        ],
        out_specs=pl.BlockSpec((tm, d), lambda i, te, nt: (i, 0)),
        scratch_shapes=[pltpu.VMEM((d, de), BF16), pltpu.VMEM((d, de), BF16), pltpu.VMEM((de, d), BF16)],
    )
    return pl.pallas_call(
        _moe_kernel,
        grid_spec=grid_spec,
        out_shape=jax.ShapeDtypeStruct((p, d), F32),
        compiler_params=_cparams("arbitrary"),
    )(tile_expert, n_tiles_used, x_sorted, w_gate, w_up, w_down)


def _combine_kernel(h_ref, y0_ref, y1_ref, w_ref, mod_ref, o_ref):
    w = w_ref[...]
    y = w[:, 0:1] * y0_ref[...] + w[:, 1:2] * y1_ref[...]
    o_ref[...] = h_ref[...] + mod_ref[0, 5:6, :] * y


def _combine(h, y0, y1, w_tok, mod, mod_row_of_tile, tm):
    t, d = h.shape
    row = lambda n: pl.BlockSpec((tm, n), lambda i: (i, 0))
    return pl.pallas_call(
        _combine_kernel,
        grid=(t // tm,),
        in_specs=[row(d), row(d), row(d), row(TOP_K),
                  pl.BlockSpec((1, 6, d), lambda i: (mod_row_of_tile(i), 0, 0))],
        out_specs=row(d),
        out_shape=jax.ShapeDtypeStruct((t, d), F32),
        compiler_params=_cparams("parallel"),
    )(h, y0, y1, w_tok, mod)


def _moe_block(h, f, idx, wts, mod, mod_row_of_tile, w_gate, w_up, w_down, tm):
    t, d = h.shape
    n_e = w_gate.shape[0]
    mt = MOE_TILE
    e_flat = idx.reshape(-1)
    onehot = (e_flat[:, None] == jnp.arange(n_e, dtype=jnp.int32)[None, :]).astype(jnp.int32)
    csum = jnp.cumsum(onehot, axis=0)
    counts = csum[-1]
    rank = jnp.sum(csum * onehot, axis=1) - 1
    padded = ((counts + mt - 1) // mt) * mt
    ends = jnp.cumsum(padded)
    starts = ends - padded
    pos = starts[e_flat] + rank
    p_total = ((TOP_K * t + mt - 1) // mt + n_e) * mt
    tok = jnp.tile(jnp.arange(t, dtype=jnp.int32), TOP_K)
    row_src = jnp.zeros((p_total,), jnp.int32).at[pos].set(tok)
    n_tiles = p_total // mt
    tile_expert = jnp.minimum(
        jnp.searchsorted(ends, jnp.arange(n_tiles, dtype=jnp.int32) * mt, side="right"), n_e - 1
    ).astype(jnp.int32)
    n_used = (ends[-1] // mt).astype(jnp.int32).reshape(1)
    x_sorted = jnp.take(f, row_src, axis=0)
    y_sorted = _moe_grouped(x_sorted, tile_expert, n_used, w_gate, w_up, w_down)
    y0 = jnp.take(y_sorted, pos[:t], axis=0)
    y1 = jnp.take(y_sorted, pos[t:], axis=0)
    return _combine(h, y0, y1, wts.T, mod, mod_row_of_tile, tm)


NA_QROWS = 4


def _softmax_pv(s_parts, v):
    m = s_parts[0].max(axis=-1, keepdims=True)
    for s in s_parts[1:]:
        m = jnp.maximum(m, s.max(axis=-1, keepdims=True))
    p_parts = [jnp.exp(s - m) for s in s_parts]
    den = p_parts[0].sum(axis=-1, keepdims=True)
    for p in p_parts[1:]:
        den = den + p.sum(axis=-1, keepdims=True)
    p = p_parts[0] if len(p_parts) == 1 else jnp.concatenate(p_parts, axis=1)
    return jnp.dot(p.astype(BF16), v, preferred_element_type=F32) / den


def _na_kernel(q_ref, k0_ref, k1_ref, k2_ref, v0_ref, v1_ref, v2_ref, kc_ref, vc_ref, bias_ref, o_ref):
    nh = q_ref.shape[2] // HEAD_DIM
    n_lat_keys = 3 * k0_ref.shape[1]
    for h in range(nh):
        sl = slice(h * HEAD_DIM, (h + 1) * HEAD_DIM)
        q = q_ref[0, :, sl] * (HEAD_DIM ** -0.5)
        k = jnp.concatenate([k0_ref[0, :, sl], k1_ref[0, :, sl], k2_ref[0, :, sl], kc_ref[0, :, sl]], axis=0)
        v = jnp.concatenate([v0_ref[0, :, sl], v1_ref[0, :, sl], v2_ref[0, :, sl], vc_ref[0, :, sl]], axis=0)
        s = lax.dot_general(q, k, (((1,), (1,)), ((), ())), preferred_element_type=F32)
        o = _softmax_pv([s[:, :n_lat_keys] + bias_ref[0, h], s[:, n_lat_keys:]], v)
        o_ref[0, :, sl] = o.astype(o_ref.dtype)


def _na_bias_tables(rpb, rows):
    qb, kb = NA_QROWS, 3 * NA_QROWS
    nb = rows // qb
    tabs = []
    for rbq, wb in ((0, 0), (1, 0), (nb - 1, nb - 3)):
        qr = np.arange(qb)[:, None, None, None]
        c = np.arange(GRID_W)[None, :, None, None]
        j = np.arange(kb)[None, None, :, None]
        kc = np.arange(GRID_W)[None, None, None, :]
        r, kr = qb * rbq + qr, qb * wb + j
        r0 = np.clip(r - NA_WIN_ROWS // 2, 0, rows - NA_WIN_ROWS)
        cs = np.clip(c - NA_WIN_COLS // 2, 0, GRID_W - NA_WIN_COLS)
        valid = (kr >= r0) & (kr < r0 + NA_WIN_ROWS) & (kc >= cs) & (kc < cs + NA_WIN_COLS)
        dr = np.clip(kr - r + NA_WIN_ROWS - 1, 0, 2 * NA_WIN_ROWS - 2)
        dc = np.clip(kc - c + NA_WIN_COLS - 1, 0, 2 * NA_WIN_COLS - 2)
        shape = (qb, GRID_W, kb, GRID_W)
        flat = lambda t: np.broadcast_to(t, shape).reshape(qb * GRID_W, kb * GRID_W)
        tabs.append(jnp.where(flat(valid), rpb[:, flat(dr), flat(dc)].astype(F32), -jnp.inf))
    return jnp.stack(tabs)


def _na_attention(na_l, na_c, bias):
    bsz, n_lat, w3 = na_l.shape
    n_ctx = na_c.shape[1]
    w = w3 // 3
    tq = NA_QROWS * GRID_W
    nb = n_lat // tq
    assert n_lat % tq == 0 and nb >= 3
    win = lambda j: jnp.clip(j - 1, 0, nb - 3)
    kv = lambda i, col: pl.BlockSpec((1, tq, w), lambda b, j: (b, win(j) + i, col))
    typ = lambda j: jnp.where(j == 0, 0, jnp.where(j == nb - 1, 2, 1))
    return pl.pallas_call(
        _na_kernel,
        grid=(bsz, nb),
        in_specs=[
            pl.BlockSpec((1, tq, w), lambda b, j: (b, j, 0)),
            kv(0, 1), kv(1, 1), kv(2, 1), kv(0, 2), kv(1, 2), kv(2, 2),
            pl.BlockSpec((1, n_ctx, w), lambda b, j: (b, 0, 1)),
            pl.BlockSpec((1, n_ctx, w), lambda b, j: (b, 0, 2)),
            pl.BlockSpec((1,) + bias.shape[1:], lambda b, j: (typ(j), 0, 0, 0)),
        ],
        out_specs=pl.BlockSpec((1, tq, w), lambda b, j: (b, j, 0)),
        out_shape=jax.ShapeDtypeStruct((bsz, n_lat, w), BF16),
        compiler_params=_cparams("parallel", "arbitrary"),
    )(na_l, na_l, na_l, na_l, na_l, na_l, na_l, na_c, na_c, bias)


def _ctx_attn_kernel(q_ref, k_ref, v_ref, o_ref):
    nh = q_ref.shape[2] // HEAD_DIM
    for h in range(nh):
        sl = slice(h * HEAD_DIM, (h + 1) * HEAD_DIM)
        q = q_ref[0, :, sl] * (HEAD_DIM ** -0.5)
        s = lax.dot_general(q, k_ref[0, :, sl], (((1,), (1,)), ((), ())), preferred_element_type=F32)
        o_ref[0, :, sl] = _softmax_pv([s], v_ref[0, :, sl]).astype(o_ref.dtype)


def _ctx_attention(na_c):
    bsz, n_ctx, w3 = na_c.shape
    w = w3 // 3
    part = lambda col: pl.BlockSpec((1, n_ctx, w), lambda b: (b, 0, col))
    return pl.pallas_call(
        _ctx_attn_kernel,
        grid=(bsz,),
        in_specs=[part(0), part(1), part(2)],
        out_specs=pl.BlockSpec((1, n_ctx, w), lambda b: (b, 0, 0)),
        out_shape=jax.ShapeDtypeStruct((bsz, n_ctx, w), BF16),
        compiler_params=_cparams("parallel"),
    )(na_c, na_c, na_c)


def _chunk_scan(q, k, v, a, s0):
    bsz, seq, g, n = q.shape
    r, p = v.shape[3], v.shape[4]
    nc = seq // SCAN_CHUNK
    qc = q.reshape(bsz, nc, SCAN_CHUNK, g, n)
    kc = k.reshape(bsz, nc, SCAN_CHUNK, g, n)
    vc = v.reshape(bsz, nc, SCAN_CHUNK, g, r, p)
    acum = jnp.cumsum(a.astype(F32).reshape(bsz, nc, SCAN_CHUNK, g, r), axis=2)
    tril = jnp.tril(jnp.ones((SCAN_CHUNK, SCAN_CHUNK), dtype=bool))[None, None, :, :, None, None]
    seg = acum[:, :, :, None] - acum[:, :, None, :]
    decay = jnp.exp(jnp.where(tril, seg, -jnp.inf))
    scores = jnp.einsum("bclgn,bcsgn->bclsg", qc, kc)
    y_diag = jnp.einsum("bclsgr,bcsgrp->bclgrp", scores[..., None] * decay, vc)
    w_end = jnp.exp(acum[:, :, -1:] - acum)
    states = jnp.einsum("bcsgn,bcsgrp->bcgrpn", kc, vc * w_end[..., None])
    chunk_decay = jnp.exp(acum[:, :, -1])
    if s0 is None:
        s0 = jnp.zeros((bsz, g, r, p, n), F32)

    def step(hs, inp):
        st, dc = inp
        return hs * dc[..., None, None] + st, hs

    s_final, h_prev = lax.scan(step, s0.astype(F32),
                               (jnp.moveaxis(states, 1, 0), jnp.moveaxis(chunk_decay, 1, 0)))
    h_prev = jnp.moveaxis(h_prev, 0, 1)
    y_off = jnp.einsum("bclgn,bcgrpn->bclgrp", qc, h_prev) * jnp.exp(acum)[..., None]
    return (y_diag + y_off).reshape(bsz, seq, g, r, p), s_final


def _flip(t):
    return None if t is None else jnp.flip(t, axis=1)


def _prefix_scan(q_c, k_c, v_c, a_c, q_l, k_l, v_l, a_l, reverse):
    if reverse:
        q_c, k_c, v_c, a_c = _flip(q_c), _flip(k_c), _flip(v_c), _flip(a_c)
        q_l, k_l, v_l, a_l = _flip(q_l), _flip(k_l), _flip(v_l), _flip(a_l)
    y_c, s_c = _chunk_scan(q_c, k_c, v_c, a_c, None)
    y_l, _ = _chunk_scan(q_l, k_l, v_l, a_l, s_c)
    if reverse:
        y_c, y_l = _flip(y_c), _flip(y_l)
    return y_c, y_l


def _head_groupnorm(y):
    mu = jnp.mean(y, axis=-1, keepdims=True)
    var = jnp.mean(jnp.square(y - mu), axis=-1, keepdims=True)
    return (y - mu) * lax.rsqrt(var + EPS)


def _retention_jnp(ret_l, ret_c, decay_f, decay_b):
    rw = ret_l.shape[-1] // 4
    nh = rw // HEAD_DIM
    heads = lambda t: t.reshape(t.shape[0], t.shape[1], nh, HEAD_DIM)
    part = lambda t, i: t[..., i * rw:(i + 1) * rw]
    q_l, k_l, v_l = heads(part(ret_l, 0)), heads(part(ret_l, 1)), heads(part(ret_l, 2))[:, :, :, None, :]
    q_c, k_c, v_c = heads(part(ret_c, 0)), heads(part(ret_c, 1)), heads(part(ret_c, 2))[:, :, :, None, :]
    y_l, y_c = 0.0, 0.0
    for param, reverse in ((decay_f, False), (decay_b, True)):
        log_gamma = -jnp.exp(param.astype(F32))[:, None]
        a_l = jnp.broadcast_to(log_gamma, q_l.shape[:2] + log_gamma.shape)
        a_c = jnp.broadcast_to(log_gamma, k_c.shape[:2] + log_gamma.shape)
        yc, yl = _prefix_scan(q_c, k_c, v_c, a_c, q_l, k_l, v_l, a_l, reverse)
        y_l, y_c = y_l + yl, y_c + yc

    def finish(y, g):
        y = _head_groupnorm(y[:, :, :, 0, :]).reshape(g.shape)
        return y * jax.nn.silu(g)

    return finish(y_l, part(ret_l, 3)), finish(y_c, part(ret_c, 3))


def _depthwise_conv(t, w, b):
    k = w.shape[0]
    y = lax.conv_general_dilated(
        t, w[:, None, :], window_strides=(1,), padding=[(k // 2, k // 2)],
        dimension_numbers=("NWC", "WIO", "NWC"), feature_group_count=t.shape[-1])
    return y + b


def _ssd_jnp(z_l, xbc_l, dt_l, z_c, xbc_c, dt_c, conv_w, conv_b, dtb_f, dtb_b, alog_f, alog_b, d_skip, norm_w):
    sw = z_l.shape[-1]
    nh = sw // HEAD_DIM
    xb = sw + SSD_GROUPS * SSD_STATE
    rep = nh // SSD_GROUPS

    def split(t):
        return t[..., :sw], t[..., sw:xb], t[..., xb:]

    x_l, b_l, c_l = split(jax.nn.silu(_depthwise_conv(xbc_l, conv_w, conv_b)))
    x_c, b_c, c_c = split(jax.nn.silu(_depthwise_conv(xbc_c, conv_w, conv_b)))
    grp = lambda t: t.reshape(t.shape[0], t.shape[1], SSD_GROUPS, SSD_STATE)
    y_l, y_c = 0.0, 0.0
    for d, (dtb, alog, reverse) in enumerate(((dtb_f, alog_f, False), (dtb_b, alog_b, True))):
        a_coef = -jnp.exp(alog.astype(F32))

        def prep(x, dt_raw):
            bsz, seq = x.shape[:2]
            dt = jax.nn.softplus(dt_raw[..., d * nh:(d + 1) * nh] + dtb)
            a = (dt * a_coef).reshape(bsz, seq, SSD_GROUPS, rep)
            v = x.reshape(bsz, seq, SSD_GROUPS, rep, HEAD_DIM) * dt.reshape(bsz, seq, SSD_GROUPS, rep)[..., None]
            return v, a

        v_l, a_l = prep(x_l, dt_l)
        v_c, a_c = prep(x_c, dt_c)
        yc, yl = _prefix_scan(grp(c_c), grp(b_c), v_c, a_c, grp(c_l), grp(b_l), v_l, a_l, reverse)
        y_l, y_c = y_l + yl, y_c + yc

    def finish(y, x, z):
        bsz, seq = x.shape[:2]
        xh = x.reshape(bsz, seq, nh, HEAD_DIM)
        y = y.reshape(bsz, seq, nh, HEAD_DIM) + d_skip[:, None] * xh
        y = y.reshape(bsz, seq, sw) * jax.nn.silu(z)
        return y * lax.rsqrt(jnp.mean(y * y, axis=-1, keepdims=True) + EPS) * norm_w

    return finish(y_l, x_l, z_l), finish(y_c, x_c, z_c)


def _na_column_tables():
    ncb = GRID_W // NA_QBLOCK_COLS
    c0 = np.arange(ncb) * NA_QBLOCK_COLS
    kc0 = np.clip(c0 - NA_WIN_COLS // 2, 0, GRID_W - NA_KEY_COLS)
    col_idx = kc0[:, None] + np.arange(NA_KEY_COLS)[None]
    qcol = c0[:, None] + np.arange(NA_QBLOCK_COLS)[None]
    cs = np.clip(qcol - NA_WIN_COLS // 2, 0, GRID_W - NA_WIN_COLS)[..., None]
    kcol = col_idx[:, None, :]
    mask = (kcol >= cs) & (kcol < cs + NA_WIN_COLS)
    dc = np.clip(kcol - qcol[..., None] + NA_WIN_COLS - 1, 0, 2 * NA_WIN_COLS - 2)
    return col_idx, dc, mask


def _na_jnp(na_l, na_c, rpb, rows):
    nw_ = na_l.shape[-1] // 3
    nh = nw_ // HEAD_DIM
    heads = lambda t: t.reshape(t.shape[0], t.shape[1], nh, HEAD_DIM)
    part = lambda t, i: t[..., i * nw_:(i + 1) * nw_]
    q, k, v = heads(part(na_l, 0)), heads(part(na_l, 1)), heads(part(na_l, 2))
    q_c, k_ctx, v_ctx = heads(part(na_c, 0)), heads(part(na_c, 1)), heads(part(na_c, 2))
    bsz, seq, _, d = q.shape
    wr = min(NA_WIN_ROWS, rows)
    col_idx, dc_idx, col_mask = _na_column_tables()
    ncb = col_idx.shape[0]
    scale = d ** -0.5
    qg = jnp.moveaxis(q.reshape(bsz, rows, ncb, NA_QBLOCK_COLS, nh, d), 1, 0)
    kg = k.reshape(bsz, rows, GRID_W, nh, d)[:, :, col_idx]
    vg = v.reshape(bsz, rows, GRID_W, nh, d)[:, :, col_idx]
    mask = jnp.asarray(col_mask)[:, :, None, :]
    n_lat_keys = wr * NA_KEY_COLS

    def one_row(args):
        q_r, r = args
        r0 = jnp.clip(r - wr // 2, 0, rows - wr)
        k_win = lax.dynamic_slice_in_dim(kg, r0, wr, axis=1)
        v_win = lax.dynamic_slice_in_dim(vg, r0, wr, axis=1)
        dr = r0 + jnp.arange(wr) - r + (NA_WIN_ROWS - 1)
        bias = rpb[:, dr[:, None, None, None], dc_idx[None]]
        bias = jnp.transpose(bias, (0, 2, 3, 1, 4)).astype(F32)
        s_lat = jnp.einsum("bjqhd,bijkhd->bhjqik", q_r, k_win).astype(F32) * scale + bias
        s_lat = jnp.where(mask, s_lat, -jnp.inf)
        s_ctx = jnp.einsum("bjqhd,bmhd->bhjqm", q_r, k_ctx).astype(F32) * scale
        s = jnp.concatenate([s_lat.reshape(bsz, nh, ncb, NA_QBLOCK_COLS, n_lat_keys), s_ctx], axis=-1)
        prob = jax.nn.softmax(s, axis=-1)
        p_lat = prob[..., :n_lat_keys].reshape(s_lat.shape)
        p_ctx = prob[..., n_lat_keys:]
        return (jnp.einsum("bhjqik,bijkhd->bjqhd", p_lat, v_win)
                + jnp.einsum("bhjqm,bmhd->bjqhd", p_ctx, v_ctx))

    o = lax.map(one_row, (qg, jnp.arange(rows)))
    out_l = jnp.moveaxis(o, 0, 1).reshape(bsz, seq, nh * d)
    s = jnp.einsum("bqhd,bkhd->bhqk", q_c, k_ctx) * scale
    prob = jax.nn.softmax(s, axis=-1)
    out_c = jnp.einsum("bhqk,bkhd->bqhd", prob, v_ctx).reshape(bsz, q_c.shape[1], nh * d)
    return out_l, out_c


def _rope_tables(n_tokens, n_heads):
    t = jnp.arange(n_tokens)
    pos = jnp.stack([t // GRID_W, t % GRID_W], axis=-1).astype(F32)
    n_freq = HEAD_DIM // 4
    inv = 1.0 / (ROPE_BASE ** (jnp.arange(n_freq, dtype=F32) / n_freq))
    ang = pos[:, :, None] * inv
    cos, sin = jnp.cos(ang), jnp.sin(ang)
    cos_h = jnp.concatenate([cos[:, 0], cos[:, 0], cos[:, 1], cos[:, 1]], axis=-1)
    sin_h = jnp.concatenate([-sin[:, 0], sin[:, 0], -sin[:, 1], sin[:, 1]], axis=-1)
    return jnp.tile(cos_h, (1, n_heads)), jnp.tile(sin_h, (1, n_heads))


def _swap_halves_perm(width):
    j = np.arange(width)
    nf = HEAD_DIM // 4
    return np.where((j % (2 * nf)) < nf, j + nf, j - nf)


def _pack_w_in(w_in, rw, sw, nw_):
    o = 0
    seg = {}
    for name, width in (("rq", rw), ("rk", rw), ("rv", rw), ("rg", rw), ("z", sw), ("x", sw),
                        ("b", SSD_GROUPS * SSD_STATE), ("c", SSD_GROUPS * SSD_STATE),
                        ("dt", 2 * (sw // HEAD_DIM)), ("nq", nw_), ("nk", nw_), ("nv", nw_)):
        seg[name] = w_in[:, o:o + width]
        o += width
    perm = _swap_halves_perm(rw)
    dt_pad = jnp.zeros((w_in.shape[0], 128 - seg["dt"].shape[1]), w_in.dtype)
    cols = [seg["rq"], seg["rk"], seg["rq"][:, perm], seg["rk"][:, perm], seg["rv"], seg["rg"],
            seg["z"], seg["x"], seg["b"], seg["c"], seg["nq"], seg["nk"], seg["nv"], seg["dt"], dt_pad]
    return jnp.concatenate(cols, axis=1).astype(BF16)


def kernel(x, c, ctx, c_ctx, w_mod, b_mod, norm_mix, norm_ffn, w_in, w_out, ret_decay_f, ret_decay_b,
           ssd_conv_w, ssd_conv_b, ssd_dt_bias_f, ssd_dt_bias_b, ssd_a_log_f, ssd_a_log_b, ssd_d, ssd_norm,
           na_q_norm, na_k_norm, na_rpb, router_w, router_b, w_gate, w_up, w_down):
    bsz, n_lat, d = x.shape
    n_ctx = ctx.shape[1]
    depth = w_mod.shape[0]
    rows = n_lat // GRID_W
    rw = ret_decay_f.shape[1] * HEAD_DIM
    sw = ssd_d.shape[1] * HEAD_DIM
    nw_ = na_rpb.shape[1] * HEAD_DIM
    dims = (rw, sw, nw_)

    n_cond = ((bsz + 1 + 7) // 8) * 8
    cond = jnp.zeros((n_cond, d), F32).at[:bsz].set(c).at[bsz].set(c_ctx)
    mod = _modulation(cond, w_mod, b_mod).reshape(depth, n_cond, 6, d)

    tm_l = 512 if n_lat % 512 == 0 else 256
    tm_c = n_ctx if n_ctx <= 512 else 256
    lat_row = lambda i: i // (n_lat // tm_l)
    ctx_row = lambda i: bsz

    cos, sin = _rope_tables(n_lat, rw // HEAD_DIM)
    ones_c = jnp.ones((tm_c, rw), F32)
    hm = jnp.asarray(np.kron(np.eye(nw_ // HEAD_DIM), np.full((HEAD_DIM, HEAD_DIM), 1.0 / HEAD_DIM)), F32)
    rw_t = router_w.T
    rb = router_b.reshape(-1, 1)

    h_lat = x.reshape(bsz * n_lat, d)
    h_ctx = ctx.reshape(bsz * n_ctx, d)
    for l in range(depth):
        need_ctx = l < depth - 1
        w_all = _pack_w_in(w_in[l], rw, sw, nw_)
        qkn = jnp.stack([jnp.tile(na_q_norm[l], nw_ // HEAD_DIM), jnp.tile(na_k_norm[l], nw_ // HEAD_DIM)])
        nmix = norm_mix[l].reshape(1, d)
        nffn = norm_ffn[l].reshape(1, d)
        wo = w_out[l].astype(BF16)

        ret_l, z_l, xbc_l, dt_l, na_l = _inproj(h_lat, mod[l], lat_row, nmix, w_all, cos, sin, qkn, hm,
                                                dims, tm_l, True)
        ret_c, z_c, xbc_c, dt_c, na_c = _inproj(h_ctx, mod[l], ctx_row, nmix, w_all, ones_c, ones_c, qkn, hm,
                                                dims, tm_c, False)
        b3 = lambda t, n: t.reshape(bsz, n, t.shape[-1])
        ro_l, ro_c = _retention_jnp(b3(ret_l, n_lat), b3(ret_c, n_ctx), ret_decay_f[l], ret_decay_b[l])
        so_l, so_c = _ssd_jnp(b3(z_l, n_lat), b3(xbc_l, n_lat), b3(dt_l, n_lat),
                              b3(z_c, n_ctx), b3(xbc_c, n_ctx), b3(dt_c, n_ctx),
                              ssd_conv_w[l], ssd_conv_b[l], ssd_dt_bias_f[l], ssd_dt_bias_b[l],
                              ssd_a_log_f[l], ssd_a_log_b[l], ssd_d[l], ssd_norm[l])
        no_l = _na_attention(b3(na_l, n_lat), b3(na_c, n_ctx), _na_bias_tables(na_rpb[l], rows))
        no_c = _ctx_attention(b3(na_c, n_ctx)) if need_ctx else None
        flat = lambda t: t.reshape(-1, t.shape[-1])

        h_lat, f_l, idx_l, wts_l = _outproj(h_lat, flat(ro_l), flat(so_l), flat(no_l), wo, mod[l], lat_row,
                                            nffn, rw_t, rb, tm_l)
        h_lat = _moe_block(h_lat, f_l, idx_l, wts_l, mod[l], lat_row, w_gate[l], w_up[l], w_down[l], tm_l)
        if need_ctx:
            h_ctx, f_c, idx_c, wts_c = _outproj(h_ctx, flat(ro_c), flat(so_c), flat(no_c), wo, mod[l], ctx_row,
                                                nffn, rw_t, rb, tm_c)
            h_ctx = _moe_block(h_ctx, f_c, idx_c, wts_c, mod[l], ctx_row, w_gate[l], w_up[l], w_down[l], tm_c)
    return h_lat.reshape(bsz, n_lat, d)
```

```python
import functools

import jax
import jax.numpy as jnp
import numpy as np
from jax import lax
from jax.experimental import pallas as pl
from jax.experimental.pallas import tpu as pltpu

F32 = jnp.float32
BF16 = jnp.bfloat16
HIGHEST = lax.Precision.HIGHEST

GRID_W = 64
HEAD_DIM = 64
SSD_GROUPS = 2
SSD_STATE = 64
NA_WIN_ROWS = 8
NA_WIN_COLS = 16
N_GROUPS = 4
EXPERTS_PER_GROUP = 4
N_EXPERTS = N_GROUPS * EXPERTS_PER_GROUP
TOP_K = 2
ROPE_BASE = 10000.0
EPS = 1e-6

VMEM_LIMIT_BYTES = 56 * 1024 * 1024
TB = 256
NA_QROWS = TB // GRID_W
MOE_TILE = 512
SUBLANES = 8


def _cparams(*sem):
    return pltpu.CompilerParams(dimension_semantics=sem, vmem_limit_bytes=VMEM_LIMIT_BYTES)


def _silu(t):
    return t * (1.0 / (1.0 + jnp.exp(-t)))


def _nt_dot(a, b, **kw):
    return lax.dot_general(a, b, (((1,), (1,)), ((), ())), preferred_element_type=F32, **kw)


def _tn_dot(a, b):
    return lax.dot_general(a, b, (((0,), (0,)), ((), ())), preferred_element_type=F32)


def _mod_kernel(s_ref, w_ref, b_ref, o_ref):
    s = _silu(s_ref[...])
    o_ref[0] = jnp.dot(s, w_ref[0], precision=HIGHEST, preferred_element_type=F32) + b_ref[0]


def _modulation(cond, w_mod, b_mod):
    depth, d, n = w_mod.shape
    rows = cond.shape[0]
    tn = 1024
    return pl.pallas_call(
        _mod_kernel,
        grid=(depth, n // tn),
        in_specs=[
            pl.BlockSpec((rows, d), lambda l, j: (0, 0)),
            pl.BlockSpec((1, d, tn), lambda l, j: (l, 0, j)),
            pl.BlockSpec((1, 1, tn), lambda l, j: (l, 0, j)),
        ],
        out_specs=pl.BlockSpec((1, rows, tn), lambda l, j: (l, 0, j)),
        out_shape=jax.ShapeDtypeStruct((depth, rows, n), F32),
        compiler_params=_cparams("parallel", "parallel"),
        name="modulation",
    )(cond, w_mod, b_mod.reshape(depth, 1, n))


def _inproj_kernel(h_ref, mod_ref, nw_ref, w_ref, cos_ref, sin_ref, qkn_ref, hm_ref,
                   ret_ref, z_ref, xbc_ref, dt_ref, na_ref, *, dims):
    rw, sw, nw_ = dims
    h = h_ref[0]
    a = h * lax.rsqrt(jnp.mean(h * h, axis=-1, keepdims=True) + EPS) * nw_ref[...]
    a = a * (1.0 + mod_ref[0, 1:2, :]) + mod_ref[0, 0:1, :]
    ab = a.astype(BF16)

    def proj(lo, hi):
        return jnp.dot(ab, w_ref[:, lo:hi], preferred_element_type=F32)

    cos, sin = cos_ref[...], sin_ref[...]
    o = 0
    ret_ref[0, :, 0:rw] = proj(o, o + rw) * cos + proj(o + 2 * rw, o + 3 * rw) * sin
    ret_ref[0, :, rw:2 * rw] = (proj(o + rw, o + 2 * rw) * cos + proj(o + 3 * rw, o + 4 * rw) * sin) * (HEAD_DIM ** -0.5)
    ret_ref[0, :, 2 * rw:4 * rw] = proj(o + 4 * rw, o + 6 * rw)
    o += 6 * rw
    z_ref[0] = proj(o, o + sw)
    o += sw
    xbc_w = sw + 2 * SSD_GROUPS * SSD_STATE
    xbc_ref[0] = proj(o, o + xbc_w)
    o += xbc_w
    hm = hm_ref[...]
    for i in range(2):
        t = proj(o + i * nw_, o + (i + 1) * nw_)
        ms = jnp.dot(t * t, hm, precision=HIGHEST, preferred_element_type=F32)
        na_ref[0, :, i * nw_:(i + 1) * nw_] = (t * lax.rsqrt(ms + EPS) * qkn_ref[i:i + 1, :]).astype(BF16)
    na_ref[0, :, 2 * nw_:3 * nw_] = proj(o + 2 * nw_, o + 3 * nw_).astype(BF16)
    o += 3 * nw_
    dt_ref[0] = proj(o, o + 128)[:, 0:dt_ref.shape[2]]


def _inproj(h, mod, mod_row, norm_w, w_all, cos, sin, qkn, hm, dims):
    bsz, ltot, d = h.shape
    rw, sw, nw_ = dims
    n_dt = 2 * (sw // HEAD_DIM)
    xbc_w = sw + 2 * SSD_GROUPS * SSD_STATE
    row = lambda n: pl.BlockSpec((1, TB, n), lambda b, j: (b, j, 0))
    const = lambda shape: pl.BlockSpec(shape, lambda b, j: (0,) * len(shape))
    widths = (4 * rw, sw, xbc_w, n_dt, 3 * nw_)
    dtypes = (F32, F32, F32, F32, BF16)
    return pl.pallas_call(
        functools.partial(_inproj_kernel, dims=dims),
        grid=(bsz, ltot // TB),
        in_specs=[
            row(d),
            pl.BlockSpec((1, 6, d), lambda b, j: (mod_row(b, j), 0, 0)),
            const((1, d)),
            const(w_all.shape),
            pl.BlockSpec((TB, rw), lambda b, j: (j, 0)),
            pl.BlockSpec((TB, rw), lambda b, j: (j, 0)),
            const((2, nw_)),
            const((nw_, nw_)),
        ],
        out_specs=[row(n) for n in widths],
        out_shape=[jax.ShapeDtypeStruct((bsz, ltot, n), dt) for n, dt in zip(widths, dtypes)],
        compiler_params=_cparams("parallel", "parallel"),
        name="inproj",
    )(h, mod, norm_w, w_all, cos, sin, qkn, hm)


def _scan_chunk_maps(nb, nbl):
    fwd = lambda s: jnp.where(s < nb, (nbl + s) % nb, 2 * nb - 1 - s)
    out = lambda s: jnp.where(s < nb, nb - 1, 2 * nb - 1 - s)
    return fwd, out


def _ret_kernel(q_ref, k_ref, v_ref, g_ref, dec_ref, o_ref, sf_all, sf, sb, dmat, rd, *, nb, nbl):
    s = pl.program_id(1)
    c = TB
    nh = k_ref.shape[2] // HEAD_DIM
    log_f = [-jnp.exp(dec_ref[h]) for h in range(nh)]
    log_b = [-jnp.exp(dec_ref[nh + h]) for h in range(nh)]

    @pl.when(s == 0)
    def _init():
        sf[...] = jnp.zeros_like(sf)
        sb[...] = jnp.zeros_like(sb)
        delta = (lax.broadcasted_iota(jnp.int32, (c, c), 0) - lax.broadcasted_iota(jnp.int32, (c, c), 1)).astype(F32)
        pos = lax.broadcasted_iota(jnp.int32, (c, HEAD_DIM), 0).astype(F32)
        for h in range(nh):
            dmat[h] = (jnp.exp(jnp.where(delta >= 0, log_f[h] * delta, -jnp.inf))
                       + jnp.exp(jnp.where(delta <= 0, -log_b[h] * delta, -jnp.inf)))
            rd[4 * h + 0] = jnp.exp(log_f[h] * (pos + 1.0))
            rd[4 * h + 1] = jnp.exp(log_b[h] * (c - pos))
            rd[4 * h + 2] = jnp.exp(log_f[h] * (c - 1.0 - pos))
            rd[4 * h + 3] = jnp.exp(log_b[h] * pos)

    @pl.when(s < nb)
    def _state_sweep():
        ci = (nbl + s) % nb
        for h in range(nh):
            sl = slice(h * HEAD_DIM, (h + 1) * HEAD_DIM)
            k, v = k_ref[0, :, sl], v_ref[0, :, sl]
            sf_all[ci, h] = sf[h]
            sf[h] = sf[h] * jnp.exp(log_f[h] * c) + _tn_dot((k * rd[4 * h + 2]).astype(BF16), v.astype(BF16))

    @pl.when(s >= nb)
    def _output_sweep():
        ci = 2 * nb - 1 - s
        for h in range(nh):
            sl = slice(h * HEAD_DIM, (h + 1) * HEAD_DIM)
            q, k, v = q_ref[0, :, sl], k_ref[0, :, sl], v_ref[0, :, sl]
            qb, vb = q.astype(BF16), v.astype(BF16)
            p = _nt_dot(qb, k.astype(BF16)) * dmat[h]
            y = jnp.dot(p.astype(BF16), vb, preferred_element_type=F32)
            y += jnp.dot(qb, sf_all[ci, h].astype(BF16), preferred_element_type=F32) * rd[4 * h + 0]
            y += jnp.dot(qb, sb[h].astype(BF16), preferred_element_type=F32) * rd[4 * h + 1]
            sb[h] = sb[h] * jnp.exp(log_b[h] * c) + _tn_dot((k * rd[4 * h + 3]).astype(BF16), vb)
            mu = jnp.mean(y, axis=-1, keepdims=True)
            yc = y - mu
            yn = yc * lax.rsqrt(jnp.mean(yc * yc, axis=-1, keepdims=True) + EPS)
            o_ref[0, :, sl] = (yn * _silu(g_ref[0, :, sl])).astype(o_ref.dtype)


def _retention(ret, decay, nbl):
    bsz, ltot, w4 = ret.shape
    w = w4 // 4
    nh = w // HEAD_DIM
    nb = ltot // TB
    fwd, out = _scan_chunk_maps(nb, nbl)
    return pl.pallas_call(
        functools.partial(_ret_kernel, nb=nb, nbl=nbl),
        grid=(bsz, 2 * nb),
        in_specs=[
            pl.BlockSpec((1, TB, w), lambda b, s: (b, out(s), 0)),
            pl.BlockSpec((1, TB, w), lambda b, s: (b, fwd(s), 1)),
            pl.BlockSpec((1, TB, w), lambda b, s: (b, fwd(s), 2)),
            pl.BlockSpec((1, TB, w), lambda b, s: (b, out(s), 3)),
            pl.BlockSpec(decay.shape, lambda b, s: (0, 0, 0)),
        ],
        out_specs=pl.BlockSpec((1, TB, w), lambda b, s: (b, out(s), 0)),
        out_shape=jax.ShapeDtypeStruct((bsz, ltot, w), BF16),
        scratch_shapes=[
            pltpu.VMEM((nb, nh, HEAD_DIM, HEAD_DIM), F32),
            pltpu.VMEM((nh, HEAD_DIM, HEAD_DIM), F32),
            pltpu.VMEM((nh, HEAD_DIM, HEAD_DIM), F32),
            pltpu.VMEM((nh, TB, TB), F32),
            pltpu.VMEM((4 * nh, TB, HEAD_DIM), F32),
        ],
        compiler_params=_cparams("parallel", "arbitrary"),
        name="retention",
    )(ret, ret, ret, ret, decay)


def _softplus(t):
    return jnp.maximum(t, 0.0) + jnp.log1p(jnp.exp(-jnp.abs(t)))


def _ssd_kernel(z_ref, x_ref, xp_ref, xn_ref, dt_ref, cw_ref, cb_ref, dtb_ref, alog_ref, dtbt_ref, alogt_ref,
                dsk_ref, nw_ref, o_ref, sf_all, sf, sb, tri, ybuf, *, nb, nbl):
    s = pl.program_id(1)
    c = TB
    sw = z_ref.shape[2]
    nh = sw // HEAD_DIM
    rep = nh // SSD_GROUPS
    gs = SSD_GROUPS * SSD_STATE
    ci = jnp.where(s < nb, (nbl + s) % nb, 2 * nb - 1 - s)

    @pl.when(s == 0)
    def _init():
        sf[...] = jnp.zeros_like(sf)
        sb[...] = jnp.zeros_like(sb)
        li = lax.broadcasted_iota(jnp.int32, (c, c), 0)
        si = lax.broadcasted_iota(jnp.int32, (c, c), 1)
        tri[0] = (si <= li).astype(F32)
        tri[1] = (si >= li).astype(F32)

    x = x_ref[0]
    has_prev = jnp.where(jnp.logical_or(ci == 0, ci == nbl), 0.0, 1.0)
    has_next = jnp.where(jnp.logical_or(ci == nbl - 1, ci == nb - 1), 0.0, 1.0)
    rows = lax.broadcasted_iota(jnp.int32, x.shape, 0)
    x_dn = jnp.where(rows == 0, xp_ref[0, SUBLANES - 1:SUBLANES, :] * has_prev, pltpu.roll(x, 1, 0))
    x_up = jnp.where(rows == c - 1, xn_ref[0, 0:1, :] * has_next, pltpu.roll(x, c - 1, 0))
    xc = _silu(cw_ref[0:1, :] * x_dn + cw_ref[1:2, :] * x + cw_ref[2:3, :] * x_up + cb_ref[...])
    xs, bm, cm = xc[:, :sw], xc[:, sw:sw + gs], xc[:, sw + gs:]

    dtr = dt_ref[0]
    eye = (lax.broadcasted_iota(jnp.int32, (2 * nh, 2 * nh), 0)
           == lax.broadcasted_iota(jnp.int32, (2 * nh, 2 * nh), 1)).astype(F32)
    dt = _softplus(dtr + dtb_ref[...])
    a = dt * -jnp.exp(alog_ref[...])
    dt_t = _softplus(_nt_dot(eye, dtr, precision=HIGHEST) + dtbt_ref[...])
    a_t = dt_t * -jnp.exp(alogt_ref[...])
    acf = jnp.dot(tri[0], a, precision=HIGHEST, preferred_element_type=F32)
    acb = jnp.dot(tri[1], a, precision=HIGHEST, preferred_element_type=F32)
    acf_t = jnp.dot(a_t, tri[1], precision=HIGHEST, preferred_element_type=F32)
    acb_t = jnp.dot(a_t, tri[0], precision=HIGHEST, preferred_element_type=F32)

    @pl.when(s < nb)
    def _state_sweep():
        for h in range(nh):
            g = h // rep
            col = acf[:, h:h + 1]
            tot = acf[c - 1:c, h:h + 1]
            bw = bm[:, g * SSD_STATE:(g + 1) * SSD_STATE] * (dt[:, h:h + 1] * jnp.exp(tot - col))
            sf_all[ci, h] = sf[h]
            sf[h] = sf[h] * jnp.exp(tot) + _tn_dot(bw.astype(BF16), xs[:, h * HEAD_DIM:(h + 1) * HEAD_DIM].astype(BF16))

    @pl.when(s >= nb)
    def _output_sweep():
        li = lax.broadcasted_iota(jnp.int32, (c, c), 0)
        si = lax.broadcasted_iota(jnp.int32, (c, c), 1)
        for g in range(SSD_GROUPS):
            cg = cm[:, g * SSD_STATE:(g + 1) * SSD_STATE].astype(BF16)
            bg = bm[:, g * SSD_STATE:(g + 1) * SSD_STATE]
            scores = _nt_dot(cg, bg.astype(BF16))
            for r in range(rep):
                h = g * rep + r
                hb = nh + h
                xh = xs[:, h * HEAD_DIM:(h + 1) * HEAD_DIM].astype(BF16)
                col_f, col_b = acf[:, h:h + 1], acb[:, hb:hb + 1]
                d_f = jnp.exp(jnp.where(li >= si, col_f - acf_t[h:h + 1, :], -jnp.inf)) * dt_t[h:h + 1, :]
                d_b = jnp.exp(jnp.where(si >= li, col_b - acb_t[hb:hb + 1, :], -jnp.inf)) * dt_t[hb:hb + 1, :]
                y = jnp.dot((scores * (d_f + d_b)).astype(BF16), xh, preferred_element_type=F32)
                y += jnp.dot(cg, sf_all[ci, h].astype(BF16), preferred_element_type=F32) * jnp.exp(col_f)
                y += jnp.dot(cg, sb[h].astype(BF16), preferred_element_type=F32) * jnp.exp(col_b)
                tot = acb[0:1, hb:hb + 1]
                bw = bg * (dt[:, hb:hb + 1] * jnp.exp(tot - col_b))
                sb[h] = sb[h] * jnp.exp(tot) + _tn_dot(bw.astype(BF16), xh)
                ybuf[:, h * HEAD_DIM:(h + 1) * HEAD_DIM] = y
        y = (ybuf[...] + dsk_ref[...] * xs) * _silu(z_ref[0])
        o_ref[0] = (y * lax.rsqrt(jnp.mean(y * y, axis=-1, keepdims=True) + EPS) * nw_ref[...]).astype(o_ref.dtype)


def _ssd(z, xbc, dt, conv_w, conv_b, dtb, alog, d_skip, norm_w, nbl):
    bsz, ltot, sw = z.shape
    nh = sw // HEAD_DIM
    xw = xbc.shape[2]
    nb = ltot // TB
    per = TB // SUBLANES
    fwd, out = _scan_chunk_maps(nb, nbl)
    const = lambda t: pl.BlockSpec(t.shape, lambda b, s: (0,) * t.ndim)
    args = (conv_w, conv_b.reshape(1, xw), dtb.reshape(1, 2 * nh), alog.reshape(1, 2 * nh),
            dtb.reshape(2 * nh, 1), alog.reshape(2 * nh, 1), jnp.repeat(d_skip, HEAD_DIM).reshape(1, sw),
            norm_w.reshape(1, sw))
    return pl.pallas_call(
        functools.partial(_ssd_kernel, nb=nb, nbl=nbl),
        grid=(bsz, 2 * nb),
        in_specs=[
            pl.BlockSpec((1, TB, sw), lambda b, s: (b, out(s), 0)),
            pl.BlockSpec((1, TB, xw), lambda b, s: (b, fwd(s), 0)),
            pl.BlockSpec((1, SUBLANES, xw), lambda b, s: (b, jnp.maximum(fwd(s) * per - 1, 0), 0)),
            pl.BlockSpec((1, SUBLANES, xw), lambda b, s: (b, jnp.minimum((fwd(s) + 1) * per, nb * per - 1), 0)),
            pl.BlockSpec((1, TB, 2 * nh), lambda b, s: (b, fwd(s), 0)),
        ] + [const(t) for t in args],
        out_specs=pl.BlockSpec((1, TB, sw), lambda b, s: (b, out(s), 0)),
        out_shape=jax.ShapeDtypeStruct((bsz, ltot, sw), BF16),
        scratch_shapes=[
            pltpu.VMEM((nb, nh, SSD_STATE, HEAD_DIM), F32),
            pltpu.VMEM((nh, SSD_STATE, HEAD_DIM), F32),
            pltpu.VMEM((nh, SSD_STATE, HEAD_DIM), F32),
            pltpu.VMEM((2, TB, TB), F32),
            pltpu.VMEM((TB, sw), F32),
        ],
        compiler_params=_cparams("parallel", "arbitrary"),
        name="ssd",
    )(z, xbc, xbc, xbc, dt, *args)


def _softmax_pv(s_parts, v):
    m = s_parts[0].max(axis=-1, keepdims=True)
    for s in s_parts[1:]:
        m = jnp.maximum(m, s.max(axis=-1, keepdims=True))
    p_parts = [jnp.exp(s - m) for s in s_parts]
    den = p_parts[0].sum(axis=-1, keepdims=True)
    for p in p_parts[1:]:
        den = den + p.sum(axis=-1, keepdims=True)
    p = p_parts[0] if len(p_parts) == 1 else jnp.concatenate(p_parts, axis=1)
    return jnp.dot(p.astype(BF16), v, preferred_element_type=F32) / den


def _na_kernel(q_ref, k0_ref, k1_ref, k2_ref, v0_ref, v1_ref, v2_ref, kc_ref, vc_ref, bias_ref, o_ref, *, nbl):
    nh = q_ref.shape[2] // HEAD_DIM
    n_lat_keys = 3 * k0_ref.shape[1]
    j = pl.program_id(1)

    @pl.when(j < nbl)
    def _latent():
        for h in range(nh):
            sl = slice(h * HEAD_DIM, (h + 1) * HEAD_DIM)
            q = q_ref[0, :, sl] * (HEAD_DIM ** -0.5)
            k = jnp.concatenate([k0_ref[0, :, sl], k1_ref[0, :, sl], k2_ref[0, :, sl], kc_ref[0, :, sl]], axis=0)
            v = jnp.concatenate([v0_ref[0, :, sl], v1_ref[0, :, sl], v2_ref[0, :, sl], vc_ref[0, :, sl]], axis=0)
            s = _nt_dot(q, k)
            o = _softmax_pv([s[:, :n_lat_keys] + bias_ref[0, h], s[:, n_lat_keys:]], v)
            o_ref[0, :, sl] = o.astype(o_ref.dtype)

    @pl.when(j >= nbl)
    def _context():
        for h in range(nh):
            sl = slice(h * HEAD_DIM, (h + 1) * HEAD_DIM)
            s = _nt_dot(q_ref[0, :, sl] * (HEAD_DIM ** -0.5), kc_ref[0, :, sl])
            o_ref[0, :, sl] = _softmax_pv([s], vc_ref[0, :, sl]).astype(o_ref.dtype)


def _na_bias_tables(rpb, rows):
    qb, kb = NA_QROWS, 3 * NA_QROWS
    nb = rows // qb
    tabs = []
    for rbq, wb in ((0, 0), (1, 0), (nb - 1, nb - 3)):
        qr = np.arange(qb)[:, None, None, None]
        c = np.arange(GRID_W)[None, :, None, None]
        j = np.arange(kb)[None, None, :, None]
        kc = np.arange(GRID_W)[None, None, None, :]
        r, kr = qb * rbq + qr, qb * wb + j
        r0 = np.clip(r - NA_WIN_ROWS // 2, 0, rows - NA_WIN_ROWS)
        cs = np.clip(c - NA_WIN_COLS // 2, 0, GRID_W - NA_WIN_COLS)
        valid = (kr >= r0) & (kr < r0 + NA_WIN_ROWS) & (kc >= cs) & (kc < cs + NA_WIN_COLS)
        dr = np.clip(kr - r + NA_WIN_ROWS - 1, 0, 2 * NA_WIN_ROWS - 2)
        dc = np.clip(kc - c + NA_WIN_COLS - 1, 0, 2 * NA_WIN_COLS - 2)
        shape = (qb, GRID_W, kb, GRID_W)
        flat = lambda t: np.broadcast_to(t, shape).reshape(qb * GRID_W, kb * GRID_W)
        tabs.append(jnp.where(flat(valid), rpb[:, flat(dr), flat(dc)].astype(F32), -jnp.inf))
    return jnp.stack(tabs)


def _na_attention(na, bias, nbl):
    bsz, ltot, w3 = na.shape
    w = w3 // 3
    nb = ltot // TB
    assert nbl >= 3 and nb == nbl + 1, "needs >= 12 grid rows and a context of one token block"
    win = lambda j: jnp.clip(j - 1, 0, nbl - 3)
    kv = lambda i, col: pl.BlockSpec((1, TB, w), lambda b, j: (b, win(j) + i, col))
    typ = lambda j: jnp.where(j == 0, 0, jnp.where(j >= nbl - 1, 2, 1))
    return pl.pallas_call(
        functools.partial(_na_kernel, nbl=nbl),
        grid=(bsz, nb),
        in_specs=[
            pl.BlockSpec((1, TB, w), lambda b, j: (b, j, 0)),
            kv(0, 1), kv(1, 1), kv(2, 1), kv(0, 2), kv(1, 2), kv(2, 2),
            pl.BlockSpec((1, TB, w), lambda b, j: (b, nbl, 1)),
            pl.BlockSpec((1, TB, w), lambda b, j: (b, nbl, 2)),
            pl.BlockSpec((1,) + bias.shape[1:], lambda b, j: (typ(j), 0, 0, 0)),
        ],
        out_specs=pl.BlockSpec((1, TB, w), lambda b, j: (b, j, 0)),
        out_shape=jax.ShapeDtypeStruct((bsz, ltot, w), BF16),
        compiler_params=_cparams("parallel", "arbitrary"),
        name="na_attention",
    )(na, na, na, na, na, na, na, na, na, bias)


def _route(aff, sel):
    rows = [sel[e:e + 1, :] for e in range(N_EXPERTS)]
    arow = [aff[e:e + 1, :] for e in range(N_EXPERTS)]
    gscore = []
    for g in range(N_GROUPS):
        a, b, c, d = rows[4 * g:4 * g + 4]
        hi1, lo1 = jnp.maximum(a, b), jnp.minimum(a, b)
        hi2, lo2 = jnp.maximum(c, d), jnp.minimum(c, d)
        gscore.append(jnp.maximum(hi1, hi2) + jnp.maximum(jnp.minimum(hi1, hi2), jnp.maximum(lo1, lo2)))
    best = jnp.zeros_like(gscore[0], dtype=jnp.int32)
    top = gscore[0]
    for g in range(1, N_GROUPS):
        upd = gscore[g] > top
        best = jnp.where(upd, g, best)
        top = jnp.where(upd, gscore[g], top)
    sv, av = [], []
    for j in range(EXPERTS_PER_GROUP):
        s_j, a_j = rows[j], arow[j]
        for g in range(1, N_GROUPS):
            s_j = jnp.where(best == g, rows[4 * g + j], s_j)
            a_j = jnp.where(best == g, arow[4 * g + j], a_j)
        sv.append(s_j)
        av.append(a_j)
    picked = []
    for j in range(EXPERTS_PER_GROUP):
        rank = jnp.zeros_like(best)
        for i in range(EXPERTS_PER_GROUP):
            if i == j:
                continue
            ahead = (sv[i] >= sv[j]) if i < j else (sv[i] > sv[j])
            rank = rank + ahead.astype(jnp.int32)
        picked.append(rank < TOP_K)
    first = jnp.full_like(best, EXPERTS_PER_GROUP - 1)
    last = jnp.zeros_like(best)
    for j in range(EXPERTS_PER_GROUP - 1, -1, -1):
        first = jnp.where(picked[j], j, first)
    for j in range(EXPERTS_PER_GROUP):
        last = jnp.where(picked[j], j, last)
    a_first, a_last = av[0], av[0]
    for j in range(1, EXPERTS_PER_GROUP):
        a_first = jnp.where(first == j, av[j], a_first)
        a_last = jnp.where(last == j, av[j], a_last)
    tot = a_first + a_last
    idx = jnp.concatenate([best * EXPERTS_PER_GROUP + first, best * EXPERTS_PER_GROUP + last], axis=0)
    wts = jnp.concatenate([a_first / tot, a_last / tot], axis=0)
    return idx, wts


def _outproj_kernel(h_ref, ret_ref, ssd_ref, na_ref, wo_ref, mod_ref, nw_ref, rwt_ref, rb_ref,
                    hn_ref, f_ref, idx_ref, wts_ref):
    rw, sw = ret_ref.shape[2], ssd_ref.shape[2]
    acc = jnp.dot(ret_ref[0], wo_ref[0:rw, :], preferred_element_type=F32)
    acc += jnp.dot(ssd_ref[0], wo_ref[rw:rw + sw, :], preferred_element_type=F32)
    acc += jnp.dot(na_ref[0], wo_ref[rw + sw:, :], preferred_element_type=F32)
    hn = h_ref[0] + mod_ref[0, 2:3, :] * acc
    hn_ref[0] = hn
    f = hn * lax.rsqrt(jnp.mean(hn * hn, axis=-1, keepdims=True) + EPS) * nw_ref[...]
    f = f * (1.0 + mod_ref[0, 4:5, :]) + mod_ref[0, 3:4, :]
    f_ref[0] = f.astype(BF16)
    logits = _nt_dot(rwt_ref[...], f, precision=HIGHEST)
    aff = 1.0 / (1.0 + jnp.exp(-logits))
    idx, wts = _route(aff, aff + rb_ref[...])
    idx_ref[0] = idx
    wts_ref[0] = wts


def _outproj(h, ret_o, ssd_o, na_o, w_out, mod, mod_row, norm_w, rw_t, rb):
    bsz, ltot, d = h.shape
    row = lambda n: pl.BlockSpec((1, TB, n), lambda b, j: (b, j, 0))
    const = lambda shape: pl.BlockSpec(shape, lambda b, j: (0,) * len(shape))
    col = lambda: pl.BlockSpec((1, TOP_K, TB), lambda b, j: (b, 0, j))
    return pl.pallas_call(
        _outproj_kernel,
        grid=(bsz, ltot // TB),
        in_specs=[
            row(d), row(ret_o.shape[2]), row(ssd_o.shape[2]), row(na_o.shape[2]),
            const(w_out.shape),
            pl.BlockSpec((1, 6, d), lambda b, j: (mod_row(b, j), 0, 0)),
            const((1, d)), const(rw_t.shape), const(rb.shape),
        ],
        out_specs=[row(d), row(d), col(), col()],
        out_shape=[
            jax.ShapeDtypeStruct((bsz, ltot, d), F32),
            jax.ShapeDtypeStruct((bsz, ltot, d), BF16),
            jax.ShapeDtypeStruct((bsz, TOP_K, ltot), jnp.int32),
            jax.ShapeDtypeStruct((bsz, TOP_K, ltot), F32),
        ],
        compiler_params=_cparams("parallel", "parallel"),
        name="outproj_router",
    )(h, ret_o, ssd_o, na_o, w_out, mod, norm_w, rw_t, rb)


def _moe_kernel(te_ref, nt_ref, x_ref, wg_ref, wu_ref, wd_ref, y_ref, wg_s, wu_s, wd_s):
    i = pl.program_id(0)
    fresh = jnp.logical_or(i == 0, te_ref[i] != te_ref[jnp.maximum(i - 1, 0)])

    @pl.when(fresh)
    def _():
        wg_s[...] = wg_ref[0].astype(BF16)
        wu_s[...] = wu_ref[0].astype(BF16)
        wd_s[...] = wd_ref[0].astype(BF16)

    @pl.when(i < nt_ref[0])
    def _():
        x = x_ref[...]
        g = jnp.dot(x, wg_s[...], preferred_element_type=F32)
        u = jnp.dot(x, wu_s[...], preferred_element_type=F32)
        he = (_silu(g) * u).astype(BF16)
        y_ref[...] = jnp.dot(he, wd_s[...], preferred_element_type=F32)

    @pl.when(i >= nt_ref[0])
    def _():
        y_ref[...] = jnp.zeros_like(y_ref)


def _moe_grouped(x_sorted, tile_expert, n_tiles_used, w_gate, w_up, w_down):
    p, d = x_sorted.shape
    de = w_gate.shape[2]
    tm = MOE_TILE
    grid_spec = pltpu.PrefetchScalarGridSpec(
        num_scalar_prefetch=2,
        grid=(p // tm,),
        in_specs=[
            pl.BlockSpec((tm, d), lambda i, te, nt: (i, 0)),
            pl.BlockSpec((1, d, de), lambda i, te, nt: (te[i], 0, 0)),
            pl.BlockSpec((1, d, de), lambda i, te, nt: (te[i], 0, 0)),
            pl.BlockSpec((1, de, d), lambda i, te, nt: (te[i], 0, 0)),
        ],
        out_specs=pl.BlockSpec((tm, d), lambda i, te, nt: (i, 0)),
        scratch_shapes=[pltpu.VMEM((d, de), BF16), pltpu.VMEM((d, de), BF16), pltpu.VMEM((de, d), BF16)],
    )
    return pl.pallas_call(
        _moe_kernel,
        grid_spec=grid_spec,
        out_shape=jax.ShapeDtypeStruct((p, d), F32),
        compiler_params=_cparams("arbitrary"),
        name="moe_grouped",
    )(tile_expert, n_tiles_used, x_sorted, w_gate, w_up, w_down)


def _combine_kernel(h_ref, y0_ref, y1_ref, w_ref, mod_ref, o_ref):
    w = w_ref[0]
    y = w[:, 0:1] * y0_ref[0] + w[:, 1:2] * y1_ref[0]
    o_ref[0] = h_ref[0] + mod_ref[0, 5:6, :] * y


def _combine(h, y0, y1, w_tok, mod, mod_row, n_out):
    bsz, ltot, d = h.shape
    row = lambda n: pl.BlockSpec((1, TB, n), lambda b, j: (b, j, 0))
    return pl.pallas_call(
        _combine_kernel,
        grid=(bsz, n_out // TB),
        in_specs=[row(d), row(d), row(d), row(TOP_K),
                  pl.BlockSpec((1, 6, d), lambda b, j: (mod_row(b, j), 0, 0))],
        out_specs=row(d),
        out_shape=jax.ShapeDtypeStruct((bsz, n_out, d), F32),
        compiler_params=_cparams("parallel", "parallel"),
        name="moe_combine",
    )(h, y0, y1, w_tok, mod)


def _moe_block(h, f, idx, wts, mod, mod_row, w_gate, w_up, w_down, n_out):
    bsz, ltot, d = h.shape
    t = bsz * ltot
    n_e = w_gate.shape[0]
    mt = MOE_TILE
    e_flat = jnp.transpose(idx, (1, 0, 2)).reshape(-1)
    onehot = (e_flat[:, None] == jnp.arange(n_e, dtype=jnp.int32)[None, :]).astype(jnp.int32)
    csum = jnp.cumsum(onehot, axis=0)
    counts = csum[-1]
    rank = jnp.sum(csum * onehot, axis=1) - 1
    padded = ((counts + mt - 1) // mt) * mt
    ends = jnp.cumsum(padded)
    starts = ends - padded
    pos = starts[e_flat] + rank
    p_total = ((TOP_K * t + mt - 1) // mt + n_e) * mt
    tok = jnp.tile(jnp.arange(t, dtype=jnp.int32), TOP_K)
    row_src = jnp.zeros((p_total,), jnp.int32).at[pos].set(tok)
    n_tiles = p_total // mt
    tile_expert = jnp.minimum(
        jnp.searchsorted(ends, jnp.arange(n_tiles, dtype=jnp.int32) * mt, side="right"), n_e - 1
    ).astype(jnp.int32)
    n_used = (ends[-1] // mt).astype(jnp.int32).reshape(1)
    x_sorted = jnp.take(f.reshape(t, d), row_src, axis=0)
    y_sorted = _moe_grouped(x_sorted, tile_expert, n_used, w_gate, w_up, w_down)
    y0 = jnp.take(y_sorted, pos[:t], axis=0).reshape(bsz, ltot, d)
    y1 = jnp.take(y_sorted, pos[t:], axis=0).reshape(bsz, ltot, d)
    return _combine(h, y0, y1, jnp.transpose(wts, (0, 2, 1)), mod, mod_row, n_out)


def _rope_tables(n_lat, n_ctx, n_heads):
    t = jnp.arange(n_lat)
    pos = jnp.stack([t // GRID_W, t % GRID_W], axis=-1).astype(F32)
    n_freq = HEAD_DIM // 4
    inv = 1.0 / (ROPE_BASE ** (jnp.arange(n_freq, dtype=F32) / n_freq))
    ang = pos[:, :, None] * inv
    cos, sin = jnp.cos(ang), jnp.sin(ang)
    cos_h = jnp.concatenate([cos[:, 0], cos[:, 0], cos[:, 1], cos[:, 1]], axis=-1)
    sin_h = jnp.concatenate([-sin[:, 0], sin[:, 0], -sin[:, 1], sin[:, 1]], axis=-1)
    cos_h = jnp.concatenate([cos_h, jnp.ones((n_ctx, HEAD_DIM), F32)], axis=0)
    sin_h = jnp.concatenate([sin_h, jnp.zeros((n_ctx, HEAD_DIM), F32)], axis=0)
    return jnp.tile(cos_h, (1, n_heads)), jnp.tile(sin_h, (1, n_heads))


def _swap_halves_perm(width):
    j = np.arange(width)
    nf = HEAD_DIM // 4
    return np.where((j % (2 * nf)) < nf, j + nf, j - nf)


def _pack_w_in(w_in, rw, sw, nw_):
    o = 0
    seg = {}
    for name, width in (("rq", rw), ("rk", rw), ("rv", rw), ("rg", rw), ("z", sw), ("x", sw),
                        ("b", SSD_GROUPS * SSD_STATE), ("c", SSD_GROUPS * SSD_STATE),
                        ("dt", 2 * (sw // HEAD_DIM)), ("nq", nw_), ("nk", nw_), ("nv", nw_)):
        seg[name] = w_in[:, o:o + width]
        o += width
    perm = _swap_halves_perm(rw)
    dt_pad = jnp.zeros((w_in.shape[0], 128 - seg["dt"].shape[1]), w_in.dtype)
    cols = [seg["rq"], seg["rk"], seg["rq"][:, perm], seg["rk"][:, perm], seg["rv"], seg["rg"],
            seg["z"], seg["x"], seg["b"], seg["c"], seg["nq"], seg["nk"], seg["nv"], seg["dt"], dt_pad]
    return jnp.concatenate(cols, axis=1).astype(BF16)


def kernel(x, c, ctx, c_ctx, w_mod, b_mod, norm_mix, norm_ffn, w_in, w_out, ret_decay_f, ret_decay_b,
           ssd_conv_w, ssd_conv_b, ssd_dt_bias_f, ssd_dt_bias_b, ssd_a_log_f, ssd_a_log_b, ssd_d, ssd_norm,
           na_q_norm, na_k_norm, na_rpb, router_w, router_b, w_gate, w_up, w_down):
    bsz, n_lat, d = x.shape
    n_ctx = ctx.shape[1]
    depth = w_mod.shape[0]
    rows = n_lat // GRID_W
    rw = ret_decay_f.shape[1] * HEAD_DIM
    sw = ssd_d.shape[1] * HEAD_DIM
    nw_ = na_rpb.shape[1] * HEAD_DIM
    dims = (rw, sw, nw_)
    assert n_lat % TB == 0 and n_ctx % TB == 0
    nbl = n_lat // TB

    n_cond = ((bsz + 1 + 7) // 8) * 8
    cond = jnp.zeros((n_cond, d), F32).at[:bsz].set(c).at[bsz].set(c_ctx)
    mod = _modulation(cond, w_mod, b_mod).reshape(depth, n_cond, 6, d)
    mod_row = lambda b, j: jnp.where(j < nbl, b, bsz)

    cos, sin = _rope_tables(n_lat, n_ctx, rw // HEAD_DIM)
    hm = jnp.asarray(np.kron(np.eye(nw_ // HEAD_DIM), np.full((HEAD_DIM, HEAD_DIM), 1.0 / HEAD_DIM)), F32)
    rw_t = router_w.T
    rb = router_b.reshape(-1, 1)

    h = jnp.concatenate([x, ctx], axis=1)
    for l in range(depth):
        last = l == depth - 1
        w_all = _pack_w_in(w_in[l], rw, sw, nw_)
        qkn = jnp.stack([jnp.tile(na_q_norm[l], nw_ // HEAD_DIM), jnp.tile(na_k_norm[l], nw_ // HEAD_DIM)])
        ret, z, xbc, dt, na = _inproj(h, mod[l], mod_row, norm_mix[l].reshape(1, d), w_all, cos, sin, qkn, hm, dims)
        ret_o = _retention(ret, jnp.concatenate([ret_decay_f[l], ret_decay_b[l]]).reshape(-1, 1, 1), nbl)
        ssd_o = _ssd(z, xbc, dt, ssd_conv_w[l], ssd_conv_b[l],
                     jnp.concatenate([ssd_dt_bias_f[l], ssd_dt_bias_b[l]]),
                     jnp.concatenate([ssd_a_log_f[l], ssd_a_log_b[l]]), ssd_d[l], ssd_norm[l], nbl)
        na_o = _na_attention(na, _na_bias_tables(na_rpb[l], rows), nbl)
        h, f, idx, wts = _outproj(h, ret_o, ssd_o, na_o, w_out[l].astype(BF16), mod[l], mod_row,
                                  norm_ffn[l].reshape(1, d), rw_t, rb)
        h = _moe_block(h, f, idx, wts, mod[l], mod_row, w_gate[l], w_up[l], w_down[l],
                       n_lat if last else n_lat + n_ctx)
    return h
```

```python
import functools

import jax
import jax.numpy as jnp
import numpy as np
from jax import lax
from jax.experimental import pallas as pl
from jax.experimental.pallas import tpu as pltpu

F32 = jnp.float32
BF16 = jnp.bfloat16
HIGHEST = lax.Precision.HIGHEST

GRID_W = 64
HEAD_DIM = 64
SSD_GROUPS = 2
SSD_STATE = 64
NA_WIN_ROWS = 8
NA_WIN_COLS = 16
N_GROUPS = 4
EXPERTS_PER_GROUP = 4
N_EXPERTS = N_GROUPS * EXPERTS_PER_GROUP
TOP_K = 2
ROPE_BASE = 10000.0
EPS = 1e-6

VMEM_LIMIT_BYTES = 56 * 1024 * 1024
TB = 256
NA_QROWS = TB // GRID_W
MOE_TILE = 512
SUBLANES = 8


def _cparams(*sem):
    return pltpu.CompilerParams(dimension_semantics=sem, vmem_limit_bytes=VMEM_LIMIT_BYTES)


def _silu(t):
    return t * (1.0 / (1.0 + jnp.exp(-t)))


def _nt_dot(a, b, **kw):
    return lax.dot_general(a, b, (((1,), (1,)), ((), ())), preferred_element_type=F32, **kw)


def _tn_dot(a, b):
    return lax.dot_general(a, b, (((0,), (0,)), ((), ())), preferred_element_type=F32)


def _mod_kernel(s_ref, w_ref, b_ref, o_ref):
    s = _silu(s_ref[...])
    o_ref[0] = jnp.dot(s, w_ref[0], precision=HIGHEST, preferred_element_type=F32) + b_ref[0]


def _modulation(cond, w_mod, b_mod):
    depth, d, n = w_mod.shape
    rows = cond.shape[0]
    tn = 1024
    return pl.pallas_call(
        _mod_kernel,
        grid=(depth, n // tn),
        in_specs=[
            pl.BlockSpec((rows, d), lambda l, j: (0, 0)),
            pl.BlockSpec((1, d, tn), lambda l, j: (l, 0, j)),
            pl.BlockSpec((1, 1, tn), lambda l, j: (l, 0, j)),
        ],
        out_specs=pl.BlockSpec((1, rows, tn), lambda l, j: (l, 0, j)),
        out_shape=jax.ShapeDtypeStruct((depth, rows, n), F32),
        compiler_params=_cparams("parallel", "parallel"),
        name="modulation",
    )(cond, w_mod, b_mod.reshape(depth, 1, n))


def _inproj_kernel(h_ref, mod_ref, nw_ref, w_ref, cos_ref, sin_ref, qkn_ref, hm_ref,
                   ret_ref, z_ref, xbc_ref, dt_ref, na_ref, *, dims):
    rw, sw, nw_ = dims
    h = h_ref[0]
    a = h * lax.rsqrt(jnp.mean(h * h, axis=-1, keepdims=True) + EPS) * nw_ref[...]
    a = a * (1.0 + mod_ref[0, 1:2, :]) + mod_ref[0, 0:1, :]
    ab = a.astype(BF16)

    def proj(lo, hi):
        return jnp.dot(ab, w_ref[:, lo:hi], preferred_element_type=F32)

    cos, sin = cos_ref[...], sin_ref[...]
    o = 0
    ret_ref[0, :, 0:rw] = proj(o, o + rw) * cos + proj(o + 2 * rw, o + 3 * rw) * sin
    ret_ref[0, :, rw:2 * rw] = (proj(o + rw, o + 2 * rw) * cos + proj(o + 3 * rw, o + 4 * rw) * sin) * (HEAD_DIM ** -0.5)
    ret_ref[0, :, 2 * rw:4 * rw] = proj(o + 4 * rw, o + 6 * rw)
    o += 6 * rw
    z_ref[0] = proj(o, o + sw)
    o += sw
    xbc_w = sw + 2 * SSD_GROUPS * SSD_STATE
    xbc_ref[0] = proj(o, o + xbc_w)
    o += xbc_w
    hm = hm_ref[...]
    for i in range(2):
        t = proj(o + i * nw_, o + (i + 1) * nw_)
        ms = jnp.dot(t * t, hm, precision=HIGHEST, preferred_element_type=F32)
        na_ref[0, :, i * nw_:(i + 1) * nw_] = (t * lax.rsqrt(ms + EPS) * qkn_ref[i:i + 1, :]).astype(BF16)
    na_ref[0, :, 2 * nw_:3 * nw_] = proj(o + 2 * nw_, o + 3 * nw_).astype(BF16)
    o += 3 * nw_
    dt_ref[0] = proj(o, o + 128)[:, 0:dt_ref.shape[2]]


def _inproj(h, mod, mod_row, norm_w, w_all, cos, sin, qkn, hm, dims):
    bsz, ltot, d = h.shape
    rw, sw, nw_ = dims
    n_dt = 2 * (sw // HEAD_DIM)
    xbc_w = sw + 2 * SSD_GROUPS * SSD_STATE
    row = lambda n: pl.BlockSpec((1, TB, n), lambda b, j: (b, j, 0))
    const = lambda shape: pl.BlockSpec(shape, lambda b, j: (0,) * len(shape))
    widths = (4 * rw, sw, xbc_w, n_dt, 3 * nw_)
    dtypes = (F32, F32, F32, F32, BF16)
    return pl.pallas_call(
        functools.partial(_inproj_kernel, dims=dims),
        grid=(bsz, ltot // TB),
        in_specs=[
            row(d),
            pl.BlockSpec((1, 6, d), lambda b, j: (mod_row(b, j), 0, 0)),
            const((1, d)),
            const(w_all.shape),
            pl.BlockSpec((TB, rw), lambda b, j: (j, 0)),
            pl.BlockSpec((TB, rw), lambda b, j: (j, 0)),
            const((2, nw_)),
            const((nw_, nw_)),
        ],
        out_specs=[row(n) for n in widths],
        out_shape=[jax.ShapeDtypeStruct((bsz, ltot, n), dt) for n, dt in zip(widths, dtypes)],
        compiler_params=_cparams("parallel", "parallel"),
        name="inproj",
    )(h, mod, norm_w, w_all, cos, sin, qkn, hm)


def _scan_chunk_maps(nb, nbl):
    fwd = lambda s: jnp.where(s < nb, (nbl + s) % nb, 2 * nb - 1 - s)
    out = lambda s: jnp.where(s < nb, nb - 1, 2 * nb - 1 - s)
    return fwd, out


def _ret_kernel(q_ref, k_ref, v_ref, g_ref, dec_ref, o_ref, sf_all, sf, sb, dmat, rd, *, nb, nbl):
    s = pl.program_id(1)
    c = TB
    nh = k_ref.shape[2] // HEAD_DIM
    log_f = [-jnp.exp(dec_ref[h]) for h in range(nh)]
    log_b = [-jnp.exp(dec_ref[nh + h]) for h in range(nh)]

    @pl.when(s == 0)
    def _init():
        sf[...] = jnp.zeros_like(sf)
        sb[...] = jnp.zeros_like(sb)
        delta = (lax.broadcasted_iota(jnp.int32, (c, c), 0) - lax.broadcasted_iota(jnp.int32, (c, c), 1)).astype(F32)
        pos = lax.broadcasted_iota(jnp.int32, (c, HEAD_DIM), 0).astype(F32)
        for h in range(nh):
            dmat[h] = (jnp.exp(jnp.where(delta >= 0, log_f[h] * delta, -jnp.inf))
                       + jnp.exp(jnp.where(delta <= 0, -log_b[h] * delta, -jnp.inf)))
            rd[4 * h + 0] = jnp.exp(log_f[h] * (pos + 1.0))
            rd[4 * h + 1] = jnp.exp(log_b[h] * (c - pos))
            rd[4 * h + 2] = jnp.exp(log_f[h] * (c - 1.0 - pos))
            rd[4 * h + 3] = jnp.exp(log_b[h] * pos)

    @pl.when(s < nb)
    def _state_sweep():
        ci = (nbl + s) % nb
        for h in range(nh):
            sl = slice(h * HEAD_DIM, (h + 1) * HEAD_DIM)
            k, v = k_ref[0, :, sl], v_ref[0, :, sl]
            sf_all[ci, h] = sf[h]
            sf[h] = sf[h] * jnp.exp(log_f[h] * c) + _tn_dot((k * rd[4 * h + 2]).astype(BF16), v.astype(BF16))

    @pl.when(s >= nb)
    def _output_sweep():
        ci = 2 * nb - 1 - s
        for h in range(nh):
            sl = slice(h * HEAD_DIM, (h + 1) * HEAD_DIM)
            q, k, v = q_ref[0, :, sl], k_ref[0, :, sl], v_ref[0, :, sl]
            qb, vb = q.astype(BF16), v.astype(BF16)
            p = _nt_dot(qb, k.astype(BF16)) * dmat[h]
            y = jnp.dot(p.astype(BF16), vb, preferred_element_type=F32)
            y += jnp.dot(qb, sf_all[ci, h].astype(BF16), preferred_element_type=F32) * rd[4 * h + 0]
            y += jnp.dot(qb, sb[h].astype(BF16), preferred_element_type=F32) * rd[4 * h + 1]
            sb[h] = sb[h] * jnp.exp(log_b[h] * c) + _tn_dot((k * rd[4 * h + 3]).astype(BF16), vb)
            mu = jnp.mean(y, axis=-1, keepdims=True)
            yc = y - mu
            yn = yc * lax.rsqrt(jnp.mean(yc * yc, axis=-1, keepdims=True) + EPS)
            o_ref[0, :, sl] = (yn * _silu(g_ref[0, :, sl])).astype(o_ref.dtype)


def _retention(ret, decay, nbl):
    bsz, ltot, w4 = ret.shape
    w = w4 // 4
    nh = w // HEAD_DIM
    nb = ltot // TB
    fwd, out = _scan_chunk_maps(nb, nbl)
    return pl.pallas_call(
        functools.partial(_ret_kernel, nb=nb, nbl=nbl),
        grid=(bsz, 2 * nb),
        in_specs=[
            pl.BlockSpec((1, TB, w), lambda b, s: (b, out(s), 0)),
            pl.BlockSpec((1, TB, w), lambda b, s: (b, fwd(s), 1)),
            pl.BlockSpec((1, TB, w), lambda b, s: (b, fwd(s), 2)),
            pl.BlockSpec((1, TB, w), lambda b, s: (b, out(s), 3)),
            pl.BlockSpec(decay.shape, lambda b, s: (0, 0, 0)),
        ],
        out_specs=pl.BlockSpec((1, TB, w), lambda b, s: (b, out(s), 0)),
        out_shape=jax.ShapeDtypeStruct((bsz, ltot, w), BF16),
        scratch_shapes=[
            pltpu.VMEM((nb, nh, HEAD_DIM, HEAD_DIM), F32),
            pltpu.VMEM((nh, HEAD_DIM, HEAD_DIM), F32),
            pltpu.VMEM((nh, HEAD_DIM, HEAD_DIM), F32),
            pltpu.VMEM((nh, TB, TB), F32),
            pltpu.VMEM((4 * nh, TB, HEAD_DIM), F32),
        ],
        compiler_params=_cparams("parallel", "arbitrary"),
        name="retention",
    )(ret, ret, ret, ret, decay)


def _softplus(t):
    return jnp.maximum(t, 0.0) + jnp.log1p(jnp.exp(-jnp.abs(t)))


def _ssd_kernel(z_ref, x_ref, xp_ref, xn_ref, dt_ref, cw_ref, cb_ref, dtb_ref, alog_ref, dtbt_ref, alogt_ref,
                dsk_ref, nw_ref, o_ref, sf_all, sf, sb, tri, ybuf, *, nb, nbl):
    s = pl.program_id(1)
    c = TB
    sw = z_ref.shape[2]
    nh = sw // HEAD_DIM
    rep = nh // SSD_GROUPS
    gs = SSD_GROUPS * SSD_STATE
    ci = jnp.where(s < nb, (nbl + s) % nb, 2 * nb - 1 - s)

    @pl.when(s == 0)
    def _init():
        sf[...] = jnp.zeros_like(sf)
        sb[...] = jnp.zeros_like(sb)
        li = lax.broadcasted_iota(jnp.int32, (c, c), 0)
        si = lax.broadcasted_iota(jnp.int32, (c, c), 1)
        tri[0] = (si <= li).astype(F32)
        tri[1] = (si >= li).astype(F32)

    x = x_ref[0]
    has_prev = jnp.where(jnp.logical_or(ci == 0, ci == nbl), 0.0, 1.0)
    has_next = jnp.where(jnp.logical_or(ci == nbl - 1, ci == nb - 1), 0.0, 1.0)
    rows = lax.broadcasted_iota(jnp.int32, x.shape, 0)
    x_dn = jnp.where(rows == 0, xp_ref[0, SUBLANES - 1:SUBLANES, :] * has_prev, pltpu.roll(x, 1, 0))
    x_up = jnp.where(rows == c - 1, xn_ref[0, 0:1, :] * has_next, pltpu.roll(x, c - 1, 0))
    xc = _silu(cw_ref[0:1, :] * x_dn + cw_ref[1:2, :] * x + cw_ref[2:3, :] * x_up + cb_ref[...])
    xs, bm, cm = xc[:, :sw], xc[:, sw:sw + gs], xc[:, sw + gs:]

    dtr = dt_ref[0]
    eye = (lax.broadcasted_iota(jnp.int32, (2 * nh, 2 * nh), 0)
           == lax.broadcasted_iota(jnp.int32, (2 * nh, 2 * nh), 1)).astype(F32)
    dt = _softplus(dtr + dtb_ref[...])
    a = dt * -jnp.exp(alog_ref[...])
    dt_t = _softplus(_nt_dot(eye, dtr, precision=HIGHEST) + dtbt_ref[...])
    a_t = dt_t * -jnp.exp(alogt_ref[...])
    acf = jnp.dot(tri[0], a, precision=HIGHEST, preferred_element_type=F32)
    acb = jnp.dot(tri[1], a, precision=HIGHEST, preferred_element_type=F32)
    acf_t = jnp.dot(a_t, tri[1], precision=HIGHEST, preferred_element_type=F32)
    acb_t = jnp.dot(a_t, tri[0], precision=HIGHEST, preferred_element_type=F32)

    @pl.when(s < nb)
    def _state_sweep():
        for h in range(nh):
            g = h // rep
            col = acf[:, h:h + 1]
            tot = acf[c - 1:c, h:h + 1]
            bw = bm[:, g * SSD_STATE:(g + 1) * SSD_STATE] * (dt[:, h:h + 1] * jnp.exp(tot - col))
            sf_all[ci, h] = sf[h]
            sf[h] = sf[h] * jnp.exp(tot) + _tn_dot(bw.astype(BF16), xs[:, h * HEAD_DIM:(h + 1) * HEAD_DIM].astype(BF16))

    @pl.when(s >= nb)
    def _output_sweep():
        li = lax.broadcasted_iota(jnp.int32, (c, c), 0)
        si = lax.broadcasted_iota(jnp.int32, (c, c), 1)
        for g in range(SSD_GROUPS):
            cg = cm[:, g * SSD_STATE:(g + 1) * SSD_STATE].astype(BF16)
            bg = bm[:, g * SSD_STATE:(g + 1) * SSD_STATE]
            scores = _nt_dot(cg, bg.astype(BF16))
            for r in range(rep):
                h = g * rep + r
                hb = nh + h
                xh = xs[:, h * HEAD_DIM:(h + 1) * HEAD_DIM].astype(BF16)
                col_f, col_b = acf[:, h:h + 1], acb[:, hb:hb + 1]
                d_f = jnp.exp(jnp.where(li >= si, col_f - acf_t[h:h + 1, :], -jnp.inf)) * dt_t[h:h + 1, :]
                d_b = jnp.exp(jnp.where(si >= li, col_b - acb_t[hb:hb + 1, :], -jnp.inf)) * dt_t[hb:hb + 1, :]
                y = jnp.dot((scores * (d_f + d_b)).astype(BF16), xh, preferred_element_type=F32)
                y += jnp.dot(cg, sf_all[ci, h].astype(BF16), preferred_element_type=F32) * jnp.exp(col_f)
                y += jnp.dot(cg, sb[h].astype(BF16), preferred_element_type=F32) * jnp.exp(col_b)
                tot = acb[0:1, hb:hb + 1]
                bw = bg * (dt[:, hb:hb + 1] * jnp.exp(tot - col_b))
                sb[h] = sb[h] * jnp.exp(tot) + _tn_dot(bw.astype(BF16), xh)
                ybuf[:, h * HEAD_DIM:(h + 1) * HEAD_DIM] = y
        y = (ybuf[...] + dsk_ref[...] * xs) * _silu(z_ref[0])
        o_ref[0] = (y * lax.rsqrt(jnp.mean(y * y, axis=-1, keepdims=True) + EPS) * nw_ref[...]).astype(o_ref.dtype)


def _ssd(z, xbc, dt, conv_w, conv_b, dtb, alog, d_skip, norm_w, nbl):
    bsz, ltot, sw = z.shape
    nh = sw // HEAD_DIM
    xw = xbc.shape[2]
    nb = ltot // TB
    per = TB // SUBLANES
    fwd, out = _scan_chunk_maps(nb, nbl)
    const = lambda t: pl.BlockSpec(t.shape, lambda b, s: (0,) * t.ndim)
    args = (conv_w, conv_b.reshape(1, xw), dtb.reshape(1, 2 * nh), alog.reshape(1, 2 * nh),
            dtb.reshape(2 * nh, 1), alog.reshape(2 * nh, 1), jnp.repeat(d_skip, HEAD_DIM).reshape(1, sw),
            norm_w.reshape(1, sw))
    return pl.pallas_call(
        functools.partial(_ssd_kernel, nb=nb, nbl=nbl),
        grid=(bsz, 2 * nb),
        in_specs=[
            pl.BlockSpec((1, TB, sw), lambda b, s: (b, out(s), 0)),
            pl.BlockSpec((1, TB, xw), lambda b, s: (b, fwd(s), 0)),
            pl.BlockSpec((1, SUBLANES, xw), lambda b, s: (b, jnp.maximum(fwd(s) * per - 1, 0), 0)),
            pl.BlockSpec((1, SUBLANES, xw), lambda b, s: (b, jnp.minimum((fwd(s) + 1) * per, nb * per - 1), 0)),
            pl.BlockSpec((1, TB, 2 * nh), lambda b, s: (b, fwd(s), 0)),
        ] + [const(t) for t in args],
        out_specs=pl.BlockSpec((1, TB, sw), lambda b, s: (b, out(s), 0)),
        out_shape=jax.ShapeDtypeStruct((bsz, ltot, sw), BF16),
        scratch_shapes=[
            pltpu.VMEM((nb, nh, SSD_STATE, HEAD_DIM), F32),
            pltpu.VMEM((nh, SSD_STATE, HEAD_DIM), F32),
            pltpu.VMEM((nh, SSD_STATE, HEAD_DIM), F32),
            pltpu.VMEM((2, TB, TB), F32),
            pltpu.VMEM((TB, sw), F32),
        ],
        compiler_params=_cparams("parallel", "arbitrary"),
        name="ssd",
    )(z, xbc, xbc, xbc, dt, *args)


def _softmax_pv(s_parts, v):
    m = s_parts[0].max(axis=-1, keepdims=True)
    for s in s_parts[1:]:
        m = jnp.maximum(m, s.max(axis=-1, keepdims=True))
    p_parts = [jnp.exp(s - m) for s in s_parts]
    den = p_parts[0].sum(axis=-1, keepdims=True)
    for p in p_parts[1:]:
        den = den + p.sum(axis=-1, keepdims=True)
    p = p_parts[0] if len(p_parts) == 1 else jnp.concatenate(p_parts, axis=1)
    return jnp.dot(p.astype(BF16), v, preferred_element_type=F32) / den


def _na_kernel(q_ref, k0_ref, k1_ref, k2_ref, v0_ref, v1_ref, v2_ref, kc_ref, vc_ref, bias_ref, o_ref, *, nbl):
    nh = q_ref.shape[2] // HEAD_DIM
    n_lat_keys = 3 * k0_ref.shape[1]
    j = pl.program_id(1)

    @pl.when(j < nbl)
    def _latent():
        for h in range(nh):
            sl = slice(h * HEAD_DIM, (h + 1) * HEAD_DIM)
            q = q_ref[0, :, sl] * (HEAD_DIM ** -0.5)
            k = jnp.concatenate([k0_ref[0, :, sl], k1_ref[0, :, sl], k2_ref[0, :, sl], kc_ref[0, :, sl]], axis=0)
            v = jnp.concatenate([v0_ref[0, :, sl], v1_ref[0, :, sl], v2_ref[0, :, sl], vc_ref[0, :, sl]], axis=0)
            s = _nt_dot(q, k)
            o = _softmax_pv([s[:, :n_lat_keys] + bias_ref[0, h], s[:, n_lat_keys:]], v)
            o_ref[0, :, sl] = o.astype(o_ref.dtype)

    @pl.when(j >= nbl)
    def _context():
        for h in range(nh):
            sl = slice(h * HEAD_DIM, (h + 1) * HEAD_DIM)
            s = _nt_dot(q_ref[0, :, sl] * (HEAD_DIM ** -0.5), kc_ref[0, :, sl])
            o_ref[0, :, sl] = _softmax_pv([s], vc_ref[0, :, sl]).astype(o_ref.dtype)


def _na_bias_tables(rpb, rows):
    qb, kb = NA_QROWS, 3 * NA_QROWS
    nb = rows // qb
    tabs = []
    for rbq, wb in ((0, 0), (1, 0), (nb - 1, nb - 3)):
        qr = np.arange(qb)[:, None, None, None]
        c = np.arange(GRID_W)[None, :, None, None]
        j = np.arange(kb)[None, None, :, None]
        kc = np.arange(GRID_W)[None, None, None, :]
        r, kr = qb * rbq + qr, qb * wb + j
        r0 = np.clip(r - NA_WIN_ROWS // 2, 0, rows - NA_WIN_ROWS)
        cs = np.clip(c - NA_WIN_COLS // 2, 0, GRID_W - NA_WIN_COLS)
        valid = (kr >= r0) & (kr < r0 + NA_WIN_ROWS) & (kc >= cs) & (kc < cs + NA_WIN_COLS)
        dr = np.clip(kr - r + NA_WIN_ROWS - 1, 0, 2 * NA_WIN_ROWS - 2)
        dc = np.clip(kc - c + NA_WIN_COLS - 1, 0, 2 * NA_WIN_COLS - 2)
        shape = (qb, GRID_W, kb, GRID_W)
        flat = lambda t: np.broadcast_to(t, shape).reshape(qb * GRID_W, kb * GRID_W)
        tabs.append(jnp.where(flat(valid), rpb[:, flat(dr), flat(dc)].astype(F32), -jnp.inf))
    return jnp.stack(tabs)


def _na_attention(na, bias, nbl):
    bsz, ltot, w3 = na.shape
    w = w3 // 3
    nb = ltot // TB
    assert nbl >= 3 and nb == nbl + 1, "needs >= 12 grid rows and a context of one token block"
    win = lambda j: jnp.clip(j - 1, 0, nbl - 3)
    kv = lambda i, col: pl.BlockSpec((1, TB, w), lambda b, j: (b, win(j) + i, col))
    typ = lambda j: jnp.where(j == 0, 0, jnp.where(j >= nbl - 1, 2, 1))
    return pl.pallas_call(
        functools.partial(_na_kernel, nbl=nbl),
        grid=(bsz, nb),
        in_specs=[
            pl.BlockSpec((1, TB, w), lambda b, j: (b, j, 0)),
            kv(0, 1), kv(1, 1), kv(2, 1), kv(0, 2), kv(1, 2), kv(2, 2),
            pl.BlockSpec((1, TB, w), lambda b, j: (b, nbl, 1)),
            pl.BlockSpec((1, TB, w), lambda b, j: (b, nbl, 2)),
            pl.BlockSpec((1,) + bias.shape[1:], lambda b, j: (typ(j), 0, 0, 0)),
        ],
        out_specs=pl.BlockSpec((1, TB, w), lambda b, j: (b, j, 0)),
        out_shape=jax.ShapeDtypeStruct((bsz, ltot, w), BF16),
        compiler_params=_cparams("parallel", "arbitrary"),
        name="na_attention",
    )(na, na, na, na, na, na, na, na, na, bias)


def _route(aff, sel):
    rows = [sel[e:e + 1, :] for e in range(N_EXPERTS)]
    arow = [aff[e:e + 1, :] for e in range(N_EXPERTS)]
    gscore = []
    for g in range(N_GROUPS):
        a, b, c, d = rows[4 * g:4 * g + 4]
        hi1, lo1 = jnp.maximum(a, b), jnp.minimum(a, b)
        hi2, lo2 = jnp.maximum(c, d), jnp.minimum(c, d)
        gscore.append(jnp.maximum(hi1, hi2) + jnp.maximum(jnp.minimum(hi1, hi2), jnp.maximum(lo1, lo2)))
    best = jnp.zeros_like(gscore[0], dtype=jnp.int32)
    top = gscore[0]
    for g in range(1, N_GROUPS):
        upd = gscore[g] > top
        best = jnp.where(upd, g, best)
        top = jnp.where(upd, gscore[g], top)
    sv, av = [], []
    for j in range(EXPERTS_PER_GROUP):
        s_j, a_j = rows[j], arow[j]
        for g in range(1, N_GROUPS):
            s_j = jnp.where(best == g, rows[4 * g + j], s_j)
            a_j = jnp.where(best == g, arow[4 * g + j], a_j)
        sv.append(s_j)
        av.append(a_j)
    picked = []
    for j in range(EXPERTS_PER_GROUP):
        rank = jnp.zeros_like(best)
        for i in range(EXPERTS_PER_GROUP):
            if i == j:
                continue
            ahead = (sv[i] >= sv[j]) if i < j else (sv[i] > sv[j])
            rank = rank + ahead.astype(jnp.int32)
        picked.append(rank < TOP_K)
    first = jnp.full_like(best, EXPERTS_PER_GROUP - 1)
    last = jnp.zeros_like(best)
    for j in range(EXPERTS_PER_GROUP - 1, -1, -1):
        first = jnp.where(picked[j], j, first)
    for j in range(EXPERTS_PER_GROUP):
        last = jnp.where(picked[j], j, last)
    a_first, a_last = av[0], av[0]
    for j in range(1, EXPERTS_PER_GROUP):
        a_first = jnp.where(first == j, av[j], a_first)
        a_last = jnp.where(last == j, av[j], a_last)
    tot = a_first + a_last
    idx = jnp.concatenate([best * EXPERTS_PER_GROUP + first, best * EXPERTS_PER_GROUP + last], axis=0)
    wts = jnp.concatenate([a_first / tot, a_last / tot], axis=0)
    return idx, wts


def _outproj_kernel(h_ref, ret_ref, ssd_ref, na_ref, wo_ref, mod_ref, nw_ref, rwt_ref, rb_ref,
                    hn_ref, f_ref, idx_ref, wts_ref):
    rw, sw = ret_ref.shape[2], ssd_ref.shape[2]
    acc = jnp.dot(ret_ref[0], wo_ref[0:rw, :], preferred_element_type=F32)
    acc += jnp.dot(ssd_ref[0], wo_ref[rw:rw + sw, :], preferred_element_type=F32)
    acc += jnp.dot(na_ref[0], wo_ref[rw + sw:, :], preferred_element_type=F32)
    hn = h_ref[0] + mod_ref[0, 2:3, :] * acc
    hn_ref[0] = hn
    f = hn * lax.rsqrt(jnp.mean(hn * hn, axis=-1, keepdims=True) + EPS) * nw_ref[...]
    f = f * (1.0 + mod_ref[0, 4:5, :]) + mod_ref[0, 3:4, :]
    f_ref[0] = f.astype(BF16)
    logits = _nt_dot(rwt_ref[...], f, precision=HIGHEST)
    aff = 1.0 / (1.0 + jnp.exp(-logits))
    idx, wts = _route(aff, aff + rb_ref[...])
    idx_ref[0] = idx
    wts_ref[0] = wts


def _outproj(h, ret_o, ssd_o, na_o, w_out, mod, mod_row, norm_w, rw_t, rb):
    bsz, ltot, d = h.shape
    row = lambda n: pl.BlockSpec((1, TB, n), lambda b, j: (b, j, 0))
    const = lambda shape: pl.BlockSpec(shape, lambda b, j: (0,) * len(shape))
    col = lambda: pl.BlockSpec((1, TOP_K, TB), lambda b, j: (b, 0, j))
    return pl.pallas_call(
        _outproj_kernel,
        grid=(bsz, ltot // TB),
        in_specs=[
            row(d), row(ret_o.shape[2]), row(ssd_o.shape[2]), row(na_o.shape[2]),
            const(w_out.shape),
            pl.BlockSpec((1, 6, d), lambda b, j: (mod_row(b, j), 0, 0)),
            const((1, d)), const(rw_t.shape), const(rb.shape),
        ],
        out_specs=[row(d), row(d), col(), col()],
        out_shape=[
            jax.ShapeDtypeStruct((bsz, ltot, d), F32),
            jax.ShapeDtypeStruct((bsz, ltot, d), BF16),
            jax.ShapeDtypeStruct((bsz, TOP_K, ltot), jnp.int32),
            jax.ShapeDtypeStruct((bsz, TOP_K, ltot), F32),
        ],
        compiler_params=_cparams("parallel", "parallel"),
        name="outproj_router",
    )(h, ret_o, ssd_o, na_o, w_out, mod, norm_w, rw_t, rb)


RUN_ALIGN = SUBLANES
RUN_BITS = 6
PERM_ROWS = TOP_K * TB + N_EXPERTS * RUN_ALIGN
GATE_LANES = 128
assert TB == RUN_ALIGN << (RUN_BITS - 1)

U32 = jnp.uint32


def _pack_bf16_pairs(y):
    n = y.shape[1] // 2
    lo = lax.bitcast_convert_type(y[:, :n].astype(BF16).astype(F32), U32) >> 16
    hi = lax.bitcast_convert_type(y[:, n:].astype(BF16).astype(F32), U32) & jnp.uint32(0xFFFF0000)
    return lo | hi


def _unpack_bf16_pairs(u):
    lo = lax.bitcast_convert_type(u << 16, F32).astype(BF16)
    hi = lax.bitcast_convert_type(u & jnp.uint32(0xFFFF0000), F32).astype(BF16)
    return lo, hi


def _block_perm(idx_ref, loff_ref):
    tb = idx_ref.shape[2]
    na = TOP_K * tb
    e_iota = lax.broadcasted_iota(jnp.int32, (N_EXPERTS, tb), 0)
    oh = jnp.concatenate([jnp.where(e_iota == idx_ref[0, k:k + 1, :], 1.0, 0.0) for k in range(TOP_K)], axis=1)
    upper = lax.broadcasted_iota(jnp.int32, (na, na), 0) <= lax.broadcasted_iota(jnp.int32, (na, na), 1)
    cum = jnp.dot(oh.astype(BF16), jnp.where(upper, 1.0, 0.0).astype(BF16), preferred_element_type=F32)
    pos = jnp.sum(oh * (cum - 1.0 + loff_ref[0]), axis=0, keepdims=True).astype(jnp.int32)
    r = lax.broadcasted_iota(jnp.int32, (PERM_ROWS, tb), 0)
    return [r == pos[:, k * tb:(k + 1) * tb] for k in range(TOP_K)]


def _for_each_run_piece(blk, hoff_ref, loff_ref, ngr_ref, fn):
    for e in range(N_EXPERTS):
        j = blk * N_EXPERTS + e
        n, lo, ho = ngr_ref[j], loff_ref[j], hoff_ref[j]
        for b in range(RUN_BITS - 1, -1, -1):
            start = ((n >> (b + 1)) << (b + 1)) * RUN_ALIGN

            @pl.when(((n >> b) & 1) == 1)
            def _():
                fn(pl.multiple_of(lo + start, RUN_ALIGN), pl.multiple_of(ho + start, RUN_ALIGN), RUN_ALIGN << b)


def _dispatch_kernel(hoff_ref, loffs_ref, ngr_ref, f_ref, idx_ref, wts_ref, loff_ref, xs_in_ref, xs_ref,
                     xperm, sem, *, nb):
    del xs_in_ref
    blk = pl.program_id(0) * nb + pl.program_id(1)
    q0, q1 = _block_perm(idx_ref, loff_ref)
    q = jnp.where(jnp.logical_or(q0, q1), 1.0, 0.0).astype(BF16)
    half = f_ref.shape[2] // 2
    xperm[:, :half] = _pack_bf16_pairs(jnp.dot(q, f_ref[0], preferred_element_type=F32))
    w = jnp.sum(jnp.where(q0, wts_ref[0, 0:1, :], 0.0) + jnp.where(q1, wts_ref[0, 1:2, :], 0.0),
                axis=1, keepdims=True)
    xperm[:, half:] = jnp.broadcast_to(lax.bitcast_convert_type(w, U32), (PERM_ROWS, GATE_LANES))

    def piece(lo, ho, rows):
        return pltpu.make_async_copy(xperm.at[pl.ds(lo, rows)], xs_ref.at[pl.ds(ho, rows)], sem)

    _for_each_run_piece(blk, hoff_ref, loffs_ref, ngr_ref, lambda lo, ho, rows: piece(lo, ho, rows).start())
    _for_each_run_piece(blk, hoff_ref, loffs_ref, ngr_ref, lambda lo, ho, rows: piece(lo, ho, rows).wait())


def _dispatch(f, idx, wts, loff_col, hoff, loffs, ngr, p_total):
    bsz, ltot, d = f.shape
    nb = ltot // TB
    words = d // 2 + GATE_LANES
    blk = lambda b, j, *_: (b, j, 0)
    grid_spec = pltpu.PrefetchScalarGridSpec(
        num_scalar_prefetch=3,
        grid=(bsz, nb),
        in_specs=[
            pl.BlockSpec((1, TB, d), blk),
            pl.BlockSpec((1, TOP_K, TB), lambda b, j, *_: (b, 0, j)),
            pl.BlockSpec((1, TOP_K, TB), lambda b, j, *_: (b, 0, j)),
            pl.BlockSpec((1, N_EXPERTS, 1), lambda b, j, *_: (b * nb + j, 0, 0)),
            pl.BlockSpec(memory_space=pl.ANY),
        ],
        out_specs=pl.BlockSpec(memory_space=pl.ANY),
        scratch_shapes=[pltpu.VMEM((PERM_ROWS, words), U32), pltpu.SemaphoreType.DMA(())],
    )
    return pl.pallas_call(
        functools.partial(_dispatch_kernel, nb=nb),
        grid_spec=grid_spec,
        out_shape=jax.ShapeDtypeStruct((p_total, words), U32),
        input_output_aliases={7: 0},
        compiler_params=_cparams("arbitrary", "arbitrary"),
        name="moe_dispatch",
    )(hoff, loffs, ngr, f, idx, wts, loff_col, jnp.zeros((p_total, words), U32))


def _moe_kernel(te_ref, nt_ref, x_ref, wg_ref, wu_ref, wd_ref, y_ref, wg_s, wu_s, wd_s):
    i = pl.program_id(0)
    fresh = jnp.logical_or(i == 0, te_ref[i] != te_ref[jnp.maximum(i - 1, 0)])

    @pl.when(fresh)
    def _():
        wg_s[...] = wg_ref[0].astype(BF16)
        wu_s[...] = wu_ref[0].astype(BF16)
        wd_s[...] = wd_ref[0].astype(BF16)

    @pl.when(i < nt_ref[0])
    def _():
        half = wg_s.shape[0] // 2
        lo, hi = _unpack_bf16_pairs(x_ref[:, :half])
        gate = lax.bitcast_convert_type(x_ref[:, half:half + 1], F32)
        g = (jnp.dot(lo, wg_s[:half, :], preferred_element_type=F32)
             + jnp.dot(hi, wg_s[half:, :], preferred_element_type=F32))
        u = (jnp.dot(lo, wu_s[:half, :], preferred_element_type=F32)
             + jnp.dot(hi, wu_s[half:, :], preferred_element_type=F32))
        he = (_silu(g) * u).astype(BF16)
        y_ref[...] = _pack_bf16_pairs(jnp.dot(he, wd_s[...], preferred_element_type=F32) * gate)

    @pl.when(i >= nt_ref[0])
    def _():
        y_ref[...] = jnp.zeros_like(y_ref)


def _moe_grouped(x_sorted, tile_expert, n_tiles_used, w_gate, w_up, w_down):
    p, words = x_sorted.shape
    _, d, de = w_gate.shape
    tm = MOE_TILE
    grid_spec = pltpu.PrefetchScalarGridSpec(
        num_scalar_prefetch=2,
        grid=(p // tm,),
        in_specs=[
            pl.BlockSpec((tm, words), lambda i, te, nt: (i, 0)),
            pl.BlockSpec((1, d, de), lambda i, te, nt: (te[i], 0, 0)),
            pl.BlockSpec((1, d, de), lambda i, te, nt: (te[i], 0, 0)),
            pl.BlockSpec((1, de, d), lambda i, te, nt: (te[i], 0, 0)),
        ],
        out_specs=pl.BlockSpec((tm, d // 2), lambda i, te, nt: (i, 0)),
        scratch_shapes=[pltpu.VMEM((d, de), BF16), pltpu.VMEM((d, de), BF16), pltpu.VMEM((de, d), BF16)],
    )
    return pl.pallas_call(
        _moe_kernel,
        grid_spec=grid_spec,
        out_shape=jax.ShapeDtypeStruct((p, d // 2), U32),
        compiler_params=_cparams("arbitrary"),
        name="moe_grouped",
    )(tile_expert, n_tiles_used, x_sorted, w_gate, w_up, w_down)


def _combine_kernel(hoff_ref, loffs_ref, ngr_ref, h_ref, idx_ref, loff_ref, mod_ref, ys_ref, o_ref,
                    yperm, sem, *, nb):
    blk = pl.program_id(0) * nb + pl.program_id(1)

    @pl.when(jnp.logical_and(pl.program_id(0) == 0, pl.program_id(1) == 0))
    def _():
        yperm[...] = jnp.zeros_like(yperm)

    def piece(lo, ho, rows):
        return pltpu.make_async_copy(ys_ref.at[pl.ds(ho, rows)], yperm.at[pl.ds(lo, rows)], sem)

    _for_each_run_piece(blk, hoff_ref, loffs_ref, ngr_ref, lambda lo, ho, rows: piece(lo, ho, rows).start())
    q0, q1 = _block_perm(idx_ref, loff_ref)
    q = jnp.where(jnp.logical_or(q0, q1), 1.0, 0.0).astype(BF16)
    _for_each_run_piece(blk, hoff_ref, loffs_ref, ngr_ref, lambda lo, ho, rows: piece(lo, ho, rows).wait())
    lo, hi = _unpack_bf16_pairs(yperm[...])
    y = jnp.concatenate([_tn_dot(q, lo), _tn_dot(q, hi)], axis=1)
    o_ref[0] = h_ref[0] + mod_ref[0, 5:6, :] * y


def _combine(h, idx, loff_col, mod, mod_row, y_sorted, hoff, loffs, ngr, n_out):
    bsz, ltot, d = h.shape
    nb = ltot // TB
    grid_spec = pltpu.PrefetchScalarGridSpec(
        num_scalar_prefetch=3,
        grid=(bsz, n_out // TB),
        in_specs=[
            pl.BlockSpec((1, TB, d), lambda b, j, *_: (b, j, 0)),
            pl.BlockSpec((1, TOP_K, TB), lambda b, j, *_: (b, 0, j)),
            pl.BlockSpec((1, N_EXPERTS, 1), lambda b, j, *_: (b * nb + j, 0, 0)),
            pl.BlockSpec((1, 6, d), lambda b, j, *_: (mod_row(b, j), 0, 0)),
            pl.BlockSpec(memory_space=pl.ANY),
        ],
        out_specs=pl.BlockSpec((1, TB, d), lambda b, j, *_: (b, j, 0)),
        scratch_shapes=[pltpu.VMEM((PERM_ROWS, d // 2), U32), pltpu.SemaphoreType.DMA(())],
    )
    return pl.pallas_call(
        functools.partial(_combine_kernel, nb=nb),
        grid_spec=grid_spec,
        out_shape=jax.ShapeDtypeStruct((bsz, n_out, d), F32),
        compiler_params=_cparams("arbitrary", "arbitrary"),
        name="moe_combine",
    )(hoff, loffs, ngr, h, idx, loff_col, mod, y_sorted)


def _moe_block(h, f, idx, wts, mod, mod_row, w_gate, w_up, w_down, n_out):
    bsz, ltot, d = h.shape
    nb = ltot // TB
    n_blocks = bsz * nb
    mt = MOE_TILE
    experts = jnp.arange(N_EXPERTS, dtype=jnp.int32)
    cnt = jnp.sum((idx.reshape(bsz, TOP_K, nb, TB, 1) == experts).astype(jnp.int32), axis=(1, 3))
    cnt = cnt.reshape(n_blocks, N_EXPERTS)
    run = ((cnt + RUN_ALIGN - 1) // RUN_ALIGN) * RUN_ALIGN
    loffs = jnp.cumsum(run, axis=1) - run
    region = ((jnp.sum(run, axis=0) + mt - 1) // mt) * mt
    ends = jnp.cumsum(region)
    hoff = (ends - region)[None, :] + jnp.cumsum(run, axis=0) - run
    p_total = ((TOP_K * bsz * ltot + n_blocks * N_EXPERTS * (RUN_ALIGN - 1)) // mt + N_EXPERTS + 1) * mt
    tile_start = jnp.arange(p_total // mt, dtype=jnp.int32) * mt
    tile_expert = jnp.minimum(jnp.sum((tile_start[:, None] >= ends[None, :]).astype(jnp.int32), axis=1),
                              N_EXPERTS - 1)
    n_used = (ends[-1] // mt).astype(jnp.int32).reshape(1)
    flat = lambda t: t.reshape(-1).astype(jnp.int32)
    hoff, loffs_flat, ngr = flat(hoff), flat(loffs), flat(run // RUN_ALIGN)
    loff_col = loffs.astype(F32).reshape(n_blocks, N_EXPERTS, 1)

    x_sorted = _dispatch(f, idx, wts, loff_col, hoff, loffs_flat, ngr, p_total)
    y_sorted = _moe_grouped(x_sorted, tile_expert, n_used, w_gate, w_up, w_down)
    return _combine(h, idx, loff_col, mod, mod_row, y_sorted, hoff, loffs_flat, ngr, n_out)


def _rope_tables(n_lat, n_ctx, n_heads):
    t = jnp.arange(n_lat)
    pos = jnp.stack([t // GRID_W, t % GRID_W], axis=-1).astype(F32)
    n_freq = HEAD_DIM // 4
    inv = 1.0 / (ROPE_BASE ** (jnp.arange(n_freq, dtype=F32) / n_freq))
    ang = pos[:, :, None] * inv
    cos, sin = jnp.cos(ang), jnp.sin(ang)
    cos_h = jnp.concatenate([cos[:, 0], cos[:, 0], cos[:, 1], cos[:, 1]], axis=-1)
    sin_h = jnp.concatenate([-sin[:, 0], sin[:, 0], -sin[:, 1], sin[:, 1]], axis=-1)
    cos_h = jnp.concatenate([cos_h, jnp.ones((n_ctx, HEAD_DIM), F32)], axis=0)
    sin_h = jnp.concatenate([sin_h, jnp.zeros((n_ctx, HEAD_DIM), F32)], axis=0)
    return jnp.tile(cos_h, (1, n_heads)), jnp.tile(sin_h, (1, n_heads))


def _swap_halves_perm(width):
    j = np.arange(width)
    nf = HEAD_DIM // 4
    return np.where((j % (2 * nf)) < nf, j + nf, j - nf)


def _pack_w_in(w_in, rw, sw, nw_):
    o = 0
    seg = {}
    for name, width in (("rq", rw), ("rk", rw), ("rv", rw), ("rg", rw), ("z", sw), ("x", sw),
                        ("b", SSD_GROUPS * SSD_STATE), ("c", SSD_GROUPS * SSD_STATE),
                        ("dt", 2 * (sw // HEAD_DIM)), ("nq", nw_), ("nk", nw_), ("nv", nw_)):
        seg[name] = w_in[:, o:o + width]
        o += width
    perm = _swap_halves_perm(rw)
    dt_pad = jnp.zeros((w_in.shape[0], 128 - seg["dt"].shape[1]), w_in.dtype)
    cols = [seg["rq"], seg["rk"], seg["rq"][:, perm], seg["rk"][:, perm], seg["rv"], seg["rg"],
            seg["z"], seg["x"], seg["b"], seg["c"], seg["nq"], seg["nk"], seg["nv"], seg["dt"], dt_pad]
    return jnp.concatenate(cols, axis=1).astype(BF16)


def kernel(x, c, ctx, c_ctx, w_mod, b_mod, norm_mix, norm_ffn, w_in, w_out, ret_decay_f, ret_decay_b,
           ssd_conv_w, ssd_conv_b, ssd_dt_bias_f, ssd_dt_bias_b, ssd_a_log_f, ssd_a_log_b, ssd_d, ssd_norm,
           na_q_norm, na_k_norm, na_rpb, router_w, router_b, w_gate, w_up, w_down):
    bsz, n_lat, d = x.shape
    n_ctx = ctx.shape[1]
    depth = w_mod.shape[0]
    rows = n_lat // GRID_W
    rw = ret_decay_f.shape[1] * HEAD_DIM
    sw = ssd_d.shape[1] * HEAD_DIM
    nw_ = na_rpb.shape[1] * HEAD_DIM
    dims = (rw, sw, nw_)
    assert n_lat % TB == 0 and n_ctx % TB == 0
    nbl = n_lat // TB

    n_cond = ((bsz + 1 + 7) // 8) * 8
    cond = jnp.zeros((n_cond, d), F32).at[:bsz].set(c).at[bsz].set(c_ctx)
    mod = _modulation(cond, w_mod, b_mod).reshape(depth, n_cond, 6, d)
    mod_row = lambda b, j: jnp.where(j < nbl, b, bsz)

    cos, sin = _rope_tables(n_lat, n_ctx, rw // HEAD_DIM)
    hm = jnp.asarray(np.kron(np.eye(nw_ // HEAD_DIM), np.full((HEAD_DIM, HEAD_DIM), 1.0 / HEAD_DIM)), F32)
    rw_t = router_w.T
    rb = router_b.reshape(-1, 1)

    h = jnp.concatenate([x, ctx], axis=1)
    for l in range(depth):
        last = l == depth - 1
        w_all = _pack_w_in(w_in[l], rw, sw, nw_)
        qkn = jnp.stack([jnp.tile(na_q_norm[l], nw_ // HEAD_DIM), jnp.tile(na_k_norm[l], nw_ // HEAD_DIM)])
        ret, z, xbc, dt, na = _inproj(h, mod[l], mod_row, norm_mix[l].reshape(1, d), w_all, cos, sin, qkn, hm, dims)
        ret_o = _retention(ret, jnp.concatenate([ret_decay_f[l], ret_decay_b[l]]).reshape(-1, 1, 1), nbl)
        ssd_o = _ssd(z, xbc, dt, ssd_conv_w[l], ssd_conv_b[l],
                     jnp.concatenate([ssd_dt_bias_f[l], ssd_dt_bias_b[l]]),
                     jnp.concatenate([ssd_a_log_f[l], ssd_a_log_b[l]]), ssd_d[l], ssd_norm[l], nbl)
        na_o = _na_attention(na, _na_bias_tables(na_rpb[l], rows), nbl)
        h, f, idx, wts = _outproj(h, ret_o, ssd_o, na_o, w_out[l].astype(BF16), mod[l], mod_row,
                                  norm_ffn[l].reshape(1, d), rw_t, rb)
        h = _moe_block(h, f, idx, wts, mod[l], mod_row, w_gate[l], w_up[l], w_down[l],
                       n_lat if last else n_lat + n_ctx)
    return h
```

```python
import functools

import jax
import jax.numpy as jnp
import numpy as np
from jax import lax
from jax.experimental import pallas as pl
from jax.experimental.pallas import tpu as pltpu

F32 = jnp.float32
BF16 = jnp.bfloat16
HIGHEST = lax.Precision.HIGHEST

GRID_W = 64
HEAD_DIM = 64
SSD_GROUPS = 2
SSD_STATE = 64
NA_WIN_ROWS = 8
NA_WIN_COLS = 16
N_GROUPS = 4
EXPERTS_PER_GROUP = 4
N_EXPERTS = N_GROUPS * EXPERTS_PER_GROUP
TOP_K = 2
ROPE_BASE = 10000.0
EPS = 1e-6

VMEM_LIMIT_BYTES = 56 * 1024 * 1024
TB = 256
NA_QROWS = TB // GRID_W
MOE_TILE = 512
SUBLANES = 8


def _cparams(*sem):
    return pltpu.CompilerParams(dimension_semantics=sem, vmem_limit_bytes=VMEM_LIMIT_BYTES)


def _silu(t):
    return t * (1.0 / (1.0 + jnp.exp(-t)))


def _nt_dot(a, b, **kw):
    return lax.dot_general(a, b, (((1,), (1,)), ((), ())), preferred_element_type=F32, **kw)


def _tn_dot(a, b):
    return lax.dot_general(a, b, (((0,), (0,)), ((), ())), preferred_element_type=F32)


def _mod_kernel(s_ref, w_ref, b_ref, o_ref):
    s = _silu(s_ref[...])
    o_ref[0] = jnp.dot(s, w_ref[0], precision=HIGHEST, preferred_element_type=F32) + b_ref[0]


def _modulation(cond, w_mod, b_mod):
    depth, d, n = w_mod.shape
    rows = cond.shape[0]
    tn = 1024
    return pl.pallas_call(
        _mod_kernel,
        grid=(depth, n // tn),
        in_specs=[
            pl.BlockSpec((rows, d), lambda l, j: (0, 0)),
            pl.BlockSpec((1, d, tn), lambda l, j: (l, 0, j)),
            pl.BlockSpec((1, 1, tn), lambda l, j: (l, 0, j)),
        ],
        out_specs=pl.BlockSpec((1, rows, tn), lambda l, j: (l, 0, j)),
        out_shape=jax.ShapeDtypeStruct((depth, rows, n), F32),
        compiler_params=_cparams("parallel", "parallel"),
        name="modulation",
    )(cond, w_mod, b_mod.reshape(depth, 1, n))


def _inproj_kernel(h_ref, mod_ref, nw_ref, w_ref, cos_ref, sin_ref, qkn_ref, hm_ref,
                   ret_ref, z_ref, xbc_ref, dt_ref, na_ref, *, dims):
    rw, sw, nw_ = dims
    h = h_ref[0]
    a = h * lax.rsqrt(jnp.mean(h * h, axis=-1, keepdims=True) + EPS) * nw_ref[...]
    a = a * (1.0 + mod_ref[0, 1:2, :]) + mod_ref[0, 0:1, :]
    ab = a.astype(BF16)

    def proj(lo, hi):
        return jnp.dot(ab, w_ref[:, lo:hi], preferred_element_type=F32)

    cos, sin = cos_ref[...], sin_ref[...]
    o = 0
    ret_ref[0, :, 0:rw] = proj(o, o + rw) * cos + proj(o + 2 * rw, o + 3 * rw) * sin
    ret_ref[0, :, rw:2 * rw] = (proj(o + rw, o + 2 * rw) * cos + proj(o + 3 * rw, o + 4 * rw) * sin) * (HEAD_DIM ** -0.5)
    ret_ref[0, :, 2 * rw:4 * rw] = proj(o + 4 * rw, o + 6 * rw)
    o += 6 * rw
    z_ref[0] = proj(o, o + sw)
    o += sw
    xbc_w = sw + 2 * SSD_GROUPS * SSD_STATE
    xbc_ref[0] = proj(o, o + xbc_w)
    o += xbc_w
    hm = hm_ref[...]
    for i in range(2):
        t = proj(o + i * nw_, o + (i + 1) * nw_)
        ms = jnp.dot(t * t, hm, precision=HIGHEST, preferred_element_type=F32)
        na_ref[0, :, i * nw_:(i + 1) * nw_] = (t * lax.rsqrt(ms + EPS) * qkn_ref[i:i + 1, :]).astype(BF16)
    na_ref[0, :, 2 * nw_:3 * nw_] = proj(o + 2 * nw_, o + 3 * nw_).astype(BF16)
    o += 3 * nw_
    dt_ref[0] = proj(o, o + 128)[:, 0:dt_ref.shape[2]]


def _inproj(h, mod, mod_row, norm_w, w_all, cos, sin, qkn, hm, dims):
    bsz, ltot, d = h.shape
    rw, sw, nw_ = dims
    n_dt = 2 * (sw // HEAD_DIM)
    xbc_w = sw + 2 * SSD_GROUPS * SSD_STATE
    row = lambda n: pl.BlockSpec((1, TB, n), lambda b, j: (b, j, 0))
    const = lambda shape: pl.BlockSpec(shape, lambda b, j: (0,) * len(shape))
    widths = (4 * rw, sw, xbc_w, n_dt, 3 * nw_)
    dtypes = (F32, F32, F32, F32, BF16)
    return pl.pallas_call(
        functools.partial(_inproj_kernel, dims=dims),
        grid=(bsz, ltot // TB),
        in_specs=[
            row(d),
            pl.BlockSpec((1, 6, d), lambda b, j: (mod_row(b, j), 0, 0)),
            const((1, d)),
            const(w_all.shape),
            pl.BlockSpec((TB, rw), lambda b, j: (j, 0)),
            pl.BlockSpec((TB, rw), lambda b, j: (j, 0)),
            const((2, nw_)),
            const((nw_, nw_)),
        ],
        out_specs=[row(n) for n in widths],
        out_shape=[jax.ShapeDtypeStruct((bsz, ltot, n), dt) for n, dt in zip(widths, dtypes)],
        compiler_params=_cparams("parallel", "parallel"),
        name="inproj",
    )(h, mod, norm_w, w_all, cos, sin, qkn, hm)


def _scan_chunk_maps(nb, nbl):
    fwd = lambda s: jnp.where(s < nb, (nbl + s) % nb, 2 * nb - 1 - s)
    out = lambda s: jnp.where(s < nb, nb - 1, 2 * nb - 1 - s)
    return fwd, out


def _ret_kernel(q_ref, k_ref, v_ref, g_ref, dec_ref, o_ref, sf_all, sf, sb, dmat, rd, *, nb, nbl):
    s = pl.program_id(1)
    c = TB
    nh = k_ref.shape[2] // HEAD_DIM
    log_f = [-jnp.exp(dec_ref[h]) for h in range(nh)]
    log_b = [-jnp.exp(dec_ref[nh + h]) for h in range(nh)]

    @pl.when(s == 0)
    def _init():
        sf[...] = jnp.zeros_like(sf)
        sb[...] = jnp.zeros_like(sb)
        delta = (lax.broadcasted_iota(jnp.int32, (c, c), 0) - lax.broadcasted_iota(jnp.int32, (c, c), 1)).astype(F32)
        pos = lax.broadcasted_iota(jnp.int32, (c, HEAD_DIM), 0).astype(F32)
        for h in range(nh):
            dmat[h] = (jnp.exp(jnp.where(delta >= 0, log_f[h] * delta, -jnp.inf))
                       + jnp.exp(jnp.where(delta <= 0, -log_b[h] * delta, -jnp.inf)))
            rd[4 * h + 0] = jnp.exp(log_f[h] * (pos + 1.0))
            rd[4 * h + 1] = jnp.exp(log_b[h] * (c - pos))
            rd[4 * h + 2] = jnp.exp(log_f[h] * (c - 1.0 - pos))
            rd[4 * h + 3] = jnp.exp(log_b[h] * pos)

    @pl.when(s < nb)
    def _state_sweep():
        ci = (nbl + s) % nb
        for h in range(nh):
            sl = slice(h * HEAD_DIM, (h + 1) * HEAD_DIM)
            k, v = k_ref[0, :, sl], v_ref[0, :, sl]
            sf_all[ci, h] = sf[h]
            sf[h] = sf[h] * jnp.exp(log_f[h] * c) + _tn_dot((k * rd[4 * h + 2]).astype(BF16), v.astype(BF16))

    @pl.when(s >= nb)
    def _output_sweep():
        ci = 2 * nb - 1 - s
        for h in range(nh):
            sl = slice(h * HEAD_DIM, (h + 1) * HEAD_DIM)
            q, k, v = q_ref[0, :, sl], k_ref[0, :, sl], v_ref[0, :, sl]
            qb, vb = q.astype(BF16), v.astype(BF16)
            p = _nt_dot(qb, k.astype(BF16)) * dmat[h]
            y = jnp.dot(p.astype(BF16), vb, preferred_element_type=F32)
            y += jnp.dot(qb, sf_all[ci, h].astype(BF16), preferred_element_type=F32) * rd[4 * h + 0]
            y += jnp.dot(qb, sb[h].astype(BF16), preferred_element_type=F32) * rd[4 * h + 1]
            sb[h] = sb[h] * jnp.exp(log_b[h] * c) + _tn_dot((k * rd[4 * h + 3]).astype(BF16), vb)
            mu = jnp.mean(y, axis=-1, keepdims=True)
            yc = y - mu
            yn = yc * lax.rsqrt(jnp.mean(yc * yc, axis=-1, keepdims=True) + EPS)
            o_ref[0, :, sl] = (yn * _silu(g_ref[0, :, sl])).astype(o_ref.dtype)


def _retention(ret, decay, nbl):
    bsz, ltot, w4 = ret.shape
    w = w4 // 4
    nh = w // HEAD_DIM
    nb = ltot // TB
    fwd, out = _scan_chunk_maps(nb, nbl)
    return pl.pallas_call(
        functools.partial(_ret_kernel, nb=nb, nbl=nbl),
        grid=(bsz, 2 * nb),
        in_specs=[
            pl.BlockSpec((1, TB, w), lambda b, s: (b, out(s), 0)),
            pl.BlockSpec((1, TB, w), lambda b, s: (b, fwd(s), 1)),
            pl.BlockSpec((1, TB, w), lambda b, s: (b, fwd(s), 2)),
            pl.BlockSpec((1, TB, w), lambda b, s: (b, out(s), 3)),
            pl.BlockSpec(decay.shape, lambda b, s: (0, 0, 0)),
        ],
        out_specs=pl.BlockSpec((1, TB, w), lambda b, s: (b, out(s), 0)),
        out_shape=jax.ShapeDtypeStruct((bsz, ltot, w), BF16),
        scratch_shapes=[
            pltpu.VMEM((nb, nh, HEAD_DIM, HEAD_DIM), F32),
            pltpu.VMEM((nh, HEAD_DIM, HEAD_DIM), F32),
            pltpu.VMEM((nh, HEAD_DIM, HEAD_DIM), F32),
            pltpu.VMEM((nh, TB, TB), F32),
            pltpu.VMEM((4 * nh, TB, HEAD_DIM), F32),
        ],
        compiler_params=_cparams("parallel", "arbitrary"),
        name="retention",
    )(ret, ret, ret, ret, decay)


def _softplus(t):
    return jnp.maximum(t, 0.0) + jnp.log1p(jnp.exp(-jnp.abs(t)))


def _split_dot(x, sel):
    hi = x.astype(BF16)
    rest = x - hi.astype(F32)
    mid = rest.astype(BF16)
    lo = (rest - mid.astype(F32)).astype(BF16)
    dot = lambda t: jnp.dot(t, sel, preferred_element_type=F32)
    return dot(hi) + dot(mid) + dot(lo)


def _lane_group_selector(n_rows, group):
    shape = (n_rows, n_rows * group)
    lane_owner = lax.broadcasted_iota(jnp.int32, shape, 1) // group
    return jnp.where(lane_owner == lax.broadcasted_iota(jnp.int32, shape, 0), 1.0, 0.0).astype(BF16)


def _ssd_kernel(z_ref, x_ref, xp_ref, xn_ref, dt_ref, cw_ref, cb_ref, dtb_ref, alog_ref, dtbt_ref, alogt_ref,
                dsk_ref, nw_ref, o_ref, sf_all, sf, sb, tri, ybuf, *, nb, nbl):
    s = pl.program_id(1)
    c = TB
    sw = z_ref.shape[2]
    nh = sw // HEAD_DIM
    rep = nh // SSD_GROUPS
    gw = rep * HEAD_DIM
    gs = SSD_GROUPS * SSD_STATE
    ci = jnp.where(s < nb, (nbl + s) % nb, 2 * nb - 1 - s)

    @pl.when(s == 0)
    def _init():
        sf[...] = jnp.zeros_like(sf)
        sb[...] = jnp.zeros_like(sb)
        li = lax.broadcasted_iota(jnp.int32, (c, c), 0)
        si = lax.broadcasted_iota(jnp.int32, (c, c), 1)
        tri[0] = (si <= li).astype(F32)
        tri[1] = (si >= li).astype(F32)

    x = x_ref[0]
    has_prev = jnp.where(jnp.logical_or(ci == 0, ci == nbl), 0.0, 1.0)
    has_next = jnp.where(jnp.logical_or(ci == nbl - 1, ci == nb - 1), 0.0, 1.0)
    rows = lax.broadcasted_iota(jnp.int32, x.shape, 0)
    x_dn = jnp.where(rows == 0, xp_ref[0, SUBLANES - 1:SUBLANES, :] * has_prev, pltpu.roll(x, 1, 0))
    x_up = jnp.where(rows == c - 1, xn_ref[0, 0:1, :] * has_next, pltpu.roll(x, c - 1, 0))
    xc = _silu(cw_ref[0:1, :] * x_dn + cw_ref[1:2, :] * x + cw_ref[2:3, :] * x_up + cb_ref[...])
    xs, bm, cm = xc[:, :sw], xc[:, sw:sw + gs], xc[:, sw + gs:]
    bm_t = bm.T.astype(BF16)

    dtr = dt_ref[0]
    dt = _softplus(dtr + dtb_ref[...])
    a = dt * -jnp.exp(alog_ref[...])
    pre = jnp.dot(tri[0], a, precision=HIGHEST, preferred_element_type=F32)
    tot = pre[c - 1:c, :]
    is_fwd = lax.broadcasted_iota(jnp.int32, (1, 2 * nh), 1) < nh
    acs = jnp.where(is_fwd, pre, tot - pre + a)
    sel_head = _lane_group_selector(2 * nh, HEAD_DIM)
    w_wide = _split_dot(dt * jnp.exp(tot - acs), sel_head)
    etot_wide = _split_dot(jnp.broadcast_to(jnp.exp(tot), (SUBLANES, 2 * nh)), sel_head)[0:1, :]

    @pl.when(s < nb)
    def _state_sweep():
        for g in range(SSD_GROUPS):
            lanes = slice(g * gw, (g + 1) * gw)
            xw = (xs[:, lanes] * w_wide[:, lanes]).astype(BF16)
            sf_all[ci, g] = sf[g]
            sf[g] = sf[g] * etot_wide[:, lanes] + jnp.dot(bm_t[g * SSD_STATE:(g + 1) * SSD_STATE, :], xw,
                                                          preferred_element_type=F32)

    @pl.when(s >= nb)
    def _output_sweep():
        eye = (lax.broadcasted_iota(jnp.int32, (2 * nh, 2 * nh), 0)
               == lax.broadcasted_iota(jnp.int32, (2 * nh, 2 * nh), 1)).astype(F32)
        dt_t = _softplus(_nt_dot(eye, dtr, precision=HIGHEST) + dtbt_ref[...])
        a_t = dt_t * -jnp.exp(alogt_ref[...])
        pre_t = jnp.dot(a_t, tri[1], precision=HIGHEST, preferred_element_type=F32)
        is_fwd_t = lax.broadcasted_iota(jnp.int32, (2 * nh, 1), 0) < nh
        acs_t = jnp.where(is_fwd_t, pre_t, pre_t[:, c - 1:c] - pre_t + a_t)
        e_wide = _split_dot(jnp.exp(acs), sel_head)
        acs_bc = _split_dot(acs, _lane_group_selector(2 * nh, 128))
        li = lax.broadcasted_iota(jnp.int32, (c, c), 0)
        si = lax.broadcasted_iota(jnp.int32, (c, c), 1)
        lower, upper = li >= si, si >= li
        for g in range(SSD_GROUPS):
            lanes = slice(g * gw, (g + 1) * gw)
            lanes_b = slice(sw + g * gw, sw + (g + 1) * gw)
            cg = cm[:, g * SSD_STATE:(g + 1) * SSD_STATE].astype(BF16)
            bg_t = bm_t[g * SSD_STATE:(g + 1) * SSD_STATE, :]
            scores = jnp.dot(cg, bg_t, preferred_element_type=F32)
            for r in range(rep):
                h = g * rep + r
                hb = nh + h
                col_f = jnp.concatenate([acs_bc[:, h * 128:(h + 1) * 128]] * (c // 128), axis=1)
                col_b = jnp.concatenate([acs_bc[:, hb * 128:(hb + 1) * 128]] * (c // 128), axis=1)
                d_f = jnp.exp(jnp.where(lower, col_f - acs_t[h:h + 1, :], -jnp.inf)) * dt_t[h:h + 1, :]
                d_b = jnp.exp(jnp.where(upper, col_b - acs_t[hb:hb + 1, :], -jnp.inf)) * dt_t[hb:hb + 1, :]
                ybuf[:, h * HEAD_DIM:(h + 1) * HEAD_DIM] = jnp.dot(
                    (scores * (d_f + d_b)).astype(BF16), xs[:, h * HEAD_DIM:(h + 1) * HEAD_DIM].astype(BF16),
                    preferred_element_type=F32)
            ybuf[:, lanes] += (jnp.dot(cg, sf_all[ci, g].astype(BF16), preferred_element_type=F32) * e_wide[:, lanes]
                               + jnp.dot(cg, sb[g].astype(BF16), preferred_element_type=F32) * e_wide[:, lanes_b])
            xw = (xs[:, lanes] * w_wide[:, lanes_b]).astype(BF16)
            sb[g] = sb[g] * etot_wide[:, lanes_b] + jnp.dot(bg_t, xw, preferred_element_type=F32)
        y = (ybuf[...] + dsk_ref[...] * xs) * _silu(z_ref[0])
        o_ref[0] = (y * lax.rsqrt(jnp.mean(y * y, axis=-1, keepdims=True) + EPS) * nw_ref[...]).astype(o_ref.dtype)


def _ssd(z, xbc, dt, conv_w, conv_b, dtb, alog, d_skip, norm_w, nbl):
    bsz, ltot, sw = z.shape
    nh = sw // HEAD_DIM
    xw = xbc.shape[2]
    nb = ltot // TB
    per = TB // SUBLANES
    fwd, out = _scan_chunk_maps(nb, nbl)
    const = lambda t: pl.BlockSpec(t.shape, lambda b, s: (0,) * t.ndim)
    args = (conv_w, conv_b.reshape(1, xw), dtb.reshape(1, 2 * nh), alog.reshape(1, 2 * nh),
            dtb.reshape(2 * nh, 1), alog.reshape(2 * nh, 1), jnp.repeat(d_skip, HEAD_DIM).reshape(1, sw),
            norm_w.reshape(1, sw))
    return pl.pallas_call(
        functools.partial(_ssd_kernel, nb=nb, nbl=nbl),
        grid=(bsz, 2 * nb),
        in_specs=[
            pl.BlockSpec((1, TB, sw), lambda b, s: (b, out(s), 0)),
            pl.BlockSpec((1, TB, xw), lambda b, s: (b, fwd(s), 0)),
            pl.BlockSpec((1, SUBLANES, xw), lambda b, s: (b, jnp.maximum(fwd(s) * per - 1, 0), 0)),
            pl.BlockSpec((1, SUBLANES, xw), lambda b, s: (b, jnp.minimum((fwd(s) + 1) * per, nb * per - 1), 0)),
            pl.BlockSpec((1, TB, 2 * nh), lambda b, s: (b, fwd(s), 0)),
        ] + [const(t) for t in args],
        out_specs=pl.BlockSpec((1, TB, sw), lambda b, s: (b, out(s), 0)),
        out_shape=jax.ShapeDtypeStruct((bsz, ltot, sw), BF16),
        scratch_shapes=[
            pltpu.VMEM((nb, SSD_GROUPS, SSD_STATE, sw // SSD_GROUPS), F32),
            pltpu.VMEM((SSD_GROUPS, SSD_STATE, sw // SSD_GROUPS), F32),
            pltpu.VMEM((SSD_GROUPS, SSD_STATE, sw // SSD_GROUPS), F32),
            pltpu.VMEM((2, TB, TB), F32),
            pltpu.VMEM((TB, sw), F32),
        ],
        compiler_params=_cparams("parallel", "arbitrary"),
        name="ssd",
    )(z, xbc, xbc, xbc, dt, *args)


def _softmax_pv(s_parts, v):
    m = s_parts[0].max(axis=-1, keepdims=True)
    for s in s_parts[1:]:
        m = jnp.maximum(m, s.max(axis=-1, keepdims=True))
    p_parts = [jnp.exp(s - m) for s in s_parts]
    den = p_parts[0].sum(axis=-1, keepdims=True)
    for p in p_parts[1:]:
        den = den + p.sum(axis=-1, keepdims=True)
    p = p_parts[0] if len(p_parts) == 1 else jnp.concatenate(p_parts, axis=1)
    return jnp.dot(p.astype(BF16), v, preferred_element_type=F32) / den


def _na_kernel(q_ref, k0_ref, k1_ref, k2_ref, v0_ref, v1_ref, v2_ref, kc_ref, vc_ref, bias_ref, o_ref, *, nbl):
    nh = q_ref.shape[2] // HEAD_DIM
    n_lat_keys = 3 * k0_ref.shape[1]
    j = pl.program_id(1)

    @pl.when(j < nbl)
    def _latent():
        for h in range(nh):
            sl = slice(h * HEAD_DIM, (h + 1) * HEAD_DIM)
            q = q_ref[0, :, sl] * (HEAD_DIM ** -0.5)
            k = jnp.concatenate([k0_ref[0, :, sl], k1_ref[0, :, sl], k2_ref[0, :, sl], kc_ref[0, :, sl]], axis=0)
            v = jnp.concatenate([v0_ref[0, :, sl], v1_ref[0, :, sl], v2_ref[0, :, sl], vc_ref[0, :, sl]], axis=0)
            s = _nt_dot(q, k)
            o = _softmax_pv([s[:, :n_lat_keys] + bias_ref[0, h], s[:, n_lat_keys:]], v)
            o_ref[0, :, sl] = o.astype(o_ref.dtype)

    @pl.when(j >= nbl)
    def _context():
        for h in range(nh):
            sl = slice(h * HEAD_DIM, (h + 1) * HEAD_DIM)
            s = _nt_dot(q_ref[0, :, sl] * (HEAD_DIM ** -0.5), kc_ref[0, :, sl])
            o_ref[0, :, sl] = _softmax_pv([s], vc_ref[0, :, sl]).astype(o_ref.dtype)


def _na_bias_tables(rpb, rows):
    qb, kb = NA_QROWS, 3 * NA_QROWS
    nb = rows // qb
    nh = rpb.shape[0]
    n_dr, n_dc = 2 * NA_WIN_ROWS - 1, 2 * NA_WIN_COLS - 1
    c = np.arange(GRID_W)[:, None]
    kc = np.arange(GRID_W)[None, :]
    cs = np.clip(c - NA_WIN_COLS // 2, 0, GRID_W - NA_WIN_COLS)
    col_ok = (kc >= cs) & (kc < cs + NA_WIN_COLS)
    dc = np.clip(kc - c + NA_WIN_COLS - 1, 0, n_dc - 1)
    col_sel = np.eye(n_dc, dtype=np.float32)[dc.reshape(-1)].T
    tabs = []
    for rbq, wb in ((0, 0), (1, 0), (nb - 1, nb - 3)):
        r = qb * rbq + np.arange(qb)[:, None]
        kr = qb * wb + np.arange(kb)[None, :]
        r0 = np.clip(r - NA_WIN_ROWS // 2, 0, rows - NA_WIN_ROWS)
        row_ok = (kr >= r0) & (kr < r0 + NA_WIN_ROWS)
        dr = np.clip(kr - r + NA_WIN_ROWS - 1, 0, n_dr - 1)
        row_sel = np.eye(n_dr, dtype=np.float32)[dr.reshape(-1)]
        t = jnp.einsum("ad,hde,ef->haf", row_sel, rpb.astype(F32), col_sel, precision=HIGHEST)
        t = t.reshape(nh, qb, kb, GRID_W, GRID_W).transpose(0, 1, 3, 2, 4).reshape(nh, qb * GRID_W, kb * GRID_W)
        valid = (row_ok[:, None, :, None] & col_ok[None, :, None, :]).reshape(qb * GRID_W, kb * GRID_W)
        tabs.append(jnp.where(valid, t, -jnp.inf))
    return jnp.stack(tabs)


def _na_attention(na, bias, nbl):
    bsz, ltot, w3 = na.shape
    w = w3 // 3
    nb = ltot // TB
    assert nbl >= 3 and nb == nbl + 1, "needs >= 12 grid rows and a context of one token block"
    win = lambda j: jnp.clip(j - 1, 0, nbl - 3)
    kv = lambda i, col: pl.BlockSpec((1, TB, w), lambda b, j: (b, win(j) + i, col))
    typ = lambda j: jnp.where(j == 0, 0, jnp.where(j >= nbl - 1, 2, 1))
    return pl.pallas_call(
        functools.partial(_na_kernel, nbl=nbl),
        grid=(bsz, nb),
        in_specs=[
            pl.BlockSpec((1, TB, w), lambda b, j: (b, j, 0)),
            kv(0, 1), kv(1, 1), kv(2, 1), kv(0, 2), kv(1, 2), kv(2, 2),
            pl.BlockSpec((1, TB, w), lambda b, j: (b, nbl, 1)),
            pl.BlockSpec((1, TB, w), lambda b, j: (b, nbl, 2)),
            pl.BlockSpec((1,) + bias.shape[1:], lambda b, j: (typ(j), 0, 0, 0)),
        ],
        out_specs=pl.BlockSpec((1, TB, w), lambda b, j: (b, j, 0)),
        out_shape=jax.ShapeDtypeStruct((bsz, ltot, w), BF16),
        compiler_params=_cparams("parallel", "arbitrary"),
        name="na_attention",
    )(na, na, na, na, na, na, na, na, na, bias)


def _route(aff, sel):
    rows = [sel[e:e + 1, :] for e in range(N_EXPERTS)]
    arow = [aff[e:e + 1, :] for e in range(N_EXPERTS)]
    gscore = []
    for g in range(N_GROUPS):
        a, b, c, d = rows[4 * g:4 * g + 4]
        hi1, lo1 = jnp.maximum(a, b), jnp.minimum(a, b)
        hi2, lo2 = jnp.maximum(c, d), jnp.minimum(c, d)
        gscore.append(jnp.maximum(hi1, hi2) + jnp.maximum(jnp.minimum(hi1, hi2), jnp.maximum(lo1, lo2)))
    best = jnp.zeros_like(gscore[0], dtype=jnp.int32)
    top = gscore[0]
    for g in range(1, N_GROUPS):
        upd = gscore[g] > top
        best = jnp.where(upd, g, best)
        top = jnp.where(upd, gscore[g], top)
    sv, av = [], []
    for j in range(EXPERTS_PER_GROUP):
        s_j, a_j = rows[j], arow[j]
        for g in range(1, N_GROUPS):
            s_j = jnp.where(best == g, rows[4 * g + j], s_j)
            a_j = jnp.where(best == g, arow[4 * g + j], a_j)
        sv.append(s_j)
        av.append(a_j)
    picked = []
    for j in range(EXPERTS_PER_GROUP):
        rank = jnp.zeros_like(best)
        for i in range(EXPERTS_PER_GROUP):
            if i == j:
                continue
            ahead = (sv[i] >= sv[j]) if i < j else (sv[i] > sv[j])
            rank = rank + ahead.astype(jnp.int32)
        picked.append(rank < TOP_K)
    first = jnp.full_like(best, EXPERTS_PER_GROUP - 1)
    last = jnp.zeros_like(best)
    for j in range(EXPERTS_PER_GROUP - 1, -1, -1):
        first = jnp.where(picked[j], j, first)
    for j in range(EXPERTS_PER_GROUP):
        last = jnp.where(picked[j], j, last)
    a_first, a_last = av[0], av[0]
    for j in range(1, EXPERTS_PER_GROUP):
        a_first = jnp.where(first == j, av[j], a_first)
        a_last = jnp.where(last == j, av[j], a_last)
    tot = a_first + a_last
    idx = jnp.concatenate([best * EXPERTS_PER_GROUP + first, best * EXPERTS_PER_GROUP + last], axis=0)
    wts = jnp.concatenate([a_first / tot, a_last / tot], axis=0)
    return idx, wts


def _outproj_kernel(h_ref, ret_ref, ssd_ref, na_ref, wo_ref, mod_ref, nw_ref, rwt_ref, rb_ref,
                    hn_ref, f_ref, idx_ref, wts_ref):
    rw, sw = ret_ref.shape[2], ssd_ref.shape[2]
    acc = jnp.dot(ret_ref[0], wo_ref[0:rw, :], preferred_element_type=F32)
    acc += jnp.dot(ssd_ref[0], wo_ref[rw:rw + sw, :], preferred_element_type=F32)
    acc += jnp.dot(na_ref[0], wo_ref[rw + sw:, :], preferred_element_type=F32)
    hn = h_ref[0] + mod_ref[0, 2:3, :] * acc
    hn_ref[0] = hn
    f = hn * lax.rsqrt(jnp.mean(hn * hn, axis=-1, keepdims=True) + EPS) * nw_ref[...]
    f = f * (1.0 + mod_ref[0, 4:5, :]) + mod_ref[0, 3:4, :]
    f_ref[0] = f.astype(BF16)
    logits = jnp.dot(f, rwt_ref[...], precision=HIGHEST, preferred_element_type=F32).T[:N_EXPERTS, :]
    aff = 1.0 / (1.0 + jnp.exp(-logits))
    idx, wts = _route(aff, aff + rb_ref[...])
    idx_ref[0] = idx
    wts_ref[0] = wts


def _outproj(h, ret_o, ssd_o, na_o, w_out, mod, mod_row, norm_w, rw_t, rb):
    bsz, ltot, d = h.shape
    row = lambda n: pl.BlockSpec((1, TB, n), lambda b, j: (b, j, 0))
    const = lambda shape: pl.BlockSpec(shape, lambda b, j: (0,) * len(shape))
    col = lambda: pl.BlockSpec((1, TOP_K, TB), lambda b, j: (b, 0, j))
    return pl.pallas_call(
        _outproj_kernel,
        grid=(bsz, ltot // TB),
        in_specs=[
            row(d), row(ret_o.shape[2]), row(ssd_o.shape[2]), row(na_o.shape[2]),
            const(w_out.shape),
            pl.BlockSpec((1, 6, d), lambda b, j: (mod_row(b, j), 0, 0)),
            const((1, d)), const(rw_t.shape), const(rb.shape),
        ],
        out_specs=[row(d), row(d), col(), col()],
        out_shape=[
            jax.ShapeDtypeStruct((bsz, ltot, d), F32),
            jax.ShapeDtypeStruct((bsz, ltot, d), BF16),
            jax.ShapeDtypeStruct((bsz, TOP_K, ltot), jnp.int32),
            jax.ShapeDtypeStruct((bsz, TOP_K, ltot), F32),
        ],
        compiler_params=_cparams("parallel", "parallel"),
        name="outproj_router",
    )(h, ret_o, ssd_o, na_o, w_out, mod, norm_w, rw_t, rb)


RUN_ALIGN = SUBLANES
RUN_BITS = 6
PERM_ROWS = TOP_K * TB + N_EXPERTS * RUN_ALIGN
GATE_LANES = 128
assert TB == RUN_ALIGN << (RUN_BITS - 1)

U32 = jnp.uint32


def _pack_bf16_pairs(y):
    n = y.shape[1] // 2
    lo = lax.bitcast_convert_type(y[:, :n].astype(BF16).astype(F32), U32) >> 16
    hi = lax.bitcast_convert_type(y[:, n:].astype(BF16).astype(F32), U32) & jnp.uint32(0xFFFF0000)
    return lo | hi


def _unpack_bf16_pairs(u):
    lo = lax.bitcast_convert_type(u << 16, F32).astype(BF16)
    hi = lax.bitcast_convert_type(u & jnp.uint32(0xFFFF0000), F32).astype(BF16)
    return lo, hi


def _block_perm(idx_ref, loff_ref):
    tb = idx_ref.shape[2]
    na = TOP_K * tb
    e_iota = lax.broadcasted_iota(jnp.int32, (N_EXPERTS, tb), 0)
    oh = jnp.concatenate([jnp.where(e_iota == idx_ref[0, k:k + 1, :], 1.0, 0.0) for k in range(TOP_K)], axis=1)
    upper = lax.broadcasted_iota(jnp.int32, (na, na), 0) <= lax.broadcasted_iota(jnp.int32, (na, na), 1)
    cum = jnp.dot(oh.astype(BF16), jnp.where(upper, 1.0, 0.0).astype(BF16), preferred_element_type=F32)
    pos = jnp.sum(oh * (cum - 1.0 + loff_ref[0]), axis=0, keepdims=True).astype(jnp.int32)
    r = lax.broadcasted_iota(jnp.int32, (PERM_ROWS, tb), 0)
    return [r == pos[:, k * tb:(k + 1) * tb] for k in range(TOP_K)]


def _for_each_run_piece(blk, hoff_ref, loff_ref, ngr_ref, fn):
    for e in range(N_EXPERTS):
        j = blk * N_EXPERTS + e
        n, lo, ho = ngr_ref[j], loff_ref[j], hoff_ref[j]
        for b in range(RUN_BITS - 1, -1, -1):
            start = ((n >> (b + 1)) << (b + 1)) * RUN_ALIGN

            @pl.when(((n >> b) & 1) == 1)
            def _():
                fn(pl.multiple_of(lo + start, RUN_ALIGN), pl.multiple_of(ho + start, RUN_ALIGN), RUN_ALIGN << b)


def _dispatch_kernel(hoff_ref, loffs_ref, ngr_ref, f_ref, idx_ref, wts_ref, loff_ref, xs_in_ref, xs_ref,
                     xperm, sem, *, nb):
    del xs_in_ref
    blk = pl.program_id(0) * nb + pl.program_id(1)
    q0, q1 = _block_perm(idx_ref, loff_ref)
    q = jnp.where(jnp.logical_or(q0, q1), 1.0, 0.0).astype(BF16)
    half = f_ref.shape[2] // 2
    xperm[:, :half] = _pack_bf16_pairs(jnp.dot(q, f_ref[0], preferred_element_type=F32))
    w = jnp.sum(jnp.where(q0, wts_ref[0, 0:1, :], 0.0) + jnp.where(q1, wts_ref[0, 1:2, :], 0.0),
                axis=1, keepdims=True)
    xperm[:, half:] = jnp.broadcast_to(lax.bitcast_convert_type(w, U32), (PERM_ROWS, GATE_LANES))

    def piece(lo, ho, rows):
        return pltpu.make_async_copy(xperm.at[pl.ds(lo, rows)], xs_ref.at[pl.ds(ho, rows)], sem)

    _for_each_run_piece(blk, hoff_ref, loffs_ref, ngr_ref, lambda lo, ho, rows: piece(lo, ho, rows).start())
    _for_each_run_piece(blk, hoff_ref, loffs_ref, ngr_ref, lambda lo, ho, rows: piece(lo, ho, rows).wait())


def _dispatch(f, idx, wts, loff_col, hoff, loffs, ngr, p_total):
    bsz, ltot, d = f.shape
    nb = ltot // TB
    words = d // 2 + GATE_LANES
    blk = lambda b, j, *_: (b, j, 0)
    grid_spec = pltpu.PrefetchScalarGridSpec(
        num_scalar_prefetch=3,
        grid=(bsz, nb),
        in_specs=[
            pl.BlockSpec((1, TB, d), blk),
            pl.BlockSpec((1, TOP_K, TB), lambda b, j, *_: (b, 0, j)),
            pl.BlockSpec((1, TOP_K, TB), lambda b, j, *_: (b, 0, j)),
            pl.BlockSpec((1, N_EXPERTS, 1), lambda b, j, *_: (b * nb + j, 0, 0)),
            pl.BlockSpec(memory_space=pl.ANY),
        ],
        out_specs=pl.BlockSpec(memory_space=pl.ANY),
        scratch_shapes=[pltpu.VMEM((PERM_ROWS, words), U32), pltpu.SemaphoreType.DMA(())],
    )
    return pl.pallas_call(
        functools.partial(_dispatch_kernel, nb=nb),
        grid_spec=grid_spec,
        out_shape=jax.ShapeDtypeStruct((p_total, words), U32),
        input_output_aliases={7: 0},
        compiler_params=_cparams("arbitrary", "arbitrary"),
        name="moe_dispatch",
    )(hoff, loffs, ngr, f, idx, wts, loff_col, jnp.zeros((p_total, words), U32))


def _moe_kernel(te_ref, nt_ref, x_ref, wg_ref, wu_ref, wd_ref, y_ref, wg_s, wu_s, wd_s):
    i = pl.program_id(0)
    fresh = jnp.logical_or(i == 0, te_ref[i] != te_ref[jnp.maximum(i - 1, 0)])

    @pl.when(fresh)
    def _():
        wg_s[...] = wg_ref[0].astype(BF16)
        wu_s[...] = wu_ref[0].astype(BF16)
        wd_s[...] = wd_ref[0].astype(BF16)

    @pl.when(i < nt_ref[0])
    def _():
        half = wg_s.shape[0] // 2
        lo, hi = _unpack_bf16_pairs(x_ref[:, :half])
        gate = lax.bitcast_convert_type(x_ref[:, half:half + 1], F32)
        g = (jnp.dot(lo, wg_s[:half, :], preferred_element_type=F32)
             + jnp.dot(hi, wg_s[half:, :], preferred_element_type=F32))
        u = (jnp.dot(lo, wu_s[:half, :], preferred_element_type=F32)
             + jnp.dot(hi, wu_s[half:, :], preferred_element_type=F32))
        he = (_silu(g) * u).astype(BF16)
        y_ref[...] = _pack_bf16_pairs(jnp.dot(he, wd_s[...], preferred_element_type=F32) * gate)

    @pl.when(i >= nt_ref[0])
    def _():
        y_ref[...] = jnp.zeros_like(y_ref)


def _moe_grouped(x_sorted, tile_expert, n_tiles_used, w_gate, w_up, w_down):
    p, words = x_sorted.shape
    _, d, de = w_gate.shape
    tm = MOE_TILE
    grid_spec = pltpu.PrefetchScalarGridSpec(
        num_scalar_prefetch=2,
        grid=(p // tm,),
        in_specs=[
            pl.BlockSpec((tm, words), lambda i, te, nt: (i, 0)),
            pl.BlockSpec((1, d, de), lambda i, te, nt: (te[i], 0, 0)),
            pl.BlockSpec((1, d, de), lambda i, te, nt: (te[i], 0, 0)),
            pl.BlockSpec((1, de, d), lambda i, te, nt: (te[i], 0, 0)),
        ],
        out_specs=pl.BlockSpec((tm, d // 2), lambda i, te, nt: (i, 0)),
        scratch_shapes=[pltpu.VMEM((d, de), BF16), pltpu.VMEM((d, de), BF16), pltpu.VMEM((de, d), BF16)],
    )
    return pl.pallas_call(
        _moe_kernel,
        grid_spec=grid_spec,
        out_shape=jax.ShapeDtypeStruct((p, d // 2), U32),
        compiler_params=_cparams("arbitrary"),
        name="moe_grouped",
    )(tile_expert, n_tiles_used, x_sorted, w_gate, w_up, w_down)


def _combine_kernel(hoff_ref, loffs_ref, ngr_ref, h_ref, idx_ref, loff_ref, mod_ref, ys_ref, o_ref,
                    yperm, sem, *, nb):
    blk = pl.program_id(0) * nb + pl.program_id(1)

    @pl.when(jnp.logical_and(pl.program_id(0) == 0, pl.program_id(1) == 0))
    def _():
        yperm[...] = jnp.zeros_like(yperm)

    def piece(lo, ho, rows):
        return pltpu.make_async_copy(ys_ref.at[pl.ds(ho, rows)], yperm.at[pl.ds(lo, rows)], sem)

    _for_each_run_piece(blk, hoff_ref, loffs_ref, ngr_ref, lambda lo, ho, rows: piece(lo, ho, rows).start())
    q0, q1 = _block_perm(idx_ref, loff_ref)
    q = jnp.where(jnp.logical_or(q0, q1), 1.0, 0.0).astype(BF16)
    _for_each_run_piece(blk, hoff_ref, loffs_ref, ngr_ref, lambda lo, ho, rows: piece(lo, ho, rows).wait())
    lo, hi = _unpack_bf16_pairs(yperm[...])
    y = jnp.concatenate([_tn_dot(q, lo), _tn_dot(q, hi)], axis=1)
    o_ref[0] = h_ref[0] + mod_ref[0, 5:6, :] * y


def _combine(h, idx, loff_col, mod, mod_row, y_sorted, hoff, loffs, ngr, n_out):
    bsz, ltot, d = h.shape
    nb = ltot // TB
    grid_spec = pltpu.PrefetchScalarGridSpec(
        num_scalar_prefetch=3,
        grid=(bsz, n_out // TB),
        in_specs=[
            pl.BlockSpec((1, TB, d), lambda b, j, *_: (b, j, 0)),
            pl.BlockSpec((1, TOP_K, TB), lambda b, j, *_: (b, 0, j)),
            pl.BlockSpec((1, N_EXPERTS, 1), lambda b, j, *_: (b * nb + j, 0, 0)),
            pl.BlockSpec((1, 6, d), lambda b, j, *_: (mod_row(b, j), 0, 0)),
            pl.BlockSpec(memory_space=pl.ANY),
        ],
        out_specs=pl.BlockSpec((1, TB, d), lambda b, j, *_: (b, j, 0)),
        scratch_shapes=[pltpu.VMEM((PERM_ROWS, d // 2), U32), pltpu.SemaphoreType.DMA(())],
    )
    return pl.pallas_call(
        functools.partial(_combine_kernel, nb=nb),
        grid_spec=grid_spec,
        out_shape=jax.ShapeDtypeStruct((bsz, n_out, d), F32),
        compiler_params=_cparams("arbitrary", "arbitrary"),
        name="moe_combine",
    )(hoff, loffs, ngr, h, idx, loff_col, mod, y_sorted)


def _moe_block(h, f, idx, wts, mod, mod_row, w_gate, w_up, w_down, n_out):
    bsz, ltot, d = h.shape
    nb = ltot // TB
    n_blocks = bsz * nb
    mt = MOE_TILE
    experts = jnp.arange(N_EXPERTS, dtype=jnp.int32)
    cnt = jnp.sum((idx.reshape(bsz, TOP_K, nb, TB, 1) == experts).astype(jnp.int32), axis=(1, 3))
    cnt = cnt.reshape(n_blocks, N_EXPERTS)
    run = ((cnt + RUN_ALIGN - 1) // RUN_ALIGN) * RUN_ALIGN
    loffs = jnp.cumsum(run, axis=1) - run
    region = ((jnp.sum(run, axis=0) + mt - 1) // mt) * mt
    ends = jnp.cumsum(region)
    hoff = (ends - region)[None, :] + jnp.cumsum(run, axis=0) - run
    p_total = ((TOP_K * bsz * ltot + n_blocks * N_EXPERTS * (RUN_ALIGN - 1)) // mt + N_EXPERTS + 1) * mt
    tile_start = jnp.arange(p_total // mt, dtype=jnp.int32) * mt
    tile_expert = jnp.minimum(jnp.sum((tile_start[:, None] >= ends[None, :]).astype(jnp.int32), axis=1),
                              N_EXPERTS - 1)
    n_used = (ends[-1] // mt).astype(jnp.int32).reshape(1)
    flat = lambda t: t.reshape(-1).astype(jnp.int32)
    hoff, loffs_flat, ngr = flat(hoff), flat(loffs), flat(run // RUN_ALIGN)
    loff_col = loffs.astype(F32).reshape(n_blocks, N_EXPERTS, 1)

    x_sorted = _dispatch(f, idx, wts, loff_col, hoff, loffs_flat, ngr, p_total)
    y_sorted = _moe_grouped(x_sorted, tile_expert, n_used, w_gate, w_up, w_down)
    return _combine(h, idx, loff_col, mod, mod_row, y_sorted, hoff, loffs_flat, ngr, n_out)


def _rope_tables(n_lat, n_ctx, n_heads):
    t = jnp.arange(n_lat)
    pos = jnp.stack([t // GRID_W, t % GRID_W], axis=-1).astype(F32)
    n_freq = HEAD_DIM // 4
    inv = 1.0 / (ROPE_BASE ** (jnp.arange(n_freq, dtype=F32) / n_freq))
    ang = pos[:, :, None] * inv
    cos, sin = jnp.cos(ang), jnp.sin(ang)
    cos_h = jnp.concatenate([cos[:, 0], cos[:, 0], cos[:, 1], cos[:, 1]], axis=-1)
    sin_h = jnp.concatenate([-sin[:, 0], sin[:, 0], -sin[:, 1], sin[:, 1]], axis=-1)
    cos_h = jnp.concatenate([cos_h, jnp.ones((n_ctx, HEAD_DIM), F32)], axis=0)
    sin_h = jnp.concatenate([sin_h, jnp.zeros((n_ctx, HEAD_DIM), F32)], axis=0)
    return jnp.tile(cos_h, (1, n_heads)), jnp.tile(sin_h, (1, n_heads))


def _swap_halves_perm(width):
    j = np.arange(width)
    nf = HEAD_DIM // 4
    return np.where((j % (2 * nf)) < nf, j + nf, j - nf)


def _pack_w_in(w_in, rw, sw, nw_):
    o = 0
    seg = {}
    for name, width in (("rq", rw), ("rk", rw), ("rv", rw), ("rg", rw), ("z", sw), ("x", sw),
                        ("b", SSD_GROUPS * SSD_STATE), ("c", SSD_GROUPS * SSD_STATE),
                        ("dt", 2 * (sw // HEAD_DIM)), ("nq", nw_), ("nk", nw_), ("nv", nw_)):
        seg[name] = w_in[:, o:o + width]
        o += width
    perm = _swap_halves_perm(rw)
    dt_pad = jnp.zeros((w_in.shape[0], 128 - seg["dt"].shape[1]), w_in.dtype)
    cols = [seg["rq"], seg["rk"], seg["rq"][:, perm], seg["rk"][:, perm], seg["rv"], seg["rg"],
            seg["z"], seg["x"], seg["b"], seg["c"], seg["nq"], seg["nk"], seg["nv"], seg["dt"], dt_pad]
    return jnp.concatenate(cols, axis=1).astype(BF16)


def kernel(x, c, ctx, c_ctx, w_mod, b_mod, norm_mix, norm_ffn, w_in, w_out, ret_decay_f, ret_decay_b,
           ssd_conv_w, ssd_conv_b, ssd_dt_bias_f, ssd_dt_bias_b, ssd_a_log_f, ssd_a_log_b, ssd_d, ssd_norm,
           na_q_norm, na_k_norm, na_rpb, router_w, router_b, w_gate, w_up, w_down):
    bsz, n_lat, d = x.shape
    n_ctx = ctx.shape[1]
    depth = w_mod.shape[0]
    rows = n_lat // GRID_W
    rw = ret_decay_f.shape[1] * HEAD_DIM
    sw = ssd_d.shape[1] * HEAD_DIM
    nw_ = na_rpb.shape[1] * HEAD_DIM
    dims = (rw, sw, nw_)
    assert n_lat % TB == 0 and n_ctx % TB == 0
    nbl = n_lat // TB

    n_cond = ((bsz + 1 + 7) // 8) * 8
    cond = jnp.zeros((n_cond, d), F32).at[:bsz].set(c).at[bsz].set(c_ctx)
    mod = _modulation(cond, w_mod, b_mod).reshape(depth, n_cond, 6, d)
    mod_row = lambda b, j: jnp.where(j < nbl, b, bsz)

    cos, sin = _rope_tables(n_lat, n_ctx, rw // HEAD_DIM)
    hm = jnp.asarray(np.kron(np.eye(nw_ // HEAD_DIM), np.full((HEAD_DIM, HEAD_DIM), 1.0 / HEAD_DIM)), F32)
    rw_t = jnp.pad(router_w, ((0, 0), (0, 128 - router_w.shape[1])))
    rb = router_b.reshape(-1, 1)

    h = jnp.concatenate([x, ctx], axis=1)
    for l in range(depth):
        last = l == depth - 1
        w_all = _pack_w_in(w_in[l], rw, sw, nw_)
        qkn = jnp.stack([jnp.tile(na_q_norm[l], nw_ // HEAD_DIM), jnp.tile(na_k_norm[l], nw_ // HEAD_DIM)])
        ret, z, xbc, dt, na = _inproj(h, mod[l], mod_row, norm_mix[l].reshape(1, d), w_all, cos, sin, qkn, hm, dims)
        ret_o = _retention(ret, jnp.concatenate([ret_decay_f[l], ret_decay_b[l]]).reshape(-1, 1, 1), nbl)
        ssd_o = _ssd(z, xbc, dt, ssd_conv_w[l], ssd_conv_b[l],
                     jnp.concatenate([ssd_dt_bias_f[l], ssd_dt_bias_b[l]]),
                     jnp.concatenate([ssd_a_log_f[l], ssd_a_log_b[l]]), ssd_d[l], ssd_norm[l], nbl)
        na_o = _na_attention(na, _na_bias_tables(na_rpb[l], rows), nbl)
        h, f, idx, wts = _outproj(h, ret_o, ssd_o, na_o, w_out[l].astype(BF16), mod[l], mod_row,
                                  norm_ffn[l].reshape(1, d), rw_t, rb)
        h = _moe_block(h, f, idx, wts, mod[l], mod_row, w_gate[l], w_up[l], w_down[l],
                       n_lat if last else n_lat + n_ctx)
    return h
```

```python
import functools

import jax
import jax.numpy as jnp
import numpy as np
from jax import lax
from jax.experimental import pallas as pl
from jax.experimental.pallas import tpu as pltpu

F32 = jnp.float32
BF16 = jnp.bfloat16
HIGHEST = lax.Precision.HIGHEST

GRID_W = 64
HEAD_DIM = 64
SSD_GROUPS = 2
SSD_STATE = 64
NA_WIN_ROWS = 8
NA_WIN_COLS = 16
N_GROUPS = 4
EXPERTS_PER_GROUP = 4
N_EXPERTS = N_GROUPS * EXPERTS_PER_GROUP
TOP_K = 2
ROPE_BASE = 10000.0
EPS = 1e-6

VMEM_LIMIT_BYTES = 56 * 1024 * 1024
TB = 256
NA_QROWS = TB // GRID_W
MOE_TILE = 512
SUBLANES = 8


def _cparams(*sem):
    return pltpu.CompilerParams(dimension_semantics=sem, vmem_limit_bytes=VMEM_LIMIT_BYTES)


def _silu(t):
    return t * (1.0 / (1.0 + jnp.exp(-t)))


def _nt_dot(a, b, **kw):
    return lax.dot_general(a, b, (((1,), (1,)), ((), ())), preferred_element_type=F32, **kw)


def _tn_dot(a, b):
    return lax.dot_general(a, b, (((0,), (0,)), ((), ())), preferred_element_type=F32)


def _bf16_parts(x, n):
    parts, rest = [], x
    for i in range(n):
        p = rest.astype(BF16)
        parts.append(p)
        if i + 1 < n:
            rest = rest - p.astype(F32)
    return parts


def _split_dot(x, sel, parts=3, left=False):
    sel = sel.astype(BF16)
    dot = (lambda p: jnp.dot(sel, p, preferred_element_type=F32)) if left else (
        lambda p: jnp.dot(p, sel, preferred_element_type=F32))
    out = None
    for p in _bf16_parts(x, parts):
        out = dot(p) if out is None else out + dot(p)
    return out


def _mod_kernel(s_ref, w_ref, b_ref, o_ref):
    s = _silu(s_ref[...])
    o_ref[0] = jnp.dot(s, w_ref[0], precision=HIGHEST, preferred_element_type=F32) + b_ref[0]


def _modulation(cond, w_mod, b_mod):
    depth, d, n = w_mod.shape
    rows = cond.shape[0]
    tn = 1024
    return pl.pallas_call(
        _mod_kernel,
        grid=(depth, n // tn),
        in_specs=[
            pl.BlockSpec((rows, d), lambda l, j: (0, 0)),
            pl.BlockSpec((1, d, tn), lambda l, j: (l, 0, j)),
            pl.BlockSpec((1, 1, tn), lambda l, j: (l, 0, j)),
        ],
        out_specs=pl.BlockSpec((1, rows, tn), lambda l, j: (l, 0, j)),
        out_shape=jax.ShapeDtypeStruct((depth, rows, n), F32),
        compiler_params=_cparams("parallel", "parallel"),
        name="modulation",
    )(cond, w_mod, b_mod.reshape(depth, 1, n))


def _inproj_kernel(h_ref, mod_ref, nw_ref, w_ref, cos_ref, sin_ref, qkn_ref, hm_ref,
                   ret_ref, z_ref, xbc_ref, dt_ref, na_ref, *, dims):
    rw, sw, nw_ = dims
    h = h_ref[0]
    a = h * lax.rsqrt(jnp.mean(h * h, axis=-1, keepdims=True) + EPS) * nw_ref[...]
    a = a * (1.0 + mod_ref[0, 1:2, :]) + mod_ref[0, 0:1, :]
    ab = a.astype(BF16)

    def proj(lo, hi):
        return jnp.dot(ab, w_ref[:, lo:hi], preferred_element_type=F32)

    cos, sin = cos_ref[...], sin_ref[...]
    o = 0
    ret_ref[0, :, 0:rw] = proj(o, o + rw) * cos + proj(o + 2 * rw, o + 3 * rw) * sin
    ret_ref[0, :, rw:2 * rw] = (proj(o + rw, o + 2 * rw) * cos + proj(o + 3 * rw, o + 4 * rw) * sin) * (HEAD_DIM ** -0.5)
    ret_ref[0, :, 2 * rw:4 * rw] = proj(o + 4 * rw, o + 6 * rw)
    o += 6 * rw
    z_ref[0] = proj(o, o + sw)
    o += sw
    xbc_w = sw + 2 * SSD_GROUPS * SSD_STATE
    xbc_ref[0] = proj(o, o + xbc_w)
    o += xbc_w
    hm = hm_ref[...]
    for i in range(2):
        t = proj(o + i * nw_, o + (i + 1) * nw_)
        ms = _split_dot(t * t, hm, parts=2)
        na_ref[0, :, i * nw_:(i + 1) * nw_] = (t * lax.rsqrt(ms + EPS) * qkn_ref[i:i + 1, :]).astype(BF16)
    na_ref[0, :, 2 * nw_:3 * nw_] = proj(o + 2 * nw_, o + 3 * nw_).astype(BF16)
    o += 3 * nw_
    dt_ref[0] = proj(o, o + 128)[:, 0:dt_ref.shape[2]]


def _inproj(h, mod, mod_row, norm_w, w_all, cos, sin, qkn, hm, dims):
    bsz, ltot, d = h.shape
    rw, sw, nw_ = dims
    n_dt = 2 * (sw // HEAD_DIM)
    xbc_w = sw + 2 * SSD_GROUPS * SSD_STATE
    row = lambda n: pl.BlockSpec((1, TB, n), lambda b, j: (b, j, 0))
    const = lambda shape: pl.BlockSpec(shape, lambda b, j: (0,) * len(shape))
    widths = (4 * rw, sw, xbc_w, n_dt, 3 * nw_)
    dtypes = (F32, F32, F32, F32, BF16)
    return pl.pallas_call(
        functools.partial(_inproj_kernel, dims=dims),
        grid=(bsz, ltot // TB),
        in_specs=[
            row(d),
            pl.BlockSpec((1, 6, d), lambda b, j: (mod_row(b, j), 0, 0)),
            const((1, d)),
            const(w_all.shape),
            pl.BlockSpec((TB, rw), lambda b, j: (j, 0)),
            pl.BlockSpec((TB, rw), lambda b, j: (j, 0)),
            const((2, nw_)),
            const((nw_, nw_)),
        ],
        out_specs=[row(n) for n in widths],
        out_shape=[jax.ShapeDtypeStruct((bsz, ltot, n), dt) for n, dt in zip(widths, dtypes)],
        compiler_params=_cparams("parallel", "parallel"),
        name="inproj",
    )(h, mod, norm_w, w_all, cos, sin, qkn, hm)


def _scan_chunk_maps(nb, nbl):
    fwd = lambda s: jnp.where(s < nb, (nbl + s) % nb, 2 * nb - 1 - s)
    out = lambda s: jnp.where(s < nb, nb - 1, 2 * nb - 1 - s)
    return fwd, out


def _ret_kernel(q_ref, k_ref, v_ref, g_ref, dec_ref, o_ref, sf_all, sf, sb, dmat, rd, *, nb, nbl):
    s = pl.program_id(1)
    c = TB
    nh = k_ref.shape[2] // HEAD_DIM
    log_f = [-jnp.exp(dec_ref[h]) for h in range(nh)]
    log_b = [-jnp.exp(dec_ref[nh + h]) for h in range(nh)]

    @pl.when(s == 0)
    def _init():
        sf[...] = jnp.zeros_like(sf)
        sb[...] = jnp.zeros_like(sb)
        delta = (lax.broadcasted_iota(jnp.int32, (c, c), 0) - lax.broadcasted_iota(jnp.int32, (c, c), 1)).astype(F32)
        pos = lax.broadcasted_iota(jnp.int32, (c, HEAD_DIM), 0).astype(F32)
        for h in range(nh):
            dmat[h] = (jnp.exp(jnp.where(delta >= 0, log_f[h] * delta, -jnp.inf))
                       + jnp.exp(jnp.where(delta <= 0, -log_b[h] * delta, -jnp.inf)))
            rd[4 * h + 0] = jnp.exp(log_f[h] * (pos + 1.0))
            rd[4 * h + 1] = jnp.exp(log_b[h] * (c - pos))
            rd[4 * h + 2] = jnp.exp(log_f[h] * (c - 1.0 - pos))
            rd[4 * h + 3] = jnp.exp(log_b[h] * pos)

    @pl.when(s < nb)
    def _state_sweep():
        ci = (nbl + s) % nb
        for h in range(nh):
            sl = slice(h * HEAD_DIM, (h + 1) * HEAD_DIM)
            k, v = k_ref[0, :, sl], v_ref[0, :, sl]
            sf_all[ci, h] = sf[h]
            sf[h] = sf[h] * jnp.exp(log_f[h] * c) + _tn_dot((k * rd[4 * h + 2]).astype(BF16), v.astype(BF16))

    @pl.when(s >= nb)
    def _output_sweep():
        ci = 2 * nb - 1 - s
        for h in range(nh):
            sl = slice(h * HEAD_DIM, (h + 1) * HEAD_DIM)
            q, k, v = q_ref[0, :, sl], k_ref[0, :, sl], v_ref[0, :, sl]
            qb, vb = q.astype(BF16), v.astype(BF16)
            p = _nt_dot(qb, k.astype(BF16)) * dmat[h]
            y = jnp.dot(p.astype(BF16), vb, preferred_element_type=F32)
            y += jnp.dot(qb, sf_all[ci, h].astype(BF16), preferred_element_type=F32) * rd[4 * h + 0]
            y += jnp.dot(qb, sb[h].astype(BF16), preferred_element_type=F32) * rd[4 * h + 1]
            sb[h] = sb[h] * jnp.exp(log_b[h] * c) + _tn_dot((k * rd[4 * h + 3]).astype(BF16), vb)
            mu = jnp.mean(y, axis=-1, keepdims=True)
            yc = y - mu
            yn = yc * lax.rsqrt(jnp.mean(yc * yc, axis=-1, keepdims=True) + EPS)
            o_ref[0, :, sl] = (yn * _silu(g_ref[0, :, sl])).astype(o_ref.dtype)


def _retention(ret, decay, nbl):
    bsz, ltot, w4 = ret.shape
    w = w4 // 4
    nh = w // HEAD_DIM
    nb = ltot // TB
    fwd, out = _scan_chunk_maps(nb, nbl)
    return pl.pallas_call(
        functools.partial(_ret_kernel, nb=nb, nbl=nbl),
        grid=(bsz, 2 * nb),
        in_specs=[
            pl.BlockSpec((1, TB, w), lambda b, s: (b, out(s), 0)),
            pl.BlockSpec((1, TB, w), lambda b, s: (b, fwd(s), 1)),
            pl.BlockSpec((1, TB, w), lambda b, s: (b, fwd(s), 2)),
            pl.BlockSpec((1, TB, w), lambda b, s: (b, out(s), 3)),
            pl.BlockSpec(decay.shape, lambda b, s: (0, 0, 0)),
        ],
        out_specs=pl.BlockSpec((1, TB, w), lambda b, s: (b, out(s), 0)),
        out_shape=jax.ShapeDtypeStruct((bsz, ltot, w), BF16),
        scratch_shapes=[
            pltpu.VMEM((nb, nh, HEAD_DIM, HEAD_DIM), F32),
            pltpu.VMEM((nh, HEAD_DIM, HEAD_DIM), F32),
            pltpu.VMEM((nh, HEAD_DIM, HEAD_DIM), F32),
            pltpu.VMEM((nh, TB, TB), F32),
            pltpu.VMEM((4 * nh, TB, HEAD_DIM), F32),
        ],
        compiler_params=_cparams("parallel", "arbitrary"),
        name="retention",
    )(ret, ret, ret, ret, decay)


def _softplus(t):
    return jnp.maximum(t, 0.0) + jnp.log1p(jnp.exp(-jnp.abs(t)))


def _lane_group_selector(n_rows, group):
    shape = (n_rows, n_rows * group)
    lane_owner = lax.broadcasted_iota(jnp.int32, shape, 1) // group
    return jnp.where(lane_owner == lax.broadcasted_iota(jnp.int32, shape, 0), 1.0, 0.0).astype(BF16)


def _ssd_kernel(z_ref, x_ref, xp_ref, xn_ref, dt_ref, cw_ref, cb_ref, dtb_ref, alog_ref, dtbt_ref, alogt_ref,
                dsk_ref, nw_ref, o_ref, sf_all, sf, sb, tri, ybuf, *, nb, nbl):
    s = pl.program_id(1)
    c = TB
    sw = z_ref.shape[2]
    nh = sw // HEAD_DIM
    rep = nh // SSD_GROUPS
    gw = rep * HEAD_DIM
    gs = SSD_GROUPS * SSD_STATE
    ci = jnp.where(s < nb, (nbl + s) % nb, 2 * nb - 1 - s)

    @pl.when(s == 0)
    def _init():
        sf[...] = jnp.zeros_like(sf)
        sb[...] = jnp.zeros_like(sb)
        li = lax.broadcasted_iota(jnp.int32, (c, c), 0)
        si = lax.broadcasted_iota(jnp.int32, (c, c), 1)
        tri[0] = jnp.where(si <= li, 1.0, 0.0).astype(BF16)
        tri[1] = jnp.where(si >= li, 1.0, 0.0).astype(BF16)

    x = x_ref[0]
    has_prev = jnp.where(jnp.logical_or(ci == 0, ci == nbl), 0.0, 1.0)
    has_next = jnp.where(jnp.logical_or(ci == nbl - 1, ci == nb - 1), 0.0, 1.0)
    rows = lax.broadcasted_iota(jnp.int32, x.shape, 0)
    x_dn = jnp.where(rows == 0, xp_ref[0, SUBLANES - 1:SUBLANES, :] * has_prev, pltpu.roll(x, 1, 0))
    x_up = jnp.where(rows == c - 1, xn_ref[0, 0:1, :] * has_next, pltpu.roll(x, c - 1, 0))
    xc = _silu(cw_ref[0:1, :] * x_dn + cw_ref[1:2, :] * x + cw_ref[2:3, :] * x_up + cb_ref[...])
    xs, bm, cm = xc[:, :sw], xc[:, sw:sw + gs], xc[:, sw + gs:]
    bm_t = bm.T.astype(BF16)

    dtr = dt_ref[0]
    dt = _softplus(dtr + dtb_ref[...])
    a = dt * -jnp.exp(alog_ref[...])
    pre = _split_dot(a, tri[0], left=True)
    tot = pre[c - 1:c, :]
    is_fwd = lax.broadcasted_iota(jnp.int32, (1, 2 * nh), 1) < nh
    acs = jnp.where(is_fwd, pre, tot - pre + a)
    sel_head = _lane_group_selector(2 * nh, HEAD_DIM)
    w_wide = _split_dot(dt * jnp.exp(tot - acs), sel_head)
    etot_wide = _split_dot(jnp.broadcast_to(jnp.exp(tot), (SUBLANES, 2 * nh)), sel_head)[0:1, :]

    @pl.when(s < nb)
    def _state_sweep():
        for g in range(SSD_GROUPS):
            lanes = slice(g * gw, (g + 1) * gw)
            xw = (xs[:, lanes] * w_wide[:, lanes]).astype(BF16)
            sf_all[ci, g] = sf[g]
            sf[g] = sf[g] * etot_wide[:, lanes] + jnp.dot(bm_t[g * SSD_STATE:(g + 1) * SSD_STATE, :], xw,
                                                          preferred_element_type=F32)

    @pl.when(s >= nb)
    def _output_sweep():
        eye = (lax.broadcasted_iota(jnp.int32, (2 * nh, 2 * nh), 0)
               == lax.broadcasted_iota(jnp.int32, (2 * nh, 2 * nh), 1)).astype(BF16)
        dtr_t = sum(_nt_dot(eye, p) for p in _bf16_parts(dtr, 3))
        dt_t = _softplus(dtr_t + dtbt_ref[...])
        a_t = dt_t * -jnp.exp(alogt_ref[...])
        pre_t = _split_dot(a_t, tri[1])
        is_fwd_t = lax.broadcasted_iota(jnp.int32, (2 * nh, 1), 0) < nh
        acs_t = jnp.where(is_fwd_t, pre_t, pre_t[:, c - 1:c] - pre_t + a_t)
        e_wide = _split_dot(jnp.exp(acs), sel_head)
        acs_bc = _split_dot(acs, _lane_group_selector(2 * nh, 128))
        li = lax.broadcasted_iota(jnp.int32, (c, c), 0)
        si = lax.broadcasted_iota(jnp.int32, (c, c), 1)
        lower, upper = li >= si, si >= li
        for g in range(SSD_GROUPS):
            lanes = slice(g * gw, (g + 1) * gw)
            lanes_b = slice(sw + g * gw, sw + (g + 1) * gw)
            cg = cm[:, g * SSD_STATE:(g + 1) * SSD_STATE].astype(BF16)
            bg_t = bm_t[g * SSD_STATE:(g + 1) * SSD_STATE, :]
            scores = jnp.dot(cg, bg_t, preferred_element_type=F32)
            for r in range(rep):
                h = g * rep + r
                hb = nh + h
                col_f = jnp.concatenate([acs_bc[:, h * 128:(h + 1) * 128]] * (c // 128), axis=1)
                col_b = jnp.concatenate([acs_bc[:, hb * 128:(hb + 1) * 128]] * (c // 128), axis=1)
                d_f = jnp.exp(jnp.where(lower, col_f - acs_t[h:h + 1, :], -jnp.inf)) * dt_t[h:h + 1, :]
                d_b = jnp.exp(jnp.where(upper, col_b - acs_t[hb:hb + 1, :], -jnp.inf)) * dt_t[hb:hb + 1, :]
                ybuf[:, h * HEAD_DIM:(h + 1) * HEAD_DIM] = jnp.dot(
                    (scores * (d_f + d_b)).astype(BF16), xs[:, h * HEAD_DIM:(h + 1) * HEAD_DIM].astype(BF16),
                    preferred_element_type=F32)
            ybuf[:, lanes] += (jnp.dot(cg, sf_all[ci, g].astype(BF16), preferred_element_type=F32) * e_wide[:, lanes]
                               + jnp.dot(cg, sb[g].astype(BF16), preferred_element_type=F32) * e_wide[:, lanes_b])
            xw = (xs[:, lanes] * w_wide[:, lanes_b]).astype(BF16)
            sb[g] = sb[g] * etot_wide[:, lanes_b] + jnp.dot(bg_t, xw, preferred_element_type=F32)
        y = (ybuf[...] + dsk_ref[...] * xs) * _silu(z_ref[0])
        o_ref[0] = (y * lax.rsqrt(jnp.mean(y * y, axis=-1, keepdims=True) + EPS) * nw_ref[...]).astype(o_ref.dtype)


def _ssd(z, xbc, dt, conv_w, conv_b, dtb, alog, d_skip, norm_w, nbl):
    bsz, ltot, sw = z.shape
    nh = sw // HEAD_DIM
    xw = xbc.shape[2]
    nb = ltot // TB
    per = TB // SUBLANES
    fwd, out = _scan_chunk_maps(nb, nbl)
    const = lambda t: pl.BlockSpec(t.shape, lambda b, s: (0,) * t.ndim)
    args = (conv_w, conv_b.reshape(1, xw), dtb.reshape(1, 2 * nh), alog.reshape(1, 2 * nh),
            dtb.reshape(2 * nh, 1), alog.reshape(2 * nh, 1), jnp.repeat(d_skip, HEAD_DIM).reshape(1, sw),
            norm_w.reshape(1, sw))
    return pl.pallas_call(
        functools.partial(_ssd_kernel, nb=nb, nbl=nbl),
        grid=(bsz, 2 * nb),
        in_specs=[
            pl.BlockSpec((1, TB, sw), lambda b, s: (b, out(s), 0)),
            pl.BlockSpec((1, TB, xw), lambda b, s: (b, fwd(s), 0)),
            pl.BlockSpec((1, SUBLANES, xw), lambda b, s: (b, jnp.maximum(fwd(s) * per - 1, 0), 0)),
            pl.BlockSpec((1, SUBLANES, xw), lambda b, s: (b, jnp.minimum((fwd(s) + 1) * per, nb * per - 1), 0)),
            pl.BlockSpec((1, TB, 2 * nh), lambda b, s: (b, fwd(s), 0)),
        ] + [const(t) for t in args],
        out_specs=pl.BlockSpec((1, TB, sw), lambda b, s: (b, out(s), 0)),
        out_shape=jax.ShapeDtypeStruct((bsz, ltot, sw), BF16),
        scratch_shapes=[
            pltpu.VMEM((nb, SSD_GROUPS, SSD_STATE, sw // SSD_GROUPS), F32),
            pltpu.VMEM((SSD_GROUPS, SSD_STATE, sw // SSD_GROUPS), F32),
            pltpu.VMEM((SSD_GROUPS, SSD_STATE, sw // SSD_GROUPS), F32),
            pltpu.VMEM((2, TB, TB), BF16),
            pltpu.VMEM((TB, sw), F32),
        ],
        compiler_params=_cparams("parallel", "arbitrary"),
        name="ssd",
    )(z, xbc, xbc, xbc, dt, *args)


def _softmax_pv(s_parts, v):
    m = s_parts[0].max(axis=-1, keepdims=True)
    for s in s_parts[1:]:
        m = jnp.maximum(m, s.max(axis=-1, keepdims=True))
    p_parts = [jnp.exp(s - m) for s in s_parts]
    den = p_parts[0].sum(axis=-1, keepdims=True)
    for p in p_parts[1:]:
        den = den + p.sum(axis=-1, keepdims=True)
    p = p_parts[0] if len(p_parts) == 1 else jnp.concatenate(p_parts, axis=1)
    return jnp.dot(p.astype(BF16), v, preferred_element_type=F32) / den


def _na_kernel(q_ref, k0_ref, k1_ref, k2_ref, v0_ref, v1_ref, v2_ref, kc_ref, vc_ref, bias_ref, o_ref, *, nbl):
    nh = q_ref.shape[2] // HEAD_DIM
    n_lat_keys = 3 * k0_ref.shape[1]
    j = pl.program_id(1)

    @pl.when(j < nbl)
    def _latent():
        for h in range(nh):
            sl = slice(h * HEAD_DIM, (h + 1) * HEAD_DIM)
            q = q_ref[0, :, sl] * (HEAD_DIM ** -0.5)
            k = jnp.concatenate([k0_ref[0, :, sl], k1_ref[0, :, sl], k2_ref[0, :, sl], kc_ref[0, :, sl]], axis=0)
            v = jnp.concatenate([v0_ref[0, :, sl], v1_ref[0, :, sl], v2_ref[0, :, sl], vc_ref[0, :, sl]], axis=0)
            s = _nt_dot(q, k)
            o = _softmax_pv([s[:, :n_lat_keys] + bias_ref[0, h], s[:, n_lat_keys:]], v)
            o_ref[0, :, sl] = o.astype(o_ref.dtype)

    @pl.when(j >= nbl)
    def _context():
        for h in range(nh):
            sl = slice(h * HEAD_DIM, (h + 1) * HEAD_DIM)
            s = _nt_dot(q_ref[0, :, sl] * (HEAD_DIM ** -0.5), kc_ref[0, :, sl])
            o_ref[0, :, sl] = _softmax_pv([s], vc_ref[0, :, sl]).astype(o_ref.dtype)


def _na_bias_tables(rpb, rows):
    qb, kb = NA_QROWS, 3 * NA_QROWS
    nb = rows // qb
    nh = rpb.shape[0]
    n_dr, n_dc = 2 * NA_WIN_ROWS - 1, 2 * NA_WIN_COLS - 1
    c = np.arange(GRID_W)[:, None]
    kc = np.arange(GRID_W)[None, :]
    cs = np.clip(c - NA_WIN_COLS // 2, 0, GRID_W - NA_WIN_COLS)
    col_ok = (kc >= cs) & (kc < cs + NA_WIN_COLS)
    dc = np.clip(kc - c + NA_WIN_COLS - 1, 0, n_dc - 1)
    col_sel = np.eye(n_dc, dtype=np.float32)[dc.reshape(-1)].T
    tabs = []
    for rbq, wb in ((0, 0), (1, 0), (nb - 1, nb - 3)):
        r = qb * rbq + np.arange(qb)[:, None]
        kr = qb * wb + np.arange(kb)[None, :]
        r0 = np.clip(r - NA_WIN_ROWS // 2, 0, rows - NA_WIN_ROWS)
        row_ok = (kr >= r0) & (kr < r0 + NA_WIN_ROWS)
        dr = np.clip(kr - r + NA_WIN_ROWS - 1, 0, n_dr - 1)
        row_sel = np.eye(n_dr, dtype=np.float32)[dr.reshape(-1)]
        t = jnp.einsum("ad,hde,ef->haf", row_sel, rpb.astype(F32), col_sel, precision=HIGHEST)
        t = t.reshape(nh, qb, kb, GRID_W, GRID_W).transpose(0, 1, 3, 2, 4).reshape(nh, qb * GRID_W, kb * GRID_W)
        valid = (row_ok[:, None, :, None] & col_ok[None, :, None, :]).reshape(qb * GRID_W, kb * GRID_W)
        tabs.append(jnp.where(valid, t, -jnp.inf))
    return jnp.stack(tabs)


def _na_attention(na, bias, nbl):
    bsz, ltot, w3 = na.shape
    w = w3 // 3
    nb = ltot // TB
    assert nbl >= 3 and nb == nbl + 1, "needs >= 12 grid rows and a context of one token block"
    win = lambda j: jnp.clip(j - 1, 0, nbl - 3)
    kv = lambda i, col: pl.BlockSpec((1, TB, w), lambda b, j: (b, win(j) + i, col))
    typ = lambda j: jnp.where(j == 0, 0, jnp.where(j >= nbl - 1, 2, 1))
    return pl.pallas_call(
        functools.partial(_na_kernel, nbl=nbl),
        grid=(bsz, nb),
        in_specs=[
            pl.BlockSpec((1, TB, w), lambda b, j: (b, j, 0)),
            kv(0, 1), kv(1, 1), kv(2, 1), kv(0, 2), kv(1, 2), kv(2, 2),
            pl.BlockSpec((1, TB, w), lambda b, j: (b, nbl, 1)),
            pl.BlockSpec((1, TB, w), lambda b, j: (b, nbl, 2)),
            pl.BlockSpec((1,) + bias.shape[1:], lambda b, j: (typ(j), 0, 0, 0)),
        ],
        out_specs=pl.BlockSpec((1, TB, w), lambda b, j: (b, j, 0)),
        out_shape=jax.ShapeDtypeStruct((bsz, ltot, w), BF16),
        compiler_params=_cparams("parallel", "arbitrary"),
        name="na_attention",
    )(na, na, na, na, na, na, na, na, na, bias)


def _route(aff, sel):
    rows = [sel[e:e + 1, :] for e in range(N_EXPERTS)]
    arow = [aff[e:e + 1, :] for e in range(N_EXPERTS)]
    gscore = []
    for g in range(N_GROUPS):
        a, b, c, d = rows[4 * g:4 * g + 4]
        hi1, lo1 = jnp.maximum(a, b), jnp.minimum(a, b)
        hi2, lo2 = jnp.maximum(c, d), jnp.minimum(c, d)
        gscore.append(jnp.maximum(hi1, hi2) + jnp.maximum(jnp.minimum(hi1, hi2), jnp.maximum(lo1, lo2)))
    best = jnp.zeros_like(gscore[0], dtype=jnp.int32)
    top = gscore[0]
    for g in range(1, N_GROUPS):
        upd = gscore[g] > top
        best = jnp.where(upd, g, best)
        top = jnp.where(upd, gscore[g], top)
    sv, av = [], []
    for j in range(EXPERTS_PER_GROUP):
        s_j, a_j = rows[j], arow[j]
        for g in range(1, N_GROUPS):
            s_j = jnp.where(best == g, rows[4 * g + j], s_j)
            a_j = jnp.where(best == g, arow[4 * g + j], a_j)
        sv.append(s_j)
        av.append(a_j)
    picked = []
    for j in range(EXPERTS_PER_GROUP):
        rank = jnp.zeros_like(best)
        for i in range(EXPERTS_PER_GROUP):
            if i == j:
                continue
            ahead = (sv[i] >= sv[j]) if i < j else (sv[i] > sv[j])
            rank = rank + ahead.astype(jnp.int32)
        picked.append(rank < TOP_K)
    first = jnp.full_like(best, EXPERTS_PER_GROUP - 1)
    last = jnp.zeros_like(best)
    for j in range(EXPERTS_PER_GROUP - 1, -1, -1):
        first = jnp.where(picked[j], j, first)
    for j in range(EXPERTS_PER_GROUP):
        last = jnp.where(picked[j], j, last)
    a_first, a_last = av[0], av[0]
    for j in range(1, EXPERTS_PER_GROUP):
        a_first = jnp.where(first == j, av[j], a_first)
        a_last = jnp.where(last == j, av[j], a_last)
    tot = a_first + a_last
    idx = jnp.concatenate([best * EXPERTS_PER_GROUP + first, best * EXPERTS_PER_GROUP + last], axis=0)
    wts = jnp.concatenate([a_first / tot, a_last / tot], axis=0)
    return idx, wts


def _outproj_kernel(h_ref, ret_ref, ssd_ref, na_ref, wo_ref, mod_ref, nw_ref, rwt_ref, rb_ref,
                    hn_ref, f_ref, idx_ref, wts_ref):
    rw, sw = ret_ref.shape[2], ssd_ref.shape[2]
    acc = jnp.dot(ret_ref[0], wo_ref[0:rw, :], preferred_element_type=F32)
    acc += jnp.dot(ssd_ref[0], wo_ref[rw:rw + sw, :], preferred_element_type=F32)
    acc += jnp.dot(na_ref[0], wo_ref[rw + sw:, :], preferred_element_type=F32)
    hn = h_ref[0] + mod_ref[0, 2:3, :] * acc
    hn_ref[0] = hn
    f = hn * lax.rsqrt(jnp.mean(hn * hn, axis=-1, keepdims=True) + EPS) * nw_ref[...]
    f = f * (1.0 + mod_ref[0, 4:5, :]) + mod_ref[0, 3:4, :]
    f_hi, f_mid = _bf16_parts(f, 2)
    f_ref[0] = f_hi
    rw2 = rwt_ref[...]
    p = (jnp.dot(f_hi, rw2, preferred_element_type=F32) + jnp.dot(f_mid, rw2, preferred_element_type=F32)).T
    logits = p[:N_EXPERTS, :] + p[N_EXPERTS:2 * N_EXPERTS, :]
    aff = 1.0 / (1.0 + jnp.exp(-logits))
    idx, wts = _route(aff, aff + rb_ref[...])
    idx_ref[0] = idx
    wts_ref[0] = wts


def _outproj(h, ret_o, ssd_o, na_o, w_out, mod, mod_row, norm_w, rw_t, rb):
    bsz, ltot, d = h.shape
    row = lambda n: pl.BlockSpec((1, TB, n), lambda b, j: (b, j, 0))
    const = lambda shape: pl.BlockSpec(shape, lambda b, j: (0,) * len(shape))
    col = lambda: pl.BlockSpec((1, TOP_K, TB), lambda b, j: (b, 0, j))
    return pl.pallas_call(
        _outproj_kernel,
        grid=(bsz, ltot // TB),
        in_specs=[
            row(d), row(ret_o.shape[2]), row(ssd_o.shape[2]), row(na_o.shape[2]),
            const(w_out.shape),
            pl.BlockSpec((1, 6, d), lambda b, j: (mod_row(b, j), 0, 0)),
            const((1, d)), const(rw_t.shape), const(rb.shape),
        ],
        out_specs=[row(d), row(d), col(), col()],
        out_shape=[
            jax.ShapeDtypeStruct((bsz, ltot, d), F32),
            jax.ShapeDtypeStruct((bsz, ltot, d), BF16),
            jax.ShapeDtypeStruct((bsz, TOP_K, ltot), jnp.int32),
            jax.ShapeDtypeStruct((bsz, TOP_K, ltot), F32),
        ],
        compiler_params=_cparams("parallel", "parallel"),
        name="outproj_router",
    )(h, ret_o, ssd_o, na_o, w_out, mod, norm_w, rw_t, rb)


RUN_ALIGN = SUBLANES
RUN_BITS = 6
PERM_ROWS = TOP_K * TB + N_EXPERTS * RUN_ALIGN
GATE_LANES = 128
GAP_BITS = 6
assert TB == RUN_ALIGN << (RUN_BITS - 1) and MOE_TILE == RUN_ALIGN << GAP_BITS

U32 = jnp.uint32


def _pack_bf16_pairs(y):
    n = y.shape[1] // 2
    lo = lax.bitcast_convert_type(y[:, :n].astype(BF16).astype(F32), U32) >> 16
    hi = lax.bitcast_convert_type(y[:, n:].astype(BF16).astype(F32), U32) & jnp.uint32(0xFFFF0000)
    return lo | hi


def _unpack_bf16_pairs(u):
    lo = lax.bitcast_convert_type(u << 16, F32).astype(BF16)
    hi = lax.bitcast_convert_type(u & jnp.uint32(0xFFFF0000), F32).astype(BF16)
    return lo, hi


def _block_perm(idx_ref, loff_ref):
    tb = idx_ref.shape[2]
    na = TOP_K * tb
    e_iota = lax.broadcasted_iota(jnp.int32, (N_EXPERTS, tb), 0)
    oh = jnp.concatenate([jnp.where(e_iota == idx_ref[0, k:k + 1, :], 1.0, 0.0) for k in range(TOP_K)], axis=1)
    upper = lax.broadcasted_iota(jnp.int32, (na, na), 0) <= lax.broadcasted_iota(jnp.int32, (na, na), 1)
    cum = jnp.dot(oh.astype(BF16), jnp.where(upper, 1.0, 0.0).astype(BF16), preferred_element_type=F32)
    pos = jnp.sum(oh * (cum - 1.0 + loff_ref[0]), axis=0, keepdims=True).astype(jnp.int32)
    r = lax.broadcasted_iota(jnp.int32, (PERM_ROWS, tb), 0)
    return [r == pos[:, k * tb:(k + 1) * tb] for k in range(TOP_K)]


def _for_each_run_piece(blk, hoff_ref, loff_ref, ngr_ref, fn):
    for e in range(N_EXPERTS):
        j = blk * N_EXPERTS + e
        n, lo, ho = ngr_ref[j], loff_ref[j], hoff_ref[j]
        for b in range(RUN_BITS - 1, -1, -1):
            start = ((n >> (b + 1)) << (b + 1)) * RUN_ALIGN

            @pl.when(((n >> b) & 1) == 1)
            def _():
                fn(pl.multiple_of(lo + start, RUN_ALIGN), pl.multiple_of(ho + start, RUN_ALIGN), RUN_ALIGN << b)


def _for_each_gap_piece(goff_ref, ggr_ref, fn):
    for e in range(N_EXPERTS):
        n, ho = ggr_ref[e], goff_ref[e]
        for b in range(GAP_BITS - 1, -1, -1):
            start = ((n >> (b + 1)) << (b + 1)) * RUN_ALIGN

            @pl.when(((n >> b) & 1) == 1)
            def _():
                fn(pl.multiple_of(ho + start, RUN_ALIGN), RUN_ALIGN << b)


def _dispatch_kernel(hoff_ref, loffs_ref, ngr_ref, goff_ref, ggr_ref, f_ref, idx_ref, wts_ref, loff_ref, xs_ref,
                     xperm, zbuf, sems, *, nb):
    step = pl.program_id(0) * nb + pl.program_id(1)
    n_steps = pl.num_programs(0) * nb
    slot = step % 2

    def run_copies(blk, slot_, wait):
        def fn(lo, ho, rows):
            cp = pltpu.make_async_copy(xperm.at[slot_, pl.ds(lo, rows)], xs_ref.at[pl.ds(ho, rows)], sems.at[slot_])
            cp.wait() if wait else cp.start()
        _for_each_run_piece(blk, hoff_ref, loffs_ref, ngr_ref, fn)

    @pl.when(step == 0)
    def _zero_region_tails():
        zbuf[...] = jnp.zeros_like(zbuf)
        for wait in (False, True):
            def fn(ho, rows, wait=wait):
                cp = pltpu.make_async_copy(zbuf.at[pl.ds(0, rows)], xs_ref.at[pl.ds(ho, rows)], sems.at[2])
                cp.wait() if wait else cp.start()
            _for_each_gap_piece(goff_ref, ggr_ref, fn)
        tail_rows = zbuf.shape[0]

        def tail_copy(i):
            dst = xs_ref.at[pl.ds(pl.multiple_of(goff_ref[N_EXPERTS] + i * tail_rows, RUN_ALIGN), tail_rows)]
            return pltpu.make_async_copy(zbuf, dst, sems.at[2])

        lax.fori_loop(0, ggr_ref[N_EXPERTS], lambda i, c: (tail_copy(i).start(), c)[1], 0)
        lax.fori_loop(0, ggr_ref[N_EXPERTS], lambda i, c: (tail_copy(i).wait(), c)[1], 0)

    q0, q1 = _block_perm(idx_ref, loff_ref)
    q = jnp.where(jnp.logical_or(q0, q1), 1.0, 0.0).astype(BF16)
    half = f_ref.shape[2] // 2
    xperm[slot, :, :half] = _pack_bf16_pairs(jnp.dot(q, f_ref[0], preferred_element_type=F32))
    w = jnp.sum(jnp.where(q0, wts_ref[0, 0:1, :], 0.0) + jnp.where(q1, wts_ref[0, 1:2, :], 0.0),
                axis=1, keepdims=True)
    xperm[slot, :, half:] = jnp.broadcast_to(lax.bitcast_convert_type(w, U32), (PERM_ROWS, GATE_LANES))

    run_copies(step, slot, wait=False)

    @pl.when(step > 0)
    def _():
        run_copies(step - 1, 1 - slot, wait=True)

    @pl.when(step == n_steps - 1)
    def _():
        run_copies(step, slot, wait=True)


def _dispatch(f, idx, wts, loff_col, hoff, loffs, ngr, goff, ggr, p_total):
    bsz, ltot, d = f.shape
    nb = ltot // TB
    words = d // 2 + GATE_LANES
    grid_spec = pltpu.PrefetchScalarGridSpec(
        num_scalar_prefetch=5,
        grid=(bsz, nb),
        in_specs=[
            pl.BlockSpec((1, TB, d), lambda b, j, *_: (b, j, 0)),
            pl.BlockSpec((1, TOP_K, TB), lambda b, j, *_: (b, 0, j)),
            pl.BlockSpec((1, TOP_K, TB), lambda b, j, *_: (b, 0, j)),
            pl.BlockSpec((1, N_EXPERTS, 1), lambda b, j, *_: (b * nb + j, 0, 0)),
        ],
        out_specs=pl.BlockSpec(memory_space=pl.ANY),
        scratch_shapes=[pltpu.VMEM((2, PERM_ROWS, words), U32),
                        pltpu.VMEM((RUN_ALIGN << (GAP_BITS - 1), words), U32),
                        pltpu.SemaphoreType.DMA((3,))],
    )
    return pl.pallas_call(
        functools.partial(_dispatch_kernel, nb=nb),
        grid_spec=grid_spec,
        out_shape=jax.ShapeDtypeStruct((p_total, words), U32),
        compiler_params=_cparams("arbitrary", "arbitrary"),
        name="moe_dispatch",
    )(hoff, loffs, ngr, goff, ggr, f, idx, wts, loff_col)


def _moe_kernel(te_ref, nt_ref, x_ref, wg_ref, wu_ref, wd_ref, y_ref, wg_s, wu_s, wd_s):
    i = pl.program_id(0)
    fresh = jnp.logical_or(i == 0, te_ref[i] != te_ref[jnp.maximum(i - 1, 0)])

    @pl.when(fresh)
    def _():
        wg_s[...] = wg_ref[0].astype(BF16)
        wu_s[...] = wu_ref[0].astype(BF16)
        wd_s[...] = wd_ref[0].astype(BF16)

    @pl.when(i < nt_ref[0])
    def _():
        half = wg_s.shape[0] // 2
        lo, hi = _unpack_bf16_pairs(x_ref[:, :half])
        gate = lax.bitcast_convert_type(x_ref[:, half:half + 1], F32)
        g = (jnp.dot(lo, wg_s[:half, :], preferred_element_type=F32)
             + jnp.dot(hi, wg_s[half:, :], preferred_element_type=F32))
        u = (jnp.dot(lo, wu_s[:half, :], preferred_element_type=F32)
             + jnp.dot(hi, wu_s[half:, :], preferred_element_type=F32))
        he = (_silu(g) * u).astype(BF16)
        y_ref[...] = _pack_bf16_pairs(jnp.dot(he, wd_s[...], preferred_element_type=F32) * gate)

    @pl.when(i >= nt_ref[0])
    def _():
        y_ref[...] = jnp.zeros_like(y_ref)


def _moe_grouped(x_sorted, tile_expert, n_tiles_used, w_gate, w_up, w_down):
    p, words = x_sorted.shape
    _, d, de = w_gate.shape
    tm = MOE_TILE
    grid_spec = pltpu.PrefetchScalarGridSpec(
        num_scalar_prefetch=2,
        grid=(p // tm,),
        in_specs=[
            pl.BlockSpec((tm, words), lambda i, te, nt: (i, 0)),
            pl.BlockSpec((1, d, de), lambda i, te, nt: (te[i], 0, 0)),
            pl.BlockSpec((1, d, de), lambda i, te, nt: (te[i], 0, 0)),
            pl.BlockSpec((1, de, d), lambda i, te, nt: (te[i], 0, 0)),
        ],
        out_specs=pl.BlockSpec((tm, d // 2), lambda i, te, nt: (i, 0)),
        scratch_shapes=[pltpu.VMEM((d, de), BF16), pltpu.VMEM((d, de), BF16), pltpu.VMEM((de, d), BF16)],
    )
    return pl.pallas_call(
        _moe_kernel,
        grid_spec=grid_spec,
        out_shape=jax.ShapeDtypeStruct((p, d // 2), U32),
        compiler_params=_cparams("arbitrary"),
        name="moe_grouped",
    )(tile_expert, n_tiles_used, x_sorted, w_gate, w_up, w_down)


def _combine_kernel(hoff_ref, loffs_ref, ngr_ref, h_ref, idx_ref, loff_ref, mod_ref, ys_ref, o_ref,
                    yperm, sems, *, nb):
    b, j = pl.program_id(0), pl.program_id(1)
    nj = pl.num_programs(1)
    step = b * nj + j
    slot = step % 2
    blk = b * nb + j
    nxt = jnp.where(j == nj - 1, (b + 1) * nb, blk + 1)

    def run_copies(blk_, slot_, wait):
        def fn(lo, ho, rows):
            cp = pltpu.make_async_copy(ys_ref.at[pl.ds(ho, rows)], yperm.at[slot_, pl.ds(lo, rows)], sems.at[slot_])
            cp.wait() if wait else cp.start()
        _for_each_run_piece(blk_, hoff_ref, loffs_ref, ngr_ref, fn)

    @pl.when(step == 0)
    def _():
        yperm[...] = jnp.zeros_like(yperm)
        run_copies(blk, 0, wait=False)

    @pl.when(step + 1 < pl.num_programs(0) * nj)
    def _():
        run_copies(nxt, 1 - slot, wait=False)

    q0, q1 = _block_perm(idx_ref, loff_ref)
    q = jnp.where(jnp.logical_or(q0, q1), 1.0, 0.0).astype(BF16)
    run_copies(blk, slot, wait=True)
    lo, hi = _unpack_bf16_pairs(yperm[slot])
    y = jnp.concatenate([_tn_dot(q, lo), _tn_dot(q, hi)], axis=1)
    o_ref[0] = h_ref[0] + mod_ref[0, 5:6, :] * y


def _combine(h, idx, loff_col, mod, mod_row, y_sorted, hoff, loffs, ngr, n_out):
    bsz, ltot, d = h.shape
    nb = ltot // TB
    grid_spec = pltpu.PrefetchScalarGridSpec(
        num_scalar_prefetch=3,
        grid=(bsz, n_out // TB),
        in_specs=[
            pl.BlockSpec((1, TB, d), lambda b, j, *_: (b, j, 0)),
            pl.BlockSpec((1, TOP_K, TB), lambda b, j, *_: (b, 0, j)),
            pl.BlockSpec((1, N_EXPERTS, 1), lambda b, j, *_: (b * nb + j, 0, 0)),
            pl.BlockSpec((1, 6, d), lambda b, j, *_: (mod_row(b, j), 0, 0)),
            pl.BlockSpec(memory_space=pl.ANY),
        ],
        out_specs=pl.BlockSpec((1, TB, d), lambda b, j, *_: (b, j, 0)),
        scratch_shapes=[pltpu.VMEM((2, PERM_ROWS, d // 2), U32), pltpu.SemaphoreType.DMA((2,))],
    )
    return pl.pallas_call(
        functools.partial(_combine_kernel, nb=nb),
        grid_spec=grid_spec,
        out_shape=jax.ShapeDtypeStruct((bsz, n_out, d), F32),
        compiler_params=_cparams("arbitrary", "arbitrary"),
        name="moe_combine",
    )(hoff, loffs, ngr, h, idx, loff_col, mod, y_sorted)


def _moe_block(h, f, idx, wts, mod, mod_row, w_gate, w_up, w_down, n_out):
    bsz, ltot, d = h.shape
    nb = ltot // TB
    n_blocks = bsz * nb
    mt = MOE_TILE
    experts = jnp.arange(N_EXPERTS, dtype=jnp.int32)
    cnt = jnp.sum((idx.reshape(bsz, TOP_K, nb, TB, 1) == experts).astype(jnp.int32), axis=(1, 3))
    cnt = cnt.reshape(n_blocks, N_EXPERTS)
    run = ((cnt + RUN_ALIGN - 1) // RUN_ALIGN) * RUN_ALIGN
    loffs = jnp.cumsum(run, axis=1) - run
    region = ((jnp.sum(run, axis=0) + mt - 1) // mt) * mt
    ends = jnp.cumsum(region)
    hoff = (ends - region)[None, :] + jnp.cumsum(run, axis=0) - run
    p_total = ((TOP_K * bsz * ltot + n_blocks * N_EXPERTS * (RUN_ALIGN - 1)) // mt + N_EXPERTS + 1) * mt
    tile_start = jnp.arange(p_total // mt, dtype=jnp.int32) * mt
    tile_expert = jnp.minimum(jnp.sum((tile_start[:, None] >= ends[None, :]).astype(jnp.int32), axis=1),
                              N_EXPERTS - 1)
    n_used = (ends[-1] // mt).astype(jnp.int32).reshape(1)
    flat = lambda t: t.reshape(-1).astype(jnp.int32)
    hoff, loffs_flat, ngr = flat(hoff), flat(loffs), flat(run // RUN_ALIGN)
    loff_col = loffs.astype(F32).reshape(n_blocks, N_EXPERTS, 1)
    used = jnp.sum(run, axis=0)
    goff = flat(jnp.concatenate([ends - region + used, ends[-1:]]))
    ggr = flat(jnp.concatenate([(region - used) // RUN_ALIGN, (p_total - ends[-1:]) // TB]))

    x_sorted = _dispatch(f, idx, wts, loff_col, hoff, loffs_flat, ngr, goff, ggr, p_total)
    y_sorted = _moe_grouped(x_sorted, tile_expert, n_used, w_gate, w_up, w_down)
    return _combine(h, idx, loff_col, mod, mod_row, y_sorted, hoff, loffs_flat, ngr, n_out)


def _rope_tables(n_lat, n_ctx, n_heads):
    t = jnp.arange(n_lat)
    pos = jnp.stack([t // GRID_W, t % GRID_W], axis=-1).astype(F32)
    n_freq = HEAD_DIM // 4
    inv = 1.0 / (ROPE_BASE ** (jnp.arange(n_freq, dtype=F32) / n_freq))
    ang = pos[:, :, None] * inv
    cos, sin = jnp.cos(ang), jnp.sin(ang)
    cos_h = jnp.concatenate([cos[:, 0], cos[:, 0], cos[:, 1], cos[:, 1]], axis=-1)
    sin_h = jnp.concatenate([-sin[:, 0], sin[:, 0], -sin[:, 1], sin[:, 1]], axis=-1)
    cos_h = jnp.concatenate([cos_h, jnp.ones((n_ctx, HEAD_DIM), F32)], axis=0)
    sin_h = jnp.concatenate([sin_h, jnp.zeros((n_ctx, HEAD_DIM), F32)], axis=0)
    return jnp.tile(cos_h, (1, n_heads)), jnp.tile(sin_h, (1, n_heads))


def _swap_halves_perm(width):
    j = np.arange(width)
    nf = HEAD_DIM // 4
    return np.where((j % (2 * nf)) < nf, j + nf, j - nf)


def _pack_w_in(w_in, rw, sw, nw_):
    o = 0
    seg = {}
    for name, width in (("rq", rw), ("rk", rw), ("rv", rw), ("rg", rw), ("z", sw), ("x", sw),
                        ("b", SSD_GROUPS * SSD_STATE), ("c", SSD_GROUPS * SSD_STATE),
                        ("dt", 2 * (sw // HEAD_DIM)), ("nq", nw_), ("nk", nw_), ("nv", nw_)):
        seg[name] = w_in[:, o:o + width]
        o += width
    perm = _swap_halves_perm(rw)
    dt_pad = jnp.zeros((w_in.shape[0], 128 - seg["dt"].shape[1]), w_in.dtype)
    cols = [seg["rq"], seg["rk"], seg["rq"][:, perm], seg["rk"][:, perm], seg["rv"], seg["rg"],
            seg["z"], seg["x"], seg["b"], seg["c"], seg["nq"], seg["nk"], seg["nv"], seg["dt"], dt_pad]
    return jnp.concatenate(cols, axis=1).astype(BF16)


def kernel(x, c, ctx, c_ctx, w_mod, b_mod, norm_mix, norm_ffn, w_in, w_out, ret_decay_f, ret_decay_b,
           ssd_conv_w, ssd_conv_b, ssd_dt_bias_f, ssd_dt_bias_b, ssd_a_log_f, ssd_a_log_b, ssd_d, ssd_norm,
           na_q_norm, na_k_norm, na_rpb, router_w, router_b, w_gate, w_up, w_down):
    bsz, n_lat, d = x.shape
    n_ctx = ctx.shape[1]
    depth = w_mod.shape[0]
    rows = n_lat // GRID_W
    rw = ret_decay_f.shape[1] * HEAD_DIM
    sw = ssd_d.shape[1] * HEAD_DIM
    nw_ = na_rpb.shape[1] * HEAD_DIM
    dims = (rw, sw, nw_)
    assert n_lat % TB == 0 and n_ctx % TB == 0
    nbl = n_lat // TB

    n_cond = ((bsz + 1 + 7) // 8) * 8
    cond = jnp.zeros((n_cond, d), F32).at[:bsz].set(c).at[bsz].set(c_ctx)
    mod = _modulation(cond, w_mod, b_mod).reshape(depth, n_cond, 6, d)
    mod_row = lambda b, j: jnp.where(j < nbl, b, bsz)

    cos, sin = _rope_tables(n_lat, n_ctx, rw // HEAD_DIM)
    hm = jnp.asarray(np.kron(np.eye(nw_ // HEAD_DIM), np.full((HEAD_DIM, HEAD_DIM), 1.0 / HEAD_DIM)), F32)
    rw_hi = router_w.astype(BF16)
    rw_mid = (router_w - rw_hi.astype(F32)).astype(BF16)
    rw_t = jnp.pad(jnp.concatenate([rw_hi, rw_mid], axis=1), ((0, 0), (0, 128 - 2 * router_w.shape[1])))
    rb = router_b.reshape(-1, 1)

    h = jnp.concatenate([x, ctx], axis=1)
    for l in range(depth):
        last = l == depth - 1
        w_all = _pack_w_in(w_in[l], rw, sw, nw_)
        qkn = jnp.stack([jnp.tile(na_q_norm[l], nw_ // HEAD_DIM), jnp.tile(na_k_norm[l], nw_ // HEAD_DIM)])
        ret, z, xbc, dt, na = _inproj(h, mod[l], mod_row, norm_mix[l].reshape(1, d), w_all, cos, sin, qkn, hm, dims)
        ret_o = _retention(ret, jnp.concatenate([ret_decay_f[l], ret_decay_b[l]]).reshape(-1, 1, 1), nbl)
        ssd_o = _ssd(z, xbc, dt, ssd_conv_w[l], ssd_conv_b[l],
                     jnp.concatenate([ssd_dt_bias_f[l], ssd_dt_bias_b[l]]),
                     jnp.concatenate([ssd_a_log_f[l], ssd_a_log_b[l]]), ssd_d[l], ssd_norm[l], nbl)
        na_o = _na_attention(na, _na_bias_tables(na_rpb[l], rows), nbl)
        h, f, idx, wts = _outproj(h, ret_o, ssd_o, na_o, w_out[l].astype(BF16), mod[l], mod_row,
                                  norm_ffn[l].reshape(1, d), rw_t, rb)
        h = _moe_block(h, f, idx, wts, mod[l], mod_row, w_gate[l], w_up[l], w_down[l],
                       n_lat if last else n_lat + n_ctx)
    return h
```

```python
import functools

import jax
import jax.numpy as jnp
import numpy as np
from jax import lax
from jax.experimental import pallas as pl
from jax.experimental.pallas import tpu as pltpu

F32 = jnp.float32
BF16 = jnp.bfloat16
HIGHEST = lax.Precision.HIGHEST

GRID_W = 64
HEAD_DIM = 64
SSD_GROUPS = 2
SSD_STATE = 64
NA_WIN_ROWS = 8
NA_WIN_COLS = 16
N_GROUPS = 4
EXPERTS_PER_GROUP = 4
N_EXPERTS = N_GROUPS * EXPERTS_PER_GROUP
TOP_K = 2
ROPE_BASE = 10000.0
EPS = 1e-6

VMEM_LIMIT_BYTES = 56 * 1024 * 1024
TB = 256
NA_QROWS = TB // GRID_W
MOE_TILE = 512
SUBLANES = 8


def _cparams(*sem):
    return pltpu.CompilerParams(dimension_semantics=sem, vmem_limit_bytes=VMEM_LIMIT_BYTES)


def _silu(t):
    return t * (1.0 / (1.0 + jnp.exp(-t)))


def _nt_dot(a, b, **kw):
    return lax.dot_general(a, b, (((1,), (1,)), ((), ())), preferred_element_type=F32, **kw)


def _tn_dot(a, b):
    return lax.dot_general(a, b, (((0,), (0,)), ((), ())), preferred_element_type=F32)


def _bf16_parts(x, n):
    parts, rest = [], x
    for i in range(n):
        p = rest.astype(BF16)
        parts.append(p)
        if i + 1 < n:
            rest = rest - p.astype(F32)
    return parts


def _split_dot(x, sel, parts=3, left=False):
    sel = sel.astype(BF16)
    dot = (lambda p: jnp.dot(sel, p, preferred_element_type=F32)) if left else (
        lambda p: jnp.dot(p, sel, preferred_element_type=F32))
    out = None
    for p in _bf16_parts(x, parts):
        out = dot(p) if out is None else out + dot(p)
    return out


def _mod_kernel(s_ref, w_ref, b_ref, o_ref):
    s = _silu(s_ref[...])
    o_ref[0] = jnp.dot(s, w_ref[0], precision=HIGHEST, preferred_element_type=F32) + b_ref[0]


def _modulation(cond, w_mod, b_mod):
    depth, d, n = w_mod.shape
    rows = cond.shape[0]
    tn = 1024
    return pl.pallas_call(
        _mod_kernel,
        grid=(depth, n // tn),
        in_specs=[
            pl.BlockSpec((rows, d), lambda l, j: (0, 0)),
            pl.BlockSpec((1, d, tn), lambda l, j: (l, 0, j)),
            pl.BlockSpec((1, 1, tn), lambda l, j: (l, 0, j)),
        ],
        out_specs=pl.BlockSpec((1, rows, tn), lambda l, j: (l, 0, j)),
        out_shape=jax.ShapeDtypeStruct((depth, rows, n), F32),
        compiler_params=_cparams("parallel", "parallel"),
        name="modulation",
    )(cond, w_mod, b_mod.reshape(depth, 1, n))


def _inproj_kernel(h_ref, mod_ref, nw_ref, w_ref, cos_ref, sin_ref, qkn_ref, hm_ref,
                   ret_ref, z_ref, xbc_ref, dt_ref, na_ref, *, dims):
    rw, sw, nw_ = dims
    h = h_ref[0]
    a = h * lax.rsqrt(jnp.mean(h * h, axis=-1, keepdims=True) + EPS) * nw_ref[...]
    a = a * (1.0 + mod_ref[0, 1:2, :]) + mod_ref[0, 0:1, :]
    ab = a.astype(BF16)

    def proj(lo, hi):
        return jnp.dot(ab, w_ref[:, lo:hi], preferred_element_type=F32)

    cos, sin = cos_ref[...], sin_ref[...]
    o = 0
    ret_ref[0, :, 0:rw] = proj(o, o + rw) * cos + proj(o + 2 * rw, o + 3 * rw) * sin
    ret_ref[0, :, rw:2 * rw] = (proj(o + rw, o + 2 * rw) * cos + proj(o + 3 * rw, o + 4 * rw) * sin) * (HEAD_DIM ** -0.5)
    ret_ref[0, :, 2 * rw:4 * rw] = proj(o + 4 * rw, o + 6 * rw)
    o += 6 * rw
    z_ref[0] = proj(o, o + sw)
    o += sw
    xbc_w = sw + 2 * SSD_GROUPS * SSD_STATE
    xbc_ref[0] = proj(o, o + xbc_w)
    o += xbc_w
    hm = hm_ref[...]
    for i in range(2):
        t = proj(o + i * nw_, o + (i + 1) * nw_)
        ms = _split_dot(t * t, hm, parts=2)
        na_ref[0, :, i * nw_:(i + 1) * nw_] = (t * lax.rsqrt(ms + EPS) * qkn_ref[i:i + 1, :]).astype(BF16)
    na_ref[0, :, 2 * nw_:3 * nw_] = proj(o + 2 * nw_, o + 3 * nw_).astype(BF16)
    o += 3 * nw_
    dt_ref[0] = proj(o, o + 128)[:, 0:dt_ref.shape[2]]


def _inproj(h, mod, mod_row, norm_w, w_all, cos, sin, qkn, hm, dims):
    bsz, ltot, d = h.shape
    rw, sw, nw_ = dims
    n_dt = 2 * (sw // HEAD_DIM)
    xbc_w = sw + 2 * SSD_GROUPS * SSD_STATE
    row = lambda n: pl.BlockSpec((1, TB, n), lambda b, j: (b, j, 0))
    const = lambda shape: pl.BlockSpec(shape, lambda b, j: (0,) * len(shape))
    widths = (4 * rw, sw, xbc_w, n_dt, 3 * nw_)
    dtypes = (F32, F32, F32, F32, BF16)
    return pl.pallas_call(
        functools.partial(_inproj_kernel, dims=dims),
        grid=(bsz, ltot // TB),
        in_specs=[
            row(d),
            pl.BlockSpec((1, 6, d), lambda b, j: (mod_row(b, j), 0, 0)),
            const((1, d)),
            const(w_all.shape),
            pl.BlockSpec((TB, rw), lambda b, j: (j, 0)),
            pl.BlockSpec((TB, rw), lambda b, j: (j, 0)),
            const((2, nw_)),
            const((nw_, nw_)),
        ],
        out_specs=[row(n) for n in widths],
        out_shape=[jax.ShapeDtypeStruct((bsz, ltot, n), dt) for n, dt in zip(widths, dtypes)],
        compiler_params=_cparams("parallel", "parallel"),
        name="inproj",
    )(h, mod, norm_w, w_all, cos, sin, qkn, hm)


def _scan_chunk_maps(nb, nbl):
    fwd = lambda s: jnp.where(s < nb, (nbl + s) % nb, 2 * nb - 1 - s)
    out = lambda s: jnp.where(s < nb, nb - 1, 2 * nb - 1 - s)
    return fwd, out


def _ret_phases(q_ref, k_ref, v_ref, g_ref, dec_ref, o_ref, sf_all, sf, sb, dmat, rd, rdt, *, nb, nbl):
    s = pl.program_id(1)
    c = TB
    nh = k_ref.shape[2] // HEAD_DIM
    log_f = [-jnp.exp(dec_ref[h]) for h in range(nh)]
    log_b = [-jnp.exp(dec_ref[nh + h]) for h in range(nh)]

    def _init():
        sf[...] = jnp.zeros_like(sf)
        sb[...] = jnp.zeros_like(sb)
        delta = (lax.broadcasted_iota(jnp.int32, (c, c), 0) - lax.broadcasted_iota(jnp.int32, (c, c), 1)).astype(F32)
        pos = lax.broadcasted_iota(jnp.int32, (c, HEAD_DIM), 0).astype(F32)
        pos_l = lax.broadcasted_iota(jnp.int32, (SUBLANES, c), 1).astype(F32)
        for h in range(nh):
            dmat[h] = (jnp.exp(jnp.where(delta >= 0, log_f[h] * delta, -jnp.inf))
                       + jnp.exp(jnp.where(delta <= 0, -log_b[h] * delta, -jnp.inf)))
            rd[2 * h + 0] = jnp.exp(log_f[h] * (pos + 1.0))
            rd[2 * h + 1] = jnp.exp(log_b[h] * (c - pos))
            rdt[2 * h + 0] = jnp.exp(log_f[h] * (c - 1.0 - pos_l))
            rdt[2 * h + 1] = jnp.exp(log_b[h] * pos_l)

    def _state_sweep():
        ci = (nbl + s) % nb
        k_t = k_ref[0].T
        for h in range(nh):
            sl = slice(h * HEAD_DIM, (h + 1) * HEAD_DIM)
            kw = (k_t[sl, :] * rdt[2 * h + 0, 0:1, :]).astype(BF16)
            sf_all[ci, h] = sf[h]
            sf[h] = sf[h] * jnp.exp(log_f[h] * c) + jnp.dot(kw, v_ref[0, :, sl].astype(BF16),
                                                            preferred_element_type=F32)

    def _output_sweep():
        ci = 2 * nb - 1 - s
        k_t = k_ref[0].T
        for h in range(nh):
            sl = slice(h * HEAD_DIM, (h + 1) * HEAD_DIM)
            qb, vb = q_ref[0, :, sl].astype(BF16), v_ref[0, :, sl].astype(BF16)
            p = jnp.dot(qb, k_t[sl, :].astype(BF16), preferred_element_type=F32) * dmat[h]
            y = jnp.dot(p.astype(BF16), vb, preferred_element_type=F32)
            y += jnp.dot(qb, sf_all[ci, h].astype(BF16), preferred_element_type=F32) * rd[2 * h + 0]
            y += jnp.dot(qb, sb[h].astype(BF16), preferred_element_type=F32) * rd[2 * h + 1]
            kw = (k_t[sl, :] * rdt[2 * h + 1, 0:1, :]).astype(BF16)
            sb[h] = sb[h] * jnp.exp(log_b[h] * c) + jnp.dot(kw, vb, preferred_element_type=F32)
            mu = jnp.mean(y, axis=-1, keepdims=True)
            yc = y - mu
            yn = yc * lax.rsqrt(jnp.mean(yc * yc, axis=-1, keepdims=True) + EPS)
            o_ref[0, :, sl] = (yn * _silu(g_ref[0, :, sl])).astype(o_ref.dtype)

    return _init, _state_sweep, _output_sweep


def _softplus(t):
    return jnp.maximum(t, 0.0) + jnp.log1p(jnp.exp(-jnp.abs(t)))


def _lane_group_selector(n_rows, group):
    shape = (n_rows, n_rows * group)
    lane_owner = lax.broadcasted_iota(jnp.int32, shape, 1) // group
    return jnp.where(lane_owner == lax.broadcasted_iota(jnp.int32, shape, 0), 1.0, 0.0).astype(BF16)


def _ssd_phases(z_ref, x_ref, xp_ref, xn_ref, dt_ref, cw_ref, cb_ref, dtb_ref, alog_ref, dtbt_ref, alogt_ref,
                dsk_ref, nw_ref, o_ref, sf_all, sf, sb, tri, ybuf, *, nb, nbl):
    s = pl.program_id(1)
    c = TB
    sw = z_ref.shape[2]
    nh = sw // HEAD_DIM
    rep = nh // SSD_GROUPS
    gw = rep * HEAD_DIM
    gs = SSD_GROUPS * SSD_STATE
    ci = jnp.where(s < nb, (nbl + s) % nb, 2 * nb - 1 - s)

    def _init():
        sf[...] = jnp.zeros_like(sf)
        sb[...] = jnp.zeros_like(sb)
        li = lax.broadcasted_iota(jnp.int32, (c, c), 0)
        si = lax.broadcasted_iota(jnp.int32, (c, c), 1)
        tri[0] = jnp.where(si <= li, 1.0, 0.0).astype(BF16)
        tri[1] = jnp.where(si >= li, 1.0, 0.0).astype(BF16)

    def _sweeps():
        x = x_ref[0]
        has_prev = jnp.where(jnp.logical_or(ci == 0, ci == nbl), 0.0, 1.0)
        has_next = jnp.where(jnp.logical_or(ci == nbl - 1, ci == nb - 1), 0.0, 1.0)
        rows = lax.broadcasted_iota(jnp.int32, x.shape, 0)
        x_dn = jnp.where(rows == 0, xp_ref[0, SUBLANES - 1:SUBLANES, :] * has_prev, pltpu.roll(x, 1, 0))
        x_up = jnp.where(rows == c - 1, xn_ref[0, 0:1, :] * has_next, pltpu.roll(x, c - 1, 0))
        xc = _silu(cw_ref[0:1, :] * x_dn + cw_ref[1:2, :] * x + cw_ref[2:3, :] * x_up + cb_ref[...])
        xs, bm, cm = xc[:, :sw], xc[:, sw:sw + gs], xc[:, sw + gs:]
        bm_t = bm.T.astype(BF16)

        dtr = dt_ref[0]
        dt = _softplus(dtr + dtb_ref[...])
        a = dt * -jnp.exp(alog_ref[...])
        pre = _split_dot(a, tri[0], left=True)
        tot = pre[c - 1:c, :]
        is_fwd = lax.broadcasted_iota(jnp.int32, (1, 2 * nh), 1) < nh
        acs = jnp.where(is_fwd, pre, tot - pre + a)
        sel_head = _lane_group_selector(2 * nh, HEAD_DIM)
        w_wide = _split_dot(dt * jnp.exp(tot - acs), sel_head, parts=2)
        etot_wide = _split_dot(jnp.broadcast_to(jnp.exp(tot), (SUBLANES, 2 * nh)), sel_head, parts=2)[0:1, :]

        def _state_sweep():
            for g in range(SSD_GROUPS):
                lanes = slice(g * gw, (g + 1) * gw)
                xw = (xs[:, lanes] * w_wide[:, lanes]).astype(BF16)
                sf_all[ci, g] = sf[g]
                sf[g] = sf[g] * etot_wide[:, lanes] + jnp.dot(bm_t[g * SSD_STATE:(g + 1) * SSD_STATE, :], xw,
                                                              preferred_element_type=F32)

        def _output_sweep():
            eye = (lax.broadcasted_iota(jnp.int32, (2 * nh, 2 * nh), 0)
                   == lax.broadcasted_iota(jnp.int32, (2 * nh, 2 * nh), 1)).astype(BF16)
            dtr_t = sum(_nt_dot(eye, p) for p in _bf16_parts(dtr, 3))
            dt_t = _softplus(dtr_t + dtbt_ref[...])
            a_t = dt_t * -jnp.exp(alogt_ref[...])
            pre_t = _split_dot(a_t, tri[1])
            is_fwd_t = lax.broadcasted_iota(jnp.int32, (2 * nh, 1), 0) < nh
            acs_t = jnp.where(is_fwd_t, pre_t, pre_t[:, c - 1:c] - pre_t + a_t)
            e_wide = _split_dot(jnp.exp(acs), sel_head, parts=2)
            li = lax.broadcasted_iota(jnp.int32, (c, c), 0)
            si = lax.broadcasted_iota(jnp.int32, (c, c), 1)
            lower, upper = li >= si, si >= li
            for g in range(SSD_GROUPS):
                lanes = slice(g * gw, (g + 1) * gw)
                lanes_b = slice(sw + g * gw, sw + (g + 1) * gw)
                cg = cm[:, g * SSD_STATE:(g + 1) * SSD_STATE].astype(BF16)
                bg_t = bm_t[g * SSD_STATE:(g + 1) * SSD_STATE, :]
                scores = jnp.dot(cg, bg_t, preferred_element_type=F32)
                for r in range(rep):
                    h = g * rep + r
                    hb = nh + h
                    col_f, col_b = acs[:, h:h + 1], acs[:, hb:hb + 1]
                    d_f = jnp.exp(jnp.where(lower, col_f - acs_t[h:h + 1, :], -jnp.inf)) * dt_t[h:h + 1, :]
                    d_b = jnp.exp(jnp.where(upper, col_b - acs_t[hb:hb + 1, :], -jnp.inf)) * dt_t[hb:hb + 1, :]
                    ybuf[:, h * HEAD_DIM:(h + 1) * HEAD_DIM] = jnp.dot(
                        (scores * (d_f + d_b)).astype(BF16), xs[:, h * HEAD_DIM:(h + 1) * HEAD_DIM].astype(BF16),
                        preferred_element_type=F32)
                ybuf[:, lanes] += (jnp.dot(cg, sf_all[ci, g].astype(BF16), preferred_element_type=F32) * e_wide[:, lanes]
                                   + jnp.dot(cg, sb[g].astype(BF16), preferred_element_type=F32) * e_wide[:, lanes_b])
                xw = (xs[:, lanes] * w_wide[:, lanes_b]).astype(BF16)
                sb[g] = sb[g] * etot_wide[:, lanes_b] + jnp.dot(bg_t, xw, preferred_element_type=F32)
            y = (ybuf[...] + dsk_ref[...] * xs) * _silu(z_ref[0])
            o_ref[0] = (y * lax.rsqrt(jnp.mean(y * y, axis=-1, keepdims=True) + EPS) * nw_ref[...]).astype(o_ref.dtype)

        return _state_sweep, _output_sweep

    return _init, _sweeps


N_RET_IN, N_SSD_IN, N_RET_SCRATCH = 5, 13, 6


def _scan_mixers_kernel(*refs, nb, nbl):
    ret_in, refs = refs[:N_RET_IN], refs[N_RET_IN:]
    ssd_in, refs = refs[:N_SSD_IN], refs[N_SSD_IN:]
    (ret_o, ssd_o), refs = refs[:2], refs[2:]
    ret_init, ret_state, ret_out = _ret_phases(*ret_in, ret_o, *refs[:N_RET_SCRATCH], nb=nb, nbl=nbl)
    ssd_init, ssd_sweeps = _ssd_phases(*ssd_in, ssd_o, *refs[N_RET_SCRATCH:], nb=nb, nbl=nbl)
    s = pl.program_id(1)
    pl.when(s == 0)(ret_init)
    pl.when(s < nb)(ret_state)
    pl.when(s >= nb)(ret_out)
    pl.when(s == 0)(ssd_init)
    ssd_state, ssd_out = ssd_sweeps()
    pl.when(s < nb)(ssd_state)
    pl.when(s >= nb)(ssd_out)


def _scan_mixers(ret, decay, z, xbc, dt, conv_w, conv_b, dtb, alog, d_skip, norm_w, nbl):
    bsz, ltot, w4 = ret.shape
    w = w4 // 4
    nh_r = w // HEAD_DIM
    sw = z.shape[2]
    nh = sw // HEAD_DIM
    xw = xbc.shape[2]
    nb = ltot // TB
    per = TB // SUBLANES
    fwd, out = _scan_chunk_maps(nb, nbl)
    const = lambda t: pl.BlockSpec(t.shape, lambda b, s: (0,) * t.ndim)
    args = (conv_w, conv_b.reshape(1, xw), dtb.reshape(1, 2 * nh), alog.reshape(1, 2 * nh),
            dtb.reshape(2 * nh, 1), alog.reshape(2 * nh, 1), jnp.repeat(d_skip, HEAD_DIM).reshape(1, sw),
            norm_w.reshape(1, sw))
    ret_specs = [
        pl.BlockSpec((1, TB, w), lambda b, s: (b, out(s), 0)),
        pl.BlockSpec((1, TB, w), lambda b, s: (b, fwd(s), 1)),
        pl.BlockSpec((1, TB, w), lambda b, s: (b, fwd(s), 2)),
        pl.BlockSpec((1, TB, w), lambda b, s: (b, out(s), 3)),
        const(decay),
    ]
    ssd_specs = [
        pl.BlockSpec((1, TB, sw), lambda b, s: (b, out(s), 0)),
        pl.BlockSpec((1, TB, xw), lambda b, s: (b, fwd(s), 0)),
        pl.BlockSpec((1, SUBLANES, xw), lambda b, s: (b, jnp.maximum(fwd(s) * per - 1, 0), 0)),
        pl.BlockSpec((1, SUBLANES, xw), lambda b, s: (b, jnp.minimum((fwd(s) + 1) * per, nb * per - 1), 0)),
        pl.BlockSpec((1, TB, 2 * nh), lambda b, s: (b, fwd(s), 0)),
    ] + [const(t) for t in args]
    assert len(ret_specs) == N_RET_IN and len(ssd_specs) == N_SSD_IN
    ret_scratch = [
        pltpu.VMEM((nb, nh_r, HEAD_DIM, HEAD_DIM), F32),
        pltpu.VMEM((nh_r, HEAD_DIM, HEAD_DIM), F32),
        pltpu.VMEM((nh_r, HEAD_DIM, HEAD_DIM), F32),
        pltpu.VMEM((nh_r, TB, TB), F32),
        pltpu.VMEM((2 * nh_r, TB, HEAD_DIM), F32),
        pltpu.VMEM((2 * nh_r, SUBLANES, TB), F32),
    ]
    ssd_scratch = [
        pltpu.VMEM((nb, SSD_GROUPS, SSD_STATE, sw // SSD_GROUPS), F32),
        pltpu.VMEM((SSD_GROUPS, SSD_STATE, sw // SSD_GROUPS), F32),
        pltpu.VMEM((SSD_GROUPS, SSD_STATE, sw // SSD_GROUPS), F32),
        pltpu.VMEM((2, TB, TB), BF16),
        pltpu.VMEM((TB, sw), F32),
    ]
    assert len(ret_scratch) == N_RET_SCRATCH
    return pl.pallas_call(
        functools.partial(_scan_mixers_kernel, nb=nb, nbl=nbl),
        grid=(bsz, 2 * nb),
        in_specs=ret_specs + ssd_specs,
        out_specs=[pl.BlockSpec((1, TB, w), lambda b, s: (b, out(s), 0)),
                   pl.BlockSpec((1, TB, sw), lambda b, s: (b, out(s), 0))],
        out_shape=[jax.ShapeDtypeStruct((bsz, ltot, w), BF16), jax.ShapeDtypeStruct((bsz, ltot, sw), BF16)],
        scratch_shapes=ret_scratch + ssd_scratch,
        compiler_params=_cparams("parallel", "arbitrary"),
        name="scan_mixers",
    )(ret, ret, ret, ret, decay, z, xbc, xbc, xbc, dt, *args)


def _softmax_pv(s_parts, v):
    m = s_parts[0].max(axis=-1, keepdims=True)
    for s in s_parts[1:]:
        m = jnp.maximum(m, s.max(axis=-1, keepdims=True))
    p_parts = [jnp.exp(s - m) for s in s_parts]
    den = p_parts[0].sum(axis=-1, keepdims=True)
    for p in p_parts[1:]:
        den = den + p.sum(axis=-1, keepdims=True)
    p = p_parts[0] if len(p_parts) == 1 else jnp.concatenate(p_parts, axis=1)
    return jnp.dot(p.astype(BF16), v, preferred_element_type=F32) / den


def _na_kernel(q_ref, k0_ref, k1_ref, k2_ref, v0_ref, v1_ref, v2_ref, kc_ref, vc_ref, bias_ref, o_ref, *, nbl):
    nh = q_ref.shape[2] // HEAD_DIM
    n_lat_keys = 3 * k0_ref.shape[1]
    j = pl.program_id(1)

    @pl.when(j < nbl)
    def _latent():
        for h in range(nh):
            sl = slice(h * HEAD_DIM, (h + 1) * HEAD_DIM)
            q = q_ref[0, :, sl] * (HEAD_DIM ** -0.5)
            k = jnp.concatenate([k0_ref[0, :, sl], k1_ref[0, :, sl], k2_ref[0, :, sl], kc_ref[0, :, sl]], axis=0)
            v = jnp.concatenate([v0_ref[0, :, sl], v1_ref[0, :, sl], v2_ref[0, :, sl], vc_ref[0, :, sl]], axis=0)
            s = _nt_dot(q, k)
            o = _softmax_pv([s[:, :n_lat_keys] + bias_ref[0, h], s[:, n_lat_keys:]], v)
            o_ref[0, :, sl] = o.astype(o_ref.dtype)

    @pl.when(j >= nbl)
    def _context():
        for h in range(nh):
            sl = slice(h * HEAD_DIM, (h + 1) * HEAD_DIM)
            s = _nt_dot(q_ref[0, :, sl] * (HEAD_DIM ** -0.5), kc_ref[0, :, sl])
            o_ref[0, :, sl] = _softmax_pv([s], vc_ref[0, :, sl]).astype(o_ref.dtype)


def _na_bias_tables(rpb, rows):
    qb, kb = NA_QROWS, 3 * NA_QROWS
    nb = rows // qb
    nh = rpb.shape[0]
    n_dr, n_dc = 2 * NA_WIN_ROWS - 1, 2 * NA_WIN_COLS - 1
    c = np.arange(GRID_W)[:, None]
    kc = np.arange(GRID_W)[None, :]
    cs = np.clip(c - NA_WIN_COLS // 2, 0, GRID_W - NA_WIN_COLS)
    col_ok = (kc >= cs) & (kc < cs + NA_WIN_COLS)
    dc = np.clip(kc - c + NA_WIN_COLS - 1, 0, n_dc - 1)
    col_sel = np.eye(n_dc, dtype=np.float32)[dc.reshape(-1)].T
    tabs = []
    for rbq, wb in ((0, 0), (1, 0), (nb - 1, nb - 3)):
        r = qb * rbq + np.arange(qb)[:, None]
        kr = qb * wb + np.arange(kb)[None, :]
        r0 = np.clip(r - NA_WIN_ROWS // 2, 0, rows - NA_WIN_ROWS)
        row_ok = (kr >= r0) & (kr < r0 + NA_WIN_ROWS)
        dr = np.clip(kr - r + NA_WIN_ROWS - 1, 0, n_dr - 1)
        row_sel = np.eye(n_dr, dtype=np.float32)[dr.reshape(-1)]
        t = jnp.einsum("ad,hde,ef->haf", row_sel, rpb.astype(F32), col_sel, precision=HIGHEST)
        t = t.reshape(nh, qb, kb, GRID_W, GRID_W).transpose(0, 1, 3, 2, 4).reshape(nh, qb * GRID_W, kb * GRID_W)
        valid = (row_ok[:, None, :, None] & col_ok[None, :, None, :]).reshape(qb * GRID_W, kb * GRID_W)
        tabs.append(jnp.where(valid, t, -jnp.inf))
    return jnp.stack(tabs)


def _na_attention(na, bias, nbl):
    bsz, ltot, w3 = na.shape
    w = w3 // 3
    nb = ltot // TB
    assert nbl >= 3 and nb == nbl + 1, "needs >= 12 grid rows and a context of one token block"
    win = lambda j: jnp.clip(j - 1, 0, nbl - 3)
    kv = lambda i, col: pl.BlockSpec((1, TB, w), lambda b, j: (b, win(j) + i, col))
    typ = lambda j: jnp.where(j == 0, 0, jnp.where(j >= nbl - 1, 2, 1))
    return pl.pallas_call(
        functools.partial(_na_kernel, nbl=nbl),
        grid=(bsz, nb),
        in_specs=[
            pl.BlockSpec((1, TB, w), lambda b, j: (b, j, 0)),
            kv(0, 1), kv(1, 1), kv(2, 1), kv(0, 2), kv(1, 2), kv(2, 2),
            pl.BlockSpec((1, TB, w), lambda b, j: (b, nbl, 1)),
            pl.BlockSpec((1, TB, w), lambda b, j: (b, nbl, 2)),
            pl.BlockSpec((1,) + bias.shape[1:], lambda b, j: (typ(j), 0, 0, 0)),
        ],
        out_specs=pl.BlockSpec((1, TB, w), lambda b, j: (b, j, 0)),
        out_shape=jax.ShapeDtypeStruct((bsz, ltot, w), BF16),
        compiler_params=_cparams("parallel", "arbitrary"),
        name="na_attention",
    )(na, na, na, na, na, na, na, na, na, bias)


def _route(aff, sel):
    rows = [sel[e:e + 1, :] for e in range(N_EXPERTS)]
    arow = [aff[e:e + 1, :] for e in range(N_EXPERTS)]
    gscore = []
    for g in range(N_GROUPS):
        a, b, c, d = rows[4 * g:4 * g + 4]
        hi1, lo1 = jnp.maximum(a, b), jnp.minimum(a, b)
        hi2, lo2 = jnp.maximum(c, d), jnp.minimum(c, d)
        gscore.append(jnp.maximum(hi1, hi2) + jnp.maximum(jnp.minimum(hi1, hi2), jnp.maximum(lo1, lo2)))
    best = jnp.zeros_like(gscore[0], dtype=jnp.int32)
    top = gscore[0]
    for g in range(1, N_GROUPS):
        upd = gscore[g] > top
        best = jnp.where(upd, g, best)
        top = jnp.where(upd, gscore[g], top)
    sv, av = [], []
    for j in range(EXPERTS_PER_GROUP):
        s_j, a_j = rows[j], arow[j]
        for g in range(1, N_GROUPS):
            s_j = jnp.where(best == g, rows[4 * g + j], s_j)
            a_j = jnp.where(best == g, arow[4 * g + j], a_j)
        sv.append(s_j)
        av.append(a_j)
    picked = []
    for j in range(EXPERTS_PER_GROUP):
        rank = jnp.zeros_like(best)
        for i in range(EXPERTS_PER_GROUP):
            if i == j:
                continue
            ahead = (sv[i] >= sv[j]) if i < j else (sv[i] > sv[j])
            rank = rank + ahead.astype(jnp.int32)
        picked.append(rank < TOP_K)
    first = jnp.full_like(best, EXPERTS_PER_GROUP - 1)
    last = jnp.zeros_like(best)
    for j in range(EXPERTS_PER_GROUP - 1, -1, -1):
        first = jnp.where(picked[j], j, first)
    for j in range(EXPERTS_PER_GROUP):
        last = jnp.where(picked[j], j, last)
    a_first, a_last = av[0], av[0]
    for j in range(1, EXPERTS_PER_GROUP):
        a_first = jnp.where(first == j, av[j], a_first)
        a_last = jnp.where(last == j, av[j], a_last)
    tot = a_first + a_last
    idx = jnp.concatenate([best * EXPERTS_PER_GROUP + first, best * EXPERTS_PER_GROUP + last], axis=0)
    wts = jnp.concatenate([a_first / tot, a_last / tot], axis=0)
    return idx, wts


def _outproj_kernel(h_ref, ret_ref, ssd_ref, na_ref, wo_ref, mod_ref, nw_ref, rwt_ref, rb_ref,
                    hn_ref, f_ref, idx_ref, wts_ref):
    rw, sw = ret_ref.shape[2], ssd_ref.shape[2]
    acc = jnp.dot(ret_ref[0], wo_ref[0:rw, :], preferred_element_type=F32)
    acc += jnp.dot(ssd_ref[0], wo_ref[rw:rw + sw, :], preferred_element_type=F32)
    acc += jnp.dot(na_ref[0], wo_ref[rw + sw:, :], preferred_element_type=F32)
    hn = h_ref[0] + mod_ref[0, 2:3, :] * acc
    hn_ref[0] = hn
    f = hn * lax.rsqrt(jnp.mean(hn * hn, axis=-1, keepdims=True) + EPS) * nw_ref[...]
    f = f * (1.0 + mod_ref[0, 4:5, :]) + mod_ref[0, 3:4, :]
    f_ref[0] = f.astype(BF16)
    p = _split_dot(f, rwt_ref[...]).T
    logits = p[:N_EXPERTS, :] + p[N_EXPERTS:2 * N_EXPERTS, :] + p[2 * N_EXPERTS:3 * N_EXPERTS, :]
    aff = 1.0 / (1.0 + jnp.exp(-logits))
    idx, wts = _route(aff, aff + rb_ref[...])
    idx_ref[0] = idx
    wts_ref[0] = wts


def _outproj(h, ret_o, ssd_o, na_o, w_out, mod, mod_row, norm_w, rw_t, rb):
    bsz, ltot, d = h.shape
    row = lambda n: pl.BlockSpec((1, TB, n), lambda b, j: (b, j, 0))
    const = lambda shape: pl.BlockSpec(shape, lambda b, j: (0,) * len(shape))
    col = lambda: pl.BlockSpec((1, TOP_K, TB), lambda b, j: (b, 0, j))
    return pl.pallas_call(
        _outproj_kernel,
        grid=(bsz, ltot // TB),
        in_specs=[
            row(d), row(ret_o.shape[2]), row(ssd_o.shape[2]), row(na_o.shape[2]),
            const(w_out.shape),
            pl.BlockSpec((1, 6, d), lambda b, j: (mod_row(b, j), 0, 0)),
            const((1, d)), const(rw_t.shape), const(rb.shape),
        ],
        out_specs=[row(d), row(d), col(), col()],
        out_shape=[
            jax.ShapeDtypeStruct((bsz, ltot, d), F32),
            jax.ShapeDtypeStruct((bsz, ltot, d), BF16),
            jax.ShapeDtypeStruct((bsz, TOP_K, ltot), jnp.int32),
            jax.ShapeDtypeStruct((bsz, TOP_K, ltot), F32),
        ],
        compiler_params=_cparams("parallel", "parallel"),
        name="outproj_router",
    )(h, ret_o, ssd_o, na_o, w_out, mod, norm_w, rw_t, rb)


RUN_ALIGN = SUBLANES
RUN_BITS = 6
PERM_ROWS = TOP_K * TB + N_EXPERTS * RUN_ALIGN
GATE_LANES = 128
GAP_BITS = 6
assert TB == RUN_ALIGN << (RUN_BITS - 1) and MOE_TILE == RUN_ALIGN << GAP_BITS

U32 = jnp.uint32


def _pack_bf16_pairs(y):
    n = y.shape[1] // 2
    lo = lax.bitcast_convert_type(y[:, :n].astype(BF16).astype(F32), U32) >> 16
    hi = lax.bitcast_convert_type(y[:, n:].astype(BF16).astype(F32), U32) & jnp.uint32(0xFFFF0000)
    return lo | hi


def _unpack_bf16_pairs(u):
    lo = lax.bitcast_convert_type(u << 16, F32).astype(BF16)
    hi = lax.bitcast_convert_type(u & jnp.uint32(0xFFFF0000), F32).astype(BF16)
    return lo, hi


def _block_perm(idx_ref, loff_ref):
    tb = idx_ref.shape[2]
    na = TOP_K * tb
    e_iota = lax.broadcasted_iota(jnp.int32, (N_EXPERTS, tb), 0)
    oh = jnp.concatenate([jnp.where(e_iota == idx_ref[0, k:k + 1, :], 1.0, 0.0) for k in range(TOP_K)], axis=1)
    upper = lax.broadcasted_iota(jnp.int32, (na, na), 0) <= lax.broadcasted_iota(jnp.int32, (na, na), 1)
    cum = jnp.dot(oh.astype(BF16), jnp.where(upper, 1.0, 0.0).astype(BF16), preferred_element_type=F32)
    pos = jnp.sum(oh * (cum - 1.0 + loff_ref[0]), axis=0, keepdims=True).astype(jnp.int32)
    r = lax.broadcasted_iota(jnp.int32, (PERM_ROWS, tb), 0)
    return [r == pos[:, k * tb:(k + 1) * tb] for k in range(TOP_K)]


def _for_each_run_piece(blk, hoff_ref, loff_ref, ngr_ref, fn):
    for e in range(N_EXPERTS):
        j = blk * N_EXPERTS + e
        n, lo, ho = ngr_ref[j], loff_ref[j], hoff_ref[j]
        for b in range(RUN_BITS - 1, -1, -1):
            start = ((n >> (b + 1)) << (b + 1)) * RUN_ALIGN

            @pl.when(((n >> b) & 1) == 1)
            def _():
                fn(pl.multiple_of(lo + start, RUN_ALIGN), pl.multiple_of(ho + start, RUN_ALIGN), RUN_ALIGN << b)


def _for_each_gap_piece(goff_ref, ggr_ref, fn):
    for e in range(N_EXPERTS):
        n, ho = ggr_ref[e], goff_ref[e]
        for b in range(GAP_BITS - 1, -1, -1):
            start = ((n >> (b + 1)) << (b + 1)) * RUN_ALIGN

            @pl.when(((n >> b) & 1) == 1)
            def _():
                fn(pl.multiple_of(ho + start, RUN_ALIGN), RUN_ALIGN << b)


def _dispatch_kernel(hoff_ref, loffs_ref, ngr_ref, goff_ref, ggr_ref, f_ref, idx_ref, wts_ref, loff_ref, xs_ref,
                     xperm, zbuf, sems, *, nb):
    step = pl.program_id(0) * nb + pl.program_id(1)
    n_steps = pl.num_programs(0) * nb
    slot = step % 2

    def run_copies(blk, slot_, wait):
        def fn(lo, ho, rows):
            cp = pltpu.make_async_copy(xperm.at[slot_, pl.ds(lo, rows)], xs_ref.at[pl.ds(ho, rows)], sems.at[slot_])
            cp.wait() if wait else cp.start()
        _for_each_run_piece(blk, hoff_ref, loffs_ref, ngr_ref, fn)

    @pl.when(step == 0)
    def _zero_region_tails():
        zbuf[...] = jnp.zeros_like(zbuf)
        for wait in (False, True):
            def fn(ho, rows, wait=wait):
                cp = pltpu.make_async_copy(zbuf.at[pl.ds(0, rows)], xs_ref.at[pl.ds(ho, rows)], sems.at[2])
                cp.wait() if wait else cp.start()
            _for_each_gap_piece(goff_ref, ggr_ref, fn)
        tail_rows = zbuf.shape[0]

        def tail_copy(i):
            dst = xs_ref.at[pl.ds(pl.multiple_of(goff_ref[N_EXPERTS] + i * tail_rows, RUN_ALIGN), tail_rows)]
            return pltpu.make_async_copy(zbuf, dst, sems.at[2])

        lax.fori_loop(0, ggr_ref[N_EXPERTS], lambda i, c: (tail_copy(i).start(), c)[1], 0)
        lax.fori_loop(0, ggr_ref[N_EXPERTS], lambda i, c: (tail_copy(i).wait(), c)[1], 0)

    q0, q1 = _block_perm(idx_ref, loff_ref)
    q = jnp.where(jnp.logical_or(q0, q1), 1.0, 0.0).astype(BF16)
    half = f_ref.shape[2] // 2
    xperm[slot, :, :half] = _pack_bf16_pairs(jnp.dot(q, f_ref[0], preferred_element_type=F32))
    w = jnp.sum(jnp.where(q0, wts_ref[0, 0:1, :], 0.0) + jnp.where(q1, wts_ref[0, 1:2, :], 0.0),
                axis=1, keepdims=True)
    xperm[slot, :, half:] = jnp.broadcast_to(lax.bitcast_convert_type(w, U32), (PERM_ROWS, GATE_LANES))

    run_copies(step, slot, wait=False)

    @pl.when(step > 0)
    def _():
        run_copies(step - 1, 1 - slot, wait=True)

    @pl.when(step == n_steps - 1)
    def _():
        run_copies(step, slot, wait=True)


def _dispatch(f, idx, wts, loff_col, hoff, loffs, ngr, goff, ggr, p_total):
    bsz, ltot, d = f.shape
    nb = ltot // TB
    words = d // 2 + GATE_LANES
    grid_spec = pltpu.PrefetchScalarGridSpec(
        num_scalar_prefetch=5,
        grid=(bsz, nb),
        in_specs=[
            pl.BlockSpec((1, TB, d), lambda b, j, *_: (b, j, 0)),
            pl.BlockSpec((1, TOP_K, TB), lambda b, j, *_: (b, 0, j)),
            pl.BlockSpec((1, TOP_K, TB), lambda b, j, *_: (b, 0, j)),
            pl.BlockSpec((1, N_EXPERTS, 1), lambda b, j, *_: (b * nb + j, 0, 0)),
        ],
        out_specs=pl.BlockSpec(memory_space=pl.ANY),
        scratch_shapes=[pltpu.VMEM((2, PERM_ROWS, words), U32),
                        pltpu.VMEM((RUN_ALIGN << (GAP_BITS - 1), words), U32),
                        pltpu.SemaphoreType.DMA((3,))],
    )
    return pl.pallas_call(
        functools.partial(_dispatch_kernel, nb=nb),
        grid_spec=grid_spec,
        out_shape=jax.ShapeDtypeStruct((p_total, words), U32),
        compiler_params=_cparams("arbitrary", "arbitrary"),
        name="moe_dispatch",
    )(hoff, loffs, ngr, goff, ggr, f, idx, wts, loff_col)


def _moe_kernel(te_ref, nt_ref, x_ref, wg_ref, wu_ref, wd_ref, y_ref, wg_s, wu_s, wd_s):
    i = pl.program_id(0)
    fresh = jnp.logical_or(i == 0, te_ref[i] != te_ref[jnp.maximum(i - 1, 0)])

    @pl.when(fresh)
    def _():
        wg_s[...] = wg_ref[0].astype(BF16)
        wu_s[...] = wu_ref[0].astype(BF16)
        wd_s[...] = wd_ref[0].astype(BF16)

    @pl.when(i < nt_ref[0])
    def _():
        half = wg_s.shape[0] // 2
        lo, hi = _unpack_bf16_pairs(x_ref[:, :half])
        gate = lax.bitcast_convert_type(x_ref[:, half:half + 1], F32)
        g = (jnp.dot(lo, wg_s[:half, :], preferred_element_type=F32)
             + jnp.dot(hi, wg_s[half:, :], preferred_element_type=F32))
        u = (jnp.dot(lo, wu_s[:half, :], preferred_element_type=F32)
             + jnp.dot(hi, wu_s[half:, :], preferred_element_type=F32))
        he = (_silu(g) * u).astype(BF16)
        y_ref[...] = _pack_bf16_pairs(jnp.dot(he, wd_s[...], preferred_element_type=F32) * gate)

    @pl.when(i >= nt_ref[0])
    def _():
        y_ref[...] = jnp.zeros_like(y_ref)


def _moe_grouped(x_sorted, tile_expert, n_tiles_used, w_gate, w_up, w_down):
    p, words = x_sorted.shape
    _, d, de = w_gate.shape
    tm = MOE_TILE
    grid_spec = pltpu.PrefetchScalarGridSpec(
        num_scalar_prefetch=2,
        grid=(p // tm,),
        in_specs=[
            pl.BlockSpec((tm, words), lambda i, te, nt: (i, 0)),
            pl.BlockSpec((1, d, de), lambda i, te, nt: (te[i], 0, 0)),
            pl.BlockSpec((1, d, de), lambda i, te, nt: (te[i], 0, 0)),
            pl.BlockSpec((1, de, d), lambda i, te, nt: (te[i], 0, 0)),
        ],
        out_specs=pl.BlockSpec((tm, d // 2), lambda i, te, nt: (i, 0)),
        scratch_shapes=[pltpu.VMEM((d, de), BF16), pltpu.VMEM((d, de), BF16), pltpu.VMEM((de, d), BF16)],
    )
    return pl.pallas_call(
        _moe_kernel,
        grid_spec=grid_spec,
        out_shape=jax.ShapeDtypeStruct((p, d // 2), U32),
        compiler_params=_cparams("arbitrary"),
        name="moe_grouped",
    )(tile_expert, n_tiles_used, x_sorted, w_gate, w_up, w_down)


def _combine_kernel(hoff_ref, loffs_ref, ngr_ref, h_ref, idx_ref, loff_ref, mod_ref, ys_ref, o_ref,
                    yperm, sems, *, nb):
    b, j = pl.program_id(0), pl.program_id(1)
    nj = pl.num_programs(1)
    step = b * nj + j
    slot = step % 2
    blk = b * nb + j
    nxt = jnp.where(j == nj - 1, (b + 1) * nb, blk + 1)

    def run_copies(blk_, slot_, wait):
        def fn(lo, ho, rows):
            cp = pltpu.make_async_copy(ys_ref.at[pl.ds(ho, rows)], yperm.at[slot_, pl.ds(lo, rows)], sems.at[slot_])
            cp.wait() if wait else cp.start()
        _for_each_run_piece(blk_, hoff_ref, loffs_ref, ngr_ref, fn)

    @pl.when(step == 0)
    def _():
        yperm[...] = jnp.zeros_like(yperm)
        run_copies(blk, 0, wait=False)

    @pl.when(step + 1 < pl.num_programs(0) * nj)
    def _():
        run_copies(nxt, 1 - slot, wait=False)

    q0, q1 = _block_perm(idx_ref, loff_ref)
    q = jnp.where(jnp.logical_or(q0, q1), 1.0, 0.0).astype(BF16)
    run_copies(blk, slot, wait=True)
    lo, hi = _unpack_bf16_pairs(yperm[slot])
    y = jnp.concatenate([_tn_dot(q, lo), _tn_dot(q, hi)], axis=1)
    o_ref[0] = h_ref[0] + mod_ref[0, 5:6, :] * y


def _combine(h, idx, loff_col, mod, mod_row, y_sorted, hoff, loffs, ngr, n_out):
    bsz, ltot, d = h.shape
    nb = ltot // TB
    grid_spec = pltpu.PrefetchScalarGridSpec(
        num_scalar_prefetch=3,
        grid=(bsz, n_out // TB),
        in_specs=[
            pl.BlockSpec((1, TB, d), lambda b, j, *_: (b, j, 0)),
            pl.BlockSpec((1, TOP_K, TB), lambda b, j, *_: (b, 0, j)),
            pl.BlockSpec((1, N_EXPERTS, 1), lambda b, j, *_: (b * nb + j, 0, 0)),
            pl.BlockSpec((1, 6, d), lambda b, j, *_: (mod_row(b, j), 0, 0)),
            pl.BlockSpec(memory_space=pl.ANY),
        ],
        out_specs=pl.BlockSpec((1, TB, d), lambda b, j, *_: (b, j, 0)),
        scratch_shapes=[pltpu.VMEM((2, PERM_ROWS, d // 2), U32), pltpu.SemaphoreType.DMA((2,))],
    )
    return pl.pallas_call(
        functools.partial(_combine_kernel, nb=nb),
        grid_spec=grid_spec,
        out_shape=jax.ShapeDtypeStruct((bsz, n_out, d), F32),
        compiler_params=_cparams("arbitrary", "arbitrary"),
        name="moe_combine",
    )(hoff, loffs, ngr, h, idx, loff_col, mod, y_sorted)


def _moe_block(h, f, idx, wts, mod, mod_row, w_gate, w_up, w_down, n_out):
    bsz, ltot, d = h.shape
    nb = ltot // TB
    n_blocks = bsz * nb
    mt = MOE_TILE
    experts = jnp.arange(N_EXPERTS, dtype=jnp.int32)
    cnt = jnp.sum((idx.reshape(bsz, TOP_K, nb, TB, 1) == experts).astype(jnp.int32), axis=(1, 3))
    cnt = cnt.reshape(n_blocks, N_EXPERTS)
    run = ((cnt + RUN_ALIGN - 1) // RUN_ALIGN) * RUN_ALIGN
    loffs = jnp.cumsum(run, axis=1) - run
    region = ((jnp.sum(run, axis=0) + mt - 1) // mt) * mt
    ends = jnp.cumsum(region)
    hoff = (ends - region)[None, :] + jnp.cumsum(run, axis=0) - run
    p_total = ((TOP_K * bsz * ltot + n_blocks * N_EXPERTS * (RUN_ALIGN - 1)) // mt + N_EXPERTS + 1) * mt
    tile_start = jnp.arange(p_total // mt, dtype=jnp.int32) * mt
    tile_expert = jnp.minimum(jnp.sum((tile_start[:, None] >= ends[None, :]).astype(jnp.int32), axis=1),
                              N_EXPERTS - 1)
    n_used = (ends[-1] // mt).astype(jnp.int32).reshape(1)
    flat = lambda t: t.reshape(-1).astype(jnp.int32)
    hoff, loffs_flat, ngr = flat(hoff), flat(loffs), flat(run // RUN_ALIGN)
    loff_col = loffs.astype(F32).reshape(n_blocks, N_EXPERTS, 1)
    used = jnp.sum(run, axis=0)
    goff = flat(jnp.concatenate([ends - region + used, ends[-1:]]))
    ggr = flat(jnp.concatenate([(region - used) // RUN_ALIGN, (p_total - ends[-1:]) // TB]))

    x_sorted = _dispatch(f, idx, wts, loff_col, hoff, loffs_flat, ngr, goff, ggr, p_total)
    y_sorted = _moe_grouped(x_sorted, tile_expert, n_used, w_gate, w_up, w_down)
    return _combine(h, idx, loff_col, mod, mod_row, y_sorted, hoff, loffs_flat, ngr, n_out)


def _rope_tables(n_lat, n_ctx, n_heads):
    t = jnp.arange(n_lat)
    pos = jnp.stack([t // GRID_W, t % GRID_W], axis=-1).astype(F32)
    n_freq = HEAD_DIM // 4
    inv = 1.0 / (ROPE_BASE ** (jnp.arange(n_freq, dtype=F32) / n_freq))
    ang = pos[:, :, None] * inv
    cos, sin = jnp.cos(ang), jnp.sin(ang)
    cos_h = jnp.concatenate([cos[:, 0], cos[:, 0], cos[:, 1], cos[:, 1]], axis=-1)
    sin_h = jnp.concatenate([-sin[:, 0], sin[:, 0], -sin[:, 1], sin[:, 1]], axis=-1)
    cos_h = jnp.concatenate([cos_h, jnp.ones((n_ctx, HEAD_DIM), F32)], axis=0)
    sin_h = jnp.concatenate([sin_h, jnp.zeros((n_ctx, HEAD_DIM), F32)], axis=0)
    return jnp.tile(cos_h, (1, n_heads)), jnp.tile(sin_h, (1, n_heads))


def _swap_halves_perm(width):
    j = np.arange(width)
    nf = HEAD_DIM // 4
    return np.where((j % (2 * nf)) < nf, j + nf, j - nf)


def _pack_w_in(w_in, rw, sw, nw_):
    o = 0
    seg = {}
    for name, width in (("rq", rw), ("rk", rw), ("rv", rw), ("rg", rw), ("z", sw), ("x", sw),
                        ("b", SSD_GROUPS * SSD_STATE), ("c", SSD_GROUPS * SSD_STATE),
                        ("dt", 2 * (sw // HEAD_DIM)), ("nq", nw_), ("nk", nw_), ("nv", nw_)):
        seg[name] = w_in[:, o:o + width]
        o += width
    perm = _swap_halves_perm(rw)
    dt_pad = jnp.zeros((w_in.shape[0], 128 - seg["dt"].shape[1]), w_in.dtype)
    cols = [seg["rq"], seg["rk"], seg["rq"][:, perm], seg["rk"][:, perm], seg["rv"], seg["rg"],
            seg["z"], seg["x"], seg["b"], seg["c"], seg["nq"], seg["nk"], seg["nv"], seg["dt"], dt_pad]
    return jnp.concatenate(cols, axis=1).astype(BF16)


def kernel(x, c, ctx, c_ctx, w_mod, b_mod, norm_mix, norm_ffn, w_in, w_out, ret_decay_f, ret_decay_b,
           ssd_conv_w, ssd_conv_b, ssd_dt_bias_f, ssd_dt_bias_b, ssd_a_log_f, ssd_a_log_b, ssd_d, ssd_norm,
           na_q_norm, na_k_norm, na_rpb, router_w, router_b, w_gate, w_up, w_down):
    bsz, n_lat, d = x.shape
    n_ctx = ctx.shape[1]
    depth = w_mod.shape[0]
    rows = n_lat // GRID_W
    rw = ret_decay_f.shape[1] * HEAD_DIM
    sw = ssd_d.shape[1] * HEAD_DIM
    nw_ = na_rpb.shape[1] * HEAD_DIM
    dims = (rw, sw, nw_)
    assert n_lat % TB == 0 and n_ctx % TB == 0
    nbl = n_lat // TB

    n_cond = ((bsz + 1 + 7) // 8) * 8
    cond = jnp.zeros((n_cond, d), F32).at[:bsz].set(c).at[bsz].set(c_ctx)
    mod = _modulation(cond, w_mod, b_mod).reshape(depth, n_cond, 6, d)
    mod_row = lambda b, j: jnp.where(j < nbl, b, bsz)

    cos, sin = _rope_tables(n_lat, n_ctx, rw // HEAD_DIM)
    hm = jnp.asarray(np.kron(np.eye(nw_ // HEAD_DIM), np.full((HEAD_DIM, HEAD_DIM), 1.0 / HEAD_DIM)), F32)
    rw_t = jnp.pad(jnp.concatenate(_bf16_parts(router_w, 3), axis=1), ((0, 0), (0, 128 - 3 * router_w.shape[1])))
    rb = router_b.reshape(-1, 1)

    h = jnp.concatenate([x, ctx], axis=1)
    for l in range(depth):
        last = l == depth - 1
        w_all = _pack_w_in(w_in[l], rw, sw, nw_)
        qkn = jnp.stack([jnp.tile(na_q_norm[l], nw_ // HEAD_DIM), jnp.tile(na_k_norm[l], nw_ // HEAD_DIM)])
        ret, z, xbc, dt, na = _inproj(h, mod[l], mod_row, norm_mix[l].reshape(1, d), w_all, cos, sin, qkn, hm, dims)
        ret_o, ssd_o = _scan_mixers(
            ret, jnp.concatenate([ret_decay_f[l], ret_decay_b[l]]).reshape(-1, 1, 1), z, xbc, dt,
            ssd_conv_w[l], ssd_conv_b[l], jnp.concatenate([ssd_dt_bias_f[l], ssd_dt_bias_b[l]]),
            jnp.concatenate([ssd_a_log_f[l], ssd_a_log_b[l]]), ssd_d[l], ssd_norm[l], nbl)
        na_o = _na_attention(na, _na_bias_tables(na_rpb[l], rows), nbl)
        h, f, idx, wts = _outproj(h, ret_o, ssd_o, na_o, w_out[l].astype(BF16), mod[l], mod_row,
                                  norm_ffn[l].reshape(1, d), rw_t, rb)
        h = _moe_block(h, f, idx, wts, mod[l], mod_row, w_gate[l], w_up[l], w_down[l],
                       n_lat if last else n_lat + n_ctx)
    return h
```

```python
import functools

import jax
import jax.numpy as jnp
import numpy as np
from jax import lax
from jax.experimental import pallas as pl
from jax.experimental.pallas import tpu as pltpu

F32 = jnp.float32
BF16 = jnp.bfloat16
HIGHEST = lax.Precision.HIGHEST

GRID_W = 64
HEAD_DIM = 64
SSD_GROUPS = 2
SSD_STATE = 64
NA_WIN_ROWS = 8
NA_WIN_COLS = 16
N_GROUPS = 4
EXPERTS_PER_GROUP = 4
N_EXPERTS = N_GROUPS * EXPERTS_PER_GROUP
TOP_K = 2
ROPE_BASE = 10000.0
EPS = 1e-6

VMEM_LIMIT_BYTES = 56 * 1024 * 1024
TB = 256
NA_QROWS = TB // GRID_W
MOE_TILE = 512
SUBLANES = 8


def _cparams(*sem):
    return pltpu.CompilerParams(dimension_semantics=sem, vmem_limit_bytes=VMEM_LIMIT_BYTES)


def _silu(t):
    return t * (1.0 / (1.0 + jnp.exp(-t)))


def _nt_dot(a, b, **kw):
    return lax.dot_general(a, b, (((1,), (1,)), ((), ())), preferred_element_type=F32, **kw)


def _tn_dot(a, b):
    return lax.dot_general(a, b, (((0,), (0,)), ((), ())), preferred_element_type=F32)


def _bf16_parts(x, n):
    parts, rest = [], x
    for i in range(n):
        p = rest.astype(BF16)
        parts.append(p)
        if i + 1 < n:
            rest = rest - p.astype(F32)
    return parts


def _split_dot(x, sel, parts=3, left=False):
    sel = sel.astype(BF16)
    dot = (lambda p: jnp.dot(sel, p, preferred_element_type=F32)) if left else (
        lambda p: jnp.dot(p, sel, preferred_element_type=F32))
    out = None
    for p in _bf16_parts(x, parts):
        out = dot(p) if out is None else out + dot(p)
    return out


def _h_specs(h, nbl):
    if not isinstance(h, tuple):
        return (h,), [pl.BlockSpec((1, TB, h.shape[2]), lambda b, j: (b, j, 0))]
    d = h[0].shape[2]
    return h, [pl.BlockSpec((1, TB, d), lambda b, j: (b, jnp.minimum(j, nbl - 1), 0)),
               pl.BlockSpec((1, TB, d), lambda b, j: (b, jnp.maximum(j - nbl, 0), 0))]


def _read_h(h_refs, nbl):
    if len(h_refs) == 1:
        return h_refs[0][0]
    return jnp.where(pl.program_id(1) < nbl, h_refs[0][0], h_refs[1][0])


def _mod_kernel(s_ref, w_ref, b_ref, o_ref):
    s = _silu(s_ref[...])
    o_ref[0] = jnp.dot(s, w_ref[0], precision=HIGHEST, preferred_element_type=F32) + b_ref[0]


def _modulation(cond, w_mod, b_mod):
    depth, d, n = w_mod.shape
    rows = cond.shape[0]
    tn = 1024
    return pl.pallas_call(
        _mod_kernel,
        grid=(depth, n // tn),
        in_specs=[
            pl.BlockSpec((rows, d), lambda l, j: (0, 0)),
            pl.BlockSpec((1, d, tn), lambda l, j: (l, 0, j)),
            pl.BlockSpec((1, 1, tn), lambda l, j: (l, 0, j)),
        ],
        out_specs=pl.BlockSpec((1, rows, tn), lambda l, j: (l, 0, j)),
        out_shape=jax.ShapeDtypeStruct((depth, rows, n), F32),
        compiler_params=_cparams("parallel", "parallel"),
        name="modulation",
    )(cond, w_mod, b_mod.reshape(depth, 1, n))


def _inproj_kernel(*refs, dims, n_h, nbl):
    h_refs, (mod_ref, nw_ref, w_ref, cos_ref, sin_ref, qkn_ref, hm_ref,
             ret_ref, z_ref, xbc_ref, dt_ref, na_ref) = refs[:n_h], refs[n_h:]
    rw, sw, nw_ = dims
    h = _read_h(h_refs, nbl)
    a = h * lax.rsqrt(jnp.mean(h * h, axis=-1, keepdims=True) + EPS) * nw_ref[...]
    a = a * (1.0 + mod_ref[0, 1:2, :]) + mod_ref[0, 0:1, :]
    ab = a.astype(BF16)

    def proj(lo, hi):
        return jnp.dot(ab, w_ref[:, lo:hi], preferred_element_type=F32)

    cos, sin = cos_ref[...], sin_ref[...]
    nf = HEAD_DIM // 4
    first_half = (lax.broadcasted_iota(jnp.int32, cos.shape, 1) % (2 * nf)) < nf

    def rotate(t):
        return t * cos + jnp.where(first_half, pltpu.roll(t, rw - nf, 1), pltpu.roll(t, nf, 1)) * sin

    o = 0
    ret_ref[0, :, 0:rw] = rotate(proj(o, o + rw))
    ret_ref[0, :, rw:2 * rw] = rotate(proj(o + rw, o + 2 * rw)) * (HEAD_DIM ** -0.5)
    ret_ref[0, :, 2 * rw:4 * rw] = proj(o + 2 * rw, o + 4 * rw)
    o += 4 * rw
    z_ref[0] = proj(o, o + sw)
    o += sw
    xbc_w = sw + 2 * SSD_GROUPS * SSD_STATE
    xbc_ref[0] = proj(o, o + xbc_w)
    o += xbc_w
    hm = hm_ref[...]
    for i in range(2):
        t = proj(o + i * nw_, o + (i + 1) * nw_)
        ms = _split_dot(t * t, hm, parts=2)
        na_ref[0, :, i * nw_:(i + 1) * nw_] = (t * lax.rsqrt(ms + EPS) * qkn_ref[i:i + 1, :]).astype(BF16)
    na_ref[0, :, 2 * nw_:3 * nw_] = proj(o + 2 * nw_, o + 3 * nw_).astype(BF16)
    o += 3 * nw_
    dt_ref[0] = proj(o, o + 128)[:, 0:dt_ref.shape[2]]


def _inproj(h, mod, mod_row, norm_w, w_all, cos, sin, qkn, hm, dims, nbl):
    h_arrays, h_specs = _h_specs(h, nbl)
    bsz, d = h_arrays[0].shape[0], h_arrays[0].shape[2]
    ltot = cos.shape[0]
    rw, sw, nw_ = dims
    n_dt = 2 * (sw // HEAD_DIM)
    xbc_w = sw + 2 * SSD_GROUPS * SSD_STATE
    row = lambda n: pl.BlockSpec((1, TB, n), lambda b, j: (b, j, 0))
    const = lambda shape: pl.BlockSpec(shape, lambda b, j: (0,) * len(shape))
    widths = (4 * rw, sw, xbc_w, n_dt, 3 * nw_)
    dtypes = (F32, F32, F32, F32, BF16)
    return pl.pallas_call(
        functools.partial(_inproj_kernel, dims=dims, n_h=len(h_arrays), nbl=nbl),
        grid=(bsz, ltot // TB),
        in_specs=h_specs + [
            pl.BlockSpec((1, 6, d), lambda b, j: (mod_row(b, j), 0, 0)),
            const((1, d)),
            const(w_all.shape),
            pl.BlockSpec((TB, rw), lambda b, j: (j, 0)),
            pl.BlockSpec((TB, rw), lambda b, j: (j, 0)),
            const((2, nw_)),
            const((nw_, nw_)),
        ],
        out_specs=[row(n) for n in widths],
        out_shape=[jax.ShapeDtypeStruct((bsz, ltot, n), dt) for n, dt in zip(widths, dtypes)],
        compiler_params=_cparams("parallel", "parallel"),
        name="inproj",
    )(*h_arrays, mod, norm_w, w_all, cos, sin, qkn, hm)


def _scan_chunk_maps(nb, nbl):
    fwd = lambda s: jnp.where(s < nb, (nbl + s) % nb, 2 * nb - 1 - s)
    out = lambda s: jnp.where(s < nb, nb - 1, 2 * nb - 1 - s)
    return fwd, out


def _ret_phases(q_ref, k_ref, v_ref, g_ref, dec_ref, o_ref, sf_all, sf, sb, dmat, rd, rdt, *, nb, nbl):
    s = pl.program_id(1)
    c = TB
    nh = k_ref.shape[2] // HEAD_DIM
    log_f = [-jnp.exp(dec_ref[h]) for h in range(nh)]
    log_b = [-jnp.exp(dec_ref[nh + h]) for h in range(nh)]

    def _init():
        sf[...] = jnp.zeros_like(sf)
        sb[...] = jnp.zeros_like(sb)
        delta = (lax.broadcasted_iota(jnp.int32, (c, c), 0) - lax.broadcasted_iota(jnp.int32, (c, c), 1)).astype(F32)
        pos = lax.broadcasted_iota(jnp.int32, (c, HEAD_DIM), 0).astype(F32)
        pos_l = lax.broadcasted_iota(jnp.int32, (SUBLANES, c), 1).astype(F32)
        for h in range(nh):
            dmat[h] = (jnp.exp(jnp.where(delta >= 0, log_f[h] * delta, -jnp.inf))
                       + jnp.exp(jnp.where(delta <= 0, -log_b[h] * delta, -jnp.inf)))
            rd[2 * h + 0] = jnp.exp(log_f[h] * (pos + 1.0))
            rd[2 * h + 1] = jnp.exp(log_b[h] * (c - pos))
            rdt[2 * h + 0] = jnp.exp(log_f[h] * (c - 1.0 - pos_l))
            rdt[2 * h + 1] = jnp.exp(log_b[h] * pos_l)

    def _state_sweep():
        ci = (nbl + s) % nb
        k_t = k_ref[0].T
        for h in range(nh):
            sl = slice(h * HEAD_DIM, (h + 1) * HEAD_DIM)
            kw = (k_t[sl, :] * rdt[2 * h + 0, 0:1, :]).astype(BF16)
            sf_all[ci, h] = sf[h]
            sf[h] = sf[h] * jnp.exp(log_f[h] * c) + jnp.dot(kw, v_ref[0, :, sl].astype(BF16),
                                                            preferred_element_type=F32)

    def _output_sweep():
        ci = 2 * nb - 1 - s
        k_t = k_ref[0].T
        for h in range(nh):
            sl = slice(h * HEAD_DIM, (h + 1) * HEAD_DIM)
            qb, vb = q_ref[0, :, sl].astype(BF16), v_ref[0, :, sl].astype(BF16)
            p = jnp.dot(qb, k_t[sl, :].astype(BF16), preferred_element_type=F32) * dmat[h]
            y = jnp.dot(p.astype(BF16), vb, preferred_element_type=F32)
            y += jnp.dot(qb, sf_all[ci, h].astype(BF16), preferred_element_type=F32) * rd[2 * h + 0]
            y += jnp.dot(qb, sb[h].astype(BF16), preferred_element_type=F32) * rd[2 * h + 1]
            kw = (k_t[sl, :] * rdt[2 * h + 1, 0:1, :]).astype(BF16)
            sb[h] = sb[h] * jnp.exp(log_b[h] * c) + jnp.dot(kw, vb, preferred_element_type=F32)
            mu = jnp.mean(y, axis=-1, keepdims=True)
            yc = y - mu
            yn = yc * lax.rsqrt(jnp.mean(yc * yc, axis=-1, keepdims=True) + EPS)
            o_ref[0, :, sl] = (yn * _silu(g_ref[0, :, sl])).astype(o_ref.dtype)

    return _init, _state_sweep, _output_sweep


def _softplus(t):
    return jnp.maximum(t, 0.0) + jnp.log1p(jnp.exp(-jnp.abs(t)))


def _lane_group_selector(n_rows, group):
    shape = (n_rows, n_rows * group)
    lane_owner = lax.broadcasted_iota(jnp.int32, shape, 1) // group
    return jnp.where(lane_owner == lax.broadcasted_iota(jnp.int32, shape, 0), 1.0, 0.0).astype(BF16)


def _ssd_phases(z_ref, x_ref, xp_ref, xn_ref, dt_ref, cw_ref, cb_ref, dtb_ref, alog_ref, dtbt_ref, alogt_ref,
                dsk_ref, nw_ref, o_ref, sf_all, sf, sb, tri, ybuf, xc_all, *, nb, nbl):
    s = pl.program_id(1)
    c = TB
    sw = z_ref.shape[2]
    nh = sw // HEAD_DIM
    rep = nh // SSD_GROUPS
    gw = rep * HEAD_DIM
    gs = SSD_GROUPS * SSD_STATE
    ci = jnp.where(s < nb, (nbl + s) % nb, 2 * nb - 1 - s)

    def _init():
        sf[...] = jnp.zeros_like(sf)
        sb[...] = jnp.zeros_like(sb)
        li = lax.broadcasted_iota(jnp.int32, (c, c), 0)
        si = lax.broadcasted_iota(jnp.int32, (c, c), 1)
        tri[0] = jnp.where(si <= li, 1.0, 0.0).astype(BF16)
        tri[1] = jnp.where(si >= li, 1.0, 0.0).astype(BF16)

    def _sweeps():
        def conv_silu():
            x = x_ref[0]
            has_prev = jnp.where(jnp.logical_or(ci == 0, ci == nbl), 0.0, 1.0)
            has_next = jnp.where(jnp.logical_or(ci == nbl - 1, ci == nb - 1), 0.0, 1.0)
            rows = lax.broadcasted_iota(jnp.int32, x.shape, 0)
            x_dn = jnp.where(rows == 0, xp_ref[0, SUBLANES - 1:SUBLANES, :] * has_prev, pltpu.roll(x, 1, 0))
            x_up = jnp.where(rows == c - 1, xn_ref[0, 0:1, :] * has_next, pltpu.roll(x, c - 1, 0))
            return _silu(cw_ref[0:1, :] * x_dn + cw_ref[1:2, :] * x + cw_ref[2:3, :] * x_up + cb_ref[...])

        def split(xc):
            xs, bm, cm = xc[:, :sw], xc[:, sw:sw + gs], xc[:, sw + gs:]
            return xs, bm.T.astype(BF16), cm

        dtr = dt_ref[0]
        dt = _softplus(dtr + dtb_ref[...])
        a = dt * -jnp.exp(alog_ref[...])
        pre = _split_dot(a, tri[0], left=True)
        tot = pre[c - 1:c, :]
        is_fwd = lax.broadcasted_iota(jnp.int32, (1, 2 * nh), 1) < nh
        acs = jnp.where(is_fwd, pre, tot - pre + a)
        sel_head = _lane_group_selector(2 * nh, HEAD_DIM)
        w_wide = _split_dot(dt * jnp.exp(tot - acs), sel_head, parts=2)
        etot_wide = _split_dot(jnp.broadcast_to(jnp.exp(tot), (SUBLANES, 2 * nh)), sel_head, parts=2)[0:1, :]

        def _state_sweep():
            xc = conv_silu()
            xc_all[ci] = xc
            xs, bm_t, _ = split(xc)
            for g in range(SSD_GROUPS):
                lanes = slice(g * gw, (g + 1) * gw)
                xw = (xs[:, lanes] * w_wide[:, lanes]).astype(BF16)
                sf_all[ci, g] = sf[g]
                sf[g] = sf[g] * etot_wide[:, lanes] + jnp.dot(bm_t[g * SSD_STATE:(g + 1) * SSD_STATE, :], xw,
                                                              preferred_element_type=F32)

        def _output_sweep():
            xs, bm_t, cm = split(xc_all[ci])
            eye = (lax.broadcasted_iota(jnp.int32, (2 * nh, 2 * nh), 0)
                   == lax.broadcasted_iota(jnp.int32, (2 * nh, 2 * nh), 1)).astype(BF16)
            dtr_t = sum(_nt_dot(eye, p) for p in _bf16_parts(dtr, 3))
            dt_t = _softplus(dtr_t + dtbt_ref[...])
            a_t = dt_t * -jnp.exp(alogt_ref[...])
            pre_t = _split_dot(a_t, tri[1])
            is_fwd_t = lax.broadcasted_iota(jnp.int32, (2 * nh, 1), 0) < nh
            acs_t = jnp.where(is_fwd_t, pre_t, pre_t[:, c - 1:c] - pre_t + a_t)
            e_wide = _split_dot(jnp.exp(acs), sel_head, parts=2)
            li = lax.broadcasted_iota(jnp.int32, (c, c), 0)
            si = lax.broadcasted_iota(jnp.int32, (c, c), 1)
            lower, upper = li >= si, si >= li
            for g in range(SSD_GROUPS):
                lanes = slice(g * gw, (g + 1) * gw)
                lanes_b = slice(sw + g * gw, sw + (g + 1) * gw)
                cg = cm[:, g * SSD_STATE:(g + 1) * SSD_STATE].astype(BF16)
                bg_t = bm_t[g * SSD_STATE:(g + 1) * SSD_STATE, :]
                scores = jnp.dot(cg, bg_t, preferred_element_type=F32)
                for r in range(rep):
                    h = g * rep + r
                    hb = nh + h
                    col_f, col_b = acs[:, h:h + 1], acs[:, hb:hb + 1]
                    d_f = jnp.exp(jnp.where(lower, col_f - acs_t[h:h + 1, :], -jnp.inf)) * dt_t[h:h + 1, :]
                    d_b = jnp.exp(jnp.where(upper, col_b - acs_t[hb:hb + 1, :], -jnp.inf)) * dt_t[hb:hb + 1, :]
                    ybuf[:, h * HEAD_DIM:(h + 1) * HEAD_DIM] = jnp.dot(
                        (scores * (d_f + d_b)).astype(BF16), xs[:, h * HEAD_DIM:(h + 1) * HEAD_DIM].astype(BF16),
                        preferred_element_type=F32)
                ybuf[:, lanes] += (jnp.dot(cg, sf_all[ci, g].astype(BF16), preferred_element_type=F32) * e_wide[:, lanes]
                                   + jnp.dot(cg, sb[g].astype(BF16), preferred_element_type=F32) * e_wide[:, lanes_b])
                xw = (xs[:, lanes] * w_wide[:, lanes_b]).astype(BF16)
                sb[g] = sb[g] * etot_wide[:, lanes_b] + jnp.dot(bg_t, xw, preferred_element_type=F32)
            y = (ybuf[...] + dsk_ref[...] * xs) * _silu(z_ref[0])
            o_ref[0] = (y * lax.rsqrt(jnp.mean(y * y, axis=-1, keepdims=True) + EPS) * nw_ref[...]).astype(o_ref.dtype)

        return _state_sweep, _output_sweep

    return _init, _sweeps


N_RET_IN, N_SSD_IN, N_RET_SCRATCH = 5, 13, 6


def _scan_mixers_kernel(*refs, nb, nbl):
    ret_in, refs = refs[:N_RET_IN], refs[N_RET_IN:]
    ssd_in, refs = refs[:N_SSD_IN], refs[N_SSD_IN:]
    (ret_o, ssd_o), refs = refs[:2], refs[2:]
    ret_init, ret_state, ret_out = _ret_phases(*ret_in, ret_o, *refs[:N_RET_SCRATCH], nb=nb, nbl=nbl)
    ssd_init, ssd_sweeps = _ssd_phases(*ssd_in, ssd_o, *refs[N_RET_SCRATCH:], nb=nb, nbl=nbl)
    s = pl.program_id(1)
    pl.when(s == 0)(ret_init)
    pl.when(s < nb)(ret_state)
    pl.when(s >= nb)(ret_out)
    pl.when(s == 0)(ssd_init)
    ssd_state, ssd_out = ssd_sweeps()
    pl.when(s < nb)(ssd_state)
    pl.when(s >= nb)(ssd_out)


def _scan_mixers(ret, decay, z, xbc, dt, conv_w, conv_b, dtb, alog, d_skip, norm_w, nbl):
    bsz, ltot, w4 = ret.shape
    w = w4 // 4
    nh_r = w // HEAD_DIM
    sw = z.shape[2]
    nh = sw // HEAD_DIM
    xw = xbc.shape[2]
    nb = ltot // TB
    per = TB // SUBLANES
    fwd, out = _scan_chunk_maps(nb, nbl)
    conv_in = lambda s: fwd(jnp.minimum(s, nb - 1))
    const = lambda t: pl.BlockSpec(t.shape, lambda b, s: (0,) * t.ndim)
    args = (conv_w, conv_b.reshape(1, xw), dtb.reshape(1, 2 * nh), alog.reshape(1, 2 * nh),
            dtb.reshape(2 * nh, 1), alog.reshape(2 * nh, 1), jnp.repeat(d_skip, HEAD_DIM).reshape(1, sw),
            norm_w.reshape(1, sw))
    ret_specs = [
        pl.BlockSpec((1, TB, w), lambda b, s: (b, out(s), 0)),
        pl.BlockSpec((1, TB, w), lambda b, s: (b, fwd(s), 1)),
        pl.BlockSpec((1, TB, w), lambda b, s: (b, fwd(s), 2)),
        pl.BlockSpec((1, TB, w), lambda b, s: (b, out(s), 3)),
        const(decay),
    ]
    ssd_specs = [
        pl.BlockSpec((1, TB, sw), lambda b, s: (b, out(s), 0)),
        pl.BlockSpec((1, TB, xw), lambda b, s: (b, conv_in(s), 0)),
        pl.BlockSpec((1, SUBLANES, xw), lambda b, s: (b, jnp.maximum(conv_in(s) * per - 1, 0), 0)),
        pl.BlockSpec((1, SUBLANES, xw), lambda b, s: (b, jnp.minimum((conv_in(s) + 1) * per, nb * per - 1), 0)),
        pl.BlockSpec((1, TB, 2 * nh), lambda b, s: (b, fwd(s), 0)),
    ] + [const(t) for t in args]
    assert len(ret_specs) == N_RET_IN and len(ssd_specs) == N_SSD_IN
    ret_scratch = [
        pltpu.VMEM((nb, nh_r, HEAD_DIM, HEAD_DIM), F32),
        pltpu.VMEM((nh_r, HEAD_DIM, HEAD_DIM), F32),
        pltpu.VMEM((nh_r, HEAD_DIM, HEAD_DIM), F32),
        pltpu.VMEM((nh_r, TB, TB), F32),
        pltpu.VMEM((2 * nh_r, TB, HEAD_DIM), F32),
        pltpu.VMEM((2 * nh_r, SUBLANES, TB), F32),
    ]
    ssd_scratch = [
        pltpu.VMEM((nb, SSD_GROUPS, SSD_STATE, sw // SSD_GROUPS), F32),
        pltpu.VMEM((SSD_GROUPS, SSD_STATE, sw // SSD_GROUPS), F32),
        pltpu.VMEM((SSD_GROUPS, SSD_STATE, sw // SSD_GROUPS), F32),
        pltpu.VMEM((2, TB, TB), BF16),
        pltpu.VMEM((TB, sw), F32),
        pltpu.VMEM((nb, TB, xw), F32),
    ]
    assert len(ret_scratch) == N_RET_SCRATCH
    return pl.pallas_call(
        functools.partial(_scan_mixers_kernel, nb=nb, nbl=nbl),
        grid=(bsz, 2 * nb),
        in_specs=ret_specs + ssd_specs,
        out_specs=[pl.BlockSpec((1, TB, w), lambda b, s: (b, out(s), 0)),
                   pl.BlockSpec((1, TB, sw), lambda b, s: (b, out(s), 0))],
        out_shape=[jax.ShapeDtypeStruct((bsz, ltot, w), BF16), jax.ShapeDtypeStruct((bsz, ltot, sw), BF16)],
        scratch_shapes=ret_scratch + ssd_scratch,
        compiler_params=_cparams("parallel", "arbitrary"),
        name="scan_mixers",
    )(ret, ret, ret, ret, decay, z, xbc, xbc, xbc, dt, *args)


def _softmax_pv(s_parts, v):
    m = s_parts[0].max(axis=-1, keepdims=True)
    for s in s_parts[1:]:
        m = jnp.maximum(m, s.max(axis=-1, keepdims=True))
    p_parts = [jnp.exp(s - m) for s in s_parts]
    den = p_parts[0].sum(axis=-1, keepdims=True)
    for p in p_parts[1:]:
        den = den + p.sum(axis=-1, keepdims=True)
    p = p_parts[0] if len(p_parts) == 1 else jnp.concatenate(p_parts, axis=1)
    return jnp.dot(p.astype(BF16), v, preferred_element_type=F32) / den


def _na_kernel(q_ref, k0_ref, k1_ref, k2_ref, v0_ref, v1_ref, v2_ref, kc_ref, vc_ref, bias_ref, o_ref, *, nbl):
    nh = q_ref.shape[2] // HEAD_DIM
    n_lat_keys = 3 * k0_ref.shape[1]
    j = pl.program_id(1)

    @pl.when(j < nbl)
    def _latent():
        for h in range(nh):
            sl = slice(h * HEAD_DIM, (h + 1) * HEAD_DIM)
            q = q_ref[0, :, sl] * (HEAD_DIM ** -0.5)
            k = jnp.concatenate([k0_ref[0, :, sl], k1_ref[0, :, sl], k2_ref[0, :, sl], kc_ref[0, :, sl]], axis=0)
            v = jnp.concatenate([v0_ref[0, :, sl], v1_ref[0, :, sl], v2_ref[0, :, sl], vc_ref[0, :, sl]], axis=0)
            s = _nt_dot(q, k)
            o = _softmax_pv([s[:, :n_lat_keys] + bias_ref[0, h], s[:, n_lat_keys:]], v)
            o_ref[0, :, sl] = o.astype(o_ref.dtype)

    @pl.when(j >= nbl)
    def _context():
        for h in range(nh):
            sl = slice(h * HEAD_DIM, (h + 1) * HEAD_DIM)
            s = _nt_dot(q_ref[0, :, sl] * (HEAD_DIM ** -0.5), kc_ref[0, :, sl])
            o_ref[0, :, sl] = _softmax_pv([s], vc_ref[0, :, sl]).astype(o_ref.dtype)


def _na_bias_tables(rpb, rows):
    qb, kb = NA_QROWS, 3 * NA_QROWS
    nb = rows // qb
    nh = rpb.shape[0]
    n_dr, n_dc = 2 * NA_WIN_ROWS - 1, 2 * NA_WIN_COLS - 1
    c = np.arange(GRID_W)[:, None]
    kc = np.arange(GRID_W)[None, :]
    cs = np.clip(c - NA_WIN_COLS // 2, 0, GRID_W - NA_WIN_COLS)
    col_ok = (kc >= cs) & (kc < cs + NA_WIN_COLS)
    dc = np.clip(kc - c + NA_WIN_COLS - 1, 0, n_dc - 1)
    col_sel = np.eye(n_dc, dtype=np.float32)[dc.reshape(-1)].T
    tabs = []
    for rbq, wb in ((0, 0), (1, 0), (nb - 1, nb - 3)):
        r = qb * rbq + np.arange(qb)[:, None]
        kr = qb * wb + np.arange(kb)[None, :]
        r0 = np.clip(r - NA_WIN_ROWS // 2, 0, rows - NA_WIN_ROWS)
        row_ok = (kr >= r0) & (kr < r0 + NA_WIN_ROWS)
        dr = np.clip(kr - r + NA_WIN_ROWS - 1, 0, n_dr - 1)
        row_sel = np.eye(n_dr, dtype=np.float32)[dr.reshape(-1)]
        t = jnp.einsum("ad,hde,ef->haf", row_sel, rpb.astype(F32), col_sel, precision=HIGHEST)
        t = t.reshape(nh, qb, kb, GRID_W, GRID_W).transpose(0, 1, 3, 2, 4).reshape(nh, qb * GRID_W, kb * GRID_W)
        valid = (row_ok[:, None, :, None] & col_ok[None, :, None, :]).reshape(qb * GRID_W, kb * GRID_W)
        tabs.append(jnp.where(valid, t, -jnp.inf))
    return jnp.stack(tabs)


def _na_attention(na, bias, nbl):
    bsz, ltot, w3 = na.shape
    w = w3 // 3
    nb = ltot // TB
    assert nbl >= 3 and nb == nbl + 1, "needs >= 12 grid rows and a context of one token block"
    win = lambda j: jnp.clip(j - 1, 0, nbl - 3)
    kv = lambda i, col: pl.BlockSpec((1, TB, w), lambda b, j: (b, win(j) + i, col))
    typ = lambda j: jnp.where(j == 0, 0, jnp.where(j >= nbl - 1, 2, 1))
    return pl.pallas_call(
        functools.partial(_na_kernel, nbl=nbl),
        grid=(bsz, nb),
        in_specs=[
            pl.BlockSpec((1, TB, w), lambda b, j: (b, j, 0)),
            kv(0, 1), kv(1, 1), kv(2, 1), kv(0, 2), kv(1, 2), kv(2, 2),
            pl.BlockSpec((1, TB, w), lambda b, j: (b, nbl, 1)),
            pl.BlockSpec((1, TB, w), lambda b, j: (b, nbl, 2)),
            pl.BlockSpec((1,) + bias.shape[1:], lambda b, j: (typ(j), 0, 0, 0)),
        ],
        out_specs=pl.BlockSpec((1, TB, w), lambda b, j: (b, j, 0)),
        out_shape=jax.ShapeDtypeStruct((bsz, ltot, w), BF16),
        compiler_params=_cparams("parallel", "arbitrary"),
        name="na_attention",
    )(na, na, na, na, na, na, na, na, na, bias)


def _route(aff, sel):
    rows = [sel[e:e + 1, :] for e in range(N_EXPERTS)]
    arow = [aff[e:e + 1, :] for e in range(N_EXPERTS)]
    gscore = []
    for g in range(N_GROUPS):
        a, b, c, d = rows[4 * g:4 * g + 4]
        hi1, lo1 = jnp.maximum(a, b), jnp.minimum(a, b)
        hi2, lo2 = jnp.maximum(c, d), jnp.minimum(c, d)
        gscore.append(jnp.maximum(hi1, hi2) + jnp.maximum(jnp.minimum(hi1, hi2), jnp.maximum(lo1, lo2)))
    best = jnp.zeros_like(gscore[0], dtype=jnp.int32)
    top = gscore[0]
    for g in range(1, N_GROUPS):
        upd = gscore[g] > top
        best = jnp.where(upd, g, best)
        top = jnp.where(upd, gscore[g], top)
    sv, av = [], []
    for j in range(EXPERTS_PER_GROUP):
        s_j, a_j = rows[j], arow[j]
        for g in range(1, N_GROUPS):
            s_j = jnp.where(best == g, rows[4 * g + j], s_j)
            a_j = jnp.where(best == g, arow[4 * g + j], a_j)
        sv.append(s_j)
        av.append(a_j)
    picked = []
    for j in range(EXPERTS_PER_GROUP):
        rank = jnp.zeros_like(best)
        for i in range(EXPERTS_PER_GROUP):
            if i == j:
                continue
            ahead = (sv[i] >= sv[j]) if i < j else (sv[i] > sv[j])
            rank = rank + ahead.astype(jnp.int32)
        picked.append(rank < TOP_K)
    first = jnp.full_like(best, EXPERTS_PER_GROUP - 1)
    last = jnp.zeros_like(best)
    for j in range(EXPERTS_PER_GROUP - 1, -1, -1):
        first = jnp.where(picked[j], j, first)
    for j in range(EXPERTS_PER_GROUP):
        last = jnp.where(picked[j], j, last)
    a_first, a_last = av[0], av[0]
    for j in range(1, EXPERTS_PER_GROUP):
        a_first = jnp.where(first == j, av[j], a_first)
        a_last = jnp.where(last == j, av[j], a_last)
    tot = a_first + a_last
    idx = jnp.concatenate([best * EXPERTS_PER_GROUP + first, best * EXPERTS_PER_GROUP + last], axis=0)
    wts = jnp.concatenate([a_first / tot, a_last / tot], axis=0)
    return idx, wts


def _outproj_kernel(*refs, n_h, nbl):
    h_refs, (ret_ref, ssd_ref, na_ref, wo_ref, mod_ref, nw_ref, rwt_ref, rb_ref,
             hn_ref, f_ref, idx_ref, wts_ref) = refs[:n_h], refs[n_h:]
    rw, sw = ret_ref.shape[2], ssd_ref.shape[2]
    acc = jnp.dot(ret_ref[0], wo_ref[0:rw, :], preferred_element_type=F32)
    acc += jnp.dot(ssd_ref[0], wo_ref[rw:rw + sw, :], preferred_element_type=F32)
    acc += jnp.dot(na_ref[0], wo_ref[rw + sw:, :], preferred_element_type=F32)
    hn = _read_h(h_refs, nbl) + mod_ref[0, 2:3, :] * acc
    hn_ref[0] = hn
    f = hn * lax.rsqrt(jnp.mean(hn * hn, axis=-1, keepdims=True) + EPS) * nw_ref[...]
    f = f * (1.0 + mod_ref[0, 4:5, :]) + mod_ref[0, 3:4, :]
    f_ref[0] = f.astype(BF16)
    p = _split_dot(f, rwt_ref[...]).T
    logits = p[:N_EXPERTS, :] + p[N_EXPERTS:2 * N_EXPERTS, :] + p[2 * N_EXPERTS:3 * N_EXPERTS, :]
    aff = 1.0 / (1.0 + jnp.exp(-logits))
    idx, wts = _route(aff, aff + rb_ref[...])
    idx_ref[0] = idx
    wts_ref[0] = wts


def _outproj(h, ret_o, ssd_o, na_o, w_out, mod, mod_row, norm_w, rw_t, rb, nbl):
    h_arrays, h_specs = _h_specs(h, nbl)
    bsz, ltot, d = ret_o.shape[0], ret_o.shape[1], h_arrays[0].shape[2]
    row = lambda n: pl.BlockSpec((1, TB, n), lambda b, j: (b, j, 0))
    const = lambda shape: pl.BlockSpec(shape, lambda b, j: (0,) * len(shape))
    col = lambda: pl.BlockSpec((1, TOP_K, TB), lambda b, j: (b, 0, j))
    return pl.pallas_call(
        functools.partial(_outproj_kernel, n_h=len(h_arrays), nbl=nbl),
        grid=(bsz, ltot // TB),
        in_specs=h_specs + [
            row(ret_o.shape[2]), row(ssd_o.shape[2]), row(na_o.shape[2]),
            const(w_out.shape),
            pl.BlockSpec((1, 6, d), lambda b, j: (mod_row(b, j), 0, 0)),
            const((1, d)), const(rw_t.shape), const(rb.shape),
        ],
        out_specs=[row(d), row(d), col(), col()],
        out_shape=[
            jax.ShapeDtypeStruct((bsz, ltot, d), F32),
            jax.ShapeDtypeStruct((bsz, ltot, d), BF16),
            jax.ShapeDtypeStruct((bsz, TOP_K, ltot), jnp.int32),
            jax.ShapeDtypeStruct((bsz, TOP_K, ltot), F32),
        ],
        compiler_params=_cparams("parallel", "parallel"),
        name="outproj_router",
    )(*h_arrays, ret_o, ssd_o, na_o, w_out, mod, norm_w, rw_t, rb)


RUN_ALIGN = SUBLANES
RUN_BITS = 6
PERM_ROWS = TOP_K * TB + N_EXPERTS * RUN_ALIGN
GATE_LANES = 128
GAP_BITS = 6
assert TB == RUN_ALIGN << (RUN_BITS - 1) and MOE_TILE == RUN_ALIGN << GAP_BITS

U32 = jnp.uint32


def _pack_bf16_pairs(y):
    n = y.shape[1] // 2
    lo = lax.bitcast_convert_type(y[:, :n].astype(BF16).astype(F32), U32) >> 16
    hi = lax.bitcast_convert_type(y[:, n:].astype(BF16).astype(F32), U32) & jnp.uint32(0xFFFF0000)
    return lo | hi


def _unpack_bf16_pairs(u):
    lo = lax.bitcast_convert_type(u << 16, F32).astype(BF16)
    hi = lax.bitcast_convert_type(u & jnp.uint32(0xFFFF0000), F32).astype(BF16)
    return lo, hi


def _block_perm(idx_ref, loff_ref):
    tb = idx_ref.shape[2]
    na = TOP_K * tb
    e_iota = lax.broadcasted_iota(jnp.int32, (N_EXPERTS, tb), 0)
    oh = jnp.concatenate([jnp.where(e_iota == idx_ref[0, k:k + 1, :], 1.0, 0.0) for k in range(TOP_K)], axis=1)
    upper = lax.broadcasted_iota(jnp.int32, (na, na), 0) <= lax.broadcasted_iota(jnp.int32, (na, na), 1)
    cum = jnp.dot(oh.astype(BF16), jnp.where(upper, 1.0, 0.0).astype(BF16), preferred_element_type=F32)
    pos = jnp.sum(oh * (cum - 1.0 + loff_ref[0]), axis=0, keepdims=True).astype(jnp.int32)
    r = lax.broadcasted_iota(jnp.int32, (PERM_ROWS, tb), 0)
    return [r == pos[:, k * tb:(k + 1) * tb] for k in range(TOP_K)]


def _for_each_run_piece(blk, hoff_ref, loff_ref, ngr_ref, fn):
    for e in range(N_EXPERTS):
        j = blk * N_EXPERTS + e
        n, lo, ho = ngr_ref[j], loff_ref[j], hoff_ref[j]
        for b in range(RUN_BITS - 1, -1, -1):
            start = ((n >> (b + 1)) << (b + 1)) * RUN_ALIGN

            @pl.when(((n >> b) & 1) == 1)
            def _():
                fn(pl.multiple_of(lo + start, RUN_ALIGN), pl.multiple_of(ho + start, RUN_ALIGN), RUN_ALIGN << b)


def _for_each_gap_piece(goff_ref, ggr_ref, fn):
    for e in range(N_EXPERTS):
        n, ho = ggr_ref[e], goff_ref[e]
        for b in range(GAP_BITS - 1, -1, -1):
            start = ((n >> (b + 1)) << (b + 1)) * RUN_ALIGN

            @pl.when(((n >> b) & 1) == 1)
            def _():
                fn(pl.multiple_of(ho + start, RUN_ALIGN), RUN_ALIGN << b)


def _dispatch_kernel(hoff_ref, loffs_ref, ngr_ref, goff_ref, ggr_ref, f_ref, idx_ref, wts_ref, loff_ref, xs_ref,
                     xperm, zbuf, sems, *, nb):
    step = pl.program_id(0) * nb + pl.program_id(1)
    n_steps = pl.num_programs(0) * nb
    slot = step % 2

    def run_copies(blk, slot_, wait):
        def fn(lo, ho, rows):
            cp = pltpu.make_async_copy(xperm.at[slot_, pl.ds(lo, rows)], xs_ref.at[pl.ds(ho, rows)], sems.at[slot_])
            cp.wait() if wait else cp.start()
        _for_each_run_piece(blk, hoff_ref, loffs_ref, ngr_ref, fn)

    @pl.when(step == 0)
    def _zero_region_tails():
        zbuf[...] = jnp.zeros_like(zbuf)
        for wait in (False, True):
            def fn(ho, rows, wait=wait):
                cp = pltpu.make_async_copy(zbuf.at[pl.ds(0, rows)], xs_ref.at[pl.ds(ho, rows)], sems.at[2])
                cp.wait() if wait else cp.start()
            _for_each_gap_piece(goff_ref, ggr_ref, fn)
        tail_rows = zbuf.shape[0]

        def tail_copy(i):
            dst = xs_ref.at[pl.ds(pl.multiple_of(goff_ref[N_EXPERTS] + i * tail_rows, RUN_ALIGN), tail_rows)]
            return pltpu.make_async_copy(zbuf, dst, sems.at[2])

        lax.fori_loop(0, ggr_ref[N_EXPERTS], lambda i, c: (tail_copy(i).start(), c)[1], 0)
        lax.fori_loop(0, ggr_ref[N_EXPERTS], lambda i, c: (tail_copy(i).wait(), c)[1], 0)

    q0, q1 = _block_perm(idx_ref, loff_ref)
    q = jnp.where(jnp.logical_or(q0, q1), 1.0, 0.0).astype(BF16)
    half = f_ref.shape[2] // 2
    xperm[slot, :, :half] = _pack_bf16_pairs(jnp.dot(q, f_ref[0], preferred_element_type=F32))
    w = jnp.sum(jnp.where(q0, wts_ref[0, 0:1, :], 0.0) + jnp.where(q1, wts_ref[0, 1:2, :], 0.0),
                axis=1, keepdims=True)
    xperm[slot, :, half:] = jnp.broadcast_to(lax.bitcast_convert_type(w, U32), (PERM_ROWS, GATE_LANES))

    run_copies(step, slot, wait=False)

    @pl.when(step > 0)
    def _():
        run_copies(step - 1, 1 - slot, wait=True)

    @pl.when(step == n_steps - 1)
    def _():
        run_copies(step, slot, wait=True)


def _dispatch(f, idx, wts, loff_col, hoff, loffs, ngr, goff, ggr, p_total):
    bsz, ltot, d = f.shape
    nb = ltot // TB
    words = d // 2 + GATE_LANES
    grid_spec = pltpu.PrefetchScalarGridSpec(
        num_scalar_prefetch=5,
        grid=(bsz, nb),
        in_specs=[
            pl.BlockSpec((1, TB, d), lambda b, j, *_: (b, j, 0)),
            pl.BlockSpec((1, TOP_K, TB), lambda b, j, *_: (b, 0, j)),
            pl.BlockSpec((1, TOP_K, TB), lambda b, j, *_: (b, 0, j)),
            pl.BlockSpec((1, N_EXPERTS, 1), lambda b, j, *_: (b * nb + j, 0, 0)),
        ],
        out_specs=pl.BlockSpec(memory_space=pl.ANY),
        scratch_shapes=[pltpu.VMEM((2, PERM_ROWS, words), U32),
                        pltpu.VMEM((RUN_ALIGN << (GAP_BITS - 1), words), U32),
                        pltpu.SemaphoreType.DMA((3,))],
    )
    return pl.pallas_call(
        functools.partial(_dispatch_kernel, nb=nb),
        grid_spec=grid_spec,
        out_shape=jax.ShapeDtypeStruct((p_total, words), U32),
        compiler_params=_cparams("arbitrary", "arbitrary"),
        name="moe_dispatch",
    )(hoff, loffs, ngr, goff, ggr, f, idx, wts, loff_col)


def _moe_kernel(te_ref, nt_ref, x_ref, wg_ref, wu_ref, wd_ref, y_ref, wg_s, wu_s, wd_s):
    i = pl.program_id(0)
    fresh = jnp.logical_or(i == 0, te_ref[i] != te_ref[jnp.maximum(i - 1, 0)])

    @pl.when(fresh)
    def _():
        wg_s[...] = wg_ref[0].astype(BF16)
        wu_s[...] = wu_ref[0].astype(BF16)
        wd_s[...] = wd_ref[0].astype(BF16)

    @pl.when(i < nt_ref[0])
    def _():
        half = wg_s.shape[0] // 2
        lo, hi = _unpack_bf16_pairs(x_ref[:, :half])
        gate = lax.bitcast_convert_type(x_ref[:, half:half + 1], F32)
        g = (jnp.dot(lo, wg_s[:half, :], preferred_element_type=F32)
             + jnp.dot(hi, wg_s[half:, :], preferred_element_type=F32))
        u = (jnp.dot(lo, wu_s[:half, :], preferred_element_type=F32)
             + jnp.dot(hi, wu_s[half:, :], preferred_element_type=F32))
        he = (_silu(g) * u).astype(BF16)
        y_ref[...] = _pack_bf16_pairs(jnp.dot(he, wd_s[...], preferred_element_type=F32) * gate)

    @pl.when(i >= nt_ref[0])
    def _():
        y_ref[...] = jnp.zeros_like(y_ref)


def _moe_grouped(x_sorted, tile_expert, n_tiles_used, w_gate, w_up, w_down):
    p, words = x_sorted.shape
    _, d, de = w_gate.shape
    tm = MOE_TILE
    grid_spec = pltpu.PrefetchScalarGridSpec(
        num_scalar_prefetch=2,
        grid=(p // tm,),
        in_specs=[
            pl.BlockSpec((tm, words), lambda i, te, nt: (i, 0)),
            pl.BlockSpec((1, d, de), lambda i, te, nt: (te[i], 0, 0)),
            pl.BlockSpec((1, d, de), lambda i, te, nt: (te[i], 0, 0)),
            pl.BlockSpec((1, de, d), lambda i, te, nt: (te[i], 0, 0)),
        ],
        out_specs=pl.BlockSpec((tm, d // 2), lambda i, te, nt: (i, 0)),
        scratch_shapes=[pltpu.VMEM((d, de), BF16), pltpu.VMEM((d, de), BF16), pltpu.VMEM((de, d), BF16)],
    )
    return pl.pallas_call(
        _moe_kernel,
        grid_spec=grid_spec,
        out_shape=jax.ShapeDtypeStruct((p, d // 2), U32),
        compiler_params=_cparams("arbitrary"),
        name="moe_grouped",
    )(tile_expert, n_tiles_used, x_sorted, w_gate, w_up, w_down)


def _combine_kernel(hoff_ref, loffs_ref, ngr_ref, h_ref, idx_ref, loff_ref, mod_ref, ys_ref, o_ref,
                    yperm, sems, *, nb):
    b, j = pl.program_id(0), pl.program_id(1)
    nj = pl.num_programs(1)
    step = b * nj + j
    slot = step % 2
    blk = b * nb + j
    nxt = jnp.where(j == nj - 1, (b + 1) * nb, blk + 1)

    def run_copies(blk_, slot_, wait):
        def fn(lo, ho, rows):
            cp = pltpu.make_async_copy(ys_ref.at[pl.ds(ho, rows)], yperm.at[slot_, pl.ds(lo, rows)], sems.at[slot_])
            cp.wait() if wait else cp.start()
        _for_each_run_piece(blk_, hoff_ref, loffs_ref, ngr_ref, fn)

    @pl.when(step == 0)
    def _():
        yperm[...] = jnp.zeros_like(yperm)
        run_copies(blk, 0, wait=False)

    @pl.when(step + 1 < pl.num_programs(0) * nj)
    def _():
        run_copies(nxt, 1 - slot, wait=False)

    q0, q1 = _block_perm(idx_ref, loff_ref)
    q = jnp.where(jnp.logical_or(q0, q1), 1.0, 0.0).astype(BF16)
    run_copies(blk, slot, wait=True)
    lo, hi = _unpack_bf16_pairs(yperm[slot])
    y = jnp.concatenate([_tn_dot(q, lo), _tn_dot(q, hi)], axis=1)
    o_ref[0] = h_ref[0] + mod_ref[0, 5:6, :] * y


def _combine(h, idx, loff_col, mod, mod_row, y_sorted, hoff, loffs, ngr, n_out):
    bsz, ltot, d = h.shape
    nb = ltot // TB
    grid_spec = pltpu.PrefetchScalarGridSpec(
        num_scalar_prefetch=3,
        grid=(bsz, n_out // TB),
        in_specs=[
            pl.BlockSpec((1, TB, d), lambda b, j, *_: (b, j, 0)),
            pl.BlockSpec((1, TOP_K, TB), lambda b, j, *_: (b, 0, j)),
            pl.BlockSpec((1, N_EXPERTS, 1), lambda b, j, *_: (b * nb + j, 0, 0)),
            pl.BlockSpec((1, 6, d), lambda b, j, *_: (mod_row(b, j), 0, 0)),
            pl.BlockSpec(memory_space=pl.ANY),
        ],
        out_specs=pl.BlockSpec((1, TB, d), lambda b, j, *_: (b, j, 0)),
        scratch_shapes=[pltpu.VMEM((2, PERM_ROWS, d // 2), U32), pltpu.SemaphoreType.DMA((2,))],
    )
    return pl.pallas_call(
        functools.partial(_combine_kernel, nb=nb),
        grid_spec=grid_spec,
        out_shape=jax.ShapeDtypeStruct((bsz, n_out, d), F32),
        compiler_params=_cparams("arbitrary", "arbitrary"),
        name="moe_combine",
    )(hoff, loffs, ngr, h, idx, loff_col, mod, y_sorted)


def _moe_block(h, f, idx, wts, mod, mod_row, w_gate, w_up, w_down, n_out):
    bsz, ltot, d = h.shape
    nb = ltot // TB
    n_blocks = bsz * nb
    mt = MOE_TILE
    experts = jnp.arange(N_EXPERTS, dtype=jnp.int32)
    cnt = jnp.sum((idx.reshape(bsz, TOP_K, nb, TB, 1) == experts).astype(jnp.int32), axis=(1, 3))
    cnt = cnt.reshape(n_blocks, N_EXPERTS)
    run = ((cnt + RUN_ALIGN - 1) // RUN_ALIGN) * RUN_ALIGN
    loffs = jnp.cumsum(run, axis=1) - run
    region = ((jnp.sum(run, axis=0) + mt - 1) // mt) * mt
    ends = jnp.cumsum(region)
    hoff = (ends - region)[None, :] + jnp.cumsum(run, axis=0) - run
    p_total = ((TOP_K * bsz * ltot + n_blocks * N_EXPERTS * (RUN_ALIGN - 1)) // mt + N_EXPERTS + 1) * mt
    tile_start = jnp.arange(p_total // mt, dtype=jnp.int32) * mt
    tile_expert = jnp.minimum(jnp.sum((tile_start[:, None] >= ends[None, :]).astype(jnp.int32), axis=1),
                              N_EXPERTS - 1)
    n_used = (ends[-1] // mt).astype(jnp.int32).reshape(1)
    flat = lambda t: t.reshape(-1).astype(jnp.int32)
    hoff, loffs_flat, ngr = flat(hoff), flat(loffs), flat(run // RUN_ALIGN)
    loff_col = loffs.astype(F32).reshape(n_blocks, N_EXPERTS, 1)
    used = jnp.sum(run, axis=0)
    goff = flat(jnp.concatenate([ends - region + used, ends[-1:]]))
    ggr = flat(jnp.concatenate([(region - used) // RUN_ALIGN, (p_total - ends[-1:]) // TB]))

    x_sorted = _dispatch(f, idx, wts, loff_col, hoff, loffs_flat, ngr, goff, ggr, p_total)
    y_sorted = _moe_grouped(x_sorted, tile_expert, n_used, w_gate, w_up, w_down)
    return _combine(h, idx, loff_col, mod, mod_row, y_sorted, hoff, loffs_flat, ngr, n_out)


def _rope_tables(n_lat, n_ctx, n_heads):
    t = jnp.arange(n_lat)
    pos = jnp.stack([t // GRID_W, t % GRID_W], axis=-1).astype(F32)
    n_freq = HEAD_DIM // 4
    inv = 1.0 / (ROPE_BASE ** (jnp.arange(n_freq, dtype=F32) / n_freq))
    ang = pos[:, :, None] * inv
    cos, sin = jnp.cos(ang), jnp.sin(ang)
    cos_h = jnp.concatenate([cos[:, 0], cos[:, 0], cos[:, 1], cos[:, 1]], axis=-1)
    sin_h = jnp.concatenate([-sin[:, 0], sin[:, 0], -sin[:, 1], sin[:, 1]], axis=-1)
    cos_h = jnp.concatenate([cos_h, jnp.ones((n_ctx, HEAD_DIM), F32)], axis=0)
    sin_h = jnp.concatenate([sin_h, jnp.zeros((n_ctx, HEAD_DIM), F32)], axis=0)
    return jnp.tile(cos_h, (1, n_heads)), jnp.tile(sin_h, (1, n_heads))


def _pack_w_in(w_in, rw, sw, nw_):
    o = 0
    seg = {}
    for name, width in (("rq", rw), ("rk", rw), ("rv", rw), ("rg", rw), ("z", sw), ("x", sw),
                        ("b", SSD_GROUPS * SSD_STATE), ("c", SSD_GROUPS * SSD_STATE),
                        ("dt", 2 * (sw // HEAD_DIM)), ("nq", nw_), ("nk", nw_), ("nv", nw_)):
        seg[name] = w_in[:, o:o + width]
        o += width
    dt_pad = jnp.zeros((w_in.shape[0], 128 - seg["dt"].shape[1]), w_in.dtype)
    cols = [seg["rq"], seg["rk"], seg["rv"], seg["rg"],
            seg["z"], seg["x"], seg["b"], seg["c"], seg["nq"], seg["nk"], seg["nv"], seg["dt"], dt_pad]
    return jnp.concatenate(cols, axis=1).astype(BF16)


def kernel(x, c, ctx, c_ctx, w_mod, b_mod, norm_mix, norm_ffn, w_in, w_out, ret_decay_f, ret_decay_b,
           ssd_conv_w, ssd_conv_b, ssd_dt_bias_f, ssd_dt_bias_b, ssd_a_log_f, ssd_a_log_b, ssd_d, ssd_norm,
           na_q_norm, na_k_norm, na_rpb, router_w, router_b, w_gate, w_up, w_down):
    bsz, n_lat, d = x.shape
    n_ctx = ctx.shape[1]
    depth = w_mod.shape[0]
    rows = n_lat // GRID_W
    rw = ret_decay_f.shape[1] * HEAD_DIM
    sw = ssd_d.shape[1] * HEAD_DIM
    nw_ = na_rpb.shape[1] * HEAD_DIM
    dims = (rw, sw, nw_)
    assert n_lat % TB == 0 and n_ctx % TB == 0
    nbl = n_lat // TB

    n_cond = ((bsz + 1 + 7) // 8) * 8
    cond = jnp.zeros((n_cond, d), F32).at[:bsz].set(c).at[bsz].set(c_ctx)
    mod = _modulation(cond, w_mod, b_mod).reshape(depth, n_cond, 6, d)
    mod_row = lambda b, j: jnp.where(j < nbl, b, bsz)

    cos, sin = _rope_tables(n_lat, n_ctx, rw // HEAD_DIM)
    hm = jnp.asarray(np.kron(np.eye(nw_ // HEAD_DIM), np.full((HEAD_DIM, HEAD_DIM), 1.0 / HEAD_DIM)), F32)
    rw_t = jnp.pad(jnp.concatenate(_bf16_parts(router_w, 3), axis=1), ((0, 0), (0, 128 - 3 * router_w.shape[1])))
    rb = router_b.reshape(-1, 1)

    h = (x, ctx)
    for l in range(depth):
        last = l == depth - 1
        w_all = _pack_w_in(w_in[l], rw, sw, nw_)
        qkn = jnp.stack([jnp.tile(na_q_norm[l], nw_ // HEAD_DIM), jnp.tile(na_k_norm[l], nw_ // HEAD_DIM)])
        ret, z, xbc, dt, na = _inproj(h, mod[l], mod_row, norm_mix[l].reshape(1, d), w_all, cos, sin, qkn, hm, dims, nbl)
        ret_o, ssd_o = _scan_mixers(
            ret, jnp.concatenate([ret_decay_f[l], ret_decay_b[l]]).reshape(-1, 1, 1), z, xbc, dt,
            ssd_conv_w[l], ssd_conv_b[l], jnp.concatenate([ssd_dt_bias_f[l], ssd_dt_bias_b[l]]),
            jnp.concatenate([ssd_a_log_f[l], ssd_a_log_b[l]]), ssd_d[l], ssd_norm[l], nbl)
        na_o = _na_attention(na, _na_bias_tables(na_rpb[l], rows), nbl)
        h, f, idx, wts = _outproj(h, ret_o, ssd_o, na_o, w_out[l].astype(BF16), mod[l], mod_row,
                                  norm_ffn[l].reshape(1, d), rw_t, rb, nbl)
        h = _moe_block(h, f, idx, wts, mod[l], mod_row, w_gate[l], w_up[l], w_down[l],
                       n_lat if last else n_lat + n_ctx)
    return h
```

```python
import functools

import jax
import jax.numpy as jnp
import numpy as np
from jax import lax
from jax.experimental import pallas as pl
from jax.experimental.pallas import tpu as pltpu

F32 = jnp.float32
BF16 = jnp.bfloat16
HIGHEST = lax.Precision.HIGHEST

GRID_W = 64
HEAD_DIM = 64
SSD_GROUPS = 2
SSD_STATE = 64
NA_WIN_ROWS = 8
NA_WIN_COLS = 16
N_GROUPS = 4
EXPERTS_PER_GROUP = 4
N_EXPERTS = N_GROUPS * EXPERTS_PER_GROUP
TOP_K = 2
ROPE_BASE = 10000.0
EPS = 1e-6

VMEM_LIMIT_BYTES = 56 * 1024 * 1024
TB = 256
NA_QROWS = TB // GRID_W
MOE_TILE = 512
SUBLANES = 8


def _cparams(*sem):
    return pltpu.CompilerParams(dimension_semantics=sem, vmem_limit_bytes=VMEM_LIMIT_BYTES)


def _silu(t):
    return t * (1.0 / (1.0 + jnp.exp(-t)))


def _nt_dot(a, b, **kw):
    return lax.dot_general(a, b, (((1,), (1,)), ((), ())), preferred_element_type=F32, **kw)


def _tn_dot(a, b):
    return lax.dot_general(a, b, (((0,), (0,)), ((), ())), preferred_element_type=F32)


def _bf16_parts(x, n):
    parts, rest = [], x
    for i in range(n):
        p = rest.astype(BF16)
        parts.append(p)
        if i + 1 < n:
            rest = rest - p.astype(F32)
    return parts


def _split_dot(x, sel, parts=3, left=False):
    sel = sel.astype(BF16)
    dot = (lambda p: jnp.dot(sel, p, preferred_element_type=F32)) if left else (
        lambda p: jnp.dot(p, sel, preferred_element_type=F32))
    out = None
    for p in _bf16_parts(x, parts):
        out = dot(p) if out is None else out + dot(p)
    return out


def _h_specs(h, nbl):
    if not isinstance(h, tuple):
        return (h,), [pl.BlockSpec((1, TB, h.shape[2]), lambda b, j: (b, j, 0))]
    d = h[0].shape[2]
    return h, [pl.BlockSpec((1, TB, d), lambda b, j: (b, jnp.minimum(j, nbl - 1), 0)),
               pl.BlockSpec((1, TB, d), lambda b, j: (b, jnp.maximum(j - nbl, 0), 0))]


def _read_h(h_refs, nbl):
    if len(h_refs) == 1:
        return h_refs[0][0]
    return jnp.where(pl.program_id(1) < nbl, h_refs[0][0], h_refs[1][0])


def _mod_kernel(s_ref, w_ref, b_ref, o_ref):
    s = _silu(s_ref[...])
    o_ref[0] = jnp.dot(s, w_ref[0], precision=HIGHEST, preferred_element_type=F32) + b_ref[0]


def _modulation(cond, w_mod, b_mod):
    depth, d, n = w_mod.shape
    rows = cond.shape[0]
    tn = 1024
    return pl.pallas_call(
        _mod_kernel,
        grid=(depth, n // tn),
        in_specs=[
            pl.BlockSpec((rows, d), lambda l, j: (0, 0)),
            pl.BlockSpec((1, d, tn), lambda l, j: (l, 0, j)),
            pl.BlockSpec((1, 1, tn), lambda l, j: (l, 0, j)),
        ],
        out_specs=pl.BlockSpec((1, rows, tn), lambda l, j: (l, 0, j)),
        out_shape=jax.ShapeDtypeStruct((depth, rows, n), F32),
        compiler_params=_cparams("parallel", "parallel"),
        name="modulation",
    )(cond, w_mod, b_mod.reshape(depth, 1, n))


def _inproj_kernel(*refs, dims, n_h, nbl):
    h_refs, (mod_ref, nw_ref, w_ref, cos_ref, sin_ref, qkn_ref, hm_ref,
             ret_ref, z_ref, xbc_ref, dt_ref, na_ref) = refs[:n_h], refs[n_h:]
    rw, sw, nw_ = dims
    h = _read_h(h_refs, nbl)
    a = h * lax.rsqrt(jnp.mean(h * h, axis=-1, keepdims=True) + EPS) * nw_ref[...]
    a = a * (1.0 + mod_ref[0, 1:2, :]) + mod_ref[0, 0:1, :]
    ab = a.astype(BF16)

    def proj(lo, hi):
        return jnp.dot(ab, w_ref[:, lo:hi], preferred_element_type=F32)

    cos, sin = cos_ref[...], sin_ref[...]
    nf = HEAD_DIM // 4
    first_half = (lax.broadcasted_iota(jnp.int32, cos.shape, 1) % (2 * nf)) < nf

    def rotate(t):
        return t * cos + jnp.where(first_half, pltpu.roll(t, rw - nf, 1), pltpu.roll(t, nf, 1)) * sin

    o = 0
    ret_ref[0, :, 0:rw] = rotate(proj(o, o + rw))
    ret_ref[0, :, rw:2 * rw] = rotate(proj(o + rw, o + 2 * rw)) * (HEAD_DIM ** -0.5)
    ret_ref[0, :, 2 * rw:4 * rw] = proj(o + 2 * rw, o + 4 * rw)
    o += 4 * rw
    z_ref[0] = proj(o, o + sw)
    o += sw
    xbc_w = sw + 2 * SSD_GROUPS * SSD_STATE
    xbc_ref[0] = proj(o, o + xbc_w)
    o += xbc_w
    hm = hm_ref[...]
    for i in range(2):
        t = proj(o + i * nw_, o + (i + 1) * nw_)
        ms = _split_dot(t * t, hm, parts=2)
        na_ref[0, :, i * nw_:(i + 1) * nw_] = (t * lax.rsqrt(ms + EPS) * qkn_ref[i:i + 1, :]).astype(BF16)
    na_ref[0, :, 2 * nw_:3 * nw_] = proj(o + 2 * nw_, o + 3 * nw_).astype(BF16)
    o += 3 * nw_
    dt_ref[0] = proj(o, o + 128)[:, 0:dt_ref.shape[2]]


def _inproj(h, mod, mod_row, norm_w, w_all, cos, sin, qkn, hm, dims, nbl):
    h_arrays, h_specs = _h_specs(h, nbl)
    bsz, d = h_arrays[0].shape[0], h_arrays[0].shape[2]
    ltot = cos.shape[0]
    rw, sw, nw_ = dims
    n_dt = 2 * (sw // HEAD_DIM)
    xbc_w = sw + 2 * SSD_GROUPS * SSD_STATE
    row = lambda n: pl.BlockSpec((1, TB, n), lambda b, j: (b, j, 0))
    const = lambda shape: pl.BlockSpec(shape, lambda b, j: (0,) * len(shape))
    widths = (4 * rw, sw, xbc_w, n_dt, 3 * nw_)
    dtypes = (F32, F32, F32, F32, BF16)
    return pl.pallas_call(
        functools.partial(_inproj_kernel, dims=dims, n_h=len(h_arrays), nbl=nbl),
        grid=(bsz, ltot // TB),
        in_specs=h_specs + [
            pl.BlockSpec((1, 6, d), lambda b, j: (mod_row(b, j), 0, 0)),
            const((1, d)),
            const(w_all.shape),
            pl.BlockSpec((TB, rw), lambda b, j: (j, 0)),
            pl.BlockSpec((TB, rw), lambda b, j: (j, 0)),
            const((2, nw_)),
            const((nw_, nw_)),
        ],
        out_specs=[row(n) for n in widths],
        out_shape=[jax.ShapeDtypeStruct((bsz, ltot, n), dt) for n, dt in zip(widths, dtypes)],
        compiler_params=_cparams("parallel", "parallel"),
        name="inproj",
    )(*h_arrays, mod, norm_w, w_all, cos, sin, qkn, hm)


def _scan_chunk_maps(nb, nbl):
    fwd = lambda s: jnp.where(s < nb, (nbl + s) % nb, 2 * nb - 1 - s)
    out = lambda s: jnp.where(s < nb, nb - 1, 2 * nb - 1 - s)
    return fwd, out


def _ret_phases(q_ref, k_ref, v_ref, g_ref, dec_ref, o_ref, sf_all, sf, sb, dmat, rd, rdt, *, nb, nbl):
    s = pl.program_id(1)
    c = TB
    nh = k_ref.shape[2] // HEAD_DIM
    log_f = [-jnp.exp(dec_ref[h]) for h in range(nh)]
    log_b = [-jnp.exp(dec_ref[nh + h]) for h in range(nh)]

    def _init():
        sf[...] = jnp.zeros_like(sf)
        sb[...] = jnp.zeros_like(sb)
        delta = (lax.broadcasted_iota(jnp.int32, (c, c), 0) - lax.broadcasted_iota(jnp.int32, (c, c), 1)).astype(F32)
        pos = lax.broadcasted_iota(jnp.int32, (c, HEAD_DIM), 0).astype(F32)
        pos_l = lax.broadcasted_iota(jnp.int32, (SUBLANES, c), 1).astype(F32)
        for h in range(nh):
            dmat[h] = (jnp.exp(jnp.where(delta >= 0, log_f[h] * delta, -jnp.inf))
                       + jnp.exp(jnp.where(delta <= 0, -log_b[h] * delta, -jnp.inf)))
            rd[2 * h + 0] = jnp.exp(log_f[h] * (pos + 1.0))
            rd[2 * h + 1] = jnp.exp(log_b[h] * (c - pos))
            rdt[2 * h + 0] = jnp.exp(log_f[h] * (c - 1.0 - pos_l))
            rdt[2 * h + 1] = jnp.exp(log_b[h] * pos_l)

    def _state_sweep():
        ci = (nbl + s) % nb
        k_t = k_ref[0].T
        for h in range(nh):
            sl = slice(h * HEAD_DIM, (h + 1) * HEAD_DIM)
            kw = (k_t[sl, :] * rdt[2 * h + 0, 0:1, :]).astype(BF16)
            sf_all[ci, h] = sf[h]
            sf[h] = sf[h] * jnp.exp(log_f[h] * c) + jnp.dot(kw, v_ref[0, :, sl].astype(BF16),
                                                            preferred_element_type=F32)

    def _output_sweep():
        ci = 2 * nb - 1 - s
        k_t = k_ref[0].T
        for h in range(nh):
            sl = slice(h * HEAD_DIM, (h + 1) * HEAD_DIM)
            qb, vb = q_ref[0, :, sl].astype(BF16), v_ref[0, :, sl].astype(BF16)
            p = jnp.dot(qb, k_t[sl, :].astype(BF16), preferred_element_type=F32) * dmat[h]
            y = jnp.dot(p.astype(BF16), vb, preferred_element_type=F32)
            y += jnp.dot(qb, sf_all[ci, h].astype(BF16), preferred_element_type=F32) * rd[2 * h + 0]
            y += jnp.dot(qb, sb[h].astype(BF16), preferred_element_type=F32) * rd[2 * h + 1]
            kw = (k_t[sl, :] * rdt[2 * h + 1, 0:1, :]).astype(BF16)
            sb[h] = sb[h] * jnp.exp(log_b[h] * c) + jnp.dot(kw, vb, preferred_element_type=F32)
            mu = jnp.mean(y, axis=-1, keepdims=True)
            yc = y - mu
            yn = yc * lax.rsqrt(jnp.mean(yc * yc, axis=-1, keepdims=True) + EPS)
            o_ref[0, :, sl] = (yn * _silu(g_ref[0, :, sl])).astype(o_ref.dtype)

    return _init, _state_sweep, _output_sweep


def _softplus(t):
    return jnp.maximum(t, 0.0) + jnp.log1p(jnp.exp(-jnp.abs(t)))


def _lane_group_selector(n_rows, group):
    shape = (n_rows, n_rows * group)
    lane_owner = lax.broadcasted_iota(jnp.int32, shape, 1) // group
    return jnp.where(lane_owner == lax.broadcasted_iota(jnp.int32, shape, 0), 1.0, 0.0).astype(BF16)


def _ssd_phases(z_ref, x_ref, xp_ref, xn_ref, dt_ref, cw_ref, cb_ref, dtb_ref, alog_ref, dtbt_ref, alogt_ref,
                dsk_ref, nw_ref, o_ref, sf_all, sf, sb, tri, ybuf, xc_all, *, nb, nbl):
    s = pl.program_id(1)
    c = TB
    sw = z_ref.shape[2]
    nh = sw // HEAD_DIM
    rep = nh // SSD_GROUPS
    gw = rep * HEAD_DIM
    gs = SSD_GROUPS * SSD_STATE
    ci = jnp.where(s < nb, (nbl + s) % nb, 2 * nb - 1 - s)

    def _init():
        sf[...] = jnp.zeros_like(sf)
        sb[...] = jnp.zeros_like(sb)
        li = lax.broadcasted_iota(jnp.int32, (c, c), 0)
        si = lax.broadcasted_iota(jnp.int32, (c, c), 1)
        tri[0] = jnp.where(si <= li, 1.0, 0.0).astype(BF16)
        tri[1] = jnp.where(si >= li, 1.0, 0.0).astype(BF16)

    def _sweeps():
        def conv_silu():
            x = x_ref[0]
            has_prev = jnp.where(jnp.logical_or(ci == 0, ci == nbl), 0.0, 1.0)
            has_next = jnp.where(jnp.logical_or(ci == nbl - 1, ci == nb - 1), 0.0, 1.0)
            rows = lax.broadcasted_iota(jnp.int32, x.shape, 0)
            x_dn = jnp.where(rows == 0, xp_ref[0, SUBLANES - 1:SUBLANES, :] * has_prev, pltpu.roll(x, 1, 0))
            x_up = jnp.where(rows == c - 1, xn_ref[0, 0:1, :] * has_next, pltpu.roll(x, c - 1, 0))
            return _silu(cw_ref[0:1, :] * x_dn + cw_ref[1:2, :] * x + cw_ref[2:3, :] * x_up + cb_ref[...])

        def split(xc):
            xs, bm, cm = xc[:, :sw], xc[:, sw:sw + gs], xc[:, sw + gs:]
            return xs, bm.T.astype(BF16), cm

        dtr = dt_ref[0]
        dt = _softplus(dtr + dtb_ref[...])
        a = dt * -jnp.exp(alog_ref[...])
        pre = _split_dot(a, tri[0], left=True)
        tot = pre[c - 1:c, :]
        is_fwd = lax.broadcasted_iota(jnp.int32, (1, 2 * nh), 1) < nh
        acs = jnp.where(is_fwd, pre, tot - pre + a)
        sel_head = _lane_group_selector(2 * nh, HEAD_DIM)
        w_wide = _split_dot(dt * jnp.exp(tot - acs), sel_head, parts=2)
        etot_wide = _split_dot(jnp.broadcast_to(jnp.exp(tot), (SUBLANES, 2 * nh)), sel_head, parts=2)[0:1, :]

        def _state_sweep():
            xc = conv_silu()
            xc_all[ci] = xc
            xs, bm_t, _ = split(xc)
            for g in range(SSD_GROUPS):
                lanes = slice(g * gw, (g + 1) * gw)
                xw = (xs[:, lanes] * w_wide[:, lanes]).astype(BF16)
                sf_all[ci, g] = sf[g]
                sf[g] = sf[g] * etot_wide[:, lanes] + jnp.dot(bm_t[g * SSD_STATE:(g + 1) * SSD_STATE, :], xw,
                                                              preferred_element_type=F32)

        def _output_sweep():
            xs, bm_t, cm = split(xc_all[ci])
            eye = (lax.broadcasted_iota(jnp.int32, (2 * nh, 2 * nh), 0)
                   == lax.broadcasted_iota(jnp.int32, (2 * nh, 2 * nh), 1)).astype(BF16)
            dtr_t = sum(_nt_dot(eye, p) for p in _bf16_parts(dtr, 3))
            dt_t = _softplus(dtr_t + dtbt_ref[...])
            a_t = dt_t * -jnp.exp(alogt_ref[...])
            pre_t = _split_dot(a_t, tri[1])
            is_fwd_t = lax.broadcasted_iota(jnp.int32, (2 * nh, 1), 0) < nh
            acs_t = jnp.where(is_fwd_t, pre_t, pre_t[:, c - 1:c] - pre_t + a_t)
            e_wide = _split_dot(jnp.exp(acs), sel_head, parts=2)
            li = lax.broadcasted_iota(jnp.int32, (c, c), 0)
            si = lax.broadcasted_iota(jnp.int32, (c, c), 1)
            lower, upper = li >= si, si >= li
            for g in range(SSD_GROUPS):
                lanes = slice(g * gw, (g + 1) * gw)
                lanes_b = slice(sw + g * gw, sw + (g + 1) * gw)
                cg = cm[:, g * SSD_STATE:(g + 1) * SSD_STATE].astype(BF16)
                bg_t = bm_t[g * SSD_STATE:(g + 1) * SSD_STATE, :]
                scores = jnp.dot(cg, bg_t, preferred_element_type=F32)
                for r in range(rep):
                    h = g * rep + r
                    hb = nh + h
                    col_f, col_b = acs[:, h:h + 1], acs[:, hb:hb + 1]
                    d_f = jnp.exp(jnp.where(lower, col_f - acs_t[h:h + 1, :], -jnp.inf)) * dt_t[h:h + 1, :]
                    d_b = jnp.exp(jnp.where(upper, col_b - acs_t[hb:hb + 1, :], -jnp.inf)) * dt_t[hb:hb + 1, :]
                    ybuf[:, h * HEAD_DIM:(h + 1) * HEAD_DIM] = jnp.dot(
                        (scores * (d_f + d_b)).astype(BF16), xs[:, h * HEAD_DIM:(h + 1) * HEAD_DIM].astype(BF16),
                        preferred_element_type=F32)
                ybuf[:, lanes] += (jnp.dot(cg, sf_all[ci, g].astype(BF16), preferred_element_type=F32) * e_wide[:, lanes]
                                   + jnp.dot(cg, sb[g].astype(BF16), preferred_element_type=F32) * e_wide[:, lanes_b])
                xw = (xs[:, lanes] * w_wide[:, lanes_b]).astype(BF16)
                sb[g] = sb[g] * etot_wide[:, lanes_b] + jnp.dot(bg_t, xw, preferred_element_type=F32)
            y = (ybuf[...] + dsk_ref[...] * xs) * _silu(z_ref[0])
            o_ref[0] = (y * lax.rsqrt(jnp.mean(y * y, axis=-1, keepdims=True) + EPS) * nw_ref[...]).astype(o_ref.dtype)

        return _state_sweep, _output_sweep

    return _init, _sweeps


N_RET_IN, N_SSD_IN, N_RET_SCRATCH = 5, 13, 6


def _scan_mixers_kernel(*refs, nb, nbl):
    ret_in, refs = refs[:N_RET_IN], refs[N_RET_IN:]
    ssd_in, refs = refs[:N_SSD_IN], refs[N_SSD_IN:]
    (ret_o, ssd_o), refs = refs[:2], refs[2:]
    ret_init, ret_state, ret_out = _ret_phases(*ret_in, ret_o, *refs[:N_RET_SCRATCH], nb=nb, nbl=nbl)
    ssd_init, ssd_sweeps = _ssd_phases(*ssd_in, ssd_o, *refs[N_RET_SCRATCH:], nb=nb, nbl=nbl)
    s = pl.program_id(1)
    pl.when(s == 0)(ret_init)
    pl.when(s < nb)(ret_state)
    pl.when(s >= nb)(ret_out)
    pl.when(s == 0)(ssd_init)
    ssd_state, ssd_out = ssd_sweeps()
    pl.when(s < nb)(ssd_state)
    pl.when(s >= nb)(ssd_out)


def _scan_mixers(ret, decay, z, xbc, dt, conv_w, conv_b, dtb, alog, d_skip, norm_w, nbl):
    bsz, ltot, w4 = ret.shape
    w = w4 // 4
    nh_r = w // HEAD_DIM
    sw = z.shape[2]
    nh = sw // HEAD_DIM
    xw = xbc.shape[2]
    nb = ltot // TB
    per = TB // SUBLANES
    fwd, out = _scan_chunk_maps(nb, nbl)
    conv_in = lambda s: fwd(jnp.minimum(s, nb - 1))
    const = lambda t: pl.BlockSpec(t.shape, lambda b, s: (0,) * t.ndim)
    args = (conv_w, conv_b.reshape(1, xw), dtb.reshape(1, 2 * nh), alog.reshape(1, 2 * nh),
            dtb.reshape(2 * nh, 1), alog.reshape(2 * nh, 1), jnp.repeat(d_skip, HEAD_DIM).reshape(1, sw),
            norm_w.reshape(1, sw))
    ret_specs = [
        pl.BlockSpec((1, TB, w), lambda b, s: (b, out(s), 0)),
        pl.BlockSpec((1, TB, w), lambda b, s: (b, fwd(s), 1)),
        pl.BlockSpec((1, TB, w), lambda b, s: (b, fwd(s), 2)),
        pl.BlockSpec((1, TB, w), lambda b, s: (b, out(s), 3)),
        const(decay),
    ]
    ssd_specs = [
        pl.BlockSpec((1, TB, sw), lambda b, s: (b, out(s), 0)),
        pl.BlockSpec((1, TB, xw), lambda b, s: (b, conv_in(s), 0)),
        pl.BlockSpec((1, SUBLANES, xw), lambda b, s: (b, jnp.maximum(conv_in(s) * per - 1, 0), 0)),
        pl.BlockSpec((1, SUBLANES, xw), lambda b, s: (b, jnp.minimum((conv_in(s) + 1) * per, nb * per - 1), 0)),
        pl.BlockSpec((1, TB, 2 * nh), lambda b, s: (b, fwd(s), 0)),
    ] + [const(t) for t in args]
    assert len(ret_specs) == N_RET_IN and len(ssd_specs) == N_SSD_IN
    ret_scratch = [
        pltpu.VMEM((nb, nh_r, HEAD_DIM, HEAD_DIM), F32),
        pltpu.VMEM((nh_r, HEAD_DIM, HEAD_DIM), F32),
        pltpu.VMEM((nh_r, HEAD_DIM, HEAD_DIM), F32),
        pltpu.VMEM((nh_r, TB, TB), F32),
        pltpu.VMEM((2 * nh_r, TB, HEAD_DIM), F32),
        pltpu.VMEM((2 * nh_r, SUBLANES, TB), F32),
    ]
    ssd_scratch = [
        pltpu.VMEM((nb, SSD_GROUPS, SSD_STATE, sw // SSD_GROUPS), F32),
        pltpu.VMEM((SSD_GROUPS, SSD_STATE, sw // SSD_GROUPS), F32),
        pltpu.VMEM((SSD_GROUPS, SSD_STATE, sw // SSD_GROUPS), F32),
        pltpu.VMEM((2, TB, TB), BF16),
        pltpu.VMEM((TB, sw), F32),
        pltpu.VMEM((nb, TB, xw), F32),
    ]
    assert len(ret_scratch) == N_RET_SCRATCH
    return pl.pallas_call(
        functools.partial(_scan_mixers_kernel, nb=nb, nbl=nbl),
        grid=(bsz, 2 * nb),
        in_specs=ret_specs + ssd_specs,
        out_specs=[pl.BlockSpec((1, TB, w), lambda b, s: (b, out(s), 0)),
                   pl.BlockSpec((1, TB, sw), lambda b, s: (b, out(s), 0))],
        out_shape=[jax.ShapeDtypeStruct((bsz, ltot, w), BF16), jax.ShapeDtypeStruct((bsz, ltot, sw), BF16)],
        scratch_shapes=ret_scratch + ssd_scratch,
        compiler_params=_cparams("parallel", "arbitrary"),
        name="scan_mixers",
    )(ret, ret, ret, ret, decay, z, xbc, xbc, xbc, dt, *args)


def _softmax_pv(s_parts, v):
    m = s_parts[0].max(axis=-1, keepdims=True)
    for s in s_parts[1:]:
        m = jnp.maximum(m, s.max(axis=-1, keepdims=True))
    p_parts = [jnp.exp(s - m) for s in s_parts]
    den = p_parts[0].sum(axis=-1, keepdims=True)
    for p in p_parts[1:]:
        den = den + p.sum(axis=-1, keepdims=True)
    p = p_parts[0] if len(p_parts) == 1 else jnp.concatenate(p_parts, axis=1)
    return jnp.dot(p.astype(BF16), v, preferred_element_type=F32) / den


def _na_kernel(q_ref, k0_ref, k1_ref, k2_ref, v0_ref, v1_ref, v2_ref, kc_ref, vc_ref, bias_ref, o_ref, *, nbl):
    nh = q_ref.shape[2] // HEAD_DIM
    n_lat_keys = 3 * k0_ref.shape[1]
    j = pl.program_id(1)

    @pl.when(j < nbl)
    def _latent():
        for h in range(nh):
            sl = slice(h * HEAD_DIM, (h + 1) * HEAD_DIM)
            q = q_ref[0, :, sl] * (HEAD_DIM ** -0.5)
            k = jnp.concatenate([k0_ref[0, :, sl], k1_ref[0, :, sl], k2_ref[0, :, sl], kc_ref[0, :, sl]], axis=0)
            v = jnp.concatenate([v0_ref[0, :, sl], v1_ref[0, :, sl], v2_ref[0, :, sl], vc_ref[0, :, sl]], axis=0)
            s = _nt_dot(q, k)
            o = _softmax_pv([s[:, :n_lat_keys] + bias_ref[0, h], s[:, n_lat_keys:]], v)
            o_ref[0, :, sl] = o.astype(o_ref.dtype)

    @pl.when(j >= nbl)
    def _context():
        for h in range(nh):
            sl = slice(h * HEAD_DIM, (h + 1) * HEAD_DIM)
            s = _nt_dot(q_ref[0, :, sl] * (HEAD_DIM ** -0.5), kc_ref[0, :, sl])
            o_ref[0, :, sl] = _softmax_pv([s], vc_ref[0, :, sl]).astype(o_ref.dtype)


def _na_bias_tables(rpb, rows):
    qb, kb = NA_QROWS, 3 * NA_QROWS
    nb = rows // qb
    nh = rpb.shape[0]
    n_dr, n_dc = 2 * NA_WIN_ROWS - 1, 2 * NA_WIN_COLS - 1
    c = np.arange(GRID_W)[:, None]
    kc = np.arange(GRID_W)[None, :]
    cs = np.clip(c - NA_WIN_COLS // 2, 0, GRID_W - NA_WIN_COLS)
    col_ok = (kc >= cs) & (kc < cs + NA_WIN_COLS)
    dc = np.clip(kc - c + NA_WIN_COLS - 1, 0, n_dc - 1)
    col_sel = np.eye(n_dc, dtype=np.float32)[dc.reshape(-1)].T
    tabs = []
    for rbq, wb in ((0, 0), (1, 0), (nb - 1, nb - 3)):
        r = qb * rbq + np.arange(qb)[:, None]
        kr = qb * wb + np.arange(kb)[None, :]
        r0 = np.clip(r - NA_WIN_ROWS // 2, 0, rows - NA_WIN_ROWS)
        row_ok = (kr >= r0) & (kr < r0 + NA_WIN_ROWS)
        dr = np.clip(kr - r + NA_WIN_ROWS - 1, 0, n_dr - 1)
        row_sel = np.eye(n_dr, dtype=np.float32)[dr.reshape(-1)]
        t = jnp.einsum("ad,hde,ef->haf", row_sel, rpb.astype(F32), col_sel, precision=HIGHEST)
        t = t.reshape(nh, qb, kb, GRID_W, GRID_W).transpose(0, 1, 3, 2, 4).reshape(nh, qb * GRID_W, kb * GRID_W)
        valid = (row_ok[:, None, :, None] & col_ok[None, :, None, :]).reshape(qb * GRID_W, kb * GRID_W)
        tabs.append(jnp.where(valid, t, -jnp.inf))
    return jnp.stack(tabs)


def _na_attention(na, bias, nbl):
    bsz, ltot, w3 = na.shape
    w = w3 // 3
    nb = ltot // TB
    assert nbl >= 3 and nb == nbl + 1, "needs >= 12 grid rows and a context of one token block"
    win = lambda j: jnp.clip(j - 1, 0, nbl - 3)
    kv = lambda i, col: pl.BlockSpec((1, TB, w), lambda b, j: (b, win(j) + i, col))
    typ = lambda j: jnp.where(j == 0, 0, jnp.where(j >= nbl - 1, 2, 1))
    return pl.pallas_call(
        functools.partial(_na_kernel, nbl=nbl),
        grid=(bsz, nb),
        in_specs=[
            pl.BlockSpec((1, TB, w), lambda b, j: (b, j, 0)),
            kv(0, 1), kv(1, 1), kv(2, 1), kv(0, 2), kv(1, 2), kv(2, 2),
            pl.BlockSpec((1, TB, w), lambda b, j: (b, nbl, 1)),
            pl.BlockSpec((1, TB, w), lambda b, j: (b, nbl, 2)),
            pl.BlockSpec((1,) + bias.shape[1:], lambda b, j: (typ(j), 0, 0, 0)),
        ],
        out_specs=pl.BlockSpec((1, TB, w), lambda b, j: (b, j, 0)),
        out_shape=jax.ShapeDtypeStruct((bsz, ltot, w), BF16),
        compiler_params=_cparams("parallel", "arbitrary"),
        name="na_attention",
    )(na, na, na, na, na, na, na, na, na, bias)


def _route(aff, sel):
    rows = [sel[e:e + 1, :] for e in range(N_EXPERTS)]
    arow = [aff[e:e + 1, :] for e in range(N_EXPERTS)]
    gscore = []
    for g in range(N_GROUPS):
        a, b, c, d = rows[4 * g:4 * g + 4]
        hi1, lo1 = jnp.maximum(a, b), jnp.minimum(a, b)
        hi2, lo2 = jnp.maximum(c, d), jnp.minimum(c, d)
        gscore.append(jnp.maximum(hi1, hi2) + jnp.maximum(jnp.minimum(hi1, hi2), jnp.maximum(lo1, lo2)))
    best = jnp.zeros_like(gscore[0], dtype=jnp.int32)
    top = gscore[0]
    for g in range(1, N_GROUPS):
        upd = gscore[g] > top
        best = jnp.where(upd, g, best)
        top = jnp.where(upd, gscore[g], top)
    sv, av = [], []
    for j in range(EXPERTS_PER_GROUP):
        s_j, a_j = rows[j], arow[j]
        for g in range(1, N_GROUPS):
            s_j = jnp.where(best == g, rows[4 * g + j], s_j)
            a_j = jnp.where(best == g, arow[4 * g + j], a_j)
        sv.append(s_j)
        av.append(a_j)
    picked = []
    for j in range(EXPERTS_PER_GROUP):
        rank = jnp.zeros_like(best)
        for i in range(EXPERTS_PER_GROUP):
            if i == j:
                continue
            ahead = (sv[i] >= sv[j]) if i < j else (sv[i] > sv[j])
            rank = rank + ahead.astype(jnp.int32)
        picked.append(rank < TOP_K)
    first = jnp.full_like(best, EXPERTS_PER_GROUP - 1)
    last = jnp.zeros_like(best)
    for j in range(EXPERTS_PER_GROUP - 1, -1, -1):
        first = jnp.where(picked[j], j, first)
    for j in range(EXPERTS_PER_GROUP):
        last = jnp.where(picked[j], j, last)
    a_first, a_last = av[0], av[0]
    for j in range(1, EXPERTS_PER_GROUP):
        a_first = jnp.where(first == j, av[j], a_first)
        a_last = jnp.where(last == j, av[j], a_last)
    tot = a_first + a_last
    idx = jnp.concatenate([best * EXPERTS_PER_GROUP + first, best * EXPERTS_PER_GROUP + last], axis=0)
    wts = jnp.concatenate([a_first / tot, a_last / tot], axis=0)
    return idx, wts


def _outproj_kernel(*refs, n_h, nbl):
    h_refs, (ret_ref, ssd_ref, na_ref, wo_ref, mod_ref, nw_ref, rwt_ref, rb_ref,
             hn_ref, f_ref, idx_ref, wts_ref) = refs[:n_h], refs[n_h:]
    rw, sw = ret_ref.shape[2], ssd_ref.shape[2]
    acc = jnp.dot(ret_ref[0], wo_ref[0:rw, :], preferred_element_type=F32)
    acc += jnp.dot(ssd_ref[0], wo_ref[rw:rw + sw, :], preferred_element_type=F32)
    acc += jnp.dot(na_ref[0], wo_ref[rw + sw:, :], preferred_element_type=F32)
    hn = _read_h(h_refs, nbl) + mod_ref[0, 2:3, :] * acc
    hn_ref[0] = hn
    f = hn * lax.rsqrt(jnp.mean(hn * hn, axis=-1, keepdims=True) + EPS) * nw_ref[...]
    f = f * (1.0 + mod_ref[0, 4:5, :]) + mod_ref[0, 3:4, :]
    f_ref[0] = f.astype(BF16)
    p = _split_dot(f, rwt_ref[...]).T
    logits = p[:N_EXPERTS, :] + p[N_EXPERTS:2 * N_EXPERTS, :] + p[2 * N_EXPERTS:3 * N_EXPERTS, :]
    aff = 1.0 / (1.0 + jnp.exp(-logits))
    idx, wts = _route(aff, aff + rb_ref[...])
    idx_ref[0] = idx
    wts_ref[0] = wts


def _outproj(h, ret_o, ssd_o, na_o, w_out, mod, mod_row, norm_w, rw_t, rb, nbl):
    h_arrays, h_specs = _h_specs(h, nbl)
    bsz, ltot, d = ret_o.shape[0], ret_o.shape[1], h_arrays[0].shape[2]
    row = lambda n: pl.BlockSpec((1, TB, n), lambda b, j: (b, j, 0))
    const = lambda shape: pl.BlockSpec(shape, lambda b, j: (0,) * len(shape))
    col = lambda: pl.BlockSpec((1, TOP_K, TB), lambda b, j: (b, 0, j))
    return pl.pallas_call(
        functools.partial(_outproj_kernel, n_h=len(h_arrays), nbl=nbl),
        grid=(bsz, ltot // TB),
        in_specs=h_specs + [
            row(ret_o.shape[2]), row(ssd_o.shape[2]), row(na_o.shape[2]),
            const(w_out.shape),
            pl.BlockSpec((1, 6, d), lambda b, j: (mod_row(b, j), 0, 0)),
            const((1, d)), const(rw_t.shape), const(rb.shape),
        ],
        out_specs=[row(d), row(d), col(), col()],
        out_shape=[
            jax.ShapeDtypeStruct((bsz, ltot, d), F32),
            jax.ShapeDtypeStruct((bsz, ltot, d), BF16),
            jax.ShapeDtypeStruct((bsz, TOP_K, ltot), jnp.int32),
            jax.ShapeDtypeStruct((bsz, TOP_K, ltot), F32),
        ],
        compiler_params=_cparams("parallel", "parallel"),
        name="outproj_router",
    )(*h_arrays, ret_o, ssd_o, na_o, w_out, mod, norm_w, rw_t, rb)


RUN_ALIGN = SUBLANES
RUN_BITS = 6
PERM_ROWS = TOP_K * TB + N_EXPERTS * RUN_ALIGN
GATE_LANES = 128
GAP_BITS = 6
assert TB == RUN_ALIGN << (RUN_BITS - 1) and MOE_TILE == RUN_ALIGN << GAP_BITS

U32 = jnp.uint32


def _pack_bf16_pairs(y):
    n = y.shape[1] // 2
    lo = lax.bitcast_convert_type(y[:, :n].astype(BF16).astype(F32), U32) >> 16
    hi = lax.bitcast_convert_type(y[:, n:].astype(BF16).astype(F32), U32) & jnp.uint32(0xFFFF0000)
    return lo | hi


def _unpack_bf16_pairs(u):
    lo = lax.bitcast_convert_type(u << 16, F32).astype(BF16)
    hi = lax.bitcast_convert_type(u & jnp.uint32(0xFFFF0000), F32).astype(BF16)
    return lo, hi


def _block_perm(idx_ref, loff_ref):
    tb = idx_ref.shape[2]
    na = TOP_K * tb
    e_iota = lax.broadcasted_iota(jnp.int32, (N_EXPERTS, tb), 0)
    oh = jnp.concatenate([jnp.where(e_iota == idx_ref[0, k:k + 1, :], 1.0, 0.0) for k in range(TOP_K)], axis=1)
    upper = lax.broadcasted_iota(jnp.int32, (na, na), 0) <= lax.broadcasted_iota(jnp.int32, (na, na), 1)
    cum = jnp.dot(oh.astype(BF16), jnp.where(upper, 1.0, 0.0).astype(BF16), preferred_element_type=F32)
    pos = jnp.sum(oh * (cum - 1.0 + loff_ref[0]), axis=0, keepdims=True).astype(jnp.int32)
    r = lax.broadcasted_iota(jnp.int32, (PERM_ROWS, tb), 0)
    return [r == pos[:, k * tb:(k + 1) * tb] for k in range(TOP_K)]


def _for_each_run_piece(blk, hoff_ref, loff_ref, ngr_ref, fn):
    for e in range(N_EXPERTS):
        j = blk * N_EXPERTS + e
        n, lo, ho = ngr_ref[j], loff_ref[j], hoff_ref[j]
        for b in range(RUN_BITS - 1, -1, -1):
            start = ((n >> (b + 1)) << (b + 1)) * RUN_ALIGN

            @pl.when(((n >> b) & 1) == 1)
            def _():
                fn(pl.multiple_of(lo + start, RUN_ALIGN), pl.multiple_of(ho + start, RUN_ALIGN), RUN_ALIGN << b)


def _for_each_gap_piece(goff_ref, ggr_ref, fn):
    for e in range(N_EXPERTS):
        n, ho = ggr_ref[e], goff_ref[e]
        for b in range(GAP_BITS - 1, -1, -1):
            start = ((n >> (b + 1)) << (b + 1)) * RUN_ALIGN

            @pl.when(((n >> b) & 1) == 1)
            def _():
                fn(pl.multiple_of(ho + start, RUN_ALIGN), RUN_ALIGN << b)


def _dispatch_kernel(hoff_ref, loffs_ref, ngr_ref, goff_ref, ggr_ref, f_ref, idx_ref, wts_ref, loff_ref, xs_ref,
                     xperm, zbuf, sems, *, nb):
    step = pl.program_id(0) * nb + pl.program_id(1)
    n_steps = pl.num_programs(0) * nb
    slot = step % 2

    def run_copies(blk, slot_, wait):
        def fn(lo, ho, rows):
            cp = pltpu.make_async_copy(xperm.at[slot_, pl.ds(lo, rows)], xs_ref.at[pl.ds(ho, rows)], sems.at[slot_])
            cp.wait() if wait else cp.start()
        _for_each_run_piece(blk, hoff_ref, loffs_ref, ngr_ref, fn)

    @pl.when(step == 0)
    def _zero_region_tails():
        zbuf[...] = jnp.zeros_like(zbuf)
        for wait in (False, True):
            def fn(ho, rows, wait=wait):
                cp = pltpu.make_async_copy(zbuf.at[pl.ds(0, rows)], xs_ref.at[pl.ds(ho, rows)], sems.at[2])
                cp.wait() if wait else cp.start()
            _for_each_gap_piece(goff_ref, ggr_ref, fn)
        tail_rows = zbuf.shape[0]

        def tail_copy(i):
            dst = xs_ref.at[pl.ds(pl.multiple_of(goff_ref[N_EXPERTS] + i * tail_rows, RUN_ALIGN), tail_rows)]
            return pltpu.make_async_copy(zbuf, dst, sems.at[2])

        lax.fori_loop(0, ggr_ref[N_EXPERTS], lambda i, c: (tail_copy(i).start(), c)[1], 0)
        lax.fori_loop(0, ggr_ref[N_EXPERTS], lambda i, c: (tail_copy(i).wait(), c)[1], 0)

    q0, q1 = _block_perm(idx_ref, loff_ref)
    q = jnp.where(jnp.logical_or(q0, q1), 1.0, 0.0).astype(BF16)
    half = f_ref.shape[2] // 2
    xperm[slot, :, :half] = _pack_bf16_pairs(jnp.dot(q, f_ref[0], preferred_element_type=F32))
    w = jnp.sum(jnp.where(q0, wts_ref[0, 0:1, :], 0.0) + jnp.where(q1, wts_ref[0, 1:2, :], 0.0),
                axis=1, keepdims=True)
    xperm[slot, :, half:] = jnp.broadcast_to(lax.bitcast_convert_type(w, U32), (PERM_ROWS, GATE_LANES))

    run_copies(step, slot, wait=False)

    @pl.when(step > 0)
    def _():
        run_copies(step - 1, 1 - slot, wait=True)

    @pl.when(step == n_steps - 1)
    def _():
        run_copies(step, slot, wait=True)


def _dispatch(f, idx, wts, loff_col, hoff, loffs, ngr, goff, ggr, p_total):
    bsz, ltot, d = f.shape
    nb = ltot // TB
    words = d // 2 + GATE_LANES
    grid_spec = pltpu.PrefetchScalarGridSpec(
        num_scalar_prefetch=5,
        grid=(bsz, nb),
        in_specs=[
            pl.BlockSpec((1, TB, d), lambda b, j, *_: (b, j, 0)),
            pl.BlockSpec((1, TOP_K, TB), lambda b, j, *_: (b, 0, j)),
            pl.BlockSpec((1, TOP_K, TB), lambda b, j, *_: (b, 0, j)),
            pl.BlockSpec((1, N_EXPERTS, 1), lambda b, j, *_: (b * nb + j, 0, 0)),
        ],
        out_specs=pl.BlockSpec(memory_space=pl.ANY),
        scratch_shapes=[pltpu.VMEM((2, PERM_ROWS, words), U32),
                        pltpu.VMEM((RUN_ALIGN << (GAP_BITS - 1), words), U32),
                        pltpu.SemaphoreType.DMA((3,))],
    )
    return pl.pallas_call(
        functools.partial(_dispatch_kernel, nb=nb),
        grid_spec=grid_spec,
        out_shape=jax.ShapeDtypeStruct((p_total, words), U32),
        compiler_params=_cparams("arbitrary", "arbitrary"),
        name="moe_dispatch",
    )(hoff, loffs, ngr, goff, ggr, f, idx, wts, loff_col)


def _moe_kernel(te_ref, nt_ref, x_ref, wg_ref, wu_ref, wd_ref, y_ref, wg_s, wu_s, wd_s):
    i = pl.program_id(0)
    fresh = jnp.logical_or(i == 0, te_ref[i] != te_ref[jnp.maximum(i - 1, 0)])

    @pl.when(fresh)
    def _():
        wg_s[...] = wg_ref[0].astype(BF16)
        wu_s[...] = wu_ref[0].astype(BF16)
        wd_s[...] = wd_ref[0].astype(BF16)

    @pl.when(i < nt_ref[0])
    def _():
        half = wg_s.shape[0] // 2
        lo, hi = _unpack_bf16_pairs(x_ref[:, :half])
        gate = lax.bitcast_convert_type(x_ref[:, half:half + 1], F32)
        g = (jnp.dot(lo, wg_s[:half, :], preferred_element_type=F32)
             + jnp.dot(hi, wg_s[half:, :], preferred_element_type=F32))
        u = (jnp.dot(lo, wu_s[:half, :], preferred_element_type=F32)
             + jnp.dot(hi, wu_s[half:, :], preferred_element_type=F32))
        he = (_silu(g) * u).astype(BF16)
        y_ref[...] = _pack_bf16_pairs(jnp.dot(he, wd_s[...], preferred_element_type=F32) * gate)

    @pl.when(i >= nt_ref[0])
    def _():
        y_ref[...] = jnp.zeros_like(y_ref)


def _moe_grouped(x_sorted, tile_expert, n_tiles_used, w_gate, w_up, w_down):
    p, words = x_sorted.shape
    _, d, de = w_gate.shape
    tm = MOE_TILE
    grid_spec = pltpu.PrefetchScalarGridSpec(
        num_scalar_prefetch=2,
        grid=(p // tm,),
        in_specs=[
            pl.BlockSpec((tm, words), lambda i, te, nt: (i, 0)),
            pl.BlockSpec((1, d, de), lambda i, te, nt: (te[i], 0, 0)),
            pl.BlockSpec((1, d, de), lambda i, te, nt: (te[i], 0, 0)),
            pl.BlockSpec((1, de, d), lambda i, te, nt: (te[i], 0, 0)),
        ],
        out_specs=pl.BlockSpec((tm, d // 2), lambda i, te, nt: (i, 0)),
        scratch_shapes=[pltpu.VMEM((d, de), BF16), pltpu.VMEM((d, de), BF16), pltpu.VMEM((de, d), BF16)],
    )
    return pl.pallas_call(
        _moe_kernel,
        grid_spec=grid_spec,
        out_shape=jax.ShapeDtypeStruct((p, d // 2), U32),
        compiler_params=_cparams("arbitrary"),
        name="moe_grouped",
    )(tile_expert, n_tiles_used, x_sorted, w_gate, w_up, w_down)


def _combine_kernel(hoff_ref, loffs_ref, ngr_ref, h_ref, idx_ref, loff_ref, mod_ref, ys_ref, o_ref,
                    yperm, sems, *, nb):
    b, j = pl.program_id(0), pl.program_id(1)
    nj = pl.num_programs(1)
    step = b * nj + j
    slot = step % 2
    blk = b * nb + j
    nxt = jnp.where(j == nj - 1, (b + 1) * nb, blk + 1)

    def run_copies(blk_, slot_, wait):
        def fn(lo, ho, rows):
            cp = pltpu.make_async_copy(ys_ref.at[pl.ds(ho, rows)], yperm.at[slot_, pl.ds(lo, rows)], sems.at[slot_])
            cp.wait() if wait else cp.start()
        _for_each_run_piece(blk_, hoff_ref, loffs_ref, ngr_ref, fn)

    @pl.when(step == 0)
    def _():
        yperm[...] = jnp.zeros_like(yperm)
        run_copies(blk, 0, wait=False)

    @pl.when(step + 1 < pl.num_programs(0) * nj)
    def _():
        run_copies(nxt, 1 - slot, wait=False)

    q0, q1 = _block_perm(idx_ref, loff_ref)
    q = jnp.where(jnp.logical_or(q0, q1), 1.0, 0.0).astype(BF16)
    run_copies(blk, slot, wait=True)
    lo, hi = _unpack_bf16_pairs(yperm[slot])
    y = jnp.concatenate([_tn_dot(q, lo), _tn_dot(q, hi)], axis=1)
    o_ref[0] = h_ref[0] + mod_ref[0, 5:6, :] * y


def _combine(h, idx, loff_col, mod, mod_row, y_sorted, hoff, loffs, ngr, n_out):
    bsz, ltot, d = h.shape
    nb = ltot // TB
    grid_spec = pltpu.PrefetchScalarGridSpec(
        num_scalar_prefetch=3,
        grid=(bsz, n_out // TB),
        in_specs=[
            pl.BlockSpec((1, TB, d), lambda b, j, *_: (b, j, 0)),
            pl.BlockSpec((1, TOP_K, TB), lambda b, j, *_: (b, 0, j)),
            pl.BlockSpec((1, N_EXPERTS, 1), lambda b, j, *_: (b * nb + j, 0, 0)),
            pl.BlockSpec((1, 6, d), lambda b, j, *_: (mod_row(b, j), 0, 0)),
            pl.BlockSpec(memory_space=pl.ANY),
        ],
        out_specs=pl.BlockSpec((1, TB, d), lambda b, j, *_: (b, j, 0)),
        scratch_shapes=[pltpu.VMEM((2, PERM_ROWS, d // 2), U32), pltpu.SemaphoreType.DMA((2,))],
    )
    return pl.pallas_call(
        functools.partial(_combine_kernel, nb=nb),
        grid_spec=grid_spec,
        out_shape=jax.ShapeDtypeStruct((bsz, n_out, d), F32),
        compiler_params=_cparams("arbitrary", "arbitrary"),
        name="moe_combine",
    )(hoff, loffs, ngr, h, idx, loff_col, mod, y_sorted)


def _moe_block(h, f, idx, wts, mod, mod_row, w_gate, w_up, w_down, first_expert, n_out):
    bsz, ltot, d = h.shape
    nb = ltot // TB
    n_blocks = bsz * nb
    mt = MOE_TILE
    experts = jnp.arange(N_EXPERTS, dtype=jnp.int32)
    cnt = jnp.sum((idx.reshape(bsz, TOP_K, nb, TB, 1) == experts).astype(jnp.int32), axis=(1, 3))
    cnt = cnt.reshape(n_blocks, N_EXPERTS)
    run = ((cnt + RUN_ALIGN - 1) // RUN_ALIGN) * RUN_ALIGN
    loffs = jnp.cumsum(run, axis=1) - run
    region = ((jnp.sum(run, axis=0) + mt - 1) // mt) * mt
    ends = jnp.cumsum(region)
    hoff = (ends - region)[None, :] + jnp.cumsum(run, axis=0) - run
    p_total = ((TOP_K * bsz * ltot + n_blocks * N_EXPERTS * (RUN_ALIGN - 1)) // mt + N_EXPERTS + 1) * mt
    tile_start = jnp.arange(p_total // mt, dtype=jnp.int32) * mt
    tile_expert = jnp.minimum(jnp.sum((tile_start[:, None] >= ends[None, :]).astype(jnp.int32), axis=1),
                              N_EXPERTS - 1)
    n_used = (ends[-1] // mt).astype(jnp.int32).reshape(1)
    flat = lambda t: t.reshape(-1).astype(jnp.int32)
    hoff, loffs_flat, ngr = flat(hoff), flat(loffs), flat(run // RUN_ALIGN)
    loff_col = loffs.astype(F32).reshape(n_blocks, N_EXPERTS, 1)
    used = jnp.sum(run, axis=0)
    goff = flat(jnp.concatenate([ends - region + used, ends[-1:]]))
    ggr = flat(jnp.concatenate([(region - used) // RUN_ALIGN, (p_total - ends[-1:]) // TB]))

    x_sorted = _dispatch(f, idx, wts, loff_col, hoff, loffs_flat, ngr, goff, ggr, p_total)
    y_sorted = _moe_grouped(x_sorted, tile_expert + first_expert, n_used, w_gate, w_up, w_down)
    return _combine(h, idx, loff_col, mod, mod_row, y_sorted, hoff, loffs_flat, ngr, n_out)


def _rope_tables(n_lat, n_ctx, n_heads):
    t = jnp.arange(n_lat)
    pos = jnp.stack([t // GRID_W, t % GRID_W], axis=-1).astype(F32)
    n_freq = HEAD_DIM // 4
    inv = 1.0 / (ROPE_BASE ** (jnp.arange(n_freq, dtype=F32) / n_freq))
    ang = pos[:, :, None] * inv
    cos, sin = jnp.cos(ang), jnp.sin(ang)
    cos_h = jnp.concatenate([cos[:, 0], cos[:, 0], cos[:, 1], cos[:, 1]], axis=-1)
    sin_h = jnp.concatenate([-sin[:, 0], sin[:, 0], -sin[:, 1], sin[:, 1]], axis=-1)
    cos_h = jnp.concatenate([cos_h, jnp.ones((n_ctx, HEAD_DIM), F32)], axis=0)
    sin_h = jnp.concatenate([sin_h, jnp.zeros((n_ctx, HEAD_DIM), F32)], axis=0)
    return jnp.tile(cos_h, (1, n_heads)), jnp.tile(sin_h, (1, n_heads))


def _pack_w_in(w_in, rw, sw, nw_):
    o = 0
    seg = {}
    for name, width in (("rq", rw), ("rk", rw), ("rv", rw), ("rg", rw), ("z", sw), ("x", sw),
                        ("b", SSD_GROUPS * SSD_STATE), ("c", SSD_GROUPS * SSD_STATE),
                        ("dt", 2 * (sw // HEAD_DIM)), ("nq", nw_), ("nk", nw_), ("nv", nw_)):
        seg[name] = w_in[:, o:o + width]
        o += width
    dt_pad = jnp.zeros((w_in.shape[0], 128 - seg["dt"].shape[1]), w_in.dtype)
    cols = [seg["rq"], seg["rk"], seg["rv"], seg["rg"],
            seg["z"], seg["x"], seg["b"], seg["c"], seg["nq"], seg["nk"], seg["nv"], seg["dt"], dt_pad]
    return jnp.concatenate(cols, axis=1).astype(BF16)


def kernel(x, c, ctx, c_ctx, w_mod, b_mod, norm_mix, norm_ffn, w_in, w_out, ret_decay_f, ret_decay_b,
           ssd_conv_w, ssd_conv_b, ssd_dt_bias_f, ssd_dt_bias_b, ssd_a_log_f, ssd_a_log_b, ssd_d, ssd_norm,
           na_q_norm, na_k_norm, na_rpb, router_w, router_b, w_gate, w_up, w_down):
    bsz, n_lat, d = x.shape
    n_ctx = ctx.shape[1]
    depth = w_mod.shape[0]
    rows = n_lat // GRID_W
    rw = ret_decay_f.shape[1] * HEAD_DIM
    sw = ssd_d.shape[1] * HEAD_DIM
    nw_ = na_rpb.shape[1] * HEAD_DIM
    dims = (rw, sw, nw_)
    assert n_lat % TB == 0 and n_ctx % TB == 0
    nbl = n_lat // TB

    n_cond = ((bsz + 1 + 7) // 8) * 8
    cond = jnp.zeros((n_cond, d), F32).at[:bsz].set(c).at[bsz].set(c_ctx)
    mod = _modulation(cond, w_mod, b_mod).reshape(depth, n_cond, 6, d)
    mod_row = lambda b, j: jnp.where(j < nbl, b, bsz)

    cos, sin = _rope_tables(n_lat, n_ctx, rw // HEAD_DIM)
    hm = jnp.asarray(np.kron(np.eye(nw_ // HEAD_DIM), np.full((HEAD_DIM, HEAD_DIM), 1.0 / HEAD_DIM)), F32)
    rw_t = jnp.pad(jnp.concatenate(_bf16_parts(router_w, 3), axis=1), ((0, 0), (0, 128 - 3 * router_w.shape[1])))
    rb = router_b.reshape(-1, 1)

    expert_w = [t.reshape((-1,) + t.shape[2:]) for t in (w_gate, w_up, w_down)]
    h = (x, ctx)
    for l in range(depth):
        last = l == depth - 1
        w_all = _pack_w_in(w_in[l], rw, sw, nw_)
        qkn = jnp.stack([jnp.tile(na_q_norm[l], nw_ // HEAD_DIM), jnp.tile(na_k_norm[l], nw_ // HEAD_DIM)])
        ret, z, xbc, dt, na = _inproj(h, mod[l], mod_row, norm_mix[l].reshape(1, d), w_all, cos, sin, qkn, hm, dims, nbl)
        ret_o, ssd_o = _scan_mixers(
            ret, jnp.concatenate([ret_decay_f[l], ret_decay_b[l]]).reshape(-1, 1, 1), z, xbc, dt,
            ssd_conv_w[l], ssd_conv_b[l], jnp.concatenate([ssd_dt_bias_f[l], ssd_dt_bias_b[l]]),
            jnp.concatenate([ssd_a_log_f[l], ssd_a_log_b[l]]), ssd_d[l], ssd_norm[l], nbl)
        na_o = _na_attention(na, _na_bias_tables(na_rpb[l], rows), nbl)
        h, f, idx, wts = _outproj(h, ret_o, ssd_o, na_o, w_out[l].astype(BF16), mod[l], mod_row,
                                  norm_ffn[l].reshape(1, d), rw_t, rb, nbl)
        h = _moe_block(h, f, idx, wts, mod[l], mod_row, *expert_w, l * N_EXPERTS,
                       n_lat if last else n_lat + n_ctx)
    return h
```

```python
import functools

import jax
import jax.numpy as jnp
import numpy as np
from jax import lax
from jax.experimental import pallas as pl
from jax.experimental.pallas import tpu as pltpu

F32 = jnp.float32
BF16 = jnp.bfloat16
HIGHEST = lax.Precision.HIGHEST

GRID_W = 64
HEAD_DIM = 64
SSD_GROUPS = 2
SSD_STATE = 64
NA_WIN_ROWS = 8
NA_WIN_COLS = 16
N_GROUPS = 4
EXPERTS_PER_GROUP = 4
N_EXPERTS = N_GROUPS * EXPERTS_PER_GROUP
TOP_K = 2
ROPE_BASE = 10000.0
EPS = 1e-6

VMEM_LIMIT_BYTES = 56 * 1024 * 1024
TB = 256
NA_QROWS = TB // GRID_W
MOE_TILE = 512
SUBLANES = 8


def _cparams(*sem):
    return pltpu.CompilerParams(dimension_semantics=sem, vmem_limit_bytes=VMEM_LIMIT_BYTES)


def _silu(t):
    return t * (1.0 / (1.0 + jnp.exp(-t)))


def _nt_dot(a, b, **kw):
    return lax.dot_general(a, b, (((1,), (1,)), ((), ())), preferred_element_type=F32, **kw)


def _tn_dot(a, b):
    return lax.dot_general(a, b, (((0,), (0,)), ((), ())), preferred_element_type=F32)


def _bf16_parts(x, n):
    parts, rest = [], x
    for i in range(n):
        p = rest.astype(BF16)
        parts.append(p)
        if i + 1 < n:
            rest = rest - p.astype(F32)
    return parts


def _split_dot(x, sel, parts=3, left=False):
    sel = sel.astype(BF16)
    dot = (lambda p: jnp.dot(sel, p, preferred_element_type=F32)) if left else (
        lambda p: jnp.dot(p, sel, preferred_element_type=F32))
    out = None
    for p in _bf16_parts(x, parts):
        out = dot(p) if out is None else out + dot(p)
    return out


def _h_specs(h, nbl):
    if not isinstance(h, tuple):
        return (h,), [pl.BlockSpec((1, TB, h.shape[2]), lambda b, j: (b, j, 0))]
    d = h[0].shape[2]
    return h, [pl.BlockSpec((1, TB, d), lambda b, j: (b, jnp.minimum(j, nbl - 1), 0)),
               pl.BlockSpec((1, TB, d), lambda b, j: (b, jnp.maximum(j - nbl, 0), 0))]


def _read_h(h_refs, nbl):
    if len(h_refs) == 1:
        return h_refs[0][0]
    return jnp.where(pl.program_id(1) < nbl, h_refs[0][0], h_refs[1][0])


def _mod_kernel(s_ref, w_ref, b_ref, o_ref):
    s = _silu(s_ref[...])
    o_ref[0] = jnp.dot(s, w_ref[0], precision=HIGHEST, preferred_element_type=F32) + b_ref[0]


def _modulation(cond, w_mod, b_mod):
    depth, d, n = w_mod.shape
    rows = cond.shape[0]
    tn = 1024
    return pl.pallas_call(
        _mod_kernel,
        grid=(depth, n // tn),
        in_specs=[
            pl.BlockSpec((rows, d), lambda l, j: (0, 0)),
            pl.BlockSpec((1, d, tn), lambda l, j: (l, 0, j)),
            pl.BlockSpec((1, 1, tn), lambda l, j: (l, 0, j)),
        ],
        out_specs=pl.BlockSpec((1, rows, tn), lambda l, j: (l, 0, j)),
        out_shape=jax.ShapeDtypeStruct((depth, rows, n), F32),
        compiler_params=_cparams("parallel", "parallel"),
        name="modulation",
    )(cond, w_mod, b_mod.reshape(depth, 1, n))


def _inproj_kernel(*refs, dims, n_h, nbl):
    h_refs, (mod_ref, nw_ref, w_ref, cos_ref, sin_ref, qkn_ref, hm_ref,
             ret_ref, z_ref, xbc_ref, dt_ref, na_ref) = refs[:n_h], refs[n_h:]
    rw, sw, nw_ = dims
    h = _read_h(h_refs, nbl)
    a = h * lax.rsqrt(jnp.mean(h * h, axis=-1, keepdims=True) + EPS) * nw_ref[...]
    a = a * (1.0 + mod_ref[0, 1:2, :]) + mod_ref[0, 0:1, :]
    ab = a.astype(BF16)

    def proj(lo, hi):
        return jnp.dot(ab, w_ref[:, lo:hi], preferred_element_type=F32)

    cos, sin = cos_ref[...], sin_ref[...]
    nf = HEAD_DIM // 4
    first_half = (lax.broadcasted_iota(jnp.int32, cos.shape, 1) % (2 * nf)) < nf

    def rotate(t):
        return t * cos + jnp.where(first_half, pltpu.roll(t, rw - nf, 1), pltpu.roll(t, nf, 1)) * sin

    o = 0
    ret_ref[0, :, 0:rw] = rotate(proj(o, o + rw))
    ret_ref[0, :, rw:2 * rw] = rotate(proj(o + rw, o + 2 * rw)) * (HEAD_DIM ** -0.5)
    ret_ref[0, :, 2 * rw:4 * rw] = proj(o + 2 * rw, o + 4 * rw)
    o += 4 * rw
    z_ref[0] = proj(o, o + sw)
    o += sw
    xbc_w = sw + 2 * SSD_GROUPS * SSD_STATE
    xbc_ref[0] = proj(o, o + xbc_w)
    o += xbc_w
    hm = hm_ref[...]
    for i in range(2):
        t = proj(o + i * nw_, o + (i + 1) * nw_)
        ms = _split_dot(t * t, hm, parts=2)
        na_ref[0, :, i * nw_:(i + 1) * nw_] = (t * lax.rsqrt(ms + EPS) * qkn_ref[i:i + 1, :]).astype(BF16)
    na_ref[0, :, 2 * nw_:3 * nw_] = proj(o + 2 * nw_, o + 3 * nw_).astype(BF16)
    o += 3 * nw_
    dt_ref[0] = proj(o, o + 128)[:, 0:dt_ref.shape[2]]


def _inproj(h, mod, mod_row, norm_w, w_all, cos, sin, qkn, hm, dims, nbl):
    h_arrays, h_specs = _h_specs(h, nbl)
    bsz, d = h_arrays[0].shape[0], h_arrays[0].shape[2]
    ltot = cos.shape[0]
    rw, sw, nw_ = dims
    n_dt = 2 * (sw // HEAD_DIM)
    xbc_w = sw + 2 * SSD_GROUPS * SSD_STATE
    row = lambda n: pl.BlockSpec((1, TB, n), lambda b, j: (b, j, 0))
    const = lambda shape: pl.BlockSpec(shape, lambda b, j: (0,) * len(shape))
    widths = (4 * rw, sw, xbc_w, n_dt, 3 * nw_)
    dtypes = (F32, F32, F32, F32, BF16)
    return pl.pallas_call(
        functools.partial(_inproj_kernel, dims=dims, n_h=len(h_arrays), nbl=nbl),
        grid=(bsz, ltot // TB),
        in_specs=h_specs + [
            pl.BlockSpec((1, 6, d), lambda b, j: (mod_row(b, j), 0, 0)),
            const((1, d)),
            const(w_all.shape),
            pl.BlockSpec((TB, rw), lambda b, j: (j, 0)),
            pl.BlockSpec((TB, rw), lambda b, j: (j, 0)),
            const((2, nw_)),
            const((nw_, nw_)),
        ],
        out_specs=[row(n) for n in widths],
        out_shape=[jax.ShapeDtypeStruct((bsz, ltot, n), dt) for n, dt in zip(widths, dtypes)],
        compiler_params=_cparams("parallel", "parallel"),
        name="inproj",
    )(*h_arrays, mod, norm_w, w_all, cos, sin, qkn, hm)


def _scan_chunk_maps(nb, nbl):
    fwd = lambda s: jnp.where(s < nb, (nbl + s) % nb, 2 * nb - 1 - s)
    out = lambda s: jnp.where(s < nb, nb - 1, 2 * nb - 1 - s)
    return fwd, out


def _ret_phases(q_ref, k_ref, v_ref, g_ref, dec_ref, o_ref, sf_all, sf, sb, dmat, rd, rdt, *, nb, nbl):
    s = pl.program_id(1)
    c = TB
    nh = k_ref.shape[2] // HEAD_DIM
    log_f = [-jnp.exp(dec_ref[h]) for h in range(nh)]
    log_b = [-jnp.exp(dec_ref[nh + h]) for h in range(nh)]

    def _init():
        sf[...] = jnp.zeros_like(sf)
        sb[...] = jnp.zeros_like(sb)
        delta = (lax.broadcasted_iota(jnp.int32, (c, c), 0) - lax.broadcasted_iota(jnp.int32, (c, c), 1)).astype(F32)
        pos = lax.broadcasted_iota(jnp.int32, (c, HEAD_DIM), 0).astype(F32)
        pos_l = lax.broadcasted_iota(jnp.int32, (SUBLANES, c), 1).astype(F32)
        for h in range(nh):
            dmat[h] = (jnp.exp(jnp.where(delta >= 0, log_f[h] * delta, -jnp.inf))
                       + jnp.exp(jnp.where(delta <= 0, -log_b[h] * delta, -jnp.inf)))
            rd[2 * h + 0] = jnp.exp(log_f[h] * (pos + 1.0))
            rd[2 * h + 1] = jnp.exp(log_b[h] * (c - pos))
            rdt[2 * h + 0] = jnp.exp(log_f[h] * (c - 1.0 - pos_l))
            rdt[2 * h + 1] = jnp.exp(log_b[h] * pos_l)

    def _state_sweep():
        ci = (nbl + s) % nb
        k_t = k_ref[0].T
        for h in range(nh):
            sl = slice(h * HEAD_DIM, (h + 1) * HEAD_DIM)
            kw = (k_t[sl, :] * rdt[2 * h + 0, 0:1, :]).astype(BF16)
            sf_all[ci, h] = sf[h]
            sf[h] = sf[h] * jnp.exp(log_f[h] * c) + jnp.dot(kw, v_ref[0, :, sl].astype(BF16),
                                                            preferred_element_type=F32)

    def _output_sweep():
        ci = 2 * nb - 1 - s
        k_t = k_ref[0].T
        for h in range(nh):
            sl = slice(h * HEAD_DIM, (h + 1) * HEAD_DIM)
            qb, vb = q_ref[0, :, sl].astype(BF16), v_ref[0, :, sl].astype(BF16)
            p = jnp.dot(qb, k_t[sl, :].astype(BF16), preferred_element_type=F32) * dmat[h]
            y = jnp.dot(p.astype(BF16), vb, preferred_element_type=F32)
            y += jnp.dot(qb, sf_all[ci, h].astype(BF16), preferred_element_type=F32) * rd[2 * h + 0]
            y += jnp.dot(qb, sb[h].astype(BF16), preferred_element_type=F32) * rd[2 * h + 1]
            kw = (k_t[sl, :] * rdt[2 * h + 1, 0:1, :]).astype(BF16)
            sb[h] = sb[h] * jnp.exp(log_b[h] * c) + jnp.dot(kw, vb, preferred_element_type=F32)
            mu = jnp.mean(y, axis=-1, keepdims=True)
            yc = y - mu
            yn = yc * lax.rsqrt(jnp.mean(yc * yc, axis=-1, keepdims=True) + EPS)
            o_ref[0, :, sl] = (yn * _silu(g_ref[0, :, sl])).astype(o_ref.dtype)

    return _init, _state_sweep, _output_sweep


def _softplus(t):
    return jnp.maximum(t, 0.0) + jnp.log1p(jnp.exp(-jnp.abs(t)))


def _lane_group_selector(n_rows, group):
    shape = (n_rows, n_rows * group)
    lane_owner = lax.broadcasted_iota(jnp.int32, shape, 1) // group
    return jnp.where(lane_owner == lax.broadcasted_iota(jnp.int32, shape, 0), 1.0, 0.0).astype(BF16)


def _ssd_phases(z_ref, x_ref, xp_ref, xn_ref, dt_ref, cw_ref, cb_ref, dtb_ref, alog_ref, dtbt_ref, alogt_ref,
                dsk_ref, nw_ref, o_ref, sf_all, sf, sb, tri, ybuf, xc_all, *, nb, nbl):
    s = pl.program_id(1)
    c = TB
    sw = z_ref.shape[2]
    nh = sw // HEAD_DIM
    rep = nh // SSD_GROUPS
    gw = rep * HEAD_DIM
    gs = SSD_GROUPS * SSD_STATE
    ci = jnp.where(s < nb, (nbl + s) % nb, 2 * nb - 1 - s)

    def _init():
        sf[...] = jnp.zeros_like(sf)
        sb[...] = jnp.zeros_like(sb)
        li = lax.broadcasted_iota(jnp.int32, (c, c), 0)
        si = lax.broadcasted_iota(jnp.int32, (c, c), 1)
        tri[0] = jnp.where(si <= li, 1.0, 0.0).astype(BF16)
        tri[1] = jnp.where(si >= li, 1.0, 0.0).astype(BF16)

    def _sweeps():
        def conv_silu():
            x = x_ref[0]
            has_prev = jnp.where(jnp.logical_or(ci == 0, ci == nbl), 0.0, 1.0)
            has_next = jnp.where(jnp.logical_or(ci == nbl - 1, ci == nb - 1), 0.0, 1.0)
            rows = lax.broadcasted_iota(jnp.int32, x.shape, 0)
            x_dn = jnp.where(rows == 0, xp_ref[0, SUBLANES - 1:SUBLANES, :] * has_prev, pltpu.roll(x, 1, 0))
            x_up = jnp.where(rows == c - 1, xn_ref[0, 0:1, :] * has_next, pltpu.roll(x, c - 1, 0))
            return _silu(cw_ref[0:1, :] * x_dn + cw_ref[1:2, :] * x + cw_ref[2:3, :] * x_up + cb_ref[...])

        def split(xc):
            xs, bm, cm = xc[:, :sw], xc[:, sw:sw + gs], xc[:, sw + gs:]
            return xs, bm.T.astype(BF16), cm

        dtr = dt_ref[0]
        dt = _softplus(dtr + dtb_ref[...])
        a = dt * -jnp.exp(alog_ref[...])
        pre = _split_dot(a, tri[0], left=True)
        tot = pre[c - 1:c, :]
        is_fwd = lax.broadcasted_iota(jnp.int32, (1, 2 * nh), 1) < nh
        acs = jnp.where(is_fwd, pre, tot - pre + a)
        sel_head = _lane_group_selector(2 * nh, HEAD_DIM)
        w_wide = _split_dot(dt * jnp.exp(tot - acs), sel_head, parts=2)
        etot_wide = _split_dot(jnp.broadcast_to(jnp.exp(tot), (SUBLANES, 2 * nh)), sel_head, parts=2)[0:1, :]

        def _state_sweep():
            xc = conv_silu()
            xc_all[ci] = xc
            xs, bm_t, _ = split(xc)
            for g in range(SSD_GROUPS):
                lanes = slice(g * gw, (g + 1) * gw)
                xw = (xs[:, lanes] * w_wide[:, lanes]).astype(BF16)
                sf_all[ci, g] = sf[g]
                sf[g] = sf[g] * etot_wide[:, lanes] + jnp.dot(bm_t[g * SSD_STATE:(g + 1) * SSD_STATE, :], xw,
                                                              preferred_element_type=F32)

        def _output_sweep():
            xs, bm_t, cm = split(xc_all[ci])
            eye = (lax.broadcasted_iota(jnp.int32, (2 * nh, 2 * nh), 0)
                   == lax.broadcasted_iota(jnp.int32, (2 * nh, 2 * nh), 1)).astype(BF16)
            dtr_t = sum(_nt_dot(eye, p) for p in _bf16_parts(dtr, 3))
            dt_t = _softplus(dtr_t + dtbt_ref[...])
            a_t = dt_t * -jnp.exp(alogt_ref[...])
            pre_t = _split_dot(a_t, tri[1])
            is_fwd_t = lax.broadcasted_iota(jnp.int32, (2 * nh, 1), 0) < nh
            acs_t = jnp.where(is_fwd_t, pre_t, pre_t[:, c - 1:c] - pre_t + a_t)
            e_wide = _split_dot(jnp.exp(acs), sel_head, parts=2)
            li = lax.broadcasted_iota(jnp.int32, (c, c), 0)
            si = lax.broadcasted_iota(jnp.int32, (c, c), 1)
            lower, upper = li >= si, si >= li
            for g in range(SSD_GROUPS):
                lanes = slice(g * gw, (g + 1) * gw)
                lanes_b = slice(sw + g * gw, sw + (g + 1) * gw)
                cg = cm[:, g * SSD_STATE:(g + 1) * SSD_STATE].astype(BF16)
                bg_t = bm_t[g * SSD_STATE:(g + 1) * SSD_STATE, :]
                scores = jnp.dot(cg, bg_t, preferred_element_type=F32)
                for r in range(rep):
                    h = g * rep + r
                    hb = nh + h
                    col_f, col_b = acs[:, h:h + 1], acs[:, hb:hb + 1]
                    d_f = jnp.exp(jnp.where(lower, col_f - acs_t[h:h + 1, :], -jnp.inf)) * dt_t[h:h + 1, :]
                    d_b = jnp.exp(jnp.where(upper, col_b - acs_t[hb:hb + 1, :], -jnp.inf)) * dt_t[hb:hb + 1, :]
                    ybuf[:, h * HEAD_DIM:(h + 1) * HEAD_DIM] = jnp.dot(
                        (scores * (d_f + d_b)).astype(BF16), xs[:, h * HEAD_DIM:(h + 1) * HEAD_DIM].astype(BF16),
                        preferred_element_type=F32)
                ybuf[:, lanes] += (jnp.dot(cg, sf_all[ci, g].astype(BF16), preferred_element_type=F32) * e_wide[:, lanes]
                                   + jnp.dot(cg, sb[g].astype(BF16), preferred_element_type=F32) * e_wide[:, lanes_b])
                xw = (xs[:, lanes] * w_wide[:, lanes_b]).astype(BF16)
                sb[g] = sb[g] * etot_wide[:, lanes_b] + jnp.dot(bg_t, xw, preferred_element_type=F32)
            y = (ybuf[...] + dsk_ref[...] * xs) * _silu(z_ref[0])
            o_ref[0] = (y * lax.rsqrt(jnp.mean(y * y, axis=-1, keepdims=True) + EPS) * nw_ref[...]).astype(o_ref.dtype)

        return _state_sweep, _output_sweep

    return _init, _sweeps


N_RET_IN, N_SSD_IN, N_RET_SCRATCH = 5, 13, 6


def _scan_mixers_kernel(*refs, nb, nbl):
    ret_in, refs = refs[:N_RET_IN], refs[N_RET_IN:]
    ssd_in, refs = refs[:N_SSD_IN], refs[N_SSD_IN:]
    (ret_o, ssd_o), refs = refs[:2], refs[2:]
    ret_init, ret_state, ret_out = _ret_phases(*ret_in, ret_o, *refs[:N_RET_SCRATCH], nb=nb, nbl=nbl)
    ssd_init, ssd_sweeps = _ssd_phases(*ssd_in, ssd_o, *refs[N_RET_SCRATCH:], nb=nb, nbl=nbl)
    s = pl.program_id(1)
    pl.when(s == 0)(ret_init)
    pl.when(s < nb)(ret_state)
    pl.when(s >= nb)(ret_out)
    pl.when(s == 0)(ssd_init)
    ssd_state, ssd_out = ssd_sweeps()
    pl.when(s < nb)(ssd_state)
    pl.when(s >= nb)(ssd_out)


def _scan_mixers(ret, decay, z, xbc, dt, conv_w, conv_b, dtb, alog, d_skip, norm_w, nbl):
    bsz, ltot, w4 = ret.shape
    w = w4 // 4
    nh_r = w // HEAD_DIM
    sw = z.shape[2]
    nh = sw // HEAD_DIM
    xw = xbc.shape[2]
    nb = ltot // TB
    per = TB // SUBLANES
    fwd, out = _scan_chunk_maps(nb, nbl)
    conv_in = lambda s: fwd(jnp.minimum(s, nb - 1))
    const = lambda t: pl.BlockSpec(t.shape, lambda b, s: (0,) * t.ndim)
    args = (conv_w, conv_b.reshape(1, xw), dtb.reshape(1, 2 * nh), alog.reshape(1, 2 * nh),
            dtb.reshape(2 * nh, 1), alog.reshape(2 * nh, 1), jnp.repeat(d_skip, HEAD_DIM).reshape(1, sw),
            norm_w.reshape(1, sw))
    ret_specs = [
        pl.BlockSpec((1, TB, w), lambda b, s: (b, out(s), 0)),
        pl.BlockSpec((1, TB, w), lambda b, s: (b, fwd(s), 1)),
        pl.BlockSpec((1, TB, w), lambda b, s: (b, fwd(s), 2)),
        pl.BlockSpec((1, TB, w), lambda b, s: (b, out(s), 3)),
        const(decay),
    ]
    ssd_specs = [
        pl.BlockSpec((1, TB, sw), lambda b, s: (b, out(s), 0)),
        pl.BlockSpec((1, TB, xw), lambda b, s: (b, conv_in(s), 0)),
        pl.BlockSpec((1, SUBLANES, xw), lambda b, s: (b, jnp.maximum(conv_in(s) * per - 1, 0), 0)),
        pl.BlockSpec((1, SUBLANES, xw), lambda b, s: (b, jnp.minimum((conv_in(s) + 1) * per, nb * per - 1), 0)),
        pl.BlockSpec((1, TB, 2 * nh), lambda b, s: (b, fwd(s), 0)),
    ] + [const(t) for t in args]
    assert len(ret_specs) == N_RET_IN and len(ssd_specs) == N_SSD_IN
    ret_scratch = [
        pltpu.VMEM((nb, nh_r, HEAD_DIM, HEAD_DIM), F32),
        pltpu.VMEM((nh_r, HEAD_DIM, HEAD_DIM), F32),
        pltpu.VMEM((nh_r, HEAD_DIM, HEAD_DIM), F32),
        pltpu.VMEM((nh_r, TB, TB), F32),
        pltpu.VMEM((2 * nh_r, TB, HEAD_DIM), F32),
        pltpu.VMEM((2 * nh_r, SUBLANES, TB), F32),
    ]
    ssd_scratch = [
        pltpu.VMEM((nb, SSD_GROUPS, SSD_STATE, sw // SSD_GROUPS), F32),
        pltpu.VMEM((SSD_GROUPS, SSD_STATE, sw // SSD_GROUPS), F32),
        pltpu.VMEM((SSD_GROUPS, SSD_STATE, sw // SSD_GROUPS), F32),
        pltpu.VMEM((2, TB, TB), BF16),
        pltpu.VMEM((TB, sw), F32),
        pltpu.VMEM((nb, TB, xw), F32),
    ]
    assert len(ret_scratch) == N_RET_SCRATCH
    return pl.pallas_call(
        functools.partial(_scan_mixers_kernel, nb=nb, nbl=nbl),
        grid=(bsz, 2 * nb),
        in_specs=ret_specs + ssd_specs,
        out_specs=[pl.BlockSpec((1, TB, w), lambda b, s: (b, out(s), 0)),
                   pl.BlockSpec((1, TB, sw), lambda b, s: (b, out(s), 0))],
        out_shape=[jax.ShapeDtypeStruct((bsz, ltot, w), BF16), jax.ShapeDtypeStruct((bsz, ltot, sw), BF16)],
        scratch_shapes=ret_scratch + ssd_scratch,
        compiler_params=_cparams("parallel", "arbitrary"),
        name="scan_mixers",
    )(ret, ret, ret, ret, decay, z, xbc, xbc, xbc, dt, *args)


def _softmax_pv(s_parts, v):
    m = s_parts[0].max(axis=-1, keepdims=True)
    for s in s_parts[1:]:
        m = jnp.maximum(m, s.max(axis=-1, keepdims=True))
    p_parts = [jnp.exp(s - m) for s in s_parts]
    den = p_parts[0].sum(axis=-1, keepdims=True)
    for p in p_parts[1:]:
        den = den + p.sum(axis=-1, keepdims=True)
    p = p_parts[0] if len(p_parts) == 1 else jnp.concatenate(p_parts, axis=1)
    return jnp.dot(p.astype(BF16), v, preferred_element_type=F32) / den


def _na_kernel(q_ref, k0_ref, k1_ref, k2_ref, v0_ref, v1_ref, v2_ref, kc_ref, vc_ref, bias_ref, o_ref, *, nbl):
    nh = q_ref.shape[2] // HEAD_DIM
    n_lat_keys = 3 * k0_ref.shape[1]
    j = pl.program_id(1)

    @pl.when(j < nbl)
    def _latent():
        for h in range(nh):
            sl = slice(h * HEAD_DIM, (h + 1) * HEAD_DIM)
            q = q_ref[0, :, sl] * (HEAD_DIM ** -0.5)
            k = jnp.concatenate([k0_ref[0, :, sl], k1_ref[0, :, sl], k2_ref[0, :, sl], kc_ref[0, :, sl]], axis=0)
            v = jnp.concatenate([v0_ref[0, :, sl], v1_ref[0, :, sl], v2_ref[0, :, sl], vc_ref[0, :, sl]], axis=0)
            s = _nt_dot(q, k)
            o = _softmax_pv([s[:, :n_lat_keys] + bias_ref[0, h], s[:, n_lat_keys:]], v)
            o_ref[0, :, sl] = o.astype(o_ref.dtype)

    @pl.when(j >= nbl)
    def _context():
        for h in range(nh):
            sl = slice(h * HEAD_DIM, (h + 1) * HEAD_DIM)
            s = _nt_dot(q_ref[0, :, sl] * (HEAD_DIM ** -0.5), kc_ref[0, :, sl])
            o_ref[0, :, sl] = _softmax_pv([s], vc_ref[0, :, sl]).astype(o_ref.dtype)


def _na_bias_tables(rpb, rows):
    qb, kb = NA_QROWS, 3 * NA_QROWS
    nb = rows // qb
    nh = rpb.shape[0]
    n_dr, n_dc = 2 * NA_WIN_ROWS - 1, 2 * NA_WIN_COLS - 1
    c = np.arange(GRID_W)[:, None]
    kc = np.arange(GRID_W)[None, :]
    cs = np.clip(c - NA_WIN_COLS // 2, 0, GRID_W - NA_WIN_COLS)
    col_ok = (kc >= cs) & (kc < cs + NA_WIN_COLS)
    dc = np.clip(kc - c + NA_WIN_COLS - 1, 0, n_dc - 1)
    col_sel = np.eye(n_dc, dtype=np.float32)[dc.reshape(-1)].T
    tabs = []
    for rbq, wb in ((0, 0), (1, 0), (nb - 1, nb - 3)):
        r = qb * rbq + np.arange(qb)[:, None]
        kr = qb * wb + np.arange(kb)[None, :]
        r0 = np.clip(r - NA_WIN_ROWS // 2, 0, rows - NA_WIN_ROWS)
        row_ok = (kr >= r0) & (kr < r0 + NA_WIN_ROWS)
        dr = np.clip(kr - r + NA_WIN_ROWS - 1, 0, n_dr - 1)
        row_sel = np.eye(n_dr, dtype=np.float32)[dr.reshape(-1)]
        t = jnp.einsum("ad,hde,ef->haf", row_sel, rpb.astype(F32), col_sel, precision=HIGHEST)
        t = t.reshape(nh, qb, kb, GRID_W, GRID_W).transpose(0, 1, 3, 2, 4).reshape(nh, qb * GRID_W, kb * GRID_W)
        valid = (row_ok[:, None, :, None] & col_ok[None, :, None, :]).reshape(qb * GRID_W, kb * GRID_W)
        tabs.append(jnp.where(valid, t, -jnp.inf))
    return jnp.stack(tabs)


def _na_attention(na, bias, nbl):
    bsz, ltot, w3 = na.shape
    w = w3 // 3
    nb = ltot // TB
    assert nbl >= 3 and nb == nbl + 1, "needs >= 12 grid rows and a context of one token block"
    win = lambda j: jnp.clip(j - 1, 0, nbl - 3)
    kv = lambda i, col: pl.BlockSpec((1, TB, w), lambda b, j: (b, win(j) + i, col))
    typ = lambda j: jnp.where(j == 0, 0, jnp.where(j >= nbl - 1, 2, 1))
    return pl.pallas_call(
        functools.partial(_na_kernel, nbl=nbl),
        grid=(bsz, nb),
        in_specs=[
            pl.BlockSpec((1, TB, w), lambda b, j: (b, j, 0)),
            kv(0, 1), kv(1, 1), kv(2, 1), kv(0, 2), kv(1, 2), kv(2, 2),
            pl.BlockSpec((1, TB, w), lambda b, j: (b, nbl, 1)),
            pl.BlockSpec((1, TB, w), lambda b, j: (b, nbl, 2)),
            pl.BlockSpec((1,) + bias.shape[1:], lambda b, j: (typ(j), 0, 0, 0)),
        ],
        out_specs=pl.BlockSpec((1, TB, w), lambda b, j: (b, j, 0)),
        out_shape=jax.ShapeDtypeStruct((bsz, ltot, w), BF16),
        compiler_params=_cparams("parallel", "arbitrary"),
        name="na_attention",
    )(na, na, na, na, na, na, na, na, na, bias)


def _route(aff, sel):
    rows = [sel[e:e + 1, :] for e in range(N_EXPERTS)]
    arow = [aff[e:e + 1, :] for e in range(N_EXPERTS)]
    gscore = []
    for g in range(N_GROUPS):
        a, b, c, d = rows[4 * g:4 * g + 4]
        hi1, lo1 = jnp.maximum(a, b), jnp.minimum(a, b)
        hi2, lo2 = jnp.maximum(c, d), jnp.minimum(c, d)
        gscore.append(jnp.maximum(hi1, hi2) + jnp.maximum(jnp.minimum(hi1, hi2), jnp.maximum(lo1, lo2)))
    best = jnp.zeros_like(gscore[0], dtype=jnp.int32)
    top = gscore[0]
    for g in range(1, N_GROUPS):
        upd = gscore[g] > top
        best = jnp.where(upd, g, best)
        top = jnp.where(upd, gscore[g], top)
    sv, av = [], []
    for j in range(EXPERTS_PER_GROUP):
        s_j, a_j = rows[j], arow[j]
        for g in range(1, N_GROUPS):
            s_j = jnp.where(best == g, rows[4 * g + j], s_j)
            a_j = jnp.where(best == g, arow[4 * g + j], a_j)
        sv.append(s_j)
        av.append(a_j)
    picked = []
    for j in range(EXPERTS_PER_GROUP):
        rank = jnp.zeros_like(best)
        for i in range(EXPERTS_PER_GROUP):
            if i == j:
                continue
            ahead = (sv[i] >= sv[j]) if i < j else (sv[i] > sv[j])
            rank = rank + ahead.astype(jnp.int32)
        picked.append(rank < TOP_K)
    first = jnp.full_like(best, EXPERTS_PER_GROUP - 1)
    last = jnp.zeros_like(best)
    for j in range(EXPERTS_PER_GROUP - 1, -1, -1):
        first = jnp.where(picked[j], j, first)
    for j in range(EXPERTS_PER_GROUP):
        last = jnp.where(picked[j], j, last)
    a_first, a_last = av[0], av[0]
    for j in range(1, EXPERTS_PER_GROUP):
        a_first = jnp.where(first == j, av[j], a_first)
        a_last = jnp.where(last == j, av[j], a_last)
    tot = a_first + a_last
    idx = jnp.concatenate([best * EXPERTS_PER_GROUP + first, best * EXPERTS_PER_GROUP + last], axis=0)
    wts = jnp.concatenate([a_first / tot, a_last / tot], axis=0)
    return idx, wts


def _outproj_kernel(*refs, n_h, nbl):
    h_refs, (ret_ref, ssd_ref, na_ref, wo_ref, mod_ref, nw_ref, rwt_ref, rb_ref,
             hn_ref, f_ref, idx_ref, wts_ref) = refs[:n_h], refs[n_h:]
    rw, sw = ret_ref.shape[2], ssd_ref.shape[2]
    acc = jnp.dot(ret_ref[0], wo_ref[0:rw, :], preferred_element_type=F32)
    acc += jnp.dot(ssd_ref[0], wo_ref[rw:rw + sw, :], preferred_element_type=F32)
    acc += jnp.dot(na_ref[0], wo_ref[rw + sw:, :], preferred_element_type=F32)
    hn = _read_h(h_refs, nbl) + mod_ref[0, 2:3, :] * acc
    hn_ref[0] = hn
    f = hn * lax.rsqrt(jnp.mean(hn * hn, axis=-1, keepdims=True) + EPS) * nw_ref[...]
    f = f * (1.0 + mod_ref[0, 4:5, :]) + mod_ref[0, 3:4, :]
    f_ref[0] = f.astype(BF16)
    p = _split_dot(f, rwt_ref[...]).T
    logits = p[:N_EXPERTS, :] + p[N_EXPERTS:2 * N_EXPERTS, :] + p[2 * N_EXPERTS:3 * N_EXPERTS, :]
    aff = 1.0 / (1.0 + jnp.exp(-logits))
    idx, wts = _route(aff, aff + rb_ref[...])
    idx_ref[0] = idx
    wts_ref[0] = wts


def _outproj(h, ret_o, ssd_o, na_o, w_out, mod, mod_row, norm_w, rw_t, rb, nbl):
    h_arrays, h_specs = _h_specs(h, nbl)
    bsz, ltot, d = ret_o.shape[0], ret_o.shape[1], h_arrays[0].shape[2]
    row = lambda n: pl.BlockSpec((1, TB, n), lambda b, j: (b, j, 0))
    const = lambda shape: pl.BlockSpec(shape, lambda b, j: (0,) * len(shape))
    col = lambda: pl.BlockSpec((1, TOP_K, TB), lambda b, j: (b, 0, j))
    return pl.pallas_call(
        functools.partial(_outproj_kernel, n_h=len(h_arrays), nbl=nbl),
        grid=(bsz, ltot // TB),
        in_specs=h_specs + [
            row(ret_o.shape[2]), row(ssd_o.shape[2]), row(na_o.shape[2]),
            const(w_out.shape),
            pl.BlockSpec((1, 6, d), lambda b, j: (mod_row(b, j), 0, 0)),
            const((1, d)), const(rw_t.shape), const(rb.shape),
        ],
        out_specs=[row(d), row(d), col(), col()],
        out_shape=[
            jax.ShapeDtypeStruct((bsz, ltot, d), F32),
            jax.ShapeDtypeStruct((bsz, ltot, d), BF16),
            jax.ShapeDtypeStruct((bsz, TOP_K, ltot), jnp.int32),
            jax.ShapeDtypeStruct((bsz, TOP_K, ltot), F32),
        ],
        compiler_params=_cparams("parallel", "parallel"),
        name="outproj_router",
    )(*h_arrays, ret_o, ssd_o, na_o, w_out, mod, norm_w, rw_t, rb)


RUN_ALIGN = SUBLANES
RUN_BITS = 6
RUN_SMALL_BITS = 3
PERM_ROWS = TOP_K * TB + N_EXPERTS * RUN_ALIGN
GATE_LANES = 128
GAP_BITS = 6
assert TB == RUN_ALIGN << (RUN_BITS - 1) and MOE_TILE == RUN_ALIGN << GAP_BITS

U32 = jnp.uint32


def _pack_bf16_pairs(y):
    n = y.shape[1] // 2
    lo = lax.bitcast_convert_type(y[:, :n].astype(BF16).astype(F32), U32) >> 16
    hi = lax.bitcast_convert_type(y[:, n:].astype(BF16).astype(F32), U32) & jnp.uint32(0xFFFF0000)
    return lo | hi


def _unpack_bf16_pairs(u):
    lo = lax.bitcast_convert_type(u << 16, F32).astype(BF16)
    hi = lax.bitcast_convert_type(u & jnp.uint32(0xFFFF0000), F32).astype(BF16)
    return lo, hi


def _block_perm(idx_ref, loff_ref):
    tb = idx_ref.shape[2]
    na = TOP_K * tb
    e_iota = lax.broadcasted_iota(jnp.int32, (N_EXPERTS, tb), 0)
    oh = jnp.concatenate([jnp.where(e_iota == idx_ref[0, k:k + 1, :], 1.0, 0.0) for k in range(TOP_K)], axis=1)
    upper = lax.broadcasted_iota(jnp.int32, (na, na), 0) <= lax.broadcasted_iota(jnp.int32, (na, na), 1)
    cum = jnp.dot(oh.astype(BF16), jnp.where(upper, 1.0, 0.0).astype(BF16), preferred_element_type=F32)
    pos = jnp.sum(oh * (cum - 1.0 + loff_ref[0]), axis=0, keepdims=True).astype(jnp.int32)
    r = lax.broadcasted_iota(jnp.int32, (PERM_ROWS, tb), 0)
    return [r == pos[:, k * tb:(k + 1) * tb] for k in range(TOP_K)]


def _for_each_run_piece(blk, hoff_ref, loff_ref, ngr_ref, fn):
    for e in range(N_EXPERTS):
        j = blk * N_EXPERTS + e
        n, lo, ho = ngr_ref[j], loff_ref[j], hoff_ref[j]
        def pieces(bits):
            for b in bits:
                start = ((n >> (b + 1)) << (b + 1)) * RUN_ALIGN

                @pl.when(((n >> b) & 1) == 1)
                def _():
                    fn(pl.multiple_of(lo + start, RUN_ALIGN), pl.multiple_of(ho + start, RUN_ALIGN), RUN_ALIGN << b)

        pl.when((n >> RUN_SMALL_BITS) != 0)(lambda: pieces(range(RUN_BITS - 1, RUN_SMALL_BITS - 1, -1)))
        pieces(range(RUN_SMALL_BITS - 1, -1, -1))


def _for_each_gap_piece(goff_ref, ggr_ref, fn):
    for e in range(N_EXPERTS):
        n, ho = ggr_ref[e], goff_ref[e]
        for b in range(GAP_BITS - 1, -1, -1):
            start = ((n >> (b + 1)) << (b + 1)) * RUN_ALIGN

            @pl.when(((n >> b) & 1) == 1)
            def _():
                fn(pl.multiple_of(ho + start, RUN_ALIGN), RUN_ALIGN << b)


def _dispatch_kernel(hoff_ref, loffs_ref, ngr_ref, goff_ref, ggr_ref, f_ref, idx_ref, wts_ref, loff_ref, xs_ref,
                     xperm, zbuf, sems, *, nb):
    step = pl.program_id(0) * nb + pl.program_id(1)
    n_steps = pl.num_programs(0) * nb
    slot = step % 2

    def run_copies(blk, slot_, wait):
        def fn(lo, ho, rows):
            cp = pltpu.make_async_copy(xperm.at[slot_, pl.ds(lo, rows)], xs_ref.at[pl.ds(ho, rows)], sems.at[slot_])
            cp.wait() if wait else cp.start()
        _for_each_run_piece(blk, hoff_ref, loffs_ref, ngr_ref, fn)

    @pl.when(step == 0)
    def _zero_region_tails():
        zbuf[...] = jnp.zeros_like(zbuf)
        for wait in (False, True):
            def fn(ho, rows, wait=wait):
                cp = pltpu.make_async_copy(zbuf.at[pl.ds(0, rows)], xs_ref.at[pl.ds(ho, rows)], sems.at[2])
                cp.wait() if wait else cp.start()
            _for_each_gap_piece(goff_ref, ggr_ref, fn)
        tail_rows = zbuf.shape[0]

        def tail_copy(i):
            dst = xs_ref.at[pl.ds(pl.multiple_of(goff_ref[N_EXPERTS] + i * tail_rows, RUN_ALIGN), tail_rows)]
            return pltpu.make_async_copy(zbuf, dst, sems.at[2])

        lax.fori_loop(0, ggr_ref[N_EXPERTS], lambda i, c: (tail_copy(i).start(), c)[1], 0)
        lax.fori_loop(0, ggr_ref[N_EXPERTS], lambda i, c: (tail_copy(i).wait(), c)[1], 0)

    q0, q1 = _block_perm(idx_ref, loff_ref)
    q = jnp.where(jnp.logical_or(q0, q1), 1.0, 0.0).astype(BF16)
    half = f_ref.shape[2] // 2
    xperm[slot, :, :half] = _pack_bf16_pairs(jnp.dot(q, f_ref[0], preferred_element_type=F32))
    w = jnp.sum(jnp.where(q0, wts_ref[0, 0:1, :], 0.0) + jnp.where(q1, wts_ref[0, 1:2, :], 0.0),
                axis=1, keepdims=True)
    xperm[slot, :, half:] = jnp.broadcast_to(lax.bitcast_convert_type(w, U32), (PERM_ROWS, GATE_LANES))

    run_copies(step, slot, wait=False)

    @pl.when(step > 0)
    def _():
        run_copies(step - 1, 1 - slot, wait=True)

    @pl.when(step == n_steps - 1)
    def _():
        run_copies(step, slot, wait=True)


def _dispatch(f, idx, wts, loff_col, hoff, loffs, ngr, goff, ggr, p_total):
    bsz, ltot, d = f.shape
    nb = ltot // TB
    words = d // 2 + GATE_LANES
    grid_spec = pltpu.PrefetchScalarGridSpec(
        num_scalar_prefetch=5,
        grid=(bsz, nb),
        in_specs=[
            pl.BlockSpec((1, TB, d), lambda b, j, *_: (b, j, 0)),
            pl.BlockSpec((1, TOP_K, TB), lambda b, j, *_: (b, 0, j)),
            pl.BlockSpec((1, TOP_K, TB), lambda b, j, *_: (b, 0, j)),
            pl.BlockSpec((1, N_EXPERTS, 1), lambda b, j, *_: (b * nb + j, 0, 0)),
        ],
        out_specs=pl.BlockSpec(memory_space=pl.ANY),
        scratch_shapes=[pltpu.VMEM((2, PERM_ROWS, words), U32),
                        pltpu.VMEM((RUN_ALIGN << (GAP_BITS - 1), words), U32),
                        pltpu.SemaphoreType.DMA((3,))],
    )
    return pl.pallas_call(
        functools.partial(_dispatch_kernel, nb=nb),
        grid_spec=grid_spec,
        out_shape=jax.ShapeDtypeStruct((p_total, words), U32),
        compiler_params=_cparams("arbitrary", "arbitrary"),
        name="moe_dispatch",
    )(hoff, loffs, ngr, goff, ggr, f, idx, wts, loff_col)


def _moe_kernel(te_ref, nt_ref, x_ref, wg_ref, wu_ref, wd_ref, y_ref, wg_s, wu_s, wd_s):
    i = pl.program_id(0)
    fresh = jnp.logical_or(i == 0, te_ref[i] != te_ref[jnp.maximum(i - 1, 0)])

    @pl.when(fresh)
    def _():
        wg_s[...] = wg_ref[0].astype(BF16)
        wu_s[...] = wu_ref[0].astype(BF16)
        wd_s[...] = wd_ref[0].astype(BF16)

    @pl.when(i < nt_ref[0])
    def _():
        half = wg_s.shape[0] // 2
        lo, hi = _unpack_bf16_pairs(x_ref[:, :half])
        gate = lax.bitcast_convert_type(x_ref[:, half:half + 1], F32)
        g = (jnp.dot(lo, wg_s[:half, :], preferred_element_type=F32)
             + jnp.dot(hi, wg_s[half:, :], preferred_element_type=F32))
        u = (jnp.dot(lo, wu_s[:half, :], preferred_element_type=F32)
             + jnp.dot(hi, wu_s[half:, :], preferred_element_type=F32))
        he = (_silu(g) * u).astype(BF16)
        y_ref[...] = _pack_bf16_pairs(jnp.dot(he, wd_s[...], preferred_element_type=F32) * gate)

    @pl.when(i >= nt_ref[0])
    def _():
        y_ref[...] = jnp.zeros_like(y_ref)


def _moe_grouped(x_sorted, tile_expert, n_tiles_used, w_gate, w_up, w_down):
    p, words = x_sorted.shape
    _, d, de = w_gate.shape
    tm = MOE_TILE
    grid_spec = pltpu.PrefetchScalarGridSpec(
        num_scalar_prefetch=2,
        grid=(p // tm,),
        in_specs=[
            pl.BlockSpec((tm, words), lambda i, te, nt: (i, 0)),
            pl.BlockSpec((1, d, de), lambda i, te, nt: (te[i], 0, 0)),
            pl.BlockSpec((1, d, de), lambda i, te, nt: (te[i], 0, 0)),
            pl.BlockSpec((1, de, d), lambda i, te, nt: (te[i], 0, 0)),
        ],
        out_specs=pl.BlockSpec((tm, d // 2), lambda i, te, nt: (i, 0)),
        scratch_shapes=[pltpu.VMEM((d, de), BF16), pltpu.VMEM((d, de), BF16), pltpu.VMEM((de, d), BF16)],
    )
    return pl.pallas_call(
        _moe_kernel,
        grid_spec=grid_spec,
        out_shape=jax.ShapeDtypeStruct((p, d // 2), U32),
        compiler_params=_cparams("arbitrary"),
        name="moe_grouped",
    )(tile_expert, n_tiles_used, x_sorted, w_gate, w_up, w_down)


def _combine_kernel(hoff_ref, loffs_ref, ngr_ref, h_ref, idx_ref, loff_ref, mod_ref, ys_ref, o_ref,
                    yperm, sems, *, nb):
    b, j = pl.program_id(0), pl.program_id(1)
    nj = pl.num_programs(1)
    step = b * nj + j
    slot = step % 2
    blk = b * nb + j
    nxt = jnp.where(j == nj - 1, (b + 1) * nb, blk + 1)

    def run_copies(blk_, slot_, wait):
        def fn(lo, ho, rows):
            cp = pltpu.make_async_copy(ys_ref.at[pl.ds(ho, rows)], yperm.at[slot_, pl.ds(lo, rows)], sems.at[slot_])
            cp.wait() if wait else cp.start()
        _for_each_run_piece(blk_, hoff_ref, loffs_ref, ngr_ref, fn)

    @pl.when(step == 0)
    def _():
        yperm[...] = jnp.zeros_like(yperm)
        run_copies(blk, 0, wait=False)

    @pl.when(step + 1 < pl.num_programs(0) * nj)
    def _():
        run_copies(nxt, 1 - slot, wait=False)

    q0, q1 = _block_perm(idx_ref, loff_ref)
    q = jnp.where(jnp.logical_or(q0, q1), 1.0, 0.0).astype(BF16)
    run_copies(blk, slot, wait=True)
    lo, hi = _unpack_bf16_pairs(yperm[slot])
    y = jnp.concatenate([_tn_dot(q, lo), _tn_dot(q, hi)], axis=1)
    o_ref[0] = h_ref[0] + mod_ref[0, 5:6, :] * y


def _combine(h, idx, loff_col, mod, mod_row, y_sorted, hoff, loffs, ngr, n_out):
    bsz, ltot, d = h.shape
    nb = ltot // TB
    grid_spec = pltpu.PrefetchScalarGridSpec(
        num_scalar_prefetch=3,
        grid=(bsz, n_out // TB),
        in_specs=[
            pl.BlockSpec((1, TB, d), lambda b, j, *_: (b, j, 0)),
            pl.BlockSpec((1, TOP_K, TB), lambda b, j, *_: (b, 0, j)),
            pl.BlockSpec((1, N_EXPERTS, 1), lambda b, j, *_: (b * nb + j, 0, 0)),
            pl.BlockSpec((1, 6, d), lambda b, j, *_: (mod_row(b, j), 0, 0)),
            pl.BlockSpec(memory_space=pl.ANY),
        ],
        out_specs=pl.BlockSpec((1, TB, d), lambda b, j, *_: (b, j, 0)),
        scratch_shapes=[pltpu.VMEM((2, PERM_ROWS, d // 2), U32), pltpu.SemaphoreType.DMA((2,))],
    )
    return pl.pallas_call(
        functools.partial(_combine_kernel, nb=nb),
        grid_spec=grid_spec,
        out_shape=jax.ShapeDtypeStruct((bsz, n_out, d), F32),
        compiler_params=_cparams("arbitrary", "arbitrary"),
        name="moe_combine",
    )(hoff, loffs, ngr, h, idx, loff_col, mod, y_sorted)


def _moe_block(h, f, idx, wts, mod, mod_row, w_gate, w_up, w_down, first_expert, n_out):
    bsz, ltot, d = h.shape
    nb = ltot // TB
    n_blocks = bsz * nb
    mt = MOE_TILE
    experts = jnp.arange(N_EXPERTS, dtype=jnp.int32)
    cnt = jnp.sum((idx.reshape(bsz, TOP_K, nb, TB, 1) == experts).astype(jnp.int32), axis=(1, 3))
    cnt = cnt.reshape(n_blocks, N_EXPERTS)
    run = ((cnt + RUN_ALIGN - 1) // RUN_ALIGN) * RUN_ALIGN
    loffs = jnp.cumsum(run, axis=1) - run
    region = ((jnp.sum(run, axis=0) + mt - 1) // mt) * mt
    ends = jnp.cumsum(region)
    hoff = (ends - region)[None, :] + jnp.cumsum(run, axis=0) - run
    p_total = ((TOP_K * bsz * ltot + n_blocks * N_EXPERTS * (RUN_ALIGN - 1)) // mt + N_EXPERTS + 1) * mt
    tile_start = jnp.arange(p_total // mt, dtype=jnp.int32) * mt
    tile_expert = jnp.minimum(jnp.sum((tile_start[:, None] >= ends[None, :]).astype(jnp.int32), axis=1),
                              N_EXPERTS - 1)
    n_used = (ends[-1] // mt).astype(jnp.int32).reshape(1)
    flat = lambda t: t.reshape(-1).astype(jnp.int32)
    hoff, loffs_flat, ngr = flat(hoff), flat(loffs), flat(run // RUN_ALIGN)
    loff_col = loffs.astype(F32).reshape(n_blocks, N_EXPERTS, 1)
    used = jnp.sum(run, axis=0)
    goff = flat(jnp.concatenate([ends - region + used, ends[-1:]]))
    ggr = flat(jnp.concatenate([(region - used) // RUN_ALIGN, (p_total - ends[-1:]) // TB]))

    x_sorted = _dispatch(f, idx, wts, loff_col, hoff, loffs_flat, ngr, goff, ggr, p_total)
    y_sorted = _moe_grouped(x_sorted, tile_expert + first_expert, n_used, w_gate, w_up, w_down)
    return _combine(h, idx, loff_col, mod, mod_row, y_sorted, hoff, loffs_flat, ngr, n_out)


def _rope_tables(n_lat, n_ctx, n_heads):
    t = jnp.arange(n_lat)
    pos = jnp.stack([t // GRID_W, t % GRID_W], axis=-1).astype(F32)
    n_freq = HEAD_DIM // 4
    inv = 1.0 / (ROPE_BASE ** (jnp.arange(n_freq, dtype=F32) / n_freq))
    ang = pos[:, :, None] * inv
    cos, sin = jnp.cos(ang), jnp.sin(ang)
    cos_h = jnp.concatenate([cos[:, 0], cos[:, 0], cos[:, 1], cos[:, 1]], axis=-1)
    sin_h = jnp.concatenate([-sin[:, 0], sin[:, 0], -sin[:, 1], sin[:, 1]], axis=-1)
    cos_h = jnp.concatenate([cos_h, jnp.ones((n_ctx, HEAD_DIM), F32)], axis=0)
    sin_h = jnp.concatenate([sin_h, jnp.zeros((n_ctx, HEAD_DIM), F32)], axis=0)
    return jnp.tile(cos_h, (1, n_heads)), jnp.tile(sin_h, (1, n_heads))


def _pack_w_in(w_in, rw, sw, nw_):
    o = 0
    seg = {}
    for name, width in (("rq", rw), ("rk", rw), ("rv", rw), ("rg", rw), ("z", sw), ("x", sw),
                        ("b", SSD_GROUPS * SSD_STATE), ("c", SSD_GROUPS * SSD_STATE),
                        ("dt", 2 * (sw // HEAD_DIM)), ("nq", nw_), ("nk", nw_), ("nv", nw_)):
        seg[name] = w_in[:, o:o + width]
        o += width
    dt_pad = jnp.zeros((w_in.shape[0], 128 - seg["dt"].shape[1]), w_in.dtype)
    cols = [seg["rq"], seg["rk"], seg["rv"], seg["rg"],
            seg["z"], seg["x"], seg["b"], seg["c"], seg["nq"], seg["nk"], seg["nv"], seg["dt"], dt_pad]
    return jnp.concatenate(cols, axis=1).astype(BF16)


def kernel(x, c, ctx, c_ctx, w_mod, b_mod, norm_mix, norm_ffn, w_in, w_out, ret_decay_f, ret_decay_b,
           ssd_conv_w, ssd_conv_b, ssd_dt_bias_f, ssd_dt_bias_b, ssd_a_log_f, ssd_a_log_b, ssd_d, ssd_norm,
           na_q_norm, na_k_norm, na_rpb, router_w, router_b, w_gate, w_up, w_down):
    bsz, n_lat, d = x.shape
    n_ctx = ctx.shape[1]
    depth = w_mod.shape[0]
    rows = n_lat // GRID_W
    rw = ret_decay_f.shape[1] * HEAD_DIM
    sw = ssd_d.shape[1] * HEAD_DIM
    nw_ = na_rpb.shape[1] * HEAD_DIM
    dims = (rw, sw, nw_)
    assert n_lat % TB == 0 and n_ctx % TB == 0
    nbl = n_lat // TB

    n_cond = ((bsz + 1 + 7) // 8) * 8
    cond = jnp.zeros((n_cond, d), F32).at[:bsz].set(c).at[bsz].set(c_ctx)
    mod = _modulation(cond, w_mod, b_mod).reshape(depth, n_cond, 6, d)
    mod_row = lambda b, j: jnp.where(j < nbl, b, bsz)

    cos, sin = _rope_tables(n_lat, n_ctx, rw // HEAD_DIM)
    hm = jnp.asarray(np.kron(np.eye(nw_ // HEAD_DIM), np.full((HEAD_DIM, HEAD_DIM), 1.0 / HEAD_DIM)), F32)
    rw_t = jnp.pad(jnp.concatenate(_bf16_parts(router_w, 3), axis=1), ((0, 0), (0, 128 - 3 * router_w.shape[1])))
    rb = router_b.reshape(-1, 1)

    expert_w = [t.reshape((-1,) + t.shape[2:]) for t in (w_gate, w_up, w_down)]
    h = (x, ctx)
    for l in range(depth):
        last = l == depth - 1
        w_all = _pack_w_in(w_in[l], rw, sw, nw_)
        qkn = jnp.stack([jnp.tile(na_q_norm[l], nw_ // HEAD_DIM), jnp.tile(na_k_norm[l], nw_ // HEAD_DIM)])
        ret, z, xbc, dt, na = _inproj(h, mod[l], mod_row, norm_mix[l].reshape(1, d), w_all, cos, sin, qkn, hm, dims, nbl)
        ret_o, ssd_o = _scan_mixers(
            ret, jnp.concatenate([ret_decay_f[l], ret_decay_b[l]]).reshape(-1, 1, 1), z, xbc, dt,
            ssd_conv_w[l], ssd_conv_b[l], jnp.concatenate([ssd_dt_bias_f[l], ssd_dt_bias_b[l]]),
            jnp.concatenate([ssd_a_log_f[l], ssd_a_log_b[l]]), ssd_d[l], ssd_norm[l], nbl)
        na_o = _na_attention(na, _na_bias_tables(na_rpb[l], rows), nbl)
        h, f, idx, wts = _outproj(h, ret_o, ssd_o, na_o, w_out[l].astype(BF16), mod[l], mod_row,
                                  norm_ffn[l].reshape(1, d), rw_t, rb, nbl)
        h = _moe_block(h, f, idx, wts, mod[l], mod_row, *expert_w, l * N_EXPERTS,
                       n_lat if last else n_lat + n_ctx)
    return h
```

```python
import functools

import jax
import jax.numpy as jnp
import numpy as np
from jax import lax
from jax.experimental import pallas as pl
from jax.experimental.pallas import tpu as pltpu

F32 = jnp.float32
BF16 = jnp.bfloat16
HIGHEST = lax.Precision.HIGHEST

GRID_W = 64
HEAD_DIM = 64
SSD_GROUPS = 2
SSD_STATE = 64
NA_WIN_ROWS = 8
NA_WIN_COLS = 16
N_GROUPS = 4
EXPERTS_PER_GROUP = 4
N_EXPERTS = N_GROUPS * EXPERTS_PER_GROUP
TOP_K = 2
ROPE_BASE = 10000.0
EPS = 1e-6

VMEM_LIMIT_BYTES = 56 * 1024 * 1024
TB = 256
NA_QROWS = TB // GRID_W
MOE_TILE = 1024
SUBLANES = 8


def _cparams(*sem):
    return pltpu.CompilerParams(dimension_semantics=sem, vmem_limit_bytes=VMEM_LIMIT_BYTES)


def _silu(t):
    return t * (1.0 / (1.0 + jnp.exp(-t)))


def _nt_dot(a, b, **kw):
    return lax.dot_general(a, b, (((1,), (1,)), ((), ())), preferred_element_type=F32, **kw)


def _tn_dot(a, b):
    return lax.dot_general(a, b, (((0,), (0,)), ((), ())), preferred_element_type=F32)


def _bf16_parts(x, n):
    parts, rest = [], x
    for i in range(n):
        p = rest.astype(BF16)
        parts.append(p)
        if i + 1 < n:
            rest = rest - p.astype(F32)
    return parts


def _split_dot(x, sel, parts=3, left=False):
    sel = sel.astype(BF16)
    dot = (lambda p: jnp.dot(sel, p, preferred_element_type=F32)) if left else (
        lambda p: jnp.dot(p, sel, preferred_element_type=F32))
    out = None
    for p in _bf16_parts(x, parts):
        out = dot(p) if out is None else out + dot(p)
    return out


def _h_specs(h, nbl):
    if not isinstance(h, tuple):
        return (h,), [pl.BlockSpec((1, TB, h.shape[2]), lambda b, j: (b, j, 0))]
    d = h[0].shape[2]
    return h, [pl.BlockSpec((1, TB, d), lambda b, j: (b, jnp.minimum(j, nbl - 1), 0)),
               pl.BlockSpec((1, TB, d), lambda b, j: (b, jnp.maximum(j - nbl, 0), 0))]


def _read_h(h_refs, nbl):
    if len(h_refs) == 1:
        return h_refs[0][0]
    return jnp.where(pl.program_id(1) < nbl, h_refs[0][0], h_refs[1][0])


def _mod_kernel(s_ref, w_ref, b_ref, o_ref):
    s = _silu(s_ref[...])
    o_ref[0] = jnp.dot(s, w_ref[0], precision=HIGHEST, preferred_element_type=F32) + b_ref[0]


def _modulation(cond, w_mod, b_mod):
    depth, d, n = w_mod.shape
    rows = cond.shape[0]
    tn = 1024
    return pl.pallas_call(
        _mod_kernel,
        grid=(depth, n // tn),
        in_specs=[
            pl.BlockSpec((rows, d), lambda l, j: (0, 0)),
            pl.BlockSpec((1, d, tn), lambda l, j: (l, 0, j)),
            pl.BlockSpec((1, 1, tn), lambda l, j: (l, 0, j)),
        ],
        out_specs=pl.BlockSpec((1, rows, tn), lambda l, j: (l, 0, j)),
        out_shape=jax.ShapeDtypeStruct((depth, rows, n), F32),
        compiler_params=_cparams("parallel", "parallel"),
        name="modulation",
    )(cond, w_mod, b_mod.reshape(depth, 1, n))


def _inproj_kernel(*refs, dims, n_h, nbl):
    h_refs, (mod_ref, nw_ref, w_ref, cos_ref, sin_ref, qkn_ref, hm_ref,
             ret_ref, z_ref, xbc_ref, dt_ref, na_ref) = refs[:n_h], refs[n_h:]
    rw, sw, nw_ = dims
    h = _read_h(h_refs, nbl)
    a = h * lax.rsqrt(jnp.mean(h * h, axis=-1, keepdims=True) + EPS) * nw_ref[...]
    a = a * (1.0 + mod_ref[0, 1:2, :]) + mod_ref[0, 0:1, :]
    ab = a.astype(BF16)

    def proj(lo, hi):
        return jnp.dot(ab, w_ref[:, lo:hi], preferred_element_type=F32)

    cos, sin = cos_ref[...], sin_ref[...]
    nf = HEAD_DIM // 4
    first_half = (lax.broadcasted_iota(jnp.int32, cos.shape, 1) % (2 * nf)) < nf

    def rotate(t):
        return t * cos + jnp.where(first_half, pltpu.roll(t, rw - nf, 1), pltpu.roll(t, nf, 1)) * sin

    o = 0
    ret_ref[0, :, 0:rw] = rotate(proj(o, o + rw))
    ret_ref[0, :, rw:2 * rw] = rotate(proj(o + rw, o + 2 * rw)) * (HEAD_DIM ** -0.5)
    ret_ref[0, :, 2 * rw:4 * rw] = proj(o + 2 * rw, o + 4 * rw)
    o += 4 * rw
    z_ref[0] = proj(o, o + sw)
    o += sw
    xbc_w = sw + 2 * SSD_GROUPS * SSD_STATE
    xbc_ref[0] = proj(o, o + xbc_w)
    o += xbc_w
    hm = hm_ref[...]
    for i in range(2):
        t = proj(o + i * nw_, o + (i + 1) * nw_)
        ms = _split_dot(t * t, hm, parts=2)
        na_ref[0, :, i * nw_:(i + 1) * nw_] = (t * lax.rsqrt(ms + EPS) * qkn_ref[i:i + 1, :]).astype(BF16)
    na_ref[0, :, 2 * nw_:3 * nw_] = proj(o + 2 * nw_, o + 3 * nw_).astype(BF16)
    o += 3 * nw_
    dt_ref[0] = proj(o, o + 128)[:, 0:dt_ref.shape[2]]


def _inproj(h, mod, mod_row, norm_w, w_all, cos, sin, qkn, hm, dims, nbl):
    h_arrays, h_specs = _h_specs(h, nbl)
    bsz, d = h_arrays[0].shape[0], h_arrays[0].shape[2]
    ltot = cos.shape[0]
    rw, sw, nw_ = dims
    n_dt = 2 * (sw // HEAD_DIM)
    xbc_w = sw + 2 * SSD_GROUPS * SSD_STATE
    row = lambda n: pl.BlockSpec((1, TB, n), lambda b, j: (b, j, 0))
    const = lambda shape: pl.BlockSpec(shape, lambda b, j: (0,) * len(shape))
    widths = (4 * rw, sw, xbc_w, n_dt, 3 * nw_)
    dtypes = (F32, F32, F32, F32, BF16)
    return pl.pallas_call(
        functools.partial(_inproj_kernel, dims=dims, n_h=len(h_arrays), nbl=nbl),
        grid=(bsz, ltot // TB),
        in_specs=h_specs + [
            pl.BlockSpec((1, 6, d), lambda b, j: (mod_row(b, j), 0, 0)),
            const((1, d)),
            const(w_all.shape),
            pl.BlockSpec((TB, rw), lambda b, j: (j, 0)),
            pl.BlockSpec((TB, rw), lambda b, j: (j, 0)),
            const((2, nw_)),
            const((nw_, nw_)),
        ],
        out_specs=[row(n) for n in widths],
        out_shape=[jax.ShapeDtypeStruct((bsz, ltot, n), dt) for n, dt in zip(widths, dtypes)],
        compiler_params=_cparams("parallel", "parallel"),
        name="inproj",
    )(*h_arrays, mod, norm_w, w_all, cos, sin, qkn, hm)


def _scan_chunk_maps(nb, nbl):
    fwd = lambda s: jnp.where(s < nb, (nbl + s) % nb, 2 * nb - 1 - s)
    out = lambda s: jnp.where(s < nb, nb - 1, 2 * nb - 1 - s)
    return fwd, out


def _ret_phases(q_ref, k_ref, v_ref, g_ref, dec_ref, o_ref, sf_all, sf, sb, dmat, rd, rdt, *, nb, nbl):
    s = pl.program_id(1)
    c = TB
    nh = k_ref.shape[2] // HEAD_DIM
    log_f = [-jnp.exp(dec_ref[h]) for h in range(nh)]
    log_b = [-jnp.exp(dec_ref[nh + h]) for h in range(nh)]

    def _init():
        sf[...] = jnp.zeros_like(sf)
        sb[...] = jnp.zeros_like(sb)
        delta = (lax.broadcasted_iota(jnp.int32, (c, c), 0) - lax.broadcasted_iota(jnp.int32, (c, c), 1)).astype(F32)
        pos = lax.broadcasted_iota(jnp.int32, (c, HEAD_DIM), 0).astype(F32)
        pos_l = lax.broadcasted_iota(jnp.int32, (SUBLANES, c), 1).astype(F32)
        for h in range(nh):
            dmat[h] = (jnp.exp(jnp.where(delta >= 0, log_f[h] * delta, -jnp.inf))
                       + jnp.exp(jnp.where(delta <= 0, -log_b[h] * delta, -jnp.inf)))
            rd[2 * h + 0] = jnp.exp(log_f[h] * (pos + 1.0))
            rd[2 * h + 1] = jnp.exp(log_b[h] * (c - pos))
            rdt[2 * h + 0] = jnp.exp(log_f[h] * (c - 1.0 - pos_l))
            rdt[2 * h + 1] = jnp.exp(log_b[h] * pos_l)

    def _state_sweep():
        ci = (nbl + s) % nb
        k_t = k_ref[0].T
        for h in range(nh):
            sl = slice(h * HEAD_DIM, (h + 1) * HEAD_DIM)
            kw = (k_t[sl, :] * rdt[2 * h + 0, 0:1, :]).astype(BF16)
            sf_all[ci, h] = sf[h]
            sf[h] = sf[h] * jnp.exp(log_f[h] * c) + jnp.dot(kw, v_ref[0, :, sl].astype(BF16),
                                                            preferred_element_type=F32)

    def _output_sweep():
        ci = 2 * nb - 1 - s
        k_t = k_ref[0].T
        for h in range(nh):
            sl = slice(h * HEAD_DIM, (h + 1) * HEAD_DIM)
            qb, vb = q_ref[0, :, sl].astype(BF16), v_ref[0, :, sl].astype(BF16)
            p = jnp.dot(qb, k_t[sl, :].astype(BF16), preferred_element_type=F32) * dmat[h]
            y = jnp.dot(p.astype(BF16), vb, preferred_element_type=F32)
            y += jnp.dot(qb, sf_all[ci, h].astype(BF16), preferred_element_type=F32) * rd[2 * h + 0]
            y += jnp.dot(qb, sb[h].astype(BF16), preferred_element_type=F32) * rd[2 * h + 1]
            kw = (k_t[sl, :] * rdt[2 * h + 1, 0:1, :]).astype(BF16)
            sb[h] = sb[h] * jnp.exp(log_b[h] * c) + jnp.dot(kw, vb, preferred_element_type=F32)
            mu = jnp.mean(y, axis=-1, keepdims=True)
            yc = y - mu
            yn = yc * lax.rsqrt(jnp.mean(yc * yc, axis=-1, keepdims=True) + EPS)
            o_ref[0, :, sl] = (yn * _silu(g_ref[0, :, sl])).astype(o_ref.dtype)

    return _init, _state_sweep, _output_sweep


def _softplus(t):
    return jnp.maximum(t, 0.0) + jnp.log1p(jnp.exp(-jnp.abs(t)))


def _lane_group_selector(n_rows, group):
    shape = (n_rows, n_rows * group)
    lane_owner = lax.broadcasted_iota(jnp.int32, shape, 1) // group
    return jnp.where(lane_owner == lax.broadcasted_iota(jnp.int32, shape, 0), 1.0, 0.0).astype(BF16)


def _ssd_phases(z_ref, x_ref, xp_ref, xn_ref, dt_ref, cw_ref, cb_ref, dtb_ref, alog_ref, dtbt_ref, alogt_ref,
                dsk_ref, nw_ref, o_ref, sf_all, sf, sb, tri, ybuf, xc_all, *, nb, nbl):
    s = pl.program_id(1)
    c = TB
    sw = z_ref.shape[2]
    nh = sw // HEAD_DIM
    rep = nh // SSD_GROUPS
    gw = rep * HEAD_DIM
    gs = SSD_GROUPS * SSD_STATE
    ci = jnp.where(s < nb, (nbl + s) % nb, 2 * nb - 1 - s)

    def _init():
        sf[...] = jnp.zeros_like(sf)
        sb[...] = jnp.zeros_like(sb)
        li = lax.broadcasted_iota(jnp.int32, (c, c), 0)
        si = lax.broadcasted_iota(jnp.int32, (c, c), 1)
        tri[0] = jnp.where(si <= li, 1.0, 0.0).astype(BF16)
        tri[1] = jnp.where(si >= li, 1.0, 0.0).astype(BF16)

    def _sweeps():
        def conv_silu():
            x = x_ref[0]
            has_prev = jnp.where(jnp.logical_or(ci == 0, ci == nbl), 0.0, 1.0)
            has_next = jnp.where(jnp.logical_or(ci == nbl - 1, ci == nb - 1), 0.0, 1.0)
            rows = lax.broadcasted_iota(jnp.int32, x.shape, 0)
            x_dn = jnp.where(rows == 0, xp_ref[0, SUBLANES - 1:SUBLANES, :] * has_prev, pltpu.roll(x, 1, 0))
            x_up = jnp.where(rows == c - 1, xn_ref[0, 0:1, :] * has_next, pltpu.roll(x, c - 1, 0))
            return _silu(cw_ref[0:1, :] * x_dn + cw_ref[1:2, :] * x + cw_ref[2:3, :] * x_up + cb_ref[...])

        def split(xc):
            xs, bm, cm = xc[:, :sw], xc[:, sw:sw + gs], xc[:, sw + gs:]
            return xs, bm.T.astype(BF16), cm

        dtr = dt_ref[0]
        dt = _softplus(dtr + dtb_ref[...])
        a = dt * -jnp.exp(alog_ref[...])
        pre = _split_dot(a, tri[0], left=True)
        tot = pre[c - 1:c, :]
        is_fwd = lax.broadcasted_iota(jnp.int32, (1, 2 * nh), 1) < nh
        acs = jnp.where(is_fwd, pre, tot - pre + a)
        sel_head = _lane_group_selector(2 * nh, HEAD_DIM)
        w_wide = _split_dot(dt * jnp.exp(tot - acs), sel_head, parts=2)
        etot_wide = _split_dot(jnp.broadcast_to(jnp.exp(tot), (SUBLANES, 2 * nh)), sel_head, parts=2)[0:1, :]

        def _state_sweep():
            xc = conv_silu()
            xc_all[ci] = xc
            xs, bm_t, _ = split(xc)
            for g in range(SSD_GROUPS):
                lanes = slice(g * gw, (g + 1) * gw)
                xw = (xs[:, lanes] * w_wide[:, lanes]).astype(BF16)
                sf_all[ci, g] = sf[g]
                sf[g] = sf[g] * etot_wide[:, lanes] + jnp.dot(bm_t[g * SSD_STATE:(g + 1) * SSD_STATE, :], xw,
                                                              preferred_element_type=F32)

        def _output_sweep():
            xs, bm_t, cm = split(xc_all[ci])
            eye = (lax.broadcasted_iota(jnp.int32, (2 * nh, 2 * nh), 0)
                   == lax.broadcasted_iota(jnp.int32, (2 * nh, 2 * nh), 1)).astype(BF16)
            dtr_t = sum(_nt_dot(eye, p) for p in _bf16_parts(dtr, 3))
            dt_t = _softplus(dtr_t + dtbt_ref[...])
            a_t = dt_t * -jnp.exp(alogt_ref[...])
            pre_t = _split_dot(a_t, tri[1])
            is_fwd_t = lax.broadcasted_iota(jnp.int32, (2 * nh, 1), 0) < nh
            acs_t = jnp.where(is_fwd_t, pre_t, pre_t[:, c - 1:c] - pre_t + a_t)
            e_wide = _split_dot(jnp.exp(acs), sel_head, parts=2)
            li = lax.broadcasted_iota(jnp.int32, (c, c), 0)
            si = lax.broadcasted_iota(jnp.int32, (c, c), 1)
            lower, upper = li >= si, si >= li
            for g in range(SSD_GROUPS):
                lanes = slice(g * gw, (g + 1) * gw)
                lanes_b = slice(sw + g * gw, sw + (g + 1) * gw)
                cg = cm[:, g * SSD_STATE:(g + 1) * SSD_STATE].astype(BF16)
                bg_t = bm_t[g * SSD_STATE:(g + 1) * SSD_STATE, :]
                scores = jnp.dot(cg, bg_t, preferred_element_type=F32)
                for r in range(rep):
                    h = g * rep + r
                    hb = nh + h
                    col_f, col_b = acs[:, h:h + 1], acs[:, hb:hb + 1]
                    d_f = jnp.exp(jnp.where(lower, col_f - acs_t[h:h + 1, :], -jnp.inf)) * dt_t[h:h + 1, :]
                    d_b = jnp.exp(jnp.where(upper, col_b - acs_t[hb:hb + 1, :], -jnp.inf)) * dt_t[hb:hb + 1, :]
                    ybuf[:, h * HEAD_DIM:(h + 1) * HEAD_DIM] = jnp.dot(
                        (scores * (d_f + d_b)).astype(BF16), xs[:, h * HEAD_DIM:(h + 1) * HEAD_DIM].astype(BF16),
                        preferred_element_type=F32)
                ybuf[:, lanes] += (jnp.dot(cg, sf_all[ci, g].astype(BF16), preferred_element_type=F32) * e_wide[:, lanes]
                                   + jnp.dot(cg, sb[g].astype(BF16), preferred_element_type=F32) * e_wide[:, lanes_b])
                xw = (xs[:, lanes] * w_wide[:, lanes_b]).astype(BF16)
                sb[g] = sb[g] * etot_wide[:, lanes_b] + jnp.dot(bg_t, xw, preferred_element_type=F32)
            y = (ybuf[...] + dsk_ref[...] * xs) * _silu(z_ref[0])
            o_ref[0] = (y * lax.rsqrt(jnp.mean(y * y, axis=-1, keepdims=True) + EPS) * nw_ref[...]).astype(o_ref.dtype)

        return _state_sweep, _output_sweep

    return _init, _sweeps


N_RET_IN, N_SSD_IN, N_RET_SCRATCH = 5, 13, 6


def _scan_mixers_kernel(*refs, nb, nbl):
    ret_in, refs = refs[:N_RET_IN], refs[N_RET_IN:]
    ssd_in, refs = refs[:N_SSD_IN], refs[N_SSD_IN:]
    (ret_o, ssd_o), refs = refs[:2], refs[2:]
    ret_init, ret_state, ret_out = _ret_phases(*ret_in, ret_o, *refs[:N_RET_SCRATCH], nb=nb, nbl=nbl)
    ssd_init, ssd_sweeps = _ssd_phases(*ssd_in, ssd_o, *refs[N_RET_SCRATCH:], nb=nb, nbl=nbl)
    s = pl.program_id(1)
    pl.when(s == 0)(ret_init)
    pl.when(s < nb)(ret_state)
    pl.when(s >= nb)(ret_out)
    pl.when(s == 0)(ssd_init)
    ssd_state, ssd_out = ssd_sweeps()
    pl.when(s < nb)(ssd_state)
    pl.when(s >= nb)(ssd_out)


def _scan_mixers(ret, decay, z, xbc, dt, conv_w, conv_b, dtb, alog, d_skip, norm_w, nbl):
    bsz, ltot, w4 = ret.shape
    w = w4 // 4
    nh_r = w // HEAD_DIM
    sw = z.shape[2]
    nh = sw // HEAD_DIM
    xw = xbc.shape[2]
    nb = ltot // TB
    per = TB // SUBLANES
    fwd, out = _scan_chunk_maps(nb, nbl)
    conv_in = lambda s: fwd(jnp.minimum(s, nb - 1))
    const = lambda t: pl.BlockSpec(t.shape, lambda b, s: (0,) * t.ndim)
    args = (conv_w, conv_b.reshape(1, xw), dtb.reshape(1, 2 * nh), alog.reshape(1, 2 * nh),
            dtb.reshape(2 * nh, 1), alog.reshape(2 * nh, 1), jnp.repeat(d_skip, HEAD_DIM).reshape(1, sw),
            norm_w.reshape(1, sw))
    ret_specs = [
        pl.BlockSpec((1, TB, w), lambda b, s: (b, out(s), 0)),
        pl.BlockSpec((1, TB, w), lambda b, s: (b, fwd(s), 1)),
        pl.BlockSpec((1, TB, w), lambda b, s: (b, fwd(s), 2)),
        pl.BlockSpec((1, TB, w), lambda b, s: (b, out(s), 3)),
        const(decay),
    ]
    ssd_specs = [
        pl.BlockSpec((1, TB, sw), lambda b, s: (b, out(s), 0)),
        pl.BlockSpec((1, TB, xw), lambda b, s: (b, conv_in(s), 0)),
        pl.BlockSpec((1, SUBLANES, xw), lambda b, s: (b, jnp.maximum(conv_in(s) * per - 1, 0), 0)),
        pl.BlockSpec((1, SUBLANES, xw), lambda b, s: (b, jnp.minimum((conv_in(s) + 1) * per, nb * per - 1), 0)),
        pl.BlockSpec((1, TB, 2 * nh), lambda b, s: (b, fwd(s), 0)),
    ] + [const(t) for t in args]
    assert len(ret_specs) == N_RET_IN and len(ssd_specs) == N_SSD_IN
    ret_scratch = [
        pltpu.VMEM((nb, nh_r, HEAD_DIM, HEAD_DIM), F32),
        pltpu.VMEM((nh_r, HEAD_DIM, HEAD_DIM), F32),
        pltpu.VMEM((nh_r, HEAD_DIM, HEAD_DIM), F32),
        pltpu.VMEM((nh_r, TB, TB), F32),
        pltpu.VMEM((2 * nh_r, TB, HEAD_DIM), F32),
        pltpu.VMEM((2 * nh_r, SUBLANES, TB), F32),
    ]
    ssd_scratch = [
        pltpu.VMEM((nb, SSD_GROUPS, SSD_STATE, sw // SSD_GROUPS), F32),
        pltpu.VMEM((SSD_GROUPS, SSD_STATE, sw // SSD_GROUPS), F32),
        pltpu.VMEM((SSD_GROUPS, SSD_STATE, sw // SSD_GROUPS), F32),
        pltpu.VMEM((2, TB, TB), BF16),
        pltpu.VMEM((TB, sw), F32),
        pltpu.VMEM((nb, TB, xw), F32),
    ]
    assert len(ret_scratch) == N_RET_SCRATCH
    return pl.pallas_call(
        functools.partial(_scan_mixers_kernel, nb=nb, nbl=nbl),
        grid=(bsz, 2 * nb),
        in_specs=ret_specs + ssd_specs,
        out_specs=[pl.BlockSpec((1, TB, w), lambda b, s: (b, out(s), 0)),
                   pl.BlockSpec((1, TB, sw), lambda b, s: (b, out(s), 0))],
        out_shape=[jax.ShapeDtypeStruct((bsz, ltot, w), BF16), jax.ShapeDtypeStruct((bsz, ltot, sw), BF16)],
        scratch_shapes=ret_scratch + ssd_scratch,
        compiler_params=_cparams("parallel", "arbitrary"),
        name="scan_mixers",
    )(ret, ret, ret, ret, decay, z, xbc, xbc, xbc, dt, *args)


def _softmax_pv(s_parts, v):
    m = s_parts[0].max(axis=-1, keepdims=True)
    for s in s_parts[1:]:
        m = jnp.maximum(m, s.max(axis=-1, keepdims=True))
    p_parts = [jnp.exp(s - m) for s in s_parts]
    den = p_parts[0].sum(axis=-1, keepdims=True)
    for p in p_parts[1:]:
        den = den + p.sum(axis=-1, keepdims=True)
    p = p_parts[0] if len(p_parts) == 1 else jnp.concatenate(p_parts, axis=1)
    return jnp.dot(p.astype(BF16), v, preferred_element_type=F32) / den


def _na_kernel(q_ref, k0_ref, k1_ref, k2_ref, v0_ref, v1_ref, v2_ref, kc_ref, vc_ref, bias_ref, o_ref, *, nbl):
    nh = q_ref.shape[2] // HEAD_DIM
    n_lat_keys = 3 * k0_ref.shape[1]
    j = pl.program_id(1)

    @pl.when(j < nbl)
    def _latent():
        for h in range(nh):
            sl = slice(h * HEAD_DIM, (h + 1) * HEAD_DIM)
            q = q_ref[0, :, sl] * (HEAD_DIM ** -0.5)
            k = jnp.concatenate([k0_ref[0, :, sl], k1_ref[0, :, sl], k2_ref[0, :, sl], kc_ref[0, :, sl]], axis=0)
            v = jnp.concatenate([v0_ref[0, :, sl], v1_ref[0, :, sl], v2_ref[0, :, sl], vc_ref[0, :, sl]], axis=0)
            s = _nt_dot(q, k)
            o = _softmax_pv([s[:, :n_lat_keys] + bias_ref[0, h], s[:, n_lat_keys:]], v)
            o_ref[0, :, sl] = o.astype(o_ref.dtype)

    @pl.when(j >= nbl)
    def _context():
        for h in range(nh):
            sl = slice(h * HEAD_DIM, (h + 1) * HEAD_DIM)
            s = _nt_dot(q_ref[0, :, sl] * (HEAD_DIM ** -0.5), kc_ref[0, :, sl])
            o_ref[0, :, sl] = _softmax_pv([s], vc_ref[0, :, sl]).astype(o_ref.dtype)


def _na_bias_tables(rpb, rows):
    qb, kb = NA_QROWS, 3 * NA_QROWS
    nb = rows // qb
    nh = rpb.shape[0]
    n_dr, n_dc = 2 * NA_WIN_ROWS - 1, 2 * NA_WIN_COLS - 1
    c = np.arange(GRID_W)[:, None]
    kc = np.arange(GRID_W)[None, :]
    cs = np.clip(c - NA_WIN_COLS // 2, 0, GRID_W - NA_WIN_COLS)
    col_ok = (kc >= cs) & (kc < cs + NA_WIN_COLS)
    dc = np.clip(kc - c + NA_WIN_COLS - 1, 0, n_dc - 1)
    col_sel = np.eye(n_dc, dtype=np.float32)[dc.reshape(-1)].T
    tabs = []
    for rbq, wb in ((0, 0), (1, 0), (nb - 1, nb - 3)):
        r = qb * rbq + np.arange(qb)[:, None]
        kr = qb * wb + np.arange(kb)[None, :]
        r0 = np.clip(r - NA_WIN_ROWS // 2, 0, rows - NA_WIN_ROWS)
        row_ok = (kr >= r0) & (kr < r0 + NA_WIN_ROWS)
        dr = np.clip(kr - r + NA_WIN_ROWS - 1, 0, n_dr - 1)
        row_sel = np.eye(n_dr, dtype=np.float32)[dr.reshape(-1)]
        t = jnp.einsum("ad,hde,ef->haf", row_sel, rpb.astype(F32), col_sel, precision=HIGHEST)
        t = t.reshape(nh, qb, kb, GRID_W, GRID_W).transpose(0, 1, 3, 2, 4).reshape(nh, qb * GRID_W, kb * GRID_W)
        valid = (row_ok[:, None, :, None] & col_ok[None, :, None, :]).reshape(qb * GRID_W, kb * GRID_W)
        tabs.append(jnp.where(valid, t, -jnp.inf))
    return jnp.stack(tabs)


def _na_attention(na, bias, nbl):
    bsz, ltot, w3 = na.shape
    w = w3 // 3
    nb = ltot // TB
    assert nbl >= 3 and nb == nbl + 1, "needs >= 12 grid rows and a context of one token block"
    win = lambda j: jnp.clip(j - 1, 0, nbl - 3)
    kv = lambda i, col: pl.BlockSpec((1, TB, w), lambda b, j: (b, win(j) + i, col))
    typ = lambda j: jnp.where(j == 0, 0, jnp.where(j >= nbl - 1, 2, 1))
    return pl.pallas_call(
        functools.partial(_na_kernel, nbl=nbl),
        grid=(bsz, nb),
        in_specs=[
            pl.BlockSpec((1, TB, w), lambda b, j: (b, j, 0)),
            kv(0, 1), kv(1, 1), kv(2, 1), kv(0, 2), kv(1, 2), kv(2, 2),
            pl.BlockSpec((1, TB, w), lambda b, j: (b, nbl, 1)),
            pl.BlockSpec((1, TB, w), lambda b, j: (b, nbl, 2)),
            pl.BlockSpec((1,) + bias.shape[1:], lambda b, j: (typ(j), 0, 0, 0)),
        ],
        out_specs=pl.BlockSpec((1, TB, w), lambda b, j: (b, j, 0)),
        out_shape=jax.ShapeDtypeStruct((bsz, ltot, w), BF16),
        compiler_params=_cparams("parallel", "arbitrary"),
        name="na_attention",
    )(na, na, na, na, na, na, na, na, na, bias)


def _route(aff, sel):
    rows = [sel[e:e + 1, :] for e in range(N_EXPERTS)]
    arow = [aff[e:e + 1, :] for e in range(N_EXPERTS)]
    gscore = []
    for g in range(N_GROUPS):
        a, b, c, d = rows[4 * g:4 * g + 4]
        hi1, lo1 = jnp.maximum(a, b), jnp.minimum(a, b)
        hi2, lo2 = jnp.maximum(c, d), jnp.minimum(c, d)
        gscore.append(jnp.maximum(hi1, hi2) + jnp.maximum(jnp.minimum(hi1, hi2), jnp.maximum(lo1, lo2)))
    best = jnp.zeros_like(gscore[0], dtype=jnp.int32)
    top = gscore[0]
    for g in range(1, N_GROUPS):
        upd = gscore[g] > top
        best = jnp.where(upd, g, best)
        top = jnp.where(upd, gscore[g], top)
    sv, av = [], []
    for j in range(EXPERTS_PER_GROUP):
        s_j, a_j = rows[j], arow[j]
        for g in range(1, N_GROUPS):
            s_j = jnp.where(best == g, rows[4 * g + j], s_j)
            a_j = jnp.where(best == g, arow[4 * g + j], a_j)
        sv.append(s_j)
        av.append(a_j)
    picked = []
    for j in range(EXPERTS_PER_GROUP):
        rank = jnp.zeros_like(best)
        for i in range(EXPERTS_PER_GROUP):
            if i == j:
                continue
            ahead = (sv[i] >= sv[j]) if i < j else (sv[i] > sv[j])
            rank = rank + ahead.astype(jnp.int32)
        picked.append(rank < TOP_K)
    first = jnp.full_like(best, EXPERTS_PER_GROUP - 1)
    last = jnp.zeros_like(best)
    for j in range(EXPERTS_PER_GROUP - 1, -1, -1):
        first = jnp.where(picked[j], j, first)
    for j in range(EXPERTS_PER_GROUP):
        last = jnp.where(picked[j], j, last)
    a_first, a_last = av[0], av[0]
    for j in range(1, EXPERTS_PER_GROUP):
        a_first = jnp.where(first == j, av[j], a_first)
        a_last = jnp.where(last == j, av[j], a_last)
    tot = a_first + a_last
    idx = jnp.concatenate([best * EXPERTS_PER_GROUP + first, best * EXPERTS_PER_GROUP + last], axis=0)
    wts = jnp.concatenate([a_first / tot, a_last / tot], axis=0)
    return idx, wts


def _outproj_kernel(*refs, n_h, nbl):
    h_refs, (ret_ref, ssd_ref, na_ref, wo_ref, mod_ref, nw_ref, rwt_ref, rb_ref,
             hn_ref, f_ref, idx_ref, wts_ref) = refs[:n_h], refs[n_h:]
    rw, sw = ret_ref.shape[2], ssd_ref.shape[2]
    acc = jnp.dot(ret_ref[0], wo_ref[0:rw, :], preferred_element_type=F32)
    acc += jnp.dot(ssd_ref[0], wo_ref[rw:rw + sw, :], preferred_element_type=F32)
    acc += jnp.dot(na_ref[0], wo_ref[rw + sw:, :], preferred_element_type=F32)
    hn = _read_h(h_refs, nbl) + mod_ref[0, 2:3, :] * acc
    hn_ref[0] = hn
    f = hn * lax.rsqrt(jnp.mean(hn * hn, axis=-1, keepdims=True) + EPS) * nw_ref[...]
    f = f * (1.0 + mod_ref[0, 4:5, :]) + mod_ref[0, 3:4, :]
    f_ref[0] = f.astype(BF16)
    p = _split_dot(f, rwt_ref[...]).T
    logits = p[:N_EXPERTS, :] + p[N_EXPERTS:2 * N_EXPERTS, :] + p[2 * N_EXPERTS:3 * N_EXPERTS, :]
    aff = 1.0 / (1.0 + jnp.exp(-logits))
    idx, wts = _route(aff, aff + rb_ref[...])
    idx_ref[0] = idx
    wts_ref[0] = wts


def _outproj(h, ret_o, ssd_o, na_o, w_out, mod, mod_row, norm_w, rw_t, rb, nbl):
    h_arrays, h_specs = _h_specs(h, nbl)
    bsz, ltot, d = ret_o.shape[0], ret_o.shape[1], h_arrays[0].shape[2]
    row = lambda n: pl.BlockSpec((1, TB, n), lambda b, j: (b, j, 0))
    const = lambda shape: pl.BlockSpec(shape, lambda b, j: (0,) * len(shape))
    col = lambda: pl.BlockSpec((1, TOP_K, TB), lambda b, j: (b, 0, j))
    return pl.pallas_call(
        functools.partial(_outproj_kernel, n_h=len(h_arrays), nbl=nbl),
        grid=(bsz, ltot // TB),
        in_specs=h_specs + [
            row(ret_o.shape[2]), row(ssd_o.shape[2]), row(na_o.shape[2]),
            const(w_out.shape),
            pl.BlockSpec((1, 6, d), lambda b, j: (mod_row(b, j), 0, 0)),
            const((1, d)), const(rw_t.shape), const(rb.shape),
        ],
        out_specs=[row(d), row(d), col(), col()],
        out_shape=[
            jax.ShapeDtypeStruct((bsz, ltot, d), F32),
            jax.ShapeDtypeStruct((bsz, ltot, d), BF16),
            jax.ShapeDtypeStruct((bsz, TOP_K, ltot), jnp.int32),
            jax.ShapeDtypeStruct((bsz, TOP_K, ltot), F32),
        ],
        compiler_params=_cparams("parallel", "parallel"),
        name="outproj_router",
    )(*h_arrays, ret_o, ssd_o, na_o, w_out, mod, norm_w, rw_t, rb)


RUN_ALIGN = SUBLANES
RUN_BITS = 6
PERM_ROWS = TOP_K * TB + N_EXPERTS * RUN_ALIGN
GATE_LANES = 128
GAP_BITS = 7
ZERO_ROWS = RUN_ALIGN << (GAP_BITS - 1)
assert TB == RUN_ALIGN << (RUN_BITS - 1) and MOE_TILE == RUN_ALIGN << GAP_BITS

U32 = jnp.uint32


def _pack_bf16_pairs(y):
    n = y.shape[1] // 2
    lo = lax.bitcast_convert_type(y[:, :n].astype(BF16).astype(F32), U32) >> 16
    hi = lax.bitcast_convert_type(y[:, n:].astype(BF16).astype(F32), U32) & jnp.uint32(0xFFFF0000)
    return lo | hi


def _unpack_bf16_pairs(u):
    lo = lax.bitcast_convert_type(u << 16, F32).astype(BF16)
    hi = lax.bitcast_convert_type(u & jnp.uint32(0xFFFF0000), F32).astype(BF16)
    return lo, hi


def _block_perm(idx_ref, loff_ref):
    tb = idx_ref.shape[2]
    na = TOP_K * tb
    e_iota = lax.broadcasted_iota(jnp.int32, (N_EXPERTS, tb), 0)
    oh = jnp.concatenate([jnp.where(e_iota == idx_ref[0, k:k + 1, :], 1.0, 0.0) for k in range(TOP_K)], axis=1)
    upper = lax.broadcasted_iota(jnp.int32, (na, na), 0) <= lax.broadcasted_iota(jnp.int32, (na, na), 1)
    cum = jnp.dot(oh.astype(BF16), jnp.where(upper, 1.0, 0.0).astype(BF16), preferred_element_type=F32)
    pos = jnp.sum(oh * (cum - 1.0 + loff_ref[0]), axis=0, keepdims=True).astype(jnp.int32)
    r = lax.broadcasted_iota(jnp.int32, (PERM_ROWS, tb), 0)
    return [r == pos[:, k * tb:(k + 1) * tb] for k in range(TOP_K)]


def _for_each_run_piece(blk, hoff_ref, loff_ref, ngr_ref, fn):
    for e in range(N_EXPERTS):
        j = blk * N_EXPERTS + e
        n, lo, ho = ngr_ref[j], loff_ref[j], hoff_ref[j]
        for b in range(RUN_BITS - 1, -1, -1):
            start = ((n >> (b + 1)) << (b + 1)) * RUN_ALIGN

            @pl.when(((n >> b) & 1) == 1)
            def _():
                fn(pl.multiple_of(lo + start, RUN_ALIGN), pl.multiple_of(ho + start, RUN_ALIGN), RUN_ALIGN << b)


def _for_each_gap_piece(goff_ref, ggr_ref, fn):
    for e in range(N_EXPERTS):
        n, ho = ggr_ref[e], goff_ref[e]
        for b in range(GAP_BITS - 1, -1, -1):
            start = ((n >> (b + 1)) << (b + 1)) * RUN_ALIGN

            @pl.when(((n >> b) & 1) == 1)
            def _():
                fn(pl.multiple_of(ho + start, RUN_ALIGN), RUN_ALIGN << b)


def _dispatch_kernel(hoff_ref, loffs_ref, ngr_ref, goff_ref, ggr_ref, f_ref, idx_ref, wts_ref, loff_ref, xs_ref,
                     xperm, zbuf, sems, *, nb):
    step = pl.program_id(0) * nb + pl.program_id(1)
    n_steps = pl.num_programs(0) * nb
    slot = step % 2

    def run_copies(blk, slot_, wait):
        def fn(lo, ho, rows):
            cp = pltpu.make_async_copy(xperm.at[slot_, pl.ds(lo, rows)], xs_ref.at[pl.ds(ho, rows)], sems.at[slot_])
            cp.wait() if wait else cp.start()
        _for_each_run_piece(blk, hoff_ref, loffs_ref, ngr_ref, fn)

    @pl.when(step == 0)
    def _zero_region_tails():
        zbuf[...] = jnp.zeros_like(zbuf)
        for wait in (False, True):
            def fn(ho, rows, wait=wait):
                cp = pltpu.make_async_copy(zbuf.at[pl.ds(0, rows)], xs_ref.at[pl.ds(ho, rows)], sems.at[2])
                cp.wait() if wait else cp.start()
            _for_each_gap_piece(goff_ref, ggr_ref, fn)
        tail_rows = zbuf.shape[0]

        def tail_copy(i):
            dst = xs_ref.at[pl.ds(pl.multiple_of(goff_ref[N_EXPERTS] + i * tail_rows, RUN_ALIGN), tail_rows)]
            return pltpu.make_async_copy(zbuf, dst, sems.at[2])

        lax.fori_loop(0, ggr_ref[N_EXPERTS], lambda i, c: (tail_copy(i).start(), c)[1], 0)
        lax.fori_loop(0, ggr_ref[N_EXPERTS], lambda i, c: (tail_copy(i).wait(), c)[1], 0)

    q0, q1 = _block_perm(idx_ref, loff_ref)
    q = jnp.where(jnp.logical_or(q0, q1), 1.0, 0.0).astype(BF16)
    half = f_ref.shape[2] // 2
    xperm[slot, :, :half] = _pack_bf16_pairs(jnp.dot(q, f_ref[0], preferred_element_type=F32))
    w = jnp.sum(jnp.where(q0, wts_ref[0, 0:1, :], 0.0) + jnp.where(q1, wts_ref[0, 1:2, :], 0.0),
                axis=1, keepdims=True)
    xperm[slot, :, half:] = jnp.broadcast_to(lax.bitcast_convert_type(w, U32), (PERM_ROWS, GATE_LANES))

    run_copies(step, slot, wait=False)

    @pl.when(step > 0)
    def _():
        run_copies(step - 1, 1 - slot, wait=True)

    @pl.when(step == n_steps - 1)
    def _():
        run_copies(step, slot, wait=True)


def _dispatch(f, idx, wts, loff_col, hoff, loffs, ngr, goff, ggr, p_total):
    bsz, ltot, d = f.shape
    nb = ltot // TB
    words = d // 2 + GATE_LANES
    grid_spec = pltpu.PrefetchScalarGridSpec(
        num_scalar_prefetch=5,
        grid=(bsz, nb),
        in_specs=[
            pl.BlockSpec((1, TB, d), lambda b, j, *_: (b, j, 0)),
            pl.BlockSpec((1, TOP_K, TB), lambda b, j, *_: (b, 0, j)),
            pl.BlockSpec((1, TOP_K, TB), lambda b, j, *_: (b, 0, j)),
            pl.BlockSpec((1, N_EXPERTS, 1), lambda b, j, *_: (b * nb + j, 0, 0)),
        ],
        out_specs=pl.BlockSpec(memory_space=pl.ANY),
        scratch_shapes=[pltpu.VMEM((2, PERM_ROWS, words), U32),
                        pltpu.VMEM((ZERO_ROWS, words), U32),
                        pltpu.SemaphoreType.DMA((3,))],
    )
    return pl.pallas_call(
        functools.partial(_dispatch_kernel, nb=nb),
        grid_spec=grid_spec,
        out_shape=jax.ShapeDtypeStruct((p_total, words), U32),
        compiler_params=_cparams("arbitrary", "arbitrary"),
        name="moe_dispatch",
    )(hoff, loffs, ngr, goff, ggr, f, idx, wts, loff_col)


def _moe_kernel(te_ref, nt_ref, x_ref, wg_ref, wu_ref, wd_ref, y_ref, wg_s, wu_s, wd_s):
    i = pl.program_id(0)
    fresh = jnp.logical_or(i == 0, te_ref[i] != te_ref[jnp.maximum(i - 1, 0)])

    @pl.when(fresh)
    def _():
        wg_s[...] = wg_ref[0].astype(BF16)
        wu_s[...] = wu_ref[0].astype(BF16)
        wd_s[...] = wd_ref[0].astype(BF16)

    @pl.when(i < nt_ref[0])
    def _():
        half = wg_s.shape[0] // 2
        lo, hi = _unpack_bf16_pairs(x_ref[:, :half])
        gate = lax.bitcast_convert_type(x_ref[:, half:half + 1], F32)
        g = (jnp.dot(lo, wg_s[:half, :], preferred_element_type=F32)
             + jnp.dot(hi, wg_s[half:, :], preferred_element_type=F32))
        u = (jnp.dot(lo, wu_s[:half, :], preferred_element_type=F32)
             + jnp.dot(hi, wu_s[half:, :], preferred_element_type=F32))
        he = (_silu(g) * u).astype(BF16)
        y_ref[...] = _pack_bf16_pairs(jnp.dot(he, wd_s[...], preferred_element_type=F32) * gate)

    @pl.when(i >= nt_ref[0])
    def _():
        y_ref[...] = jnp.zeros_like(y_ref)


def _moe_grouped(x_sorted, tile_expert, n_tiles_used, w_gate, w_up, w_down):
    p, words = x_sorted.shape
    _, d, de = w_gate.shape
    tm = MOE_TILE
    grid_spec = pltpu.PrefetchScalarGridSpec(
        num_scalar_prefetch=2,
        grid=(p // tm,),
        in_specs=[
            pl.BlockSpec((tm, words), lambda i, te, nt: (i, 0)),
            pl.BlockSpec((1, d, de), lambda i, te, nt: (te[i], 0, 0)),
            pl.BlockSpec((1, d, de), lambda i, te, nt: (te[i], 0, 0)),
            pl.BlockSpec((1, de, d), lambda i, te, nt: (te[i], 0, 0)),
        ],
        out_specs=pl.BlockSpec((tm, d // 2), lambda i, te, nt: (i, 0)),
        scratch_shapes=[pltpu.VMEM((d, de), BF16), pltpu.VMEM((d, de), BF16), pltpu.VMEM((de, d), BF16)],
    )
    return pl.pallas_call(
        _moe_kernel,
        grid_spec=grid_spec,
        out_shape=jax.ShapeDtypeStruct((p, d // 2), U32),
        compiler_params=_cparams("arbitrary"),
        name="moe_grouped",
    )(tile_expert, n_tiles_used, x_sorted, w_gate, w_up, w_down)


def _combine_kernel(hoff_ref, loffs_ref, ngr_ref, h_ref, idx_ref, loff_ref, mod_ref, ys_ref, o_ref,
                    yperm, sems, *, nb):
    b, j = pl.program_id(0), pl.program_id(1)
    nj = pl.num_programs(1)
    step = b * nj + j
    slot = step % 2
    blk = b * nb + j
    nxt = jnp.where(j == nj - 1, (b + 1) * nb, blk + 1)

    def run_copies(blk_, slot_, wait):
        def fn(lo, ho, rows):
            cp = pltpu.make_async_copy(ys_ref.at[pl.ds(ho, rows)], yperm.at[slot_, pl.ds(lo, rows)], sems.at[slot_])
            cp.wait() if wait else cp.start()
        _for_each_run_piece(blk_, hoff_ref, loffs_ref, ngr_ref, fn)

    @pl.when(step == 0)
    def _():
        yperm[...] = jnp.zeros_like(yperm)
        run_copies(blk, 0, wait=False)

    @pl.when(step + 1 < pl.num_programs(0) * nj)
    def _():
        run_copies(nxt, 1 - slot, wait=False)

    q0, q1 = _block_perm(idx_ref, loff_ref)
    q = jnp.where(jnp.logical_or(q0, q1), 1.0, 0.0).astype(BF16)
    run_copies(blk, slot, wait=True)
    lo, hi = _unpack_bf16_pairs(yperm[slot])
    y = jnp.concatenate([_tn_dot(q, lo), _tn_dot(q, hi)], axis=1)
    o_ref[0] = h_ref[0] + mod_ref[0, 5:6, :] * y


def _combine(h, idx, loff_col, mod, mod_row, y_sorted, hoff, loffs, ngr, n_out):
    bsz, ltot, d = h.shape
    nb = ltot // TB
    grid_spec = pltpu.PrefetchScalarGridSpec(
        num_scalar_prefetch=3,
        grid=(bsz, n_out // TB),
        in_specs=[
            pl.BlockSpec((1, TB, d), lambda b, j, *_: (b, j, 0)),
            pl.BlockSpec((1, TOP_K, TB), lambda b, j, *_: (b, 0, j)),
            pl.BlockSpec((1, N_EXPERTS, 1), lambda b, j, *_: (b * nb + j, 0, 0)),
            pl.BlockSpec((1, 6, d), lambda b, j, *_: (mod_row(b, j), 0, 0)),
            pl.BlockSpec(memory_space=pl.ANY),
        ],
        out_specs=pl.BlockSpec((1, TB, d), lambda b, j, *_: (b, j, 0)),
        scratch_shapes=[pltpu.VMEM((2, PERM_ROWS, d // 2), U32), pltpu.SemaphoreType.DMA((2,))],
    )
    return pl.pallas_call(
        functools.partial(_combine_kernel, nb=nb),
        grid_spec=grid_spec,
        out_shape=jax.ShapeDtypeStruct((bsz, n_out, d), F32),
        compiler_params=_cparams("arbitrary", "arbitrary"),
        name="moe_combine",
    )(hoff, loffs, ngr, h, idx, loff_col, mod, y_sorted)


def _moe_block(h, f, idx, wts, mod, mod_row, w_gate, w_up, w_down, first_expert, n_out):
    bsz, ltot, d = h.shape
    nb = ltot // TB
    n_blocks = bsz * nb
    mt = MOE_TILE
    experts = jnp.arange(N_EXPERTS, dtype=jnp.int32)
    cnt = jnp.sum((idx.reshape(bsz, TOP_K, nb, TB, 1) == experts).astype(jnp.int32), axis=(1, 3))
    cnt = cnt.reshape(n_blocks, N_EXPERTS)
    run = ((cnt + RUN_ALIGN - 1) // RUN_ALIGN) * RUN_ALIGN
    loffs = jnp.cumsum(run, axis=1) - run
    region = ((jnp.sum(run, axis=0) + mt - 1) // mt) * mt
    ends = jnp.cumsum(region)
    hoff = (ends - region)[None, :] + jnp.cumsum(run, axis=0) - run
    p_total = ((TOP_K * bsz * ltot + n_blocks * N_EXPERTS * (RUN_ALIGN - 1)) // mt + N_EXPERTS + 1) * mt
    tile_start = jnp.arange(p_total // mt, dtype=jnp.int32) * mt
    tile_expert = jnp.minimum(jnp.sum((tile_start[:, None] >= ends[None, :]).astype(jnp.int32), axis=1),
                              N_EXPERTS - 1)
    n_used = (ends[-1] // mt).astype(jnp.int32).reshape(1)
    flat = lambda t: t.reshape(-1).astype(jnp.int32)
    hoff, loffs_flat, ngr = flat(hoff), flat(loffs), flat(run // RUN_ALIGN)
    loff_col = loffs.astype(F32).reshape(n_blocks, N_EXPERTS, 1)
    used = jnp.sum(run, axis=0)
    goff = flat(jnp.concatenate([ends - region + used, ends[-1:]]))
    ggr = flat(jnp.concatenate([(region - used) // RUN_ALIGN, (p_total - ends[-1:]) // ZERO_ROWS]))

    x_sorted = _dispatch(f, idx, wts, loff_col, hoff, loffs_flat, ngr, goff, ggr, p_total)
    y_sorted = _moe_grouped(x_sorted, tile_expert + first_expert, n_used, w_gate, w_up, w_down)
    return _combine(h, idx, loff_col, mod, mod_row, y_sorted, hoff, loffs_flat, ngr, n_out)


def _rope_tables(n_lat, n_ctx, n_heads):
    t = jnp.arange(n_lat)
    pos = jnp.stack([t // GRID_W, t % GRID_W], axis=-1).astype(F32)
    n_freq = HEAD_DIM // 4
    inv = 1.0 / (ROPE_BASE ** (jnp.arange(n_freq, dtype=F32) / n_freq))
    ang = pos[:, :, None] * inv
    cos, sin = jnp.cos(ang), jnp.sin(ang)
    cos_h = jnp.concatenate([cos[:, 0], cos[:, 0], cos[:, 1], cos[:, 1]], axis=-1)
    sin_h = jnp.concatenate([-sin[:, 0], sin[:, 0], -sin[:, 1], sin[:, 1]], axis=-1)
    cos_h = jnp.concatenate([cos_h, jnp.ones((n_ctx, HEAD_DIM), F32)], axis=0)
    sin_h = jnp.concatenate([sin_h, jnp.zeros((n_ctx, HEAD_DIM), F32)], axis=0)
    return jnp.tile(cos_h, (1, n_heads)), jnp.tile(sin_h, (1, n_heads))


def _pack_w_in(w_in, rw, sw, nw_):
    o = 0
    seg = {}
    for name, width in (("rq", rw), ("rk", rw), ("rv", rw), ("rg", rw), ("z", sw), ("x", sw),
                        ("b", SSD_GROUPS * SSD_STATE), ("c", SSD_GROUPS * SSD_STATE),
                        ("dt", 2 * (sw // HEAD_DIM)), ("nq", nw_), ("nk", nw_), ("nv", nw_)):
        seg[name] = w_in[:, o:o + width]
        o += width
    dt_pad = jnp.zeros((w_in.shape[0], 128 - seg["dt"].shape[1]), w_in.dtype)
    cols = [seg["rq"], seg["rk"], seg["rv"], seg["rg"],
            seg["z"], seg["x"], seg["b"], seg["c"], seg["nq"], seg["nk"], seg["nv"], seg["dt"], dt_pad]
    return jnp.concatenate(cols, axis=1).astype(BF16)


def kernel(x, c, ctx, c_ctx, w_mod, b_mod, norm_mix, norm_ffn, w_in, w_out, ret_decay_f, ret_decay_b,
           ssd_conv_w, ssd_conv_b, ssd_dt_bias_f, ssd_dt_bias_b, ssd_a_log_f, ssd_a_log_b, ssd_d, ssd_norm,
           na_q_norm, na_k_norm, na_rpb, router_w, router_b, w_gate, w_up, w_down):
    bsz, n_lat, d = x.shape
    n_ctx = ctx.shape[1]
    depth = w_mod.shape[0]
    rows = n_lat // GRID_W
    rw = ret_decay_f.shape[1] * HEAD_DIM
    sw = ssd_d.shape[1] * HEAD_DIM
    nw_ = na_rpb.shape[1] * HEAD_DIM
    dims = (rw, sw, nw_)
    assert n_lat % TB == 0 and n_ctx % TB == 0
    nbl = n_lat // TB

    n_cond = ((bsz + 1 + 7) // 8) * 8
    cond = jnp.zeros((n_cond, d), F32).at[:bsz].set(c).at[bsz].set(c_ctx)
    mod = _modulation(cond, w_mod, b_mod).reshape(depth, n_cond, 6, d)
    mod_row = lambda b, j: jnp.where(j < nbl, b, bsz)

    cos, sin = _rope_tables(n_lat, n_ctx, rw // HEAD_DIM)
    hm = jnp.asarray(np.kron(np.eye(nw_ // HEAD_DIM), np.full((HEAD_DIM, HEAD_DIM), 1.0 / HEAD_DIM)), F32)
    rw_t = jnp.pad(jnp.concatenate(_bf16_parts(router_w, 3), axis=1), ((0, 0), (0, 128 - 3 * router_w.shape[1])))
    rb = router_b.reshape(-1, 1)

    expert_w = [t.reshape((-1,) + t.shape[2:]) for t in (w_gate, w_up, w_down)]
    h = (x, ctx)
    for l in range(depth):
        last = l == depth - 1
        w_all = _pack_w_in(w_in[l], rw, sw, nw_)
        qkn = jnp.stack([jnp.tile(na_q_norm[l], nw_ // HEAD_DIM), jnp.tile(na_k_norm[l], nw_ // HEAD_DIM)])
        ret, z, xbc, dt, na = _inproj(h, mod[l], mod_row, norm_mix[l].reshape(1, d), w_all, cos, sin, qkn, hm, dims, nbl)
        ret_o, ssd_o = _scan_mixers(
            ret, jnp.concatenate([ret_decay_f[l], ret_decay_b[l]]).reshape(-1, 1, 1), z, xbc, dt,
            ssd_conv_w[l], ssd_conv_b[l], jnp.concatenate([ssd_dt_bias_f[l], ssd_dt_bias_b[l]]),
            jnp.concatenate([ssd_a_log_f[l], ssd_a_log_b[l]]), ssd_d[l], ssd_norm[l], nbl)
        na_o = _na_attention(na, _na_bias_tables(na_rpb[l], rows), nbl)
        h, f, idx, wts = _outproj(h, ret_o, ssd_o, na_o, w_out[l].astype(BF16), mod[l], mod_row,
                                  norm_ffn[l].reshape(1, d), rw_t, rb, nbl)
        h = _moe_block(h, f, idx, wts, mod[l], mod_row, *expert_w, l * N_EXPERTS,
                       n_lat if last else n_lat + n_ctx)
    return h
```

```python
import functools

import jax
import jax.numpy as jnp
import numpy as np
from jax import lax
from jax.experimental import pallas as pl
from jax.experimental.pallas import tpu as pltpu

F32 = jnp.float32
BF16 = jnp.bfloat16
HIGHEST = lax.Precision.HIGHEST

GRID_W = 64
HEAD_DIM = 64
SSD_GROUPS = 2
SSD_STATE = 64
NA_WIN_ROWS = 8
NA_WIN_COLS = 16
N_GROUPS = 4
EXPERTS_PER_GROUP = 4
N_EXPERTS = N_GROUPS * EXPERTS_PER_GROUP
TOP_K = 2
ROPE_BASE = 10000.0
EPS = 1e-6

VMEM_LIMIT_BYTES = 56 * 1024 * 1024
TB = 256
NA_QROWS = TB // GRID_W
MOE_TILE = 1024
SUBLANES = 8


def _cparams(*sem):
    return pltpu.CompilerParams(dimension_semantics=sem, vmem_limit_bytes=VMEM_LIMIT_BYTES)


def _silu(t):
    return t * (1.0 / (1.0 + jnp.exp(-t)))


def _nt_dot(a, b, **kw):
    return lax.dot_general(a, b, (((1,), (1,)), ((), ())), preferred_element_type=F32, **kw)


def _tn_dot(a, b):
    return lax.dot_general(a, b, (((0,), (0,)), ((), ())), preferred_element_type=F32)


def _bf16_parts(x, n):
    parts, rest = [], x
    for i in range(n):
        p = rest.astype(BF16)
        parts.append(p)
        if i + 1 < n:
            rest = rest - p.astype(F32)
    return parts


def _split_dot(x, sel, parts=3, left=False):
    sel = sel.astype(BF16)
    dot = (lambda p: jnp.dot(sel, p, preferred_element_type=F32)) if left else (
        lambda p: jnp.dot(p, sel, preferred_element_type=F32))
    out = None
    for p in _bf16_parts(x, parts):
        out = dot(p) if out is None else out + dot(p)
    return out


def _h_specs(h, nbl):
    if not isinstance(h, tuple):
        return (h,), [pl.BlockSpec((1, TB, h.shape[2]), lambda b, j: (b, j, 0))]
    d = h[0].shape[2]
    return h, [pl.BlockSpec((1, TB, d), lambda b, j: (b, jnp.minimum(j, nbl - 1), 0)),
               pl.BlockSpec((1, TB, d), lambda b, j: (b, jnp.maximum(j - nbl, 0), 0))]


def _read_h(h_refs, nbl):
    if len(h_refs) == 1:
        return h_refs[0][0]
    return jnp.where(pl.program_id(1) < nbl, h_refs[0][0], h_refs[1][0])


def _mod_kernel(s_ref, w_ref, b_ref, o_ref):
    s = _silu(s_ref[...])
    o_ref[0] = jnp.dot(s, w_ref[0], precision=HIGHEST, preferred_element_type=F32) + b_ref[0]


def _modulation(cond, w_mod, b_mod):
    depth, d, n = w_mod.shape
    rows = cond.shape[0]
    tn = 1024
    return pl.pallas_call(
        _mod_kernel,
        grid=(depth, n // tn),
        in_specs=[
            pl.BlockSpec((rows, d), lambda l, j: (0, 0)),
            pl.BlockSpec((1, d, tn), lambda l, j: (l, 0, j)),
            pl.BlockSpec((1, 1, tn), lambda l, j: (l, 0, j)),
        ],
        out_specs=pl.BlockSpec((1, rows, tn), lambda l, j: (l, 0, j)),
        out_shape=jax.ShapeDtypeStruct((depth, rows, n), F32),
        compiler_params=_cparams("parallel", "parallel"),
        name="modulation",
    )(cond, w_mod, b_mod.reshape(depth, 1, n))


def _inproj_kernel(*refs, dims, n_h, nbl):
    h_refs, (mod_ref, nw_ref, w_ref, cos_ref, sin_ref, qkn_ref, hm_ref,
             ret_ref, z_ref, xbc_ref, dt_ref, na_ref) = refs[:n_h], refs[n_h:]
    rw, sw, nw_ = dims
    h = _read_h(h_refs, nbl)
    a = h * lax.rsqrt(jnp.mean(h * h, axis=-1, keepdims=True) + EPS) * nw_ref[...]
    a = a * (1.0 + mod_ref[0, 1:2, :]) + mod_ref[0, 0:1, :]
    ab = a.astype(BF16)

    u_all = jnp.dot(ab, w_ref[...], preferred_element_type=F32)

    def proj(lo, hi):
        return u_all[:, lo:hi]

    cos, sin = cos_ref[...], sin_ref[...]
    nf = HEAD_DIM // 4
    first_half = (lax.broadcasted_iota(jnp.int32, cos.shape, 1) % (2 * nf)) < nf

    def rotate(t):
        return t * cos + jnp.where(first_half, pltpu.roll(t, rw - nf, 1), pltpu.roll(t, nf, 1)) * sin

    o = 0
    ret_ref[0, :, 0:rw] = rotate(proj(o, o + rw))
    ret_ref[0, :, rw:2 * rw] = rotate(proj(o + rw, o + 2 * rw)) * (HEAD_DIM ** -0.5)
    ret_ref[0, :, 2 * rw:4 * rw] = proj(o + 2 * rw, o + 4 * rw)
    o += 4 * rw
    z_ref[0] = proj(o, o + sw)
    o += sw
    xbc_w = sw + 2 * SSD_GROUPS * SSD_STATE
    xbc_ref[0] = proj(o, o + xbc_w)
    o += xbc_w
    hm = hm_ref[...]
    for i in range(2):
        t = proj(o + i * nw_, o + (i + 1) * nw_)
        ms = _split_dot(t * t, hm, parts=2)
        na_ref[0, :, i * nw_:(i + 1) * nw_] = (t * lax.rsqrt(ms + EPS) * qkn_ref[i:i + 1, :]).astype(BF16)
    na_ref[0, :, 2 * nw_:3 * nw_] = proj(o + 2 * nw_, o + 3 * nw_).astype(BF16)
    o += 3 * nw_
    dt_ref[0] = proj(o, o + 128)[:, 0:dt_ref.shape[2]]


def _inproj(h, mod, mod_row, norm_w, w_all, cos, sin, qkn, hm, dims, nbl):
    h_arrays, h_specs = _h_specs(h, nbl)
    bsz, d = h_arrays[0].shape[0], h_arrays[0].shape[2]
    ltot = cos.shape[0]
    rw, sw, nw_ = dims
    n_dt = 2 * (sw // HEAD_DIM)
    xbc_w = sw + 2 * SSD_GROUPS * SSD_STATE
    row = lambda n: pl.BlockSpec((1, TB, n), lambda b, j: (b, j, 0))
    const = lambda shape: pl.BlockSpec(shape, lambda b, j: (0,) * len(shape))
    widths = (4 * rw, sw, xbc_w, n_dt, 3 * nw_)
    dtypes = (F32, F32, F32, F32, BF16)
    return pl.pallas_call(
        functools.partial(_inproj_kernel, dims=dims, n_h=len(h_arrays), nbl=nbl),
        grid=(bsz, ltot // TB),
        in_specs=h_specs + [
            pl.BlockSpec((1, 6, d), lambda b, j: (mod_row(b, j), 0, 0)),
            const((1, d)),
            const(w_all.shape),
            pl.BlockSpec((TB, rw), lambda b, j: (j, 0)),
            pl.BlockSpec((TB, rw), lambda b, j: (j, 0)),
            const((2, nw_)),
            const((nw_, nw_)),
        ],
        out_specs=[row(n) for n in widths],
        out_shape=[jax.ShapeDtypeStruct((bsz, ltot, n), dt) for n, dt in zip(widths, dtypes)],
        compiler_params=_cparams("parallel", "parallel"),
        name="inproj",
    )(*h_arrays, mod, norm_w, w_all, cos, sin, qkn, hm)


def _scan_chunk_maps(nb, nbl):
    fwd = lambda s: jnp.where(s < nb, (nbl + s) % nb, 2 * nb - 1 - s)
    out = lambda s: jnp.where(s < nb, nb - 1, 2 * nb - 1 - s)
    return fwd, out


def _ret_phases(q_ref, k_ref, v_ref, g_ref, dec_ref, o_ref, sf_all, sf, sb, dmat, rd, rdt, *, nb, nbl):
    s = pl.program_id(1)
    c = TB
    nh = k_ref.shape[2] // HEAD_DIM
    log_f = [-jnp.exp(dec_ref[h]) for h in range(nh)]
    log_b = [-jnp.exp(dec_ref[nh + h]) for h in range(nh)]

    def _init():
        sf[...] = jnp.zeros_like(sf)
        sb[...] = jnp.zeros_like(sb)
        delta = (lax.broadcasted_iota(jnp.int32, (c, c), 0) - lax.broadcasted_iota(jnp.int32, (c, c), 1)).astype(F32)
        pos = lax.broadcasted_iota(jnp.int32, (c, HEAD_DIM), 0).astype(F32)
        pos_l = lax.broadcasted_iota(jnp.int32, (SUBLANES, c), 1).astype(F32)
        for h in range(nh):
            dmat[h] = (jnp.exp(jnp.where(delta >= 0, log_f[h] * delta, -jnp.inf))
                       + jnp.exp(jnp.where(delta <= 0, -log_b[h] * delta, -jnp.inf)))
            rd[2 * h + 0] = jnp.exp(log_f[h] * (pos + 1.0))
            rd[2 * h + 1] = jnp.exp(log_b[h] * (c - pos))
            rdt[2 * h + 0] = jnp.exp(log_f[h] * (c - 1.0 - pos_l))
            rdt[2 * h + 1] = jnp.exp(log_b[h] * pos_l)

    def _state_sweep():
        ci = (nbl + s) % nb
        k_t = k_ref[0].T
        for h in range(nh):
            sl = slice(h * HEAD_DIM, (h + 1) * HEAD_DIM)
            kw = (k_t[sl, :] * rdt[2 * h + 0, 0:1, :]).astype(BF16)
            sf_all[ci, h] = sf[h]
            sf[h] = sf[h] * jnp.exp(log_f[h] * c) + jnp.dot(kw, v_ref[0, :, sl].astype(BF16),
                                                            preferred_element_type=F32)

    def _output_sweep():
        ci = 2 * nb - 1 - s
        k_t = k_ref[0].T
        for h in range(nh):
            sl = slice(h * HEAD_DIM, (h + 1) * HEAD_DIM)
            qb, vb = q_ref[0, :, sl].astype(BF16), v_ref[0, :, sl].astype(BF16)
            p = jnp.dot(qb, k_t[sl, :].astype(BF16), preferred_element_type=F32) * dmat[h]
            y = jnp.dot(p.astype(BF16), vb, preferred_element_type=F32)
            y += jnp.dot(qb, sf_all[ci, h].astype(BF16), preferred_element_type=F32) * rd[2 * h + 0]
            y += jnp.dot(qb, sb[h].astype(BF16), preferred_element_type=F32) * rd[2 * h + 1]
            kw = (k_t[sl, :] * rdt[2 * h + 1, 0:1, :]).astype(BF16)
            sb[h] = sb[h] * jnp.exp(log_b[h] * c) + jnp.dot(kw, vb, preferred_element_type=F32)
            mu = jnp.mean(y, axis=-1, keepdims=True)
            yc = y - mu
            yn = yc * lax.rsqrt(jnp.mean(yc * yc, axis=-1, keepdims=True) + EPS)
            o_ref[0, :, sl] = (yn * _silu(g_ref[0, :, sl])).astype(o_ref.dtype)

    return _init, _state_sweep, _output_sweep


def _softplus(t):
    return jnp.maximum(t, 0.0) + jnp.log1p(jnp.exp(-jnp.abs(t)))


def _lane_group_selector(n_rows, group):
    shape = (n_rows, n_rows * group)
    lane_owner = lax.broadcasted_iota(jnp.int32, shape, 1) // group
    return jnp.where(lane_owner == lax.broadcasted_iota(jnp.int32, shape, 0), 1.0, 0.0).astype(BF16)


def _ssd_phases(z_ref, x_ref, xp_ref, xn_ref, dt_ref, cw_ref, cb_ref, dtb_ref, alog_ref, dtbt_ref, alogt_ref,
                dsk_ref, nw_ref, o_ref, sf_all, sf, sb, tri, ybuf, xc_all, *, nb, nbl):
    s = pl.program_id(1)
    c = TB
    sw = z_ref.shape[2]
    nh = sw // HEAD_DIM
    rep = nh // SSD_GROUPS
    gw = rep * HEAD_DIM
    gs = SSD_GROUPS * SSD_STATE
    ci = jnp.where(s < nb, (nbl + s) % nb, 2 * nb - 1 - s)

    def _init():
        sf[...] = jnp.zeros_like(sf)
        sb[...] = jnp.zeros_like(sb)
        li = lax.broadcasted_iota(jnp.int32, (c, c), 0)
        si = lax.broadcasted_iota(jnp.int32, (c, c), 1)
        tri[0] = jnp.where(si <= li, 1.0, 0.0).astype(BF16)
        tri[1] = jnp.where(si >= li, 1.0, 0.0).astype(BF16)

    def _sweeps():
        def conv_silu():
            x = x_ref[0]
            has_prev = jnp.where(jnp.logical_or(ci == 0, ci == nbl), 0.0, 1.0)
            has_next = jnp.where(jnp.logical_or(ci == nbl - 1, ci == nb - 1), 0.0, 1.0)
            rows = lax.broadcasted_iota(jnp.int32, x.shape, 0)
            x_dn = jnp.where(rows == 0, xp_ref[0, SUBLANES - 1:SUBLANES, :] * has_prev, pltpu.roll(x, 1, 0))
            x_up = jnp.where(rows == c - 1, xn_ref[0, 0:1, :] * has_next, pltpu.roll(x, c - 1, 0))
            return _silu(cw_ref[0:1, :] * x_dn + cw_ref[1:2, :] * x + cw_ref[2:3, :] * x_up + cb_ref[...])

        def split(xc):
            xs, bm, cm = xc[:, :sw], xc[:, sw:sw + gs], xc[:, sw + gs:]
            return xs, bm.T.astype(BF16), cm

        dtr = dt_ref[0]
        dt = _softplus(dtr + dtb_ref[...])
        a = dt * -jnp.exp(alog_ref[...])
        pre = _split_dot(a, tri[0], left=True)
        tot = pre[c - 1:c, :]
        is_fwd = lax.broadcasted_iota(jnp.int32, (1, 2 * nh), 1) < nh
        acs = jnp.where(is_fwd, pre, tot - pre + a)
        sel_head = _lane_group_selector(2 * nh, HEAD_DIM)
        w_wide = _split_dot(dt * jnp.exp(tot - acs), sel_head, parts=2)
        etot_wide = _split_dot(jnp.broadcast_to(jnp.exp(tot), (SUBLANES, 2 * nh)), sel_head, parts=2)[0:1, :]

        def _state_sweep():
            xc = conv_silu()
            xc_all[ci] = xc
            xs, bm_t, _ = split(xc)
            for g in range(SSD_GROUPS):
                lanes = slice(g * gw, (g + 1) * gw)
                xw = (xs[:, lanes] * w_wide[:, lanes]).astype(BF16)
                sf_all[ci, g] = sf[g]
                sf[g] = sf[g] * etot_wide[:, lanes] + jnp.dot(bm_t[g * SSD_STATE:(g + 1) * SSD_STATE, :], xw,
                                                              preferred_element_type=F32)

        def _output_sweep():
            xs, bm_t, cm = split(xc_all[ci])
            eye = (lax.broadcasted_iota(jnp.int32, (2 * nh, 2 * nh), 0)
                   == lax.broadcasted_iota(jnp.int32, (2 * nh, 2 * nh), 1)).astype(BF16)
            dtr_t = sum(_nt_dot(eye, p) for p in _bf16_parts(dtr, 3))
            dt_t = _softplus(dtr_t + dtbt_ref[...])
            a_t = dt_t * -jnp.exp(alogt_ref[...])
            pre_t = _split_dot(a_t, tri[1])
            is_fwd_t = lax.broadcasted_iota(jnp.int32, (2 * nh, 1), 0) < nh
            acs_t = jnp.where(is_fwd_t, pre_t, pre_t[:, c - 1:c] - pre_t + a_t)
            e_wide = _split_dot(jnp.exp(acs), sel_head, parts=2)
            li = lax.broadcasted_iota(jnp.int32, (c, c), 0)
            si = lax.broadcasted_iota(jnp.int32, (c, c), 1)
            lower, upper = li >= si, si >= li
            for g in range(SSD_GROUPS):
                lanes = slice(g * gw, (g + 1) * gw)
                lanes_b = slice(sw + g * gw, sw + (g + 1) * gw)
                cg = cm[:, g * SSD_STATE:(g + 1) * SSD_STATE].astype(BF16)
                bg_t = bm_t[g * SSD_STATE:(g + 1) * SSD_STATE, :]
                scores = jnp.dot(cg, bg_t, preferred_element_type=F32)
                for r in range(rep):
                    h = g * rep + r
                    hb = nh + h
                    col_f, col_b = acs[:, h:h + 1], acs[:, hb:hb + 1]
                    d_f = jnp.exp(jnp.where(lower, col_f - acs_t[h:h + 1, :], -jnp.inf)) * dt_t[h:h + 1, :]
                    d_b = jnp.exp(jnp.where(upper, col_b - acs_t[hb:hb + 1, :], -jnp.inf)) * dt_t[hb:hb + 1, :]
                    ybuf[:, h * HEAD_DIM:(h + 1) * HEAD_DIM] = jnp.dot(
                        (scores * (d_f + d_b)).astype(BF16), xs[:, h * HEAD_DIM:(h + 1) * HEAD_DIM].astype(BF16),
                        preferred_element_type=F32)
                ybuf[:, lanes] += (jnp.dot(cg, sf_all[ci, g].astype(BF16), preferred_element_type=F32) * e_wide[:, lanes]
                                   + jnp.dot(cg, sb[g].astype(BF16), preferred_element_type=F32) * e_wide[:, lanes_b])
                xw = (xs[:, lanes] * w_wide[:, lanes_b]).astype(BF16)
                sb[g] = sb[g] * etot_wide[:, lanes_b] + jnp.dot(bg_t, xw, preferred_element_type=F32)
            y = (ybuf[...] + dsk_ref[...] * xs) * _silu(z_ref[0])
            o_ref[0] = (y * lax.rsqrt(jnp.mean(y * y, axis=-1, keepdims=True) + EPS) * nw_ref[...]).astype(o_ref.dtype)

        return _state_sweep, _output_sweep

    return _init, _sweeps


N_RET_IN, N_SSD_IN, N_RET_SCRATCH = 5, 13, 6


def _scan_mixers_kernel(*refs, nb, nbl):
    ret_in, refs = refs[:N_RET_IN], refs[N_RET_IN:]
    ssd_in, refs = refs[:N_SSD_IN], refs[N_SSD_IN:]
    (ret_o, ssd_o), refs = refs[:2], refs[2:]
    ret_init, ret_state, ret_out = _ret_phases(*ret_in, ret_o, *refs[:N_RET_SCRATCH], nb=nb, nbl=nbl)
    ssd_init, ssd_sweeps = _ssd_phases(*ssd_in, ssd_o, *refs[N_RET_SCRATCH:], nb=nb, nbl=nbl)
    s = pl.program_id(1)
    pl.when(s == 0)(ret_init)
    pl.when(s < nb)(ret_state)
    pl.when(s >= nb)(ret_out)
    pl.when(s == 0)(ssd_init)
    ssd_state, ssd_out = ssd_sweeps()
    pl.when(s < nb)(ssd_state)
    pl.when(s >= nb)(ssd_out)


def _scan_mixers(ret, decay, z, xbc, dt, conv_w, conv_b, dtb, alog, d_skip, norm_w, nbl):
    bsz, ltot, w4 = ret.shape
    w = w4 // 4
    nh_r = w // HEAD_DIM
    sw = z.shape[2]
    nh = sw // HEAD_DIM
    xw = xbc.shape[2]
    nb = ltot // TB
    per = TB // SUBLANES
    fwd, out = _scan_chunk_maps(nb, nbl)
    conv_in = lambda s: fwd(jnp.minimum(s, nb - 1))
    const = lambda t: pl.BlockSpec(t.shape, lambda b, s: (0,) * t.ndim)
    args = (conv_w, conv_b.reshape(1, xw), dtb.reshape(1, 2 * nh), alog.reshape(1, 2 * nh),
            dtb.reshape(2 * nh, 1), alog.reshape(2 * nh, 1), jnp.repeat(d_skip, HEAD_DIM).reshape(1, sw),
            norm_w.reshape(1, sw))
    ret_specs = [
        pl.BlockSpec((1, TB, w), lambda b, s: (b, out(s), 0)),
        pl.BlockSpec((1, TB, w), lambda b, s: (b, fwd(s), 1)),
        pl.BlockSpec((1, TB, w), lambda b, s: (b, fwd(s), 2)),
        pl.BlockSpec((1, TB, w), lambda b, s: (b, out(s), 3)),
        const(decay),
    ]
    ssd_specs = [
        pl.BlockSpec((1, TB, sw), lambda b, s: (b, out(s), 0)),
        pl.BlockSpec((1, TB, xw), lambda b, s: (b, conv_in(s), 0)),
        pl.BlockSpec((1, SUBLANES, xw), lambda b, s: (b, jnp.maximum(conv_in(s) * per - 1, 0), 0)),
        pl.BlockSpec((1, SUBLANES, xw), lambda b, s: (b, jnp.minimum((conv_in(s) + 1) * per, nb * per - 1), 0)),
        pl.BlockSpec((1, TB, 2 * nh), lambda b, s: (b, fwd(s), 0)),
    ] + [const(t) for t in args]
    assert len(ret_specs) == N_RET_IN and len(ssd_specs) == N_SSD_IN
    ret_scratch = [
        pltpu.VMEM((nb, nh_r, HEAD_DIM, HEAD_DIM), F32),
        pltpu.VMEM((nh_r, HEAD_DIM, HEAD_DIM), F32),
        pltpu.VMEM((nh_r, HEAD_DIM, HEAD_DIM), F32),
        pltpu.VMEM((nh_r, TB, TB), F32),
        pltpu.VMEM((2 * nh_r, TB, HEAD_DIM), F32),
        pltpu.VMEM((2 * nh_r, SUBLANES, TB), F32),
    ]
    ssd_scratch = [
        pltpu.VMEM((nb, SSD_GROUPS, SSD_STATE, sw // SSD_GROUPS), F32),
        pltpu.VMEM((SSD_GROUPS, SSD_STATE, sw // SSD_GROUPS), F32),
        pltpu.VMEM((SSD_GROUPS, SSD_STATE, sw // SSD_GROUPS), F32),
        pltpu.VMEM((2, TB, TB), BF16),
        pltpu.VMEM((TB, sw), F32),
        pltpu.VMEM((nb, TB, xw), F32),
    ]
    assert len(ret_scratch) == N_RET_SCRATCH
    return pl.pallas_call(
        functools.partial(_scan_mixers_kernel, nb=nb, nbl=nbl),
        grid=(bsz, 2 * nb),
        in_specs=ret_specs + ssd_specs,
        out_specs=[pl.BlockSpec((1, TB, w), lambda b, s: (b, out(s), 0)),
                   pl.BlockSpec((1, TB, sw), lambda b, s: (b, out(s), 0))],
        out_shape=[jax.ShapeDtypeStruct((bsz, ltot, w), BF16), jax.ShapeDtypeStruct((bsz, ltot, sw), BF16)],
        scratch_shapes=ret_scratch + ssd_scratch,
        compiler_params=_cparams("parallel", "arbitrary"),
        name="scan_mixers",
    )(ret, ret, ret, ret, decay, z, xbc, xbc, xbc, dt, *args)


def _softmax_pv(s_parts, v):
    m = s_parts[0].max(axis=-1, keepdims=True)
    for s in s_parts[1:]:
        m = jnp.maximum(m, s.max(axis=-1, keepdims=True))
    p_parts = [jnp.exp(s - m) for s in s_parts]
    den = p_parts[0].sum(axis=-1, keepdims=True)
    for p in p_parts[1:]:
        den = den + p.sum(axis=-1, keepdims=True)
    p = p_parts[0] if len(p_parts) == 1 else jnp.concatenate(p_parts, axis=1)
    return jnp.dot(p.astype(BF16), v, preferred_element_type=F32) / den


def _na_kernel(q_ref, k0_ref, k1_ref, k2_ref, v0_ref, v1_ref, v2_ref, kc_ref, vc_ref, bias_ref, o_ref, *, nbl):
    nh = q_ref.shape[2] // HEAD_DIM
    n_lat_keys = 3 * k0_ref.shape[1]
    j = pl.program_id(1)

    @pl.when(j < nbl)
    def _latent():
        for h in range(nh):
            sl = slice(h * HEAD_DIM, (h + 1) * HEAD_DIM)
            q = q_ref[0, :, sl] * (HEAD_DIM ** -0.5)
            k = jnp.concatenate([k0_ref[0, :, sl], k1_ref[0, :, sl], k2_ref[0, :, sl], kc_ref[0, :, sl]], axis=0)
            v = jnp.concatenate([v0_ref[0, :, sl], v1_ref[0, :, sl], v2_ref[0, :, sl], vc_ref[0, :, sl]], axis=0)
            s = _nt_dot(q, k)
            o = _softmax_pv([s[:, :n_lat_keys] + bias_ref[0, h], s[:, n_lat_keys:]], v)
            o_ref[0, :, sl] = o.astype(o_ref.dtype)

    @pl.when(j >= nbl)
    def _context():
        for h in range(nh):
            sl = slice(h * HEAD_DIM, (h + 1) * HEAD_DIM)
            s = _nt_dot(q_ref[0, :, sl] * (HEAD_DIM ** -0.5), kc_ref[0, :, sl])
            o_ref[0, :, sl] = _softmax_pv([s], vc_ref[0, :, sl]).astype(o_ref.dtype)


def _na_bias_tables(rpb, rows):
    qb, kb = NA_QROWS, 3 * NA_QROWS
    nb = rows // qb
    nh = rpb.shape[0]
    n_dr, n_dc = 2 * NA_WIN_ROWS - 1, 2 * NA_WIN_COLS - 1
    c = np.arange(GRID_W)[:, None]
    kc = np.arange(GRID_W)[None, :]
    cs = np.clip(c - NA_WIN_COLS // 2, 0, GRID_W - NA_WIN_COLS)
    col_ok = (kc >= cs) & (kc < cs + NA_WIN_COLS)
    dc = np.clip(kc - c + NA_WIN_COLS - 1, 0, n_dc - 1)
    col_sel = np.eye(n_dc, dtype=np.float32)[dc.reshape(-1)].T
    tabs = []
    for rbq, wb in ((0, 0), (1, 0), (nb - 1, nb - 3)):
        r = qb * rbq + np.arange(qb)[:, None]
        kr = qb * wb + np.arange(kb)[None, :]
        r0 = np.clip(r - NA_WIN_ROWS // 2, 0, rows - NA_WIN_ROWS)
        row_ok = (kr >= r0) & (kr < r0 + NA_WIN_ROWS)
        dr = np.clip(kr - r + NA_WIN_ROWS - 1, 0, n_dr - 1)
        row_sel = np.eye(n_dr, dtype=np.float32)[dr.reshape(-1)]
        t = jnp.einsum("ad,hde,ef->haf", row_sel, rpb.astype(F32), col_sel, precision=HIGHEST)
        t = t.reshape(nh, qb, kb, GRID_W, GRID_W).transpose(0, 1, 3, 2, 4).reshape(nh, qb * GRID_W, kb * GRID_W)
        valid = (row_ok[:, None, :, None] & col_ok[None, :, None, :]).reshape(qb * GRID_W, kb * GRID_W)
        tabs.append(jnp.where(valid, t, -jnp.inf))
    return jnp.stack(tabs)


def _na_attention(na, bias, nbl):
    bsz, ltot, w3 = na.shape
    w = w3 // 3
    nb = ltot // TB
    assert nbl >= 3 and nb == nbl + 1, "needs >= 12 grid rows and a context of one token block"
    win = lambda j: jnp.clip(j - 1, 0, nbl - 3)
    kv = lambda i, col: pl.BlockSpec((1, TB, w), lambda b, j: (b, win(j) + i, col))
    typ = lambda j: jnp.where(j == 0, 0, jnp.where(j >= nbl - 1, 2, 1))
    return pl.pallas_call(
        functools.partial(_na_kernel, nbl=nbl),
        grid=(bsz, nb),
        in_specs=[
            pl.BlockSpec((1, TB, w), lambda b, j: (b, j, 0)),
            kv(0, 1), kv(1, 1), kv(2, 1), kv(0, 2), kv(1, 2), kv(2, 2),
            pl.BlockSpec((1, TB, w), lambda b, j: (b, nbl, 1)),
            pl.BlockSpec((1, TB, w), lambda b, j: (b, nbl, 2)),
            pl.BlockSpec((1,) + bias.shape[1:], lambda b, j: (typ(j), 0, 0, 0)),
        ],
        out_specs=pl.BlockSpec((1, TB, w), lambda b, j: (b, j, 0)),
        out_shape=jax.ShapeDtypeStruct((bsz, ltot, w), BF16),
        compiler_params=_cparams("parallel", "arbitrary"),
        name="na_attention",
    )(na, na, na, na, na, na, na, na, na, bias)


def _route(aff, sel):
    rows = [sel[e:e + 1, :] for e in range(N_EXPERTS)]
    arow = [aff[e:e + 1, :] for e in range(N_EXPERTS)]
    gscore = []
    for g in range(N_GROUPS):
        a, b, c, d = rows[4 * g:4 * g + 4]
        hi1, lo1 = jnp.maximum(a, b), jnp.minimum(a, b)
        hi2, lo2 = jnp.maximum(c, d), jnp.minimum(c, d)
        gscore.append(jnp.maximum(hi1, hi2) + jnp.maximum(jnp.minimum(hi1, hi2), jnp.maximum(lo1, lo2)))
    best = jnp.zeros_like(gscore[0], dtype=jnp.int32)
    top = gscore[0]
    for g in range(1, N_GROUPS):
        upd = gscore[g] > top
        best = jnp.where(upd, g, best)
        top = jnp.where(upd, gscore[g], top)
    sv, av = [], []
    for j in range(EXPERTS_PER_GROUP):
        s_j, a_j = rows[j], arow[j]
        for g in range(1, N_GROUPS):
            s_j = jnp.where(best == g, rows[4 * g + j], s_j)
            a_j = jnp.where(best == g, arow[4 * g + j], a_j)
        sv.append(s_j)
        av.append(a_j)
    picked = []
    for j in range(EXPERTS_PER_GROUP):
        rank = jnp.zeros_like(best)
        for i in range(EXPERTS_PER_GROUP):
            if i == j:
                continue
            ahead = (sv[i] >= sv[j]) if i < j else (sv[i] > sv[j])
            rank = rank + ahead.astype(jnp.int32)
        picked.append(rank < TOP_K)
    first = jnp.full_like(best, EXPERTS_PER_GROUP - 1)
    last = jnp.zeros_like(best)
    for j in range(EXPERTS_PER_GROUP - 1, -1, -1):
        first = jnp.where(picked[j], j, first)
    for j in range(EXPERTS_PER_GROUP):
        last = jnp.where(picked[j], j, last)
    a_first, a_last = av[0], av[0]
    for j in range(1, EXPERTS_PER_GROUP):
        a_first = jnp.where(first == j, av[j], a_first)
        a_last = jnp.where(last == j, av[j], a_last)
    tot = a_first + a_last
    idx = jnp.concatenate([best * EXPERTS_PER_GROUP + first, best * EXPERTS_PER_GROUP + last], axis=0)
    wts = jnp.concatenate([a_first / tot, a_last / tot], axis=0)
    return idx, wts


def _outproj_kernel(*refs, n_h, nbl):
    h_refs, (ret_ref, ssd_ref, na_ref, wo_ref, mod_ref, nw_ref, rwt_ref, rb_ref,
             hn_ref, f_ref, idx_ref, wts_ref) = refs[:n_h], refs[n_h:]
    rw, sw = ret_ref.shape[2], ssd_ref.shape[2]
    acc = jnp.dot(ret_ref[0], wo_ref[0:rw, :], preferred_element_type=F32)
    acc += jnp.dot(ssd_ref[0], wo_ref[rw:rw + sw, :], preferred_element_type=F32)
    acc += jnp.dot(na_ref[0], wo_ref[rw + sw:, :], preferred_element_type=F32)
    hn = _read_h(h_refs, nbl) + mod_ref[0, 2:3, :] * acc
    hn_ref[0] = hn
    f = hn * lax.rsqrt(jnp.mean(hn * hn, axis=-1, keepdims=True) + EPS) * nw_ref[...]
    f = f * (1.0 + mod_ref[0, 4:5, :]) + mod_ref[0, 3:4, :]
    f_ref[0] = f.astype(BF16)
    p = _split_dot(f, rwt_ref[...]).T
    logits = p[:N_EXPERTS, :] + p[N_EXPERTS:2 * N_EXPERTS, :] + p[2 * N_EXPERTS:3 * N_EXPERTS, :]
    aff = 1.0 / (1.0 + jnp.exp(-logits))
    idx, wts = _route(aff, aff + rb_ref[...])
    idx_ref[0] = idx
    wts_ref[0] = wts


def _outproj(h, ret_o, ssd_o, na_o, w_out, mod, mod_row, norm_w, rw_t, rb, nbl):
    h_arrays, h_specs = _h_specs(h, nbl)
    bsz, ltot, d = ret_o.shape[0], ret_o.shape[1], h_arrays[0].shape[2]
    row = lambda n: pl.BlockSpec((1, TB, n), lambda b, j: (b, j, 0))
    const = lambda shape: pl.BlockSpec(shape, lambda b, j: (0,) * len(shape))
    col = lambda: pl.BlockSpec((1, TOP_K, TB), lambda b, j: (b, 0, j))
    return pl.pallas_call(
        functools.partial(_outproj_kernel, n_h=len(h_arrays), nbl=nbl),
        grid=(bsz, ltot // TB),
        in_specs=h_specs + [
            row(ret_o.shape[2]), row(ssd_o.shape[2]), row(na_o.shape[2]),
            const(w_out.shape),
            pl.BlockSpec((1, 6, d), lambda b, j: (mod_row(b, j), 0, 0)),
            const((1, d)), const(rw_t.shape), const(rb.shape),
        ],
        out_specs=[row(d), row(d), col(), col()],
        out_shape=[
            jax.ShapeDtypeStruct((bsz, ltot, d), F32),
            jax.ShapeDtypeStruct((bsz, ltot, d), BF16),
            jax.ShapeDtypeStruct((bsz, TOP_K, ltot), jnp.int32),
            jax.ShapeDtypeStruct((bsz, TOP_K, ltot), F32),
        ],
        compiler_params=_cparams("parallel", "parallel"),
        name="outproj_router",
    )(*h_arrays, ret_o, ssd_o, na_o, w_out, mod, norm_w, rw_t, rb)


RUN_ALIGN = SUBLANES
RUN_BITS = 6
PERM_ROWS = TOP_K * TB + N_EXPERTS * RUN_ALIGN
GATE_LANES = 128
GAP_BITS = 7
ZERO_ROWS = RUN_ALIGN << (GAP_BITS - 1)
assert TB == RUN_ALIGN << (RUN_BITS - 1) and MOE_TILE == RUN_ALIGN << GAP_BITS

U32 = jnp.uint32


def _pack_bf16_pairs(y):
    n = y.shape[1] // 2
    lo = lax.bitcast_convert_type(y[:, :n].astype(BF16).astype(F32), U32) >> 16
    hi = lax.bitcast_convert_type(y[:, n:].astype(BF16).astype(F32), U32) & jnp.uint32(0xFFFF0000)
    return lo | hi


def _unpack_bf16_pairs(u):
    lo = lax.bitcast_convert_type(u << 16, F32).astype(BF16)
    hi = lax.bitcast_convert_type(u & jnp.uint32(0xFFFF0000), F32).astype(BF16)
    return lo, hi


def _block_perm(idx_ref, loff_ref):
    tb = idx_ref.shape[2]
    na = TOP_K * tb
    e_iota = lax.broadcasted_iota(jnp.int32, (N_EXPERTS, tb), 0)
    oh = jnp.concatenate([jnp.where(e_iota == idx_ref[0, k:k + 1, :], 1.0, 0.0) for k in range(TOP_K)], axis=1)
    upper = lax.broadcasted_iota(jnp.int32, (na, na), 0) <= lax.broadcasted_iota(jnp.int32, (na, na), 1)
    cum = jnp.dot(oh.astype(BF16), jnp.where(upper, 1.0, 0.0).astype(BF16), preferred_element_type=F32)
    pos = jnp.sum(oh * (cum - 1.0 + loff_ref[0]), axis=0, keepdims=True).astype(jnp.int32)
    r = lax.broadcasted_iota(jnp.int32, (PERM_ROWS, tb), 0)
    return [r == pos[:, k * tb:(k + 1) * tb] for k in range(TOP_K)]


def _for_each_run_piece(blk, hoff_ref, loff_ref, ngr_ref, fn):
    for e in range(N_EXPERTS):
        j = blk * N_EXPERTS + e
        n, lo, ho = ngr_ref[j], loff_ref[j], hoff_ref[j]
        for b in range(RUN_BITS - 1, -1, -1):
            start = ((n >> (b + 1)) << (b + 1)) * RUN_ALIGN

            @pl.when(((n >> b) & 1) == 1)
            def _():
                fn(pl.multiple_of(lo + start, RUN_ALIGN), pl.multiple_of(ho + start, RUN_ALIGN), RUN_ALIGN << b)


def _for_each_gap_piece(goff_ref, ggr_ref, fn):
    for e in range(N_EXPERTS):
        n, ho = ggr_ref[e], goff_ref[e]
        for b in range(GAP_BITS - 1, -1, -1):
            start = ((n >> (b + 1)) << (b + 1)) * RUN_ALIGN

            @pl.when(((n >> b) & 1) == 1)
            def _():
                fn(pl.multiple_of(ho + start, RUN_ALIGN), RUN_ALIGN << b)


def _dispatch_kernel(hoff_ref, loffs_ref, ngr_ref, goff_ref, ggr_ref, f_ref, idx_ref, wts_ref, loff_ref, xs_ref,
                     xperm, zbuf, sems, *, nb):
    step = pl.program_id(0) * nb + pl.program_id(1)
    n_steps = pl.num_programs(0) * nb
    slot = step % 2

    def run_copies(blk, slot_, wait):
        def fn(lo, ho, rows):
            cp = pltpu.make_async_copy(xperm.at[slot_, pl.ds(lo, rows)], xs_ref.at[pl.ds(ho, rows)], sems.at[slot_])
            cp.wait() if wait else cp.start()
        _for_each_run_piece(blk, hoff_ref, loffs_ref, ngr_ref, fn)

    @pl.when(step == 0)
    def _zero_region_tails():
        zbuf[...] = jnp.zeros_like(zbuf)
        for wait in (False, True):
            def fn(ho, rows, wait=wait):
                cp = pltpu.make_async_copy(zbuf.at[pl.ds(0, rows)], xs_ref.at[pl.ds(ho, rows)], sems.at[2])
                cp.wait() if wait else cp.start()
            _for_each_gap_piece(goff_ref, ggr_ref, fn)
        tail_rows = zbuf.shape[0]

        def tail_copy(i):
            dst = xs_ref.at[pl.ds(pl.multiple_of(goff_ref[N_EXPERTS] + i * tail_rows, RUN_ALIGN), tail_rows)]
            return pltpu.make_async_copy(zbuf, dst, sems.at[2])

        lax.fori_loop(0, ggr_ref[N_EXPERTS], lambda i, c: (tail_copy(i).start(), c)[1], 0)
        lax.fori_loop(0, ggr_ref[N_EXPERTS], lambda i, c: (tail_copy(i).wait(), c)[1], 0)

    q0, q1 = _block_perm(idx_ref, loff_ref)
    q = jnp.where(jnp.logical_or(q0, q1), 1.0, 0.0).astype(BF16)
    half = f_ref.shape[2] // 2
    xperm[slot, :, :half] = _pack_bf16_pairs(jnp.dot(q, f_ref[0], preferred_element_type=F32))
    w = jnp.sum(jnp.where(q0, wts_ref[0, 0:1, :], 0.0) + jnp.where(q1, wts_ref[0, 1:2, :], 0.0),
                axis=1, keepdims=True)
    xperm[slot, :, half:] = jnp.broadcast_to(lax.bitcast_convert_type(w, U32), (PERM_ROWS, GATE_LANES))

    run_copies(step, slot, wait=False)

    @pl.when(step > 0)
    def _():
        run_copies(step - 1, 1 - slot, wait=True)

    @pl.when(step == n_steps - 1)
    def _():
        run_copies(step, slot, wait=True)


def _dispatch(f, idx, wts, loff_col, hoff, loffs, ngr, goff, ggr, p_total):
    bsz, ltot, d = f.shape
    nb = ltot // TB
    words = d // 2 + GATE_LANES
    grid_spec = pltpu.PrefetchScalarGridSpec(
        num_scalar_prefetch=5,
        grid=(bsz, nb),
        in_specs=[
            pl.BlockSpec((1, TB, d), lambda b, j, *_: (b, j, 0)),
            pl.BlockSpec((1, TOP_K, TB), lambda b, j, *_: (b, 0, j)),
            pl.BlockSpec((1, TOP_K, TB), lambda b, j, *_: (b, 0, j)),
            pl.BlockSpec((1, N_EXPERTS, 1), lambda b, j, *_: (b * nb + j, 0, 0)),
        ],
        out_specs=pl.BlockSpec(memory_space=pl.ANY),
        scratch_shapes=[pltpu.VMEM((2, PERM_ROWS, words), U32),
                        pltpu.VMEM((ZERO_ROWS, words), U32),
                        pltpu.SemaphoreType.DMA((3,))],
    )
    return pl.pallas_call(
        functools.partial(_dispatch_kernel, nb=nb),
        grid_spec=grid_spec,
        out_shape=jax.ShapeDtypeStruct((p_total, words), U32),
        compiler_params=_cparams("arbitrary", "arbitrary"),
        name="moe_dispatch",
    )(hoff, loffs, ngr, goff, ggr, f, idx, wts, loff_col)


def _moe_kernel(te_ref, nt_ref, x_ref, wg_ref, wu_ref, wd_ref, y_ref, wg_s, wu_s, wd_s):
    i = pl.program_id(0)
    fresh = jnp.logical_or(i == 0, te_ref[i] != te_ref[jnp.maximum(i - 1, 0)])

    @pl.when(fresh)
    def _():
        wg_s[...] = wg_ref[0].astype(BF16)
        wu_s[...] = wu_ref[0].astype(BF16)
        wd_s[...] = wd_ref[0].astype(BF16)

    @pl.when(i < nt_ref[0])
    def _():
        half = wg_s.shape[0] // 2
        lo, hi = _unpack_bf16_pairs(x_ref[:, :half])
        gate = lax.bitcast_convert_type(x_ref[:, half:half + 1], F32)
        g = (jnp.dot(lo, wg_s[:half, :], preferred_element_type=F32)
             + jnp.dot(hi, wg_s[half:, :], preferred_element_type=F32))
        u = (jnp.dot(lo, wu_s[:half, :], preferred_element_type=F32)
             + jnp.dot(hi, wu_s[half:, :], preferred_element_type=F32))
        he = (_silu(g) * u).astype(BF16)
        y_ref[...] = _pack_bf16_pairs(jnp.dot(he, wd_s[...], preferred_element_type=F32) * gate)

    @pl.when(i >= nt_ref[0])
    def _():
        y_ref[...] = jnp.zeros_like(y_ref)


def _moe_grouped(x_sorted, tile_expert, n_tiles_used, w_gate, w_up, w_down):
    p, words = x_sorted.shape
    _, d, de = w_gate.shape
    tm = MOE_TILE
    grid_spec = pltpu.PrefetchScalarGridSpec(
        num_scalar_prefetch=2,
        grid=(p // tm,),
        in_specs=[
            pl.BlockSpec((tm, words), lambda i, te, nt: (i, 0)),
            pl.BlockSpec((1, d, de), lambda i, te, nt: (te[i], 0, 0)),
            pl.BlockSpec((1, d, de), lambda i, te, nt: (te[i], 0, 0)),
            pl.BlockSpec((1, de, d), lambda i, te, nt: (te[i], 0, 0)),
        ],
        out_specs=pl.BlockSpec((tm, d // 2), lambda i, te, nt: (i, 0)),
        scratch_shapes=[pltpu.VMEM((d, de), BF16), pltpu.VMEM((d, de), BF16), pltpu.VMEM((de, d), BF16)],
    )
    return pl.pallas_call(
        _moe_kernel,
        grid_spec=grid_spec,
        out_shape=jax.ShapeDtypeStruct((p, d // 2), U32),
        compiler_params=_cparams("arbitrary"),
        name="moe_grouped",
    )(tile_expert, n_tiles_used, x_sorted, w_gate, w_up, w_down)


def _combine_kernel(hoff_ref, loffs_ref, ngr_ref, h_ref, idx_ref, loff_ref, mod_ref, ys_ref, o_ref,
                    yperm, sems, *, nb):
    b, j = pl.program_id(0), pl.program_id(1)
    nj = pl.num_programs(1)
    step = b * nj + j
    slot = step % 2
    blk = b * nb + j
    nxt = jnp.where(j == nj - 1, (b + 1) * nb, blk + 1)

    def run_copies(blk_, slot_, wait):
        def fn(lo, ho, rows):
            cp = pltpu.make_async_copy(ys_ref.at[pl.ds(ho, rows)], yperm.at[slot_, pl.ds(lo, rows)], sems.at[slot_])
            cp.wait() if wait else cp.start()
        _for_each_run_piece(blk_, hoff_ref, loffs_ref, ngr_ref, fn)

    @pl.when(step == 0)
    def _():
        yperm[...] = jnp.zeros_like(yperm)
        run_copies(blk, 0, wait=False)

    @pl.when(step + 1 < pl.num_programs(0) * nj)
    def _():
        run_copies(nxt, 1 - slot, wait=False)

    q0, q1 = _block_perm(idx_ref, loff_ref)
    q = jnp.where(jnp.logical_or(q0, q1), 1.0, 0.0).astype(BF16)
    run_copies(blk, slot, wait=True)
    lo, hi = _unpack_bf16_pairs(yperm[slot])
    y = jnp.concatenate([_tn_dot(q, lo), _tn_dot(q, hi)], axis=1)
    o_ref[0] = h_ref[0] + mod_ref[0, 5:6, :] * y


def _combine(h, idx, loff_col, mod, mod_row, y_sorted, hoff, loffs, ngr, n_out):
    bsz, ltot, d = h.shape
    nb = ltot // TB
    grid_spec = pltpu.PrefetchScalarGridSpec(
        num_scalar_prefetch=3,
        grid=(bsz, n_out // TB),
        in_specs=[
            pl.BlockSpec((1, TB, d), lambda b, j, *_: (b, j, 0)),
            pl.BlockSpec((1, TOP_K, TB), lambda b, j, *_: (b, 0, j)),
            pl.BlockSpec((1, N_EXPERTS, 1), lambda b, j, *_: (b * nb + j, 0, 0)),
            pl.BlockSpec((1, 6, d), lambda b, j, *_: (mod_row(b, j), 0, 0)),
            pl.BlockSpec(memory_space=pl.ANY),
        ],
        out_specs=pl.BlockSpec((1, TB, d), lambda b, j, *_: (b, j, 0)),
        scratch_shapes=[pltpu.VMEM((2, PERM_ROWS, d // 2), U32), pltpu.SemaphoreType.DMA((2,))],
    )
    return pl.pallas_call(
        functools.partial(_combine_kernel, nb=nb),
        grid_spec=grid_spec,
        out_shape=jax.ShapeDtypeStruct((bsz, n_out, d), F32),
        compiler_params=_cparams("arbitrary", "arbitrary"),
        name="moe_combine",
    )(hoff, loffs, ngr, h, idx, loff_col, mod, y_sorted)


def _moe_block(h, f, idx, wts, mod, mod_row, w_gate, w_up, w_down, first_expert, n_out):
    bsz, ltot, d = h.shape
    nb = ltot // TB
    n_blocks = bsz * nb
    mt = MOE_TILE
    experts = jnp.arange(N_EXPERTS, dtype=jnp.int32)
    cnt = jnp.sum((idx.reshape(bsz, TOP_K, nb, TB, 1) == experts).astype(jnp.int32), axis=(1, 3))
    cnt = cnt.reshape(n_blocks, N_EXPERTS)
    run = ((cnt + RUN_ALIGN - 1) // RUN_ALIGN) * RUN_ALIGN
    loffs = jnp.cumsum(run, axis=1) - run
    region = ((jnp.sum(run, axis=0) + mt - 1) // mt) * mt
    ends = jnp.cumsum(region)
    hoff = (ends - region)[None, :] + jnp.cumsum(run, axis=0) - run
    p_total = ((TOP_K * bsz * ltot + n_blocks * N_EXPERTS * (RUN_ALIGN - 1)) // mt + N_EXPERTS + 1) * mt
    tile_start = jnp.arange(p_total // mt, dtype=jnp.int32) * mt
    tile_expert = jnp.minimum(jnp.sum((tile_start[:, None] >= ends[None, :]).astype(jnp.int32), axis=1),
                              N_EXPERTS - 1)
    n_used = (ends[-1] // mt).astype(jnp.int32).reshape(1)
    flat = lambda t: t.reshape(-1).astype(jnp.int32)
    hoff, loffs_flat, ngr = flat(hoff), flat(loffs), flat(run // RUN_ALIGN)
    loff_col = loffs.astype(F32).reshape(n_blocks, N_EXPERTS, 1)
    used = jnp.sum(run, axis=0)
    goff = flat(jnp.concatenate([ends - region + used, ends[-1:]]))
    ggr = flat(jnp.concatenate([(region - used) // RUN_ALIGN, (p_total - ends[-1:]) // ZERO_ROWS]))

    x_sorted = _dispatch(f, idx, wts, loff_col, hoff, loffs_flat, ngr, goff, ggr, p_total)
    y_sorted = _moe_grouped(x_sorted, tile_expert + first_expert, n_used, w_gate, w_up, w_down)
    return _combine(h, idx, loff_col, mod, mod_row, y_sorted, hoff, loffs_flat, ngr, n_out)


def _rope_tables(n_lat, n_ctx, n_heads):
    t = jnp.arange(n_lat)
    pos = jnp.stack([t // GRID_W, t % GRID_W], axis=-1).astype(F32)
    n_freq = HEAD_DIM // 4
    inv = 1.0 / (ROPE_BASE ** (jnp.arange(n_freq, dtype=F32) / n_freq))
    ang = pos[:, :, None] * inv
    cos, sin = jnp.cos(ang), jnp.sin(ang)
    cos_h = jnp.concatenate([cos[:, 0], cos[:, 0], cos[:, 1], cos[:, 1]], axis=-1)
    sin_h = jnp.concatenate([-sin[:, 0], sin[:, 0], -sin[:, 1], sin[:, 1]], axis=-1)
    cos_h = jnp.concatenate([cos_h, jnp.ones((n_ctx, HEAD_DIM), F32)], axis=0)
    sin_h = jnp.concatenate([sin_h, jnp.zeros((n_ctx, HEAD_DIM), F32)], axis=0)
    return jnp.tile(cos_h, (1, n_heads)), jnp.tile(sin_h, (1, n_heads))


def _pack_w_in(w_in, rw, sw, nw_):
    o = 0
    seg = {}
    for name, width in (("rq", rw), ("rk", rw), ("rv", rw), ("rg", rw), ("z", sw), ("x", sw),
                        ("b", SSD_GROUPS * SSD_STATE), ("c", SSD_GROUPS * SSD_STATE),
                        ("dt", 2 * (sw // HEAD_DIM)), ("nq", nw_), ("nk", nw_), ("nv", nw_)):
        seg[name] = w_in[:, o:o + width]
        o += width
    dt_pad = jnp.zeros((w_in.shape[0], 128 - seg["dt"].shape[1]), w_in.dtype)
    cols = [seg["rq"], seg["rk"], seg["rv"], seg["rg"],
            seg["z"], seg["x"], seg["b"], seg["c"], seg["nq"], seg["nk"], seg["nv"], seg["dt"], dt_pad]
    return jnp.concatenate(cols, axis=1).astype(BF16)


def kernel(x, c, ctx, c_ctx, w_mod, b_mod, norm_mix, norm_ffn, w_in, w_out, ret_decay_f, ret_decay_b,
           ssd_conv_w, ssd_conv_b, ssd_dt_bias_f, ssd_dt_bias_b, ssd_a_log_f, ssd_a_log_b, ssd_d, ssd_norm,
           na_q_norm, na_k_norm, na_rpb, router_w, router_b, w_gate, w_up, w_down):
    bsz, n_lat, d = x.shape
    n_ctx = ctx.shape[1]
    depth = w_mod.shape[0]
    rows = n_lat // GRID_W
    rw = ret_decay_f.shape[1] * HEAD_DIM
    sw = ssd_d.shape[1] * HEAD_DIM
    nw_ = na_rpb.shape[1] * HEAD_DIM
    dims = (rw, sw, nw_)
    assert n_lat % TB == 0 and n_ctx % TB == 0
    nbl = n_lat // TB

    n_cond = ((bsz + 1 + 7) // 8) * 8
    cond = jnp.zeros((n_cond, d), F32).at[:bsz].set(c).at[bsz].set(c_ctx)
    mod = _modulation(cond, w_mod, b_mod).reshape(depth, n_cond, 6, d)
    mod_row = lambda b, j: jnp.where(j < nbl, b, bsz)

    cos, sin = _rope_tables(n_lat, n_ctx, rw // HEAD_DIM)
    hm = jnp.asarray(np.kron(np.eye(nw_ // HEAD_DIM), np.full((HEAD_DIM, HEAD_DIM), 1.0 / HEAD_DIM)), F32)
    rw_t = jnp.pad(jnp.concatenate(_bf16_parts(router_w, 3), axis=1), ((0, 0), (0, 128 - 3 * router_w.shape[1])))
    rb = router_b.reshape(-1, 1)

    expert_w = [t.reshape((-1,) + t.shape[2:]) for t in (w_gate, w_up, w_down)]
    h = (x, ctx)
    for l in range(depth):
        last = l == depth - 1
        w_all = _pack_w_in(w_in[l], rw, sw, nw_)
        qkn = jnp.stack([jnp.tile(na_q_norm[l], nw_ // HEAD_DIM), jnp.tile(na_k_norm[l], nw_ // HEAD_DIM)])
        ret, z, xbc, dt, na = _inproj(h, mod[l], mod_row, norm_mix[l].reshape(1, d), w_all, cos, sin, qkn, hm, dims, nbl)
        ret_o, ssd_o = _scan_mixers(
            ret, jnp.concatenate([ret_decay_f[l], ret_decay_b[l]]).reshape(-1, 1, 1), z, xbc, dt,
            ssd_conv_w[l], ssd_conv_b[l], jnp.concatenate([ssd_dt_bias_f[l], ssd_dt_bias_b[l]]),
            jnp.concatenate([ssd_a_log_f[l], ssd_a_log_b[l]]), ssd_d[l], ssd_norm[l], nbl)
        na_o = _na_attention(na, _na_bias_tables(na_rpb[l], rows), nbl)
        h, f, idx, wts = _outproj(h, ret_o, ssd_o, na_o, w_out[l].astype(BF16), mod[l], mod_row,
                                  norm_ffn[l].reshape(1, d), rw_t, rb, nbl)
        h = _moe_block(h, f, idx, wts, mod[l], mod_row, *expert_w, l * N_EXPERTS,
                       n_lat if last else n_lat + n_ctx)
    return h
```

```python
import functools

import jax
import jax.numpy as jnp
import numpy as np
from jax import lax
from jax.experimental import pallas as pl
from jax.experimental.pallas import tpu as pltpu

F32 = jnp.float32
BF16 = jnp.bfloat16
HIGHEST = lax.Precision.HIGHEST

GRID_W = 64
HEAD_DIM = 64
SSD_GROUPS = 2
SSD_STATE = 64
NA_WIN_ROWS = 8
NA_WIN_COLS = 16
N_GROUPS = 4
EXPERTS_PER_GROUP = 4
N_EXPERTS = N_GROUPS * EXPERTS_PER_GROUP
TOP_K = 2
ROPE_BASE = 10000.0
EPS = 1e-6

VMEM_LIMIT_BYTES = 56 * 1024 * 1024
TB = 256
NA_QROWS = TB // GRID_W
MOE_TILE = 1024
SUBLANES = 8


def _cparams(*sem):
    return pltpu.CompilerParams(dimension_semantics=sem, vmem_limit_bytes=VMEM_LIMIT_BYTES)


def _silu(t):
    return t * (1.0 / (1.0 + jnp.exp(-t)))


def _nt_dot(a, b, **kw):
    return lax.dot_general(a, b, (((1,), (1,)), ((), ())), preferred_element_type=F32, **kw)


def _tn_dot(a, b):
    return lax.dot_general(a, b, (((0,), (0,)), ((), ())), preferred_element_type=F32)


def _bf16_parts(x, n):
    parts, rest = [], x
    for i in range(n):
        p = rest.astype(BF16)
        parts.append(p)
        if i + 1 < n:
            rest = rest - p.astype(F32)
    return parts


def _split_dot(x, sel, parts=3, left=False):
    sel = sel.astype(BF16)
    dot = (lambda p: jnp.dot(sel, p, preferred_element_type=F32)) if left else (
        lambda p: jnp.dot(p, sel, preferred_element_type=F32))
    out = None
    for p in _bf16_parts(x, parts):
        out = dot(p) if out is None else out + dot(p)
    return out


def _h_specs(h, nbl):
    if not isinstance(h, tuple):
        return (h,), [pl.BlockSpec((1, TB, h.shape[2]), lambda b, j: (b, j, 0))]
    d = h[0].shape[2]
    return h, [pl.BlockSpec((1, TB, d), lambda b, j: (b, jnp.minimum(j, nbl - 1), 0)),
               pl.BlockSpec((1, TB, d), lambda b, j: (b, jnp.maximum(j - nbl, 0), 0))]


def _read_h(h_refs, nbl):
    if len(h_refs) == 1:
        return h_refs[0][0]
    return jnp.where(pl.program_id(1) < nbl, h_refs[0][0], h_refs[1][0])


def _mod_kernel(s_ref, w_ref, b_ref, o_ref):
    s = _silu(s_ref[...])
    o_ref[0] = jnp.dot(s, w_ref[0], precision=HIGHEST, preferred_element_type=F32) + b_ref[0]


def _modulation(cond, w_mod, b_mod):
    depth, d, n = w_mod.shape
    rows = cond.shape[0]
    tn = 1024
    return pl.pallas_call(
        _mod_kernel,
        grid=(depth, n // tn),
        in_specs=[
            pl.BlockSpec((rows, d), lambda l, j: (0, 0)),
            pl.BlockSpec((1, d, tn), lambda l, j: (l, 0, j)),
            pl.BlockSpec((1, 1, tn), lambda l, j: (l, 0, j)),
        ],
        out_specs=pl.BlockSpec((1, rows, tn), lambda l, j: (l, 0, j)),
        out_shape=jax.ShapeDtypeStruct((depth, rows, n), F32),
        compiler_params=_cparams("parallel", "parallel"),
        name="modulation",
    )(cond, w_mod, b_mod.reshape(depth, 1, n))


def _inproj_kernel(*refs, dims, n_h, nbl):
    h_refs, (mod_ref, nw_ref, w_ref, cos_ref, sin_ref, qkn_ref, hm_ref,
             ret_ref, z_ref, xbc_ref, dt_ref, na_ref) = refs[:n_h], refs[n_h:]
    rw, sw, nw_ = dims
    h = _read_h(h_refs, nbl)
    a = h * lax.rsqrt(jnp.mean(h * h, axis=-1, keepdims=True) + EPS) * nw_ref[...]
    a = a * (1.0 + mod_ref[0, 1:2, :]) + mod_ref[0, 0:1, :]
    ab = a.astype(BF16)

    u_all = jnp.dot(ab, w_ref[...], preferred_element_type=F32)

    def proj(lo, hi):
        return u_all[:, lo:hi]

    cos, sin = cos_ref[...], sin_ref[...]
    nf = HEAD_DIM // 4
    first_half = (lax.broadcasted_iota(jnp.int32, cos.shape, 1) % (2 * nf)) < nf

    def rotate(t):
        return t * cos + jnp.where(first_half, pltpu.roll(t, rw - nf, 1), pltpu.roll(t, nf, 1)) * sin

    o = 0
    ret_ref[0, :, 0:rw] = rotate(proj(o, o + rw))
    ret_ref[0, :, rw:2 * rw] = rotate(proj(o + rw, o + 2 * rw)) * (HEAD_DIM ** -0.5)
    ret_ref[0, :, 2 * rw:4 * rw] = proj(o + 2 * rw, o + 4 * rw)
    o += 4 * rw
    z_ref[0] = proj(o, o + sw)
    o += sw
    xbc_w = sw + 2 * SSD_GROUPS * SSD_STATE
    xbc_ref[0] = proj(o, o + xbc_w)
    o += xbc_w
    hm = hm_ref[...]
    for i in range(2):
        t = proj(o + i * nw_, o + (i + 1) * nw_)
        ms = _split_dot(t * t, hm, parts=2)
        na_ref[0, :, i * nw_:(i + 1) * nw_] = (t * lax.rsqrt(ms + EPS) * qkn_ref[i:i + 1, :]).astype(BF16)
    na_ref[0, :, 2 * nw_:3 * nw_] = proj(o + 2 * nw_, o + 3 * nw_).astype(BF16)
    o += 3 * nw_
    dt_ref[0] = proj(o, o + 128)[:, 0:dt_ref.shape[2]]


def _inproj(h, mod, mod_row, norm_w, w_all, cos, sin, qkn, hm, dims, nbl):
    h_arrays, h_specs = _h_specs(h, nbl)
    bsz, d = h_arrays[0].shape[0], h_arrays[0].shape[2]
    ltot = cos.shape[0]
    rw, sw, nw_ = dims
    n_dt = 2 * (sw // HEAD_DIM)
    xbc_w = sw + 2 * SSD_GROUPS * SSD_STATE
    row = lambda n: pl.BlockSpec((1, TB, n), lambda b, j: (b, j, 0))
    const = lambda shape: pl.BlockSpec(shape, lambda b, j: (0,) * len(shape))
    widths = (4 * rw, sw, xbc_w, n_dt, 3 * nw_)
    dtypes = (F32, F32, F32, F32, BF16)
    return pl.pallas_call(
        functools.partial(_inproj_kernel, dims=dims, n_h=len(h_arrays), nbl=nbl),
        grid=(bsz, ltot // TB),
        in_specs=h_specs + [
            pl.BlockSpec((1, 6, d), lambda b, j: (mod_row(b, j), 0, 0)),
            const((1, d)),
            const(w_all.shape),
            pl.BlockSpec((TB, rw), lambda b, j: (j, 0)),
            pl.BlockSpec((TB, rw), lambda b, j: (j, 0)),
            const((2, nw_)),
            const((nw_, nw_)),
        ],
        out_specs=[row(n) for n in widths],
        out_shape=[jax.ShapeDtypeStruct((bsz, ltot, n), dt) for n, dt in zip(widths, dtypes)],
        compiler_params=_cparams("parallel", "parallel"),
        name="inproj",
    )(*h_arrays, mod, norm_w, w_all, cos, sin, qkn, hm)


def _scan_chunk_maps(nb, nbl):
    fwd = lambda s: jnp.where(s < nb, (nbl + s) % nb, 2 * nb - 1 - s)
    out = lambda s: jnp.where(s < nb, nb - 1, 2 * nb - 1 - s)
    return fwd, out


def _ret_phases(q_ref, k_ref, v_ref, g_ref, dec_ref, o_ref, sf_all, sf, sb, dmat, rd, rdt, *, nb, nbl):
    s = pl.program_id(1)
    c = TB
    nh = k_ref.shape[2] // HEAD_DIM
    log_f = [-jnp.exp(dec_ref[h]) for h in range(nh)]
    log_b = [-jnp.exp(dec_ref[nh + h]) for h in range(nh)]

    def _init():
        sf[...] = jnp.zeros_like(sf)
        sb[...] = jnp.zeros_like(sb)
        delta = (lax.broadcasted_iota(jnp.int32, (c, c), 0) - lax.broadcasted_iota(jnp.int32, (c, c), 1)).astype(F32)
        pos = lax.broadcasted_iota(jnp.int32, (c, HEAD_DIM), 0).astype(F32)
        pos_l = lax.broadcasted_iota(jnp.int32, (SUBLANES, c), 1).astype(F32)
        for h in range(nh):
            dmat[h] = (jnp.exp(jnp.where(delta >= 0, log_f[h] * delta, -jnp.inf))
                       + jnp.exp(jnp.where(delta <= 0, -log_b[h] * delta, -jnp.inf)))
            rd[2 * h + 0] = jnp.exp(log_f[h] * (pos + 1.0))
            rd[2 * h + 1] = jnp.exp(log_b[h] * (c - pos))
            rdt[2 * h + 0] = jnp.exp(log_f[h] * (c - 1.0 - pos_l))
            rdt[2 * h + 1] = jnp.exp(log_b[h] * pos_l)

    def _state_sweep():
        ci = (nbl + s) % nb
        k_t = k_ref[0].T
        for h in range(nh):
            sl = slice(h * HEAD_DIM, (h + 1) * HEAD_DIM)
            kw = (k_t[sl, :] * rdt[2 * h + 0, 0:1, :]).astype(BF16)
            sf_all[ci, h] = sf[h]
            sf[h] = sf[h] * jnp.exp(log_f[h] * c) + jnp.dot(kw, v_ref[0, :, sl].astype(BF16),
                                                            preferred_element_type=F32)

    def _output_sweep():
        ci = 2 * nb - 1 - s
        k_t = k_ref[0].T
        for h in range(nh):
            sl = slice(h * HEAD_DIM, (h + 1) * HEAD_DIM)
            qb, vb = q_ref[0, :, sl].astype(BF16), v_ref[0, :, sl].astype(BF16)
            p = jnp.dot(qb, k_t[sl, :].astype(BF16), preferred_element_type=F32) * dmat[h]
            y = jnp.dot(p.astype(BF16), vb, preferred_element_type=F32)
            y += jnp.dot(qb, sf_all[ci, h].astype(BF16), preferred_element_type=F32) * rd[2 * h + 0]
            y += jnp.dot(qb, sb[h].astype(BF16), preferred_element_type=F32) * rd[2 * h + 1]
            kw = (k_t[sl, :] * rdt[2 * h + 1, 0:1, :]).astype(BF16)
            sb[h] = sb[h] * jnp.exp(log_b[h] * c) + jnp.dot(kw, vb, preferred_element_type=F32)
            mu = jnp.mean(y, axis=-1, keepdims=True)
            yc = y - mu
            yn = yc * lax.rsqrt(jnp.mean(yc * yc, axis=-1, keepdims=True) + EPS)
            o_ref[0, :, sl] = (yn * _silu(g_ref[0, :, sl])).astype(o_ref.dtype)

    return _init, _state_sweep, _output_sweep


def _softplus(t):
    return jnp.maximum(t, 0.0) + jnp.log1p(jnp.exp(-jnp.abs(t)))


def _lane_group_selector(n_rows, group):
    shape = (n_rows, n_rows * group)
    lane_owner = lax.broadcasted_iota(jnp.int32, shape, 1) // group
    return jnp.where(lane_owner == lax.broadcasted_iota(jnp.int32, shape, 0), 1.0, 0.0).astype(BF16)


def _ssd_phases(z_ref, x_ref, xp_ref, xn_ref, dt_ref, cw_ref, cb_ref, dtb_ref, alog_ref, dtbt_ref, alogt_ref,
                dsk_ref, nw_ref, o_ref, sf_all, sf, sb, tri, ybuf, xc_all, *, nb, nbl):
    s = pl.program_id(1)
    c = TB
    sw = z_ref.shape[2]
    nh = sw // HEAD_DIM
    rep = nh // SSD_GROUPS
    gw = rep * HEAD_DIM
    gs = SSD_GROUPS * SSD_STATE
    ci = jnp.where(s < nb, (nbl + s) % nb, 2 * nb - 1 - s)

    def _init():
        sf[...] = jnp.zeros_like(sf)
        sb[...] = jnp.zeros_like(sb)
        li = lax.broadcasted_iota(jnp.int32, (c, c), 0)
        si = lax.broadcasted_iota(jnp.int32, (c, c), 1)
        tri[0] = jnp.where(si <= li, 1.0, 0.0).astype(BF16)
        tri[1] = jnp.where(si >= li, 1.0, 0.0).astype(BF16)

    def _sweeps():
        def conv_silu():
            x = x_ref[0]
            has_prev = jnp.where(jnp.logical_or(ci == 0, ci == nbl), 0.0, 1.0)
            has_next = jnp.where(jnp.logical_or(ci == nbl - 1, ci == nb - 1), 0.0, 1.0)
            rows = lax.broadcasted_iota(jnp.int32, x.shape, 0)
            x_dn = jnp.where(rows == 0, xp_ref[0, SUBLANES - 1:SUBLANES, :] * has_prev, pltpu.roll(x, 1, 0))
            x_up = jnp.where(rows == c - 1, xn_ref[0, 0:1, :] * has_next, pltpu.roll(x, c - 1, 0))
            return _silu(cw_ref[0:1, :] * x_dn + cw_ref[1:2, :] * x + cw_ref[2:3, :] * x_up + cb_ref[...])

        def split(xc):
            xs, bm, cm = xc[:, :sw], xc[:, sw:sw + gs], xc[:, sw + gs:]
            return xs, bm.T.astype(BF16), cm

        dtr = dt_ref[0]
        dt = _softplus(dtr + dtb_ref[...])
        a = dt * -jnp.exp(alog_ref[...])
        pre = _split_dot(a, tri[0], left=True)
        tot = pre[c - 1:c, :]
        is_fwd = lax.broadcasted_iota(jnp.int32, (1, 2 * nh), 1) < nh
        acs = jnp.where(is_fwd, pre, tot - pre + a)
        sel_head = _lane_group_selector(2 * nh, HEAD_DIM)
        w_wide = _split_dot(dt * jnp.exp(tot - acs), sel_head, parts=2)
        etot_wide = _split_dot(jnp.broadcast_to(jnp.exp(tot), (SUBLANES, 2 * nh)), sel_head, parts=2)[0:1, :]

        def _state_sweep():
            xc = conv_silu()
            xc_all[ci] = xc
            xs, bm_t, _ = split(xc)
            for g in range(SSD_GROUPS):
                lanes = slice(g * gw, (g + 1) * gw)
                xw = (xs[:, lanes] * w_wide[:, lanes]).astype(BF16)
                sf_all[ci, g] = sf[g]
                sf[g] = sf[g] * etot_wide[:, lanes] + jnp.dot(bm_t[g * SSD_STATE:(g + 1) * SSD_STATE, :], xw,
                                                              preferred_element_type=F32)

        def _output_sweep():
            xs, bm_t, cm = split(xc_all[ci])
            eye = (lax.broadcasted_iota(jnp.int32, (2 * nh, 2 * nh), 0)
                   == lax.broadcasted_iota(jnp.int32, (2 * nh, 2 * nh), 1)).astype(BF16)
            dtr_t = sum(_nt_dot(eye, p) for p in _bf16_parts(dtr, 3))
            dt_t = _softplus(dtr_t + dtbt_ref[...])
            a_t = dt_t * -jnp.exp(alogt_ref[...])
            pre_t = _split_dot(a_t, tri[1])
            is_fwd_t = lax.broadcasted_iota(jnp.int32, (2 * nh, 1), 0) < nh
            acs_t = jnp.where(is_fwd_t, pre_t, pre_t[:, c - 1:c] - pre_t + a_t)
            e_wide = _split_dot(jnp.exp(acs), sel_head, parts=2)
            li = lax.broadcasted_iota(jnp.int32, (c, c), 0)
            si = lax.broadcasted_iota(jnp.int32, (c, c), 1)
            lower, upper = li >= si, si >= li
            for g in range(SSD_GROUPS):
                lanes = slice(g * gw, (g + 1) * gw)
                lanes_b = slice(sw + g * gw, sw + (g + 1) * gw)
                cg = cm[:, g * SSD_STATE:(g + 1) * SSD_STATE].astype(BF16)
                bg_t = bm_t[g * SSD_STATE:(g + 1) * SSD_STATE, :]
                scores = jnp.dot(cg, bg_t, preferred_element_type=F32)
                for r in range(rep):
                    h = g * rep + r
                    hb = nh + h
                    col_f, col_b = acs[:, h:h + 1], acs[:, hb:hb + 1]
                    d_f = jnp.exp(jnp.where(lower, col_f - acs_t[h:h + 1, :], -jnp.inf)) * dt_t[h:h + 1, :]
                    d_b = jnp.exp(jnp.where(upper, col_b - acs_t[hb:hb + 1, :], -jnp.inf)) * dt_t[hb:hb + 1, :]
                    ybuf[:, h * HEAD_DIM:(h + 1) * HEAD_DIM] = jnp.dot(
                        (scores * (d_f + d_b)).astype(BF16), xs[:, h * HEAD_DIM:(h + 1) * HEAD_DIM].astype(BF16),
                        preferred_element_type=F32)
                ybuf[:, lanes] += (jnp.dot(cg, sf_all[ci, g].astype(BF16), preferred_element_type=F32) * e_wide[:, lanes]
                                   + jnp.dot(cg, sb[g].astype(BF16), preferred_element_type=F32) * e_wide[:, lanes_b])
                xw = (xs[:, lanes] * w_wide[:, lanes_b]).astype(BF16)
                sb[g] = sb[g] * etot_wide[:, lanes_b] + jnp.dot(bg_t, xw, preferred_element_type=F32)
            y = (ybuf[...] + dsk_ref[...] * xs) * _silu(z_ref[0])
            o_ref[0] = (y * lax.rsqrt(jnp.mean(y * y, axis=-1, keepdims=True) + EPS) * nw_ref[...]).astype(o_ref.dtype)

        return _state_sweep, _output_sweep

    return _init, _sweeps


N_RET_IN, N_SSD_IN, N_RET_SCRATCH = 5, 13, 6


def _scan_mixers_kernel(*refs, nb, nbl):
    ret_in, refs = refs[:N_RET_IN], refs[N_RET_IN:]
    ssd_in, refs = refs[:N_SSD_IN], refs[N_SSD_IN:]
    (ret_o, ssd_o), refs = refs[:2], refs[2:]
    ret_init, ret_state, ret_out = _ret_phases(*ret_in, ret_o, *refs[:N_RET_SCRATCH], nb=nb, nbl=nbl)
    ssd_init, ssd_sweeps = _ssd_phases(*ssd_in, ssd_o, *refs[N_RET_SCRATCH:], nb=nb, nbl=nbl)
    s = pl.program_id(1)
    pl.when(s == 0)(ret_init)
    pl.when(s < nb)(ret_state)
    pl.when(s >= nb)(ret_out)
    pl.when(s == 0)(ssd_init)
    ssd_state, ssd_out = ssd_sweeps()
    pl.when(s < nb)(ssd_state)
    pl.when(s >= nb)(ssd_out)


def _scan_mixers(ret, decay, z, xbc, dt, conv_w, conv_b, dtb, alog, d_skip, norm_w, nbl):
    bsz, ltot, w4 = ret.shape
    w = w4 // 4
    nh_r = w // HEAD_DIM
    sw = z.shape[2]
    nh = sw // HEAD_DIM
    xw = xbc.shape[2]
    nb = ltot // TB
    per = TB // SUBLANES
    fwd, out = _scan_chunk_maps(nb, nbl)
    conv_in = lambda s: fwd(jnp.minimum(s, nb - 1))
    const = lambda t: pl.BlockSpec(t.shape, lambda b, s: (0,) * t.ndim)
    args = (conv_w, conv_b.reshape(1, xw), dtb.reshape(1, 2 * nh), alog.reshape(1, 2 * nh),
            dtb.reshape(2 * nh, 1), alog.reshape(2 * nh, 1), jnp.repeat(d_skip, HEAD_DIM).reshape(1, sw),
            norm_w.reshape(1, sw))
    ret_specs = [
        pl.BlockSpec((1, TB, w), lambda b, s: (b, out(s), 0)),
        pl.BlockSpec((1, TB, w), lambda b, s: (b, fwd(s), 1)),
        pl.BlockSpec((1, TB, w), lambda b, s: (b, fwd(s), 2)),
        pl.BlockSpec((1, TB, w), lambda b, s: (b, out(s), 3)),
        const(decay),
    ]
    ssd_specs = [
        pl.BlockSpec((1, TB, sw), lambda b, s: (b, out(s), 0)),
        pl.BlockSpec((1, TB, xw), lambda b, s: (b, conv_in(s), 0)),
        pl.BlockSpec((1, SUBLANES, xw), lambda b, s: (b, jnp.maximum(conv_in(s) * per - 1, 0), 0)),
        pl.BlockSpec((1, SUBLANES, xw), lambda b, s: (b, jnp.minimum((conv_in(s) + 1) * per, nb * per - 1), 0)),
        pl.BlockSpec((1, TB, 2 * nh), lambda b, s: (b, fwd(s), 0)),
    ] + [const(t) for t in args]
    assert len(ret_specs) == N_RET_IN and len(ssd_specs) == N_SSD_IN
    ret_scratch = [
        pltpu.VMEM((nb, nh_r, HEAD_DIM, HEAD_DIM), F32),
        pltpu.VMEM((nh_r, HEAD_DIM, HEAD_DIM), F32),
        pltpu.VMEM((nh_r, HEAD_DIM, HEAD_DIM), F32),
        pltpu.VMEM((nh_r, TB, TB), F32),
        pltpu.VMEM((2 * nh_r, TB, HEAD_DIM), F32),
        pltpu.VMEM((2 * nh_r, SUBLANES, TB), F32),
    ]
    ssd_scratch = [
        pltpu.VMEM((nb, SSD_GROUPS, SSD_STATE, sw // SSD_GROUPS), F32),
        pltpu.VMEM((SSD_GROUPS, SSD_STATE, sw // SSD_GROUPS), F32),
        pltpu.VMEM((SSD_GROUPS, SSD_STATE, sw // SSD_GROUPS), F32),
        pltpu.VMEM((2, TB, TB), BF16),
        pltpu.VMEM((TB, sw), F32),
        pltpu.VMEM((nb, TB, xw), F32),
    ]
    assert len(ret_scratch) == N_RET_SCRATCH
    return pl.pallas_call(
        functools.partial(_scan_mixers_kernel, nb=nb, nbl=nbl),
        grid=(bsz, 2 * nb),
        in_specs=ret_specs + ssd_specs,
        out_specs=[pl.BlockSpec((1, TB, w), lambda b, s: (b, out(s), 0)),
                   pl.BlockSpec((1, TB, sw), lambda b, s: (b, out(s), 0))],
        out_shape=[jax.ShapeDtypeStruct((bsz, ltot, w), BF16), jax.ShapeDtypeStruct((bsz, ltot, sw), BF16)],
        scratch_shapes=ret_scratch + ssd_scratch,
        compiler_params=_cparams("parallel", "arbitrary"),
        name="scan_mixers",
    )(ret, ret, ret, ret, decay, z, xbc, xbc, xbc, dt, *args)


def _softmax_pv(s_parts, v):
    m = s_parts[0].max(axis=-1, keepdims=True)
    for s in s_parts[1:]:
        m = jnp.maximum(m, s.max(axis=-1, keepdims=True))
    p_parts = [jnp.exp(s - m) for s in s_parts]
    den = p_parts[0].sum(axis=-1, keepdims=True)
    for p in p_parts[1:]:
        den = den + p.sum(axis=-1, keepdims=True)
    p = p_parts[0] if len(p_parts) == 1 else jnp.concatenate(p_parts, axis=1)
    return jnp.dot(p.astype(BF16), v, preferred_element_type=F32) / den


def _na_kernel(q_ref, k0_ref, k1_ref, k2_ref, v0_ref, v1_ref, v2_ref, kc_ref, vc_ref, bias_ref, o_ref, *, nbl):
    nh = q_ref.shape[2] // HEAD_DIM
    n_lat_keys = 3 * k0_ref.shape[1]
    j = pl.program_id(1)

    @pl.when(j < nbl)
    def _latent():
        for h in range(nh):
            sl = slice(h * HEAD_DIM, (h + 1) * HEAD_DIM)
            q = q_ref[0, :, sl] * (HEAD_DIM ** -0.5)
            k = jnp.concatenate([k0_ref[0, :, sl], k1_ref[0, :, sl], k2_ref[0, :, sl], kc_ref[0, :, sl]], axis=0)
            v = jnp.concatenate([v0_ref[0, :, sl], v1_ref[0, :, sl], v2_ref[0, :, sl], vc_ref[0, :, sl]], axis=0)
            s = _nt_dot(q, k)
            o = _softmax_pv([s[:, :n_lat_keys] + bias_ref[0, h], s[:, n_lat_keys:]], v)
            o_ref[0, :, sl] = o.astype(o_ref.dtype)

    @pl.when(j >= nbl)
    def _context():
        for h in range(nh):
            sl = slice(h * HEAD_DIM, (h + 1) * HEAD_DIM)
            s = _nt_dot(q_ref[0, :, sl] * (HEAD_DIM ** -0.5), kc_ref[0, :, sl])
            o_ref[0, :, sl] = _softmax_pv([s], vc_ref[0, :, sl]).astype(o_ref.dtype)


def _na_bias_tables(rpb, rows):
    qb, kb = NA_QROWS, 3 * NA_QROWS
    nb = rows // qb
    nh = rpb.shape[0]
    n_dr, n_dc = 2 * NA_WIN_ROWS - 1, 2 * NA_WIN_COLS - 1
    c = np.arange(GRID_W)[:, None]
    kc = np.arange(GRID_W)[None, :]
    cs = np.clip(c - NA_WIN_COLS // 2, 0, GRID_W - NA_WIN_COLS)
    col_ok = (kc >= cs) & (kc < cs + NA_WIN_COLS)
    dc = np.clip(kc - c + NA_WIN_COLS - 1, 0, n_dc - 1)
    col_sel = np.eye(n_dc, dtype=np.float32)[dc.reshape(-1)].T
    tabs = []
    for rbq, wb in ((0, 0), (1, 0), (nb - 1, nb - 3)):
        r = qb * rbq + np.arange(qb)[:, None]
        kr = qb * wb + np.arange(kb)[None, :]
        r0 = np.clip(r - NA_WIN_ROWS // 2, 0, rows - NA_WIN_ROWS)
        row_ok = (kr >= r0) & (kr < r0 + NA_WIN_ROWS)
        dr = np.clip(kr - r + NA_WIN_ROWS - 1, 0, n_dr - 1)
        row_sel = np.eye(n_dr, dtype=np.float32)[dr.reshape(-1)]
        t = jnp.einsum("ad,hde,ef->haf", row_sel, rpb.astype(F32), col_sel, precision=HIGHEST)
        t = t.reshape(nh, qb, kb, GRID_W, GRID_W).transpose(0, 1, 3, 2, 4).reshape(nh, qb * GRID_W, kb * GRID_W)
        valid = (row_ok[:, None, :, None] & col_ok[None, :, None, :]).reshape(qb * GRID_W, kb * GRID_W)
        tabs.append(jnp.where(valid, t, -jnp.inf))
    return jnp.stack(tabs)


def _na_attention(na, bias, nbl):
    bsz, ltot, w3 = na.shape
    w = w3 // 3
    nb = ltot // TB
    assert nbl >= 3 and nb == nbl + 1, "needs >= 12 grid rows and a context of one token block"
    win = lambda j: jnp.clip(j - 1, 0, nbl - 3)
    kv = lambda i, col: pl.BlockSpec((1, TB, w), lambda b, j: (b, win(j) + i, col))
    typ = lambda j: jnp.where(j == 0, 0, jnp.where(j >= nbl - 1, 2, 1))
    return pl.pallas_call(
        functools.partial(_na_kernel, nbl=nbl),
        grid=(bsz, nb),
        in_specs=[
            pl.BlockSpec((1, TB, w), lambda b, j: (b, j, 0)),
            kv(0, 1), kv(1, 1), kv(2, 1), kv(0, 2), kv(1, 2), kv(2, 2),
            pl.BlockSpec((1, TB, w), lambda b, j: (b, nbl, 1)),
            pl.BlockSpec((1, TB, w), lambda b, j: (b, nbl, 2)),
            pl.BlockSpec((1,) + bias.shape[1:], lambda b, j: (typ(j), 0, 0, 0)),
        ],
        out_specs=pl.BlockSpec((1, TB, w), lambda b, j: (b, j, 0)),
        out_shape=jax.ShapeDtypeStruct((bsz, ltot, w), BF16),
        compiler_params=_cparams("parallel", "arbitrary"),
        name="na_attention",
    )(na, na, na, na, na, na, na, na, na, bias)


def _route(aff, sel):
    rows = [sel[e:e + 1, :] for e in range(N_EXPERTS)]
    arow = [aff[e:e + 1, :] for e in range(N_EXPERTS)]
    gscore = []
    for g in range(N_GROUPS):
        a, b, c, d = rows[4 * g:4 * g + 4]
        hi1, lo1 = jnp.maximum(a, b), jnp.minimum(a, b)
        hi2, lo2 = jnp.maximum(c, d), jnp.minimum(c, d)
        gscore.append(jnp.maximum(hi1, hi2) + jnp.maximum(jnp.minimum(hi1, hi2), jnp.maximum(lo1, lo2)))
    best = jnp.zeros_like(gscore[0], dtype=jnp.int32)
    top = gscore[0]
    for g in range(1, N_GROUPS):
        upd = gscore[g] > top
        best = jnp.where(upd, g, best)
        top = jnp.where(upd, gscore[g], top)
    sv, av = [], []
    for j in range(EXPERTS_PER_GROUP):
        s_j, a_j = rows[j], arow[j]
        for g in range(1, N_GROUPS):
            s_j = jnp.where(best == g, rows[4 * g + j], s_j)
            a_j = jnp.where(best == g, arow[4 * g + j], a_j)
        sv.append(s_j)
        av.append(a_j)
    picked = []
    for j in range(EXPERTS_PER_GROUP):
        rank = jnp.zeros_like(best)
        for i in range(EXPERTS_PER_GROUP):
            if i == j:
                continue
            ahead = (sv[i] >= sv[j]) if i < j else (sv[i] > sv[j])
            rank = rank + ahead.astype(jnp.int32)
        picked.append(rank < TOP_K)
    first = jnp.full_like(best, EXPERTS_PER_GROUP - 1)
    last = jnp.zeros_like(best)
    for j in range(EXPERTS_PER_GROUP - 1, -1, -1):
        first = jnp.where(picked[j], j, first)
    for j in range(EXPERTS_PER_GROUP):
        last = jnp.where(picked[j], j, last)
    a_first, a_last = av[0], av[0]
    for j in range(1, EXPERTS_PER_GROUP):
        a_first = jnp.where(first == j, av[j], a_first)
        a_last = jnp.where(last == j, av[j], a_last)
    tot = a_first + a_last
    idx = jnp.concatenate([best * EXPERTS_PER_GROUP + first, best * EXPERTS_PER_GROUP + last], axis=0)
    wts = jnp.concatenate([a_first / tot, a_last / tot], axis=0)
    return idx, wts


def _outproj_kernel(*refs, n_h, nbl):
    h_refs, (ret_ref, ssd_ref, na_ref, wo_ref, mod_ref, nw_ref, rwt_ref, rb_ref,
             hn_ref, f_ref, idx_ref, wts_ref) = refs[:n_h], refs[n_h:]
    rw, sw = ret_ref.shape[2], ssd_ref.shape[2]
    acc = jnp.dot(ret_ref[0], wo_ref[0:rw, :], preferred_element_type=F32)
    acc += jnp.dot(ssd_ref[0], wo_ref[rw:rw + sw, :], preferred_element_type=F32)
    acc += jnp.dot(na_ref[0], wo_ref[rw + sw:, :], preferred_element_type=F32)
    hn = _read_h(h_refs, nbl) + mod_ref[0, 2:3, :] * acc
    hn_ref[0] = hn
    f = hn * lax.rsqrt(jnp.mean(hn * hn, axis=-1, keepdims=True) + EPS) * nw_ref[...]
    f = f * (1.0 + mod_ref[0, 4:5, :]) + mod_ref[0, 3:4, :]
    f_ref[0] = f.astype(BF16)
    p = _split_dot(f, rwt_ref[...]).T
    logits = p[:N_EXPERTS, :] + p[N_EXPERTS:2 * N_EXPERTS, :] + p[2 * N_EXPERTS:3 * N_EXPERTS, :]
    aff = 1.0 / (1.0 + jnp.exp(-logits))
    idx, wts = _route(aff, aff + rb_ref[...])
    idx_ref[0] = idx
    wts_ref[0] = wts


def _outproj(h, ret_o, ssd_o, na_o, w_out, mod, mod_row, norm_w, rw_t, rb, nbl):
    h_arrays, h_specs = _h_specs(h, nbl)
    bsz, ltot, d = ret_o.shape[0], ret_o.shape[1], h_arrays[0].shape[2]
    row = lambda n: pl.BlockSpec((1, TB, n), lambda b, j: (b, j, 0))
    const = lambda shape: pl.BlockSpec(shape, lambda b, j: (0,) * len(shape))
    col = lambda: pl.BlockSpec((1, TOP_K, TB), lambda b, j: (b, 0, j))
    return pl.pallas_call(
        functools.partial(_outproj_kernel, n_h=len(h_arrays), nbl=nbl),
        grid=(bsz, ltot // TB),
        in_specs=h_specs + [
            row(ret_o.shape[2]), row(ssd_o.shape[2]), row(na_o.shape[2]),
            const(w_out.shape),
            pl.BlockSpec((1, 6, d), lambda b, j: (mod_row(b, j), 0, 0)),
            const((1, d)), const(rw_t.shape), const(rb.shape),
        ],
        out_specs=[row(d), row(d), col(), col()],
        out_shape=[
            jax.ShapeDtypeStruct((bsz, ltot, d), F32),
            jax.ShapeDtypeStruct((bsz, ltot, d), BF16),
            jax.ShapeDtypeStruct((bsz, TOP_K, ltot), jnp.int32),
            jax.ShapeDtypeStruct((bsz, TOP_K, ltot), F32),
        ],
        compiler_params=_cparams("parallel", "parallel"),
        name="outproj_router",
    )(*h_arrays, ret_o, ssd_o, na_o, w_out, mod, norm_w, rw_t, rb)


RUN_ALIGN = SUBLANES
RUN_BITS = 6
PERM_ROWS = TOP_K * TB + N_EXPERTS * RUN_ALIGN
GATE_LANES = 128
GAP_BITS = 7
ZERO_ROWS = RUN_ALIGN << (GAP_BITS - 1)
assert TB == RUN_ALIGN << (RUN_BITS - 1) and MOE_TILE == RUN_ALIGN << GAP_BITS

U32 = jnp.uint32


def _pack_bf16_pairs(y):
    n = y.shape[1] // 2
    lo = lax.bitcast_convert_type(y[:, :n].astype(BF16).astype(F32), U32) >> 16
    hi = lax.bitcast_convert_type(y[:, n:].astype(BF16).astype(F32), U32) & jnp.uint32(0xFFFF0000)
    return lo | hi


def _unpack_bf16_pairs(u):
    lo = lax.bitcast_convert_type(u << 16, F32).astype(BF16)
    hi = lax.bitcast_convert_type(u & jnp.uint32(0xFFFF0000), F32).astype(BF16)
    return lo, hi


def _block_perm(idx_ref, loff_ref):
    tb = idx_ref.shape[2]
    na = TOP_K * tb
    e_iota = lax.broadcasted_iota(jnp.int32, (N_EXPERTS, tb), 0)
    oh = jnp.concatenate([jnp.where(e_iota == idx_ref[0, k:k + 1, :], 1.0, 0.0) for k in range(TOP_K)], axis=1)
    upper = lax.broadcasted_iota(jnp.int32, (na, na), 0) <= lax.broadcasted_iota(jnp.int32, (na, na), 1)
    cum = jnp.dot(oh.astype(BF16), jnp.where(upper, 1.0, 0.0).astype(BF16), preferred_element_type=F32)
    pos = jnp.sum(oh * (cum - 1.0 + loff_ref[0]), axis=0, keepdims=True).astype(jnp.int32)
    r = lax.broadcasted_iota(jnp.int32, (PERM_ROWS, tb), 0)
    return [r == pos[:, k * tb:(k + 1) * tb] for k in range(TOP_K)]


def _for_each_run_piece(blk, hoff_ref, loff_ref, ngr_ref, fn):
    for e in range(N_EXPERTS):
        j = blk * N_EXPERTS + e
        n, lo, ho = ngr_ref[j], loff_ref[j], hoff_ref[j]
        for b in range(RUN_BITS - 1, -1, -1):
            start = ((n >> (b + 1)) << (b + 1)) * RUN_ALIGN

            @pl.when(((n >> b) & 1) == 1)
            def _():
                fn(pl.multiple_of(lo + start, RUN_ALIGN), pl.multiple_of(ho + start, RUN_ALIGN), RUN_ALIGN << b, e % 2)


def _for_each_gap_piece(goff_ref, ggr_ref, fn):
    for e in range(N_EXPERTS):
        n, ho = ggr_ref[e], goff_ref[e]
        for b in range(GAP_BITS - 1, -1, -1):
            start = ((n >> (b + 1)) << (b + 1)) * RUN_ALIGN

            @pl.when(((n >> b) & 1) == 1)
            def _():
                fn(pl.multiple_of(ho + start, RUN_ALIGN), RUN_ALIGN << b)


def _dispatch_kernel(hoff_ref, loffs_ref, ngr_ref, goff_ref, ggr_ref, f_ref, idx_ref, wts_ref, loff_ref, xs_ref,
                     xperm, zbuf, sems, *, nb):
    step = pl.program_id(0) * nb + pl.program_id(1)
    n_steps = pl.num_programs(0) * nb
    slot = step % 2

    def run_copies(blk, slot_, wait):
        def fn(lo, ho, rows, dma_thread):
            cp = pltpu.make_async_copy(xperm.at[slot_, pl.ds(lo, rows)], xs_ref.at[pl.ds(ho, rows)], sems.at[slot_])
            cp.wait() if wait else cp.start(priority=dma_thread)
        _for_each_run_piece(blk, hoff_ref, loffs_ref, ngr_ref, fn)

    @pl.when(step == 0)
    def _zero_region_tails():
        zbuf[...] = jnp.zeros_like(zbuf)
        for wait in (False, True):
            def fn(ho, rows, wait=wait):
                cp = pltpu.make_async_copy(zbuf.at[pl.ds(0, rows)], xs_ref.at[pl.ds(ho, rows)], sems.at[2])
                cp.wait() if wait else cp.start()
            _for_each_gap_piece(goff_ref, ggr_ref, fn)
        tail_rows = zbuf.shape[0]

        def tail_copy(i):
            dst = xs_ref.at[pl.ds(pl.multiple_of(goff_ref[N_EXPERTS] + i * tail_rows, RUN_ALIGN), tail_rows)]
            return pltpu.make_async_copy(zbuf, dst, sems.at[2])

        lax.fori_loop(0, ggr_ref[N_EXPERTS], lambda i, c: (tail_copy(i).start(), c)[1], 0)
        lax.fori_loop(0, ggr_ref[N_EXPERTS], lambda i, c: (tail_copy(i).wait(), c)[1], 0)

    q0, q1 = _block_perm(idx_ref, loff_ref)
    q = jnp.where(jnp.logical_or(q0, q1), 1.0, 0.0).astype(BF16)
    half = f_ref.shape[2] // 2
    xperm[slot, :, :half] = _pack_bf16_pairs(jnp.dot(q, f_ref[0], preferred_element_type=F32))
    w = jnp.sum(jnp.where(q0, wts_ref[0, 0:1, :], 0.0) + jnp.where(q1, wts_ref[0, 1:2, :], 0.0),
                axis=1, keepdims=True)
    xperm[slot, :, half:] = jnp.broadcast_to(lax.bitcast_convert_type(w, U32), (PERM_ROWS, GATE_LANES))

    run_copies(step, slot, wait=False)

    @pl.when(step > 0)
    def _():
        run_copies(step - 1, 1 - slot, wait=True)

    @pl.when(step == n_steps - 1)
    def _():
        run_copies(step, slot, wait=True)


def _dispatch(f, idx, wts, loff_col, hoff, loffs, ngr, goff, ggr, p_total):
    bsz, ltot, d = f.shape
    nb = ltot // TB
    words = d // 2 + GATE_LANES
    grid_spec = pltpu.PrefetchScalarGridSpec(
        num_scalar_prefetch=5,
        grid=(bsz, nb),
        in_specs=[
            pl.BlockSpec((1, TB, d), lambda b, j, *_: (b, j, 0)),
            pl.BlockSpec((1, TOP_K, TB), lambda b, j, *_: (b, 0, j)),
            pl.BlockSpec((1, TOP_K, TB), lambda b, j, *_: (b, 0, j)),
            pl.BlockSpec((1, N_EXPERTS, 1), lambda b, j, *_: (b * nb + j, 0, 0)),
        ],
        out_specs=pl.BlockSpec(memory_space=pl.ANY),
        scratch_shapes=[pltpu.VMEM((2, PERM_ROWS, words), U32),
                        pltpu.VMEM((ZERO_ROWS, words), U32),
                        pltpu.SemaphoreType.DMA((3,))],
    )
    return pl.pallas_call(
        functools.partial(_dispatch_kernel, nb=nb),
        grid_spec=grid_spec,
        out_shape=jax.ShapeDtypeStruct((p_total, words), U32),
        compiler_params=_cparams("arbitrary", "arbitrary"),
        name="moe_dispatch",
    )(hoff, loffs, ngr, goff, ggr, f, idx, wts, loff_col)


def _moe_kernel(te_ref, nt_ref, x_ref, wg_ref, wu_ref, wd_ref, y_ref, wg_s, wu_s, wd_s):
    i = pl.program_id(0)
    fresh = jnp.logical_or(i == 0, te_ref[i] != te_ref[jnp.maximum(i - 1, 0)])

    @pl.when(fresh)
    def _():
        wg_s[...] = wg_ref[0].astype(BF16)
        wu_s[...] = wu_ref[0].astype(BF16)
        wd_s[...] = wd_ref[0].astype(BF16)

    @pl.when(i < nt_ref[0])
    def _():
        half = wg_s.shape[0] // 2
        lo, hi = _unpack_bf16_pairs(x_ref[:, :half])
        gate = lax.bitcast_convert_type(x_ref[:, half:half + 1], F32)
        g = (jnp.dot(lo, wg_s[:half, :], preferred_element_type=F32)
             + jnp.dot(hi, wg_s[half:, :], preferred_element_type=F32))
        u = (jnp.dot(lo, wu_s[:half, :], preferred_element_type=F32)
             + jnp.dot(hi, wu_s[half:, :], preferred_element_type=F32))
        he = (_silu(g) * u).astype(BF16)
        y_ref[...] = _pack_bf16_pairs(jnp.dot(he, wd_s[...], preferred_element_type=F32) * gate)

    @pl.when(i >= nt_ref[0])
    def _():
        y_ref[...] = jnp.zeros_like(y_ref)


def _moe_grouped(x_sorted, tile_expert, n_tiles_used, w_gate, w_up, w_down):
    p, words = x_sorted.shape
    _, d, de = w_gate.shape
    tm = MOE_TILE
    grid_spec = pltpu.PrefetchScalarGridSpec(
        num_scalar_prefetch=2,
        grid=(p // tm,),
        in_specs=[
            pl.BlockSpec((tm, words), lambda i, te, nt: (i, 0)),
            pl.BlockSpec((1, d, de), lambda i, te, nt: (te[i], 0, 0)),
            pl.BlockSpec((1, d, de), lambda i, te, nt: (te[i], 0, 0)),
            pl.BlockSpec((1, de, d), lambda i, te, nt: (te[i], 0, 0)),
        ],
        out_specs=pl.BlockSpec((tm, d // 2), lambda i, te, nt: (i, 0)),
        scratch_shapes=[pltpu.VMEM((d, de), BF16), pltpu.VMEM((d, de), BF16), pltpu.VMEM((de, d), BF16)],
    )
    return pl.pallas_call(
        _moe_kernel,
        grid_spec=grid_spec,
        out_shape=jax.ShapeDtypeStruct((p, d // 2), U32),
        compiler_params=_cparams("arbitrary"),
        name="moe_grouped",
    )(tile_expert, n_tiles_used, x_sorted, w_gate, w_up, w_down)


def _combine_kernel(hoff_ref, loffs_ref, ngr_ref, h_ref, idx_ref, loff_ref, mod_ref, ys_ref, o_ref,
                    yperm, sems, *, nb):
    b, j = pl.program_id(0), pl.program_id(1)
    nj = pl.num_programs(1)
    step = b * nj + j
    slot = step % 2
    blk = b * nb + j
    nxt = jnp.where(j == nj - 1, (b + 1) * nb, blk + 1)

    def run_copies(blk_, slot_, wait):
        def fn(lo, ho, rows, dma_thread):
            cp = pltpu.make_async_copy(ys_ref.at[pl.ds(ho, rows)], yperm.at[slot_, pl.ds(lo, rows)], sems.at[slot_])
            cp.wait() if wait else cp.start(priority=dma_thread)
        _for_each_run_piece(blk_, hoff_ref, loffs_ref, ngr_ref, fn)

    @pl.when(step == 0)
    def _():
        yperm[...] = jnp.zeros_like(yperm)
        run_copies(blk, 0, wait=False)

    @pl.when(step + 1 < pl.num_programs(0) * nj)
    def _():
        run_copies(nxt, 1 - slot, wait=False)

    q0, q1 = _block_perm(idx_ref, loff_ref)
    q = jnp.where(jnp.logical_or(q0, q1), 1.0, 0.0).astype(BF16)
    run_copies(blk, slot, wait=True)
    lo, hi = _unpack_bf16_pairs(yperm[slot])
    y = jnp.concatenate([_tn_dot(q, lo), _tn_dot(q, hi)], axis=1)
    o_ref[0] = h_ref[0] + mod_ref[0, 5:6, :] * y


def _combine(h, idx, loff_col, mod, mod_row, y_sorted, hoff, loffs, ngr, n_out):
    bsz, ltot, d = h.shape
    nb = ltot // TB
    grid_spec = pltpu.PrefetchScalarGridSpec(
        num_scalar_prefetch=3,
        grid=(bsz, n_out // TB),
        in_specs=[
            pl.BlockSpec((1, TB, d), lambda b, j, *_: (b, j, 0)),
            pl.BlockSpec((1, TOP_K, TB), lambda b, j, *_: (b, 0, j)),
            pl.BlockSpec((1, N_EXPERTS, 1), lambda b, j, *_: (b * nb + j, 0, 0)),
            pl.BlockSpec((1, 6, d), lambda b, j, *_: (mod_row(b, j), 0, 0)),
            pl.BlockSpec(memory_space=pl.ANY),
        ],
        out_specs=pl.BlockSpec((1, TB, d), lambda b, j, *_: (b, j, 0)),
        scratch_shapes=[pltpu.VMEM((2, PERM_ROWS, d // 2), U32), pltpu.SemaphoreType.DMA((2,))],
    )
    return pl.pallas_call(
        functools.partial(_combine_kernel, nb=nb),
        grid_spec=grid_spec,
        out_shape=jax.ShapeDtypeStruct((bsz, n_out, d), F32),
        compiler_params=_cparams("arbitrary", "arbitrary"),
        name="moe_combine",
    )(hoff, loffs, ngr, h, idx, loff_col, mod, y_sorted)


def _moe_block(h, f, idx, wts, mod, mod_row, w_gate, w_up, w_down, first_expert, n_out):
    bsz, ltot, d = h.shape
    nb = ltot // TB
    n_blocks = bsz * nb
    mt = MOE_TILE
    experts = jnp.arange(N_EXPERTS, dtype=jnp.int32)
    cnt = jnp.sum((idx.reshape(bsz, TOP_K, nb, TB, 1) == experts).astype(jnp.int32), axis=(1, 3))
    cnt = cnt.reshape(n_blocks, N_EXPERTS)
    run = ((cnt + RUN_ALIGN - 1) // RUN_ALIGN) * RUN_ALIGN
    loffs = jnp.cumsum(run, axis=1) - run
    region = ((jnp.sum(run, axis=0) + mt - 1) // mt) * mt
    ends = jnp.cumsum(region)
    hoff = (ends - region)[None, :] + jnp.cumsum(run, axis=0) - run
    p_total = ((TOP_K * bsz * ltot + n_blocks * N_EXPERTS * (RUN_ALIGN - 1)) // mt + N_EXPERTS + 1) * mt
    tile_start = jnp.arange(p_total // mt, dtype=jnp.int32) * mt
    tile_expert = jnp.minimum(jnp.sum((tile_start[:, None] >= ends[None, :]).astype(jnp.int32), axis=1),
                              N_EXPERTS - 1)
    n_used = (ends[-1] // mt).astype(jnp.int32).reshape(1)
    flat = lambda t: t.reshape(-1).astype(jnp.int32)
    hoff, loffs_flat, ngr = flat(hoff), flat(loffs), flat(run // RUN_ALIGN)
    loff_col = loffs.astype(F32).reshape(n_blocks, N_EXPERTS, 1)
    used = jnp.sum(run, axis=0)
    goff = flat(jnp.concatenate([ends - region + used, ends[-1:]]))
    ggr = flat(jnp.concatenate([(region - used) // RUN_ALIGN, (p_total - ends[-1:]) // ZERO_ROWS]))

    x_sorted = _dispatch(f, idx, wts, loff_col, hoff, loffs_flat, ngr, goff, ggr, p_total)
    y_sorted = _moe_grouped(x_sorted, tile_expert + first_expert, n_used, w_gate, w_up, w_down)
    return _combine(h, idx, loff_col, mod, mod_row, y_sorted, hoff, loffs_flat, ngr, n_out)


def _rope_tables(n_lat, n_ctx, n_heads):
    t = jnp.arange(n_lat)
    pos = jnp.stack([t // GRID_W, t % GRID_W], axis=-1).astype(F32)
    n_freq = HEAD_DIM // 4
    inv = 1.0 / (ROPE_BASE ** (jnp.arange(n_freq, dtype=F32) / n_freq))
    ang = pos[:, :, None] * inv
    cos, sin = jnp.cos(ang), jnp.sin(ang)
    cos_h = jnp.concatenate([cos[:, 0], cos[:, 0], cos[:, 1], cos[:, 1]], axis=-1)
    sin_h = jnp.concatenate([-sin[:, 0], sin[:, 0], -sin[:, 1], sin[:, 1]], axis=-1)
    cos_h = jnp.concatenate([cos_h, jnp.ones((n_ctx, HEAD_DIM), F32)], axis=0)
    sin_h = jnp.concatenate([sin_h, jnp.zeros((n_ctx, HEAD_DIM), F32)], axis=0)
    return jnp.tile(cos_h, (1, n_heads)), jnp.tile(sin_h, (1, n_heads))


def _pack_w_in(w_in, rw, sw, nw_):
    o = 0
    seg = {}
    for name, width in (("rq", rw), ("rk", rw), ("rv", rw), ("rg", rw), ("z", sw), ("x", sw),
                        ("b", SSD_GROUPS * SSD_STATE), ("c", SSD_GROUPS * SSD_STATE),
                        ("dt", 2 * (sw // HEAD_DIM)), ("nq", nw_), ("nk", nw_), ("nv", nw_)):
        seg[name] = w_in[:, o:o + width]
        o += width
    dt_pad = jnp.zeros((w_in.shape[0], 128 - seg["dt"].shape[1]), w_in.dtype)
    cols = [seg["rq"], seg["rk"], seg["rv"], seg["rg"],
            seg["z"], seg["x"], seg["b"], seg["c"], seg["nq"], seg["nk"], seg["nv"], seg["dt"], dt_pad]
    return jnp.concatenate(cols, axis=1).astype(BF16)


def kernel(x, c, ctx, c_ctx, w_mod, b_mod, norm_mix, norm_ffn, w_in, w_out, ret_decay_f, ret_decay_b,
           ssd_conv_w, ssd_conv_b, ssd_dt_bias_f, ssd_dt_bias_b, ssd_a_log_f, ssd_a_log_b, ssd_d, ssd_norm,
           na_q_norm, na_k_norm, na_rpb, router_w, router_b, w_gate, w_up, w_down):
    bsz, n_lat, d = x.shape
    n_ctx = ctx.shape[1]
    depth = w_mod.shape[0]
    rows = n_lat // GRID_W
    rw = ret_decay_f.shape[1] * HEAD_DIM
    sw = ssd_d.shape[1] * HEAD_DIM
    nw_ = na_rpb.shape[1] * HEAD_DIM
    dims = (rw, sw, nw_)
    assert n_lat % TB == 0 and n_ctx % TB == 0
    nbl = n_lat // TB

    n_cond = ((bsz + 1 + 7) // 8) * 8
    cond = jnp.zeros((n_cond, d), F32).at[:bsz].set(c).at[bsz].set(c_ctx)
    mod = _modulation(cond, w_mod, b_mod).reshape(depth, n_cond, 6, d)
    mod_row = lambda b, j: jnp.where(j < nbl, b, bsz)

    cos, sin = _rope_tables(n_lat, n_ctx, rw // HEAD_DIM)
    hm = jnp.asarray(np.kron(np.eye(nw_ // HEAD_DIM), np.full((HEAD_DIM, HEAD_DIM), 1.0 / HEAD_DIM)), F32)
    rw_t = jnp.pad(jnp.concatenate(_bf16_parts(router_w, 3), axis=1), ((0, 0), (0, 128 - 3 * router_w.shape[1])))
    rb = router_b.reshape(-1, 1)

    expert_w = [t.reshape((-1,) + t.shape[2:]) for t in (w_gate, w_up, w_down)]
    h = (x, ctx)
    for l in range(depth):
        last = l == depth - 1
        w_all = _pack_w_in(w_in[l], rw, sw, nw_)
        qkn = jnp.stack([jnp.tile(na_q_norm[l], nw_ // HEAD_DIM), jnp.tile(na_k_norm[l], nw_ // HEAD_DIM)])
        ret, z, xbc, dt, na = _inproj(h, mod[l], mod_row, norm_mix[l].reshape(1, d), w_all, cos, sin, qkn, hm, dims, nbl)
        ret_o, ssd_o = _scan_mixers(
            ret, jnp.concatenate([ret_decay_f[l], ret_decay_b[l]]).reshape(-1, 1, 1), z, xbc, dt,
            ssd_conv_w[l], ssd_conv_b[l], jnp.concatenate([ssd_dt_bias_f[l], ssd_dt_bias_b[l]]),
            jnp.concatenate([ssd_a_log_f[l], ssd_a_log_b[l]]), ssd_d[l], ssd_norm[l], nbl)
        na_o = _na_attention(na, _na_bias_tables(na_rpb[l], rows), nbl)
        h, f, idx, wts = _outproj(h, ret_o, ssd_o, na_o, w_out[l].astype(BF16), mod[l], mod_row,
                                  norm_ffn[l].reshape(1, d), rw_t, rb, nbl)
        h = _moe_block(h, f, idx, wts, mod[l], mod_row, *expert_w, l * N_EXPERTS,
                       n_lat if last else n_lat + n_ctx)
    return h
```
